```python
import jax
import jax.numpy as jnp
from jax import lax
import numpy as np

D_MODEL = 1024
BATCH = 8
SEQ = 2048
DEPTH = 2

GRID_W = 64
CTX_LEN = 256
N_HEADS = 8
QK_NOPE = 64
QK_ROPE = 32
QK_HEAD = QK_NOPE + QK_ROPE
V_HEAD = 64
Q_LORA = 384
KV_LORA = 256
ROPE_BASE = 10000.0
Q_BLOCK = 128
CONV_CH = 256
CONV_WIDTH = 31
CONV_PAD = (CONV_WIDTH - 1) // 2
FOURIER_GROUPS = 4
FOURIER_GROUP_DIM = 64
FOURIER_CH = FOURIER_GROUPS * FOURIER_GROUP_DIM
ATTN_OUT = N_HEADS * V_HEAD
MIX_WIDTH = ATTN_OUT + CONV_CH + FOURIER_CH
OFF_CQ = 0
OFF_CKV = OFF_CQ + Q_LORA
OFF_KR = OFF_CKV + KV_LORA
OFF_CONV = OFF_KR + QK_ROPE
OFF_FOUR = OFF_CONV + 2 * CONV_CH
IN_COLS = OFF_FOUR + FOURIER_CH
D_FF = 2816
N_EXPERTS = 8
TOP_K = 2
N_DENSE = (DEPTH + 1) // 2
N_MOE = DEPTH // 2
N_MOD = 6
EPS = 1e-6

kernel_name = 'hybrid_mla_conformer_fnet_moe_dit'


def _rmsnorm(x, g):
    xf = x.astype(jnp.float32)
    y = xf * lax.rsqrt(jnp.mean(xf * xf, axis=-1, keepdims=True) + EPS)
    return (y * g.astype(jnp.float32)).astype(x.dtype)


def _layernorm(x, g, b):
    xf = x.astype(jnp.float32)
    mu = jnp.mean(xf, axis=-1, keepdims=True)
    var = jnp.mean(jnp.square(xf - mu), axis=-1, keepdims=True)
    y = (xf - mu) * lax.rsqrt(var + EPS)
    return (y * g.astype(jnp.float32) + b.astype(jnp.float32)).astype(x.dtype)


def _modulation(cvec, w, b):
    m = jax.nn.silu(cvec) @ w + b
    return [p[:, None, :] for p in jnp.split(m, N_MOD, axis=-1)]


def _adaln(x, g, shift, scale):
    return _rmsnorm(x, g) * (1 + scale) + shift


def _axial_rope(rows):
    row = jnp.repeat(jnp.arange(rows, dtype=jnp.float32), GRID_W)
    col = jnp.tile(jnp.arange(GRID_W, dtype=jnp.float32), rows)
    n_freq = QK_ROPE // 4
    inv = ROPE_BASE ** (-jnp.arange(n_freq, dtype=jnp.float32) / n_freq)
    ang = jnp.concatenate([row[:, None] * inv, col[:, None] * inv], axis=-1)
    return jnp.cos(ang), jnp.sin(ang)


def _apply_rope(x, rope):
    if rope is None:
        return x
    cos, sin = rope
    cos = cos[:, None, :]
    sin = sin[:, None, :]
    xf = x.astype(jnp.float32)
    x1, x2 = jnp.split(xf, 2, axis=-1)
    return jnp.concatenate([x1 * cos - x2 * sin, x2 * cos + x1 * sin], axis=-1).astype(x.dtype)


def _mla_queries(cq, w_uq, q_lat_g, q_norm_g, rope):
    b, s, _ = cq.shape
    q = (_rmsnorm(cq, q_lat_g) @ w_uq).reshape(b, s, N_HEADS, QK_HEAD)
    q = _rmsnorm(q, q_norm_g)
    return jnp.concatenate([q[..., :QK_NOPE], _apply_rope(q[..., QK_NOPE:], rope)], axis=-1)


def _mla_keys_values(ckv, kr, w_ukv, kv_lat_g, k_norm_g, rope):
    b, s, _ = ckv.shape
    kv = (_rmsnorm(ckv, kv_lat_g) @ w_ukv).reshape(b, s, N_HEADS, QK_NOPE + V_HEAD)
    k_rope = jnp.broadcast_to(kr[:, :, None, :], (b, s, N_HEADS, QK_ROPE))
    k = _rmsnorm(jnp.concatenate([kv[..., :QK_NOPE], k_rope], axis=-1), k_norm_g)
    k = jnp.concatenate([k[..., :QK_NOPE], _apply_rope(k[..., QK_NOPE:], rope)], axis=-1)
    return k, kv[..., QK_NOPE:]


def _attend(q, k, v):
    b, s, h, d = q.shape
    nb = s // Q_BLOCK
    scale = d ** -0.5
    qb = jnp.moveaxis(q.reshape(b, nb, Q_BLOCK, h, d), 1, 0)

    def block(qblk):
        sc = jnp.einsum('bqhd,bkhd->bhqk', qblk, k).astype(jnp.float32) * scale
        p = jax.nn.softmax(sc, axis=-1).astype(v.dtype)
        return jnp.einsum('bhqk,bkhd->bqhd', p, v)

    o = lax.map(block, qb)
    return jnp.moveaxis(o, 0, 1).reshape(b, s, h * v.shape[-1])


def _conformer_conv(u, conv_w, conv_b, ln_g, ln_b):
    a, g = jnp.split(u, 2, axis=-1)
    y = a * jax.nn.sigmoid(g)
    y = lax.conv_general_dilated(
        y, conv_w[:, None, :], window_strides=(1,), padding=[(CONV_PAD, CONV_PAD)],
        dimension_numbers=('NWC', 'WIO', 'NWC'), feature_group_count=CONV_CH) + conv_b
    return jax.nn.silu(_layernorm(y, ln_g, ln_b))


def _fourier_mix(u):
    b, s, _ = u.shape
    z = u.astype(jnp.float32).reshape(b, s, FOURIER_GROUPS, FOURIER_GROUP_DIM)
    f = jnp.real(jnp.fft.fft2(z, axes=(1, 3), norm='ortho'))
    return f.reshape(b, s, FOURIER_CH).astype(u.dtype)


def _mixer_out(cols, k, v, rope, w_uq, q_lat_g, q_norm_g, conv_w, conv_b, conv_ln_g, conv_ln_b, w_out):
    q = _mla_queries(cols[..., OFF_CQ:OFF_CKV], w_uq, q_lat_g, q_norm_g, rope)
    attn = _attend(q, k, v)
    conv = _conformer_conv(cols[..., OFF_CONV:OFF_FOUR], conv_w, conv_b, conv_ln_g, conv_ln_b)
    four = _fourier_mix(cols[..., OFF_FOUR:IN_COLS])
    return jnp.concatenate([attn, conv, four], axis=-1) @ w_out


def _swiglu(h, wg, wu, wd):
    return (jax.nn.silu(h @ wg) * (h @ wu)) @ wd


def _moe_swiglu(h, router_w, wg, wu, wd):
    b, s, d = h.shape
    t = h.reshape(b * s, d)
    logits = (t @ router_w).astype(jnp.float32)
    top_v, top_i = lax.top_k(logits, TOP_K)
    w = jax.nn.softmax(top_v, axis=-1)
    gates = jnp.einsum('nk,nke->ne', w, jax.nn.one_hot(top_i, N_EXPERTS, dtype=jnp.float32)).astype(h.dtype)
    out = jnp.zeros_like(t)
    for e in range(N_EXPERTS):
        out = out + gates[:, e:e + 1] * _swiglu(t, wg[e], wu[e], wd[e])
    return out.reshape(b, s, d)


def _channel_mixer(h, layer, ffn_w_gate, ffn_w_up, ffn_w_down, router_w, moe_w_gate, moe_w_up, moe_w_down):
    i = layer // 2
    if layer % 2 == 0:
        return _swiglu(h, ffn_w_gate[i], ffn_w_up[i], ffn_w_down[i])
    return _moe_swiglu(h, router_w[i], moe_w_gate[i], moe_w_up[i], moe_w_down[i])


def _normal(k, shape, scale):
    return jax.random.normal(k, shape, jnp.float32) * scale


def setup_inputs(seed: int = 0) -> dict:
    key = jax.random.key(seed)
    ks = jax.random.split(key, 32)
    D = D_MODEL
    return {
        'x': _normal(ks[0], (BATCH, SEQ, D), 1.0),
        'c': _normal(ks[1], (BATCH, D), 1.0),
        'ctx': _normal(ks[2], (BATCH, CTX_LEN, D), 1.0),
        'c_ctx': _normal(ks[3], (D,), 1.0),
        'ada_w': _normal(ks[4], (DEPTH, D, N_MOD * D), 0.5 * D ** -0.5),
        'ada_b': _normal(ks[5], (DEPTH, N_MOD * D), 0.02),
        'mix_norm_g': 1.0 + _normal(ks[6], (DEPTH, D), 0.05),
        'ffn_norm_g': 1.0 + _normal(ks[7], (DEPTH, D), 0.05),
        'w_in': _normal(ks[8], (DEPTH, D, IN_COLS), D ** -0.5),
        'q_lat_g': 1.0 + _normal(ks[9], (DEPTH, Q_LORA), 0.05),
        'kv_lat_g': 1.0 + _normal(ks[10], (DEPTH, KV_LORA), 0.05),
        'w_uq': _normal(ks[11], (DEPTH, Q_LORA, N_HEADS * QK_HEAD), Q_LORA ** -0.5),
        'w_ukv': _normal(ks[12], (DEPTH, KV_LORA, N_HEADS * (QK_NOPE + V_HEAD)), KV_LORA ** -0.5),
        'q_norm_g': 1.0 + _normal(ks[13], (DEPTH, QK_HEAD), 0.05),
        'k_norm_g': 1.0 + _normal(ks[14], (DEPTH, QK_HEAD), 0.05),
        'conv_w': _normal(ks[15], (DEPTH, CONV_WIDTH, CONV_CH), CONV_WIDTH ** -0.5),
        'conv_b': _normal(ks[16], (DEPTH, CONV_CH), 0.02),
        'conv_ln_g': 1.0 + _normal(ks[17], (DEPTH, CONV_CH), 0.05),
        'conv_ln_b': _normal(ks[18], (DEPTH, CONV_CH), 0.02),
        'w_out': _normal(ks[19], (DEPTH, MIX_WIDTH, D), MIX_WIDTH ** -0.5),
        'ffn_w_gate': _normal(ks[20], (N_DENSE, D, D_FF), D ** -0.5),
        'ffn_w_up': _normal(ks[21], (N_DENSE, D, D_FF), D ** -0.5),
        'ffn_w_down': _normal(ks[22], (N_DENSE, D_FF, D), D_FF ** -0.5),
        'router_w': _normal(ks[23], (N_MOE, D, N_EXPERTS), D ** -0.5),
        'moe_w_gate': _normal(ks[24], (N_MOE, N_EXPERTS, D, D_FF), D ** -0.5),
        'moe_w_up': _normal(ks[25], (N_MOE, N_EXPERTS, D, D_FF), D ** -0.5),
        'moe_w_down': _normal(ks[26], (N_MOE, N_EXPERTS, D_FF, D), D_FF ** -0.5),
    }


def reference(x, c, ctx, c_ctx, ada_w, ada_b, mix_norm_g, ffn_norm_g, w_in, q_lat_g, kv_lat_g, w_uq, w_ukv,
              q_norm_g, k_norm_g, conv_w, conv_b, conv_ln_g, conv_ln_b, w_out, ffn_w_gate, ffn_w_up, ffn_w_down,
              router_w, moe_w_gate, moe_w_up, moe_w_down):
    rows = x.shape[1] // GRID_W
    rope = _axial_rope(rows)
    for l in range(DEPTH):
        last = l == DEPTH - 1
        sx = _modulation(c, ada_w[l], ada_b[l])
        sc = _modulation(c_ctx[None, :], ada_w[l], ada_b[l])

        hc = _adaln(ctx, mix_norm_g[l], sc[0], sc[1])
        if last:
            kv_cols = hc @ w_in[l][:, OFF_CKV:OFF_CONV]
            k_c, v_c = _mla_keys_values(kv_cols[..., :KV_LORA], kv_cols[..., KV_LORA:],
                                        w_ukv[l], kv_lat_g[l], k_norm_g[l], None)
        else:
            cols_c = hc @ w_in[l]
            k_c, v_c = _mla_keys_values(cols_c[..., OFF_CKV:OFF_KR], cols_c[..., OFF_KR:OFF_CONV],
                                        w_ukv[l], kv_lat_g[l], k_norm_g[l], None)
            mix_c = _mixer_out(cols_c, k_c, v_c, None, w_uq[l], q_lat_g[l], q_norm_g[l],
                               conv_w[l], conv_b[l], conv_ln_g[l], conv_ln_b[l], w_out[l])
            ctx_next = ctx + sc[2] * mix_c
            ctx_next = ctx_next + sc[5] * _channel_mixer(
                _adaln(ctx_next, ffn_norm_g[l], sc[3], sc[4]), l,
                ffn_w_gate, ffn_w_up, ffn_w_down, router_w, moe_w_gate, moe_w_up, moe_w_down)

        hx = _adaln(x, mix_norm_g[l], sx[0], sx[1])
        cols_x = hx @ w_in[l]
        k_x, v_x = _mla_keys_values(cols_x[..., OFF_CKV:OFF_KR], cols_x[..., OFF_KR:OFF_CONV],
                                    w_ukv[l], kv_lat_g[l], k_norm_g[l], rope)
        k_all = jnp.concatenate([k_x, k_c], axis=1)
        v_all = jnp.concatenate([v_x, v_c], axis=1)
        mix_x = _mixer_out(cols_x, k_all, v_all, rope, w_uq[l], q_lat_g[l], q_norm_g[l],
                           conv_w[l], conv_b[l], conv_ln_g[l], conv_ln_b[l], w_out[l])
        x = x + sx[2] * mix_x
        x = x + sx[5] * _channel_mixer(
            _adaln(x, ffn_norm_g[l], sx[3], sx[4]), l,
            ffn_w_gate, ffn_w_up, ffn_w_down, router_w, moe_w_gate, moe_w_up, moe_w_down)

        if not last:
            ctx = ctx_next
    return x
```

```python
import functools
import math

import numpy as np
import jax
import jax.numpy as jnp
from jax import lax
from jax.experimental import pallas as pl
from jax.experimental.pallas import tpu as pltpu

F32 = jnp.float32
BF16 = jnp.bfloat16

D_MODEL = 1024
DEPTH = 2
GRID_W = 64
N_HEADS = 8
QK_NOPE = 64
QK_ROPE = 32
QK_HEAD = QK_NOPE + QK_ROPE
V_HEAD = 64
Q_LORA = 384
KV_LORA = 256
ROPE_BASE = 10000.0
CONV_CH = 256
CONV_WIDTH = 31
CONV_PAD = (CONV_WIDTH - 1) // 2
FOURIER_GROUPS = 4
FOURIER_GROUP_DIM = 64
FOURIER_CH = FOURIER_GROUPS * FOURIER_GROUP_DIM
ATTN_OUT = N_HEADS * V_HEAD
OFF_CQ = 0
OFF_CKV = OFF_CQ + Q_LORA
OFF_KR = OFF_CKV + KV_LORA
OFF_CONV = OFF_KR + QK_ROPE
OFF_FOUR = OFF_CONV + 2 * CONV_CH
IN_COLS = OFF_FOUR + FOURIER_CH
D_FF = 2816
N_EXPERTS = 8
N_MOD = 6
EPS = 1e-6

LANES = 128
SUBLANES = 8
HEAD_PAD = LANES
VMEM_CAP = 56 * 1024 * 1024
FF_CHUNK = 256

TM_FRONT = 256
TQ_ATTN = 256
TM_OUT = 256
TM_FFN = 512
TM_MOE = 512
TM_ROUTE = 256
CONV_ROWS = 128
TR_FOURIER = 512


def _cparams(sem, vmem_bytes):
    return pltpu.CompilerParams(dimension_semantics=sem, vmem_limit_bytes=int(min(VMEM_CAP, vmem_bytes)))


def _rms(v, g):
    return v * lax.rsqrt(jnp.mean(v * v, axis=-1, keepdims=True) + EPS) * g


def _sigmoid(v):
    return 1.0 / (1.0 + jnp.exp(-v))


def _mod_kernel(c_ref, w_ref, b_ref, o_ref):
    c = c_ref[...]
    s = (c * _sigmoid(c)).astype(BF16)
    o_ref[...] = jnp.dot(s, w_ref[...].astype(BF16), preferred_element_type=F32) + b_ref[...]


def _modulation(cc, ada_w, ada_b):
    rows = cc.shape[0]
    tn = 1536
    n_out = N_MOD * D_MODEL
    return pl.pallas_call(
        _mod_kernel,
        grid=(DEPTH, n_out // tn),
        in_specs=[
            pl.BlockSpec((rows, D_MODEL), lambda l, j: (0, 0)),
            pl.BlockSpec((None, D_MODEL, tn), lambda l, j: (l, 0, j)),
            pl.BlockSpec((None, 1, tn), lambda l, j: (l, 0, j)),
        ],
        out_specs=pl.BlockSpec((None, rows, tn), lambda l, j: (l, 0, j)),
        out_shape=jax.ShapeDtypeStruct((DEPTH, rows, n_out), F32),
        compiler_params=_cparams(("arbitrary", "arbitrary"), 4 * D_MODEL * tn * 4),
        name="modulation",
    )(cc, ada_w, ada_b.reshape(DEPTH, 1, n_out))


def _rope(v, cos, sa, sb):
    return v * cos + pltpu.roll(v, 16, 1) * sa + pltpu.roll(v, LANES - 16, 1) * sb


def _front_kernel(*refs, full, rope):
    it = iter(refs)
    x_ref, mod_ref, g_ref, win_ref = next(it), next(it), next(it), next(it)
    if full:
        qlg_ref, wuq_ref, qg_ref = next(it), next(it), next(it)
    kvlg_ref, wukv_ref, kg_ref = next(it), next(it), next(it)
    if rope:
        cos_ref, sa_ref, sb_ref = next(it), next(it), next(it)
    if full:
        bd_ref = next(it)
        q_out = next(it)
    k_out, v_out = next(it), next(it)
    if full:
        y_out, zz_out = next(it), next(it)

    x = x_ref[...]
    shift = mod_ref[0:1, :]
    scale = mod_ref[1:2, :]
    h = _rms(x, g_ref[...]) * (1.0 + scale) + shift
    cols = jnp.dot(h.astype(BF16), win_ref[...], preferred_element_type=F32)

    if rope:
        cos, sa, sb = cos_ref[...], sa_ref[...], sb_ref[...]

    def head_norm(v, g):
        ss = jnp.sum(v * v, axis=-1, keepdims=True) * (1.0 / QK_HEAD)
        v = v * lax.rsqrt(ss + EPS) * g
        if rope:
            v = _rope(v, cos, sa, sb)
        return v.astype(BF16)

    o = 0
    if full:
        cq = cols[:, 0:Q_LORA]
        o = Q_LORA
        qr = jnp.dot(_rms(cq, qlg_ref[...]).astype(BF16), wuq_ref[...], preferred_element_type=F32)
        qg = qg_ref[...]
        for hd in range(N_HEADS):
            sl = slice(hd * HEAD_PAD, (hd + 1) * HEAD_PAD)
            q_out[:, sl] = head_norm(qr[:, sl], qg)

    ckv = cols[:, o:o + KV_LORA]
    o += KV_LORA
    kv = jnp.dot(_rms(ckv, kvlg_ref[...]).astype(BF16), wukv_ref[...], preferred_element_type=F32)
    if full:
        a = cols[:, o:o + CONV_CH]
        gt = cols[:, o + CONV_CH:o + 2 * CONV_CH]
        y_out[...] = a * _sigmoid(gt)
        o += 2 * CONV_CH
        z = cols[:, o:o + FOURIER_CH]
        o += FOURIER_CH
        zz_out[...] = jnp.dot(z.astype(BF16), bd_ref[...], preferred_element_type=F32).astype(BF16)
    krb = cols[:, o:o + HEAD_PAD]
    kg = kg_ref[...]
    for hd in range(N_HEADS):
        sl = slice(hd * HEAD_PAD, (hd + 1) * HEAD_PAD)
        k_out[:, sl] = head_norm(kv[:, sl] + krb, kg)
    v_out[...] = kv[:, N_HEADS * HEAD_PAD:].astype(BF16)


def _front(xs, mod, norm_g, p, rope_tabs, *, full):
    B, S, D = xs.shape
    tm = min(TM_FRONT, S)
    rope = rope_tabs is not None
    ncol = p["w_in_full"].shape[1] if full else p["w_in_kv"].shape[1]
    bm = mod.shape[0]
    mod_map = (lambda b, i: (b, 0, 0)) if bm > 1 else (lambda b, i: (0, 0, 0))
    const2 = lambda b, i: (0, 0)

    args = [xs, mod, norm_g, p["w_in_full"] if full else p["w_in_kv"]]
    specs = [
        pl.BlockSpec((None, tm, D), lambda b, i: (b, i, 0)),
        pl.BlockSpec((None, N_MOD, D), mod_map),
        pl.BlockSpec((1, D), const2),
        pl.BlockSpec((D, ncol), const2),
    ]
    if full:
        args += [p["q_lat_g"], p["w_uq"], p["q_norm_g"]]
        specs += [pl.BlockSpec((1, Q_LORA), const2),
                  pl.BlockSpec((Q_LORA, N_HEADS * HEAD_PAD), const2),
                  pl.BlockSpec((1, HEAD_PAD), const2)]
    args += [p["kv_lat_g"], p["w_ukv"], p["k_norm_g"]]
    specs += [pl.BlockSpec((1, KV_LORA), const2),
              pl.BlockSpec((KV_LORA, N_HEADS * HEAD_PAD + ATTN_OUT), const2),
              pl.BlockSpec((1, HEAD_PAD), const2)]
    if rope:
        args += list(rope_tabs)
        specs += [pl.BlockSpec((tm, HEAD_PAD), lambda b, i: (i, 0))] * 3
    if full:
        args += [p["bd"]]
        specs += [pl.BlockSpec((FOURIER_CH, 2 * FOURIER_CH), const2)]

    out_shapes, out_specs = [], []

    def add_out(width, dtype):
        out_shapes.append(jax.ShapeDtypeStruct((B, S, width), dtype))
        out_specs.append(pl.BlockSpec((None, tm, width), lambda b, i: (b, i, 0)))

    if full:
        add_out(N_HEADS * HEAD_PAD, BF16)
    add_out(N_HEADS * HEAD_PAD, BF16)
    add_out(ATTN_OUT, BF16)
    if full:
        add_out(CONV_CH, F32)
        add_out(2 * FOURIER_CH, BF16)

    vmem = 2 * (D * ncol * 2 + Q_LORA * 1024 * 2 + KV_LORA * 1536 * 2) + 24 * tm * D * 4
    return pl.pallas_call(
        functools.partial(_front_kernel, full=full, rope=rope),
        grid=(B, S // tm),
        in_specs=specs,
        out_specs=out_specs,
        out_shape=out_shapes,
        compiler_params=_cparams(("arbitrary", "arbitrary"), vmem),
        name="front_full" if full else "front_kv",
    )(*args)


def _attn_kernel(*refs, nsrc):
    q_ref = refs[0]
    k_refs = refs[1:1 + nsrc]
    v_refs = refs[1 + nsrc:1 + 2 * nsrc]
    o_ref = refs[-1]
    outs = []
    for hh in range(2):
        sl = slice(hh * HEAD_PAD, (hh + 1) * HEAD_PAD)
        q = q_ref[:, sl]
        scores = [lax.dot_general(q, k[:, sl], (((1,), (1,)), ((), ())), preferred_element_type=F32)
                  for k in k_refs]
        m = functools.reduce(jnp.maximum, [jnp.max(s, axis=-1, keepdims=True) for s in scores])
        ps = [jnp.exp(s - m) for s in scores]
        l = functools.reduce(jnp.add, [jnp.sum(pp, axis=-1, keepdims=True) for pp in ps])
        acc = functools.reduce(jnp.add, [jnp.dot(pp.astype(BF16), v[...], preferred_element_type=F32)
                                         for pp, v in zip(ps, v_refs)])
        outs.append(acc / l)
    lane = lax.broadcasted_iota(jnp.int32, outs[0].shape, 1)
    o_ref[...] = jnp.where(lane < V_HEAD, outs[0], outs[1]).astype(BF16)


def _attend(q, ks, vs):
    B, S, _ = q.shape
    tq = min(TQ_ATTN, S)
    nsrc = len(ks)
    specs = [pl.BlockSpec((None, tq, 2 * HEAD_PAD), lambda b, hp, i: (b, i, hp))]
    for k in ks:
        specs.append(pl.BlockSpec((None, k.shape[1], 2 * HEAD_PAD), lambda b, hp, i: (b, 0, hp)))
    for v in vs:
        specs.append(pl.BlockSpec((None, v.shape[1], 2 * V_HEAD), lambda b, hp, i: (b, 0, hp)))
    t_all = sum(k.shape[1] for k in ks)
    vmem = 2 * (t_all * 384 * 2) * 2 + 10 * tq * t_all * 4 + (8 << 20)
    return pl.pallas_call(
        functools.partial(_attn_kernel, nsrc=nsrc),
        grid=(B, N_HEADS // 2, S // tq),
        in_specs=specs,
        out_specs=pl.BlockSpec((None, tq, 2 * V_HEAD), lambda b, hp, i: (b, i, hp)),
        out_shape=jax.ShapeDtypeStruct((B, S, ATTN_OUT), BF16),
        compiler_params=_cparams(("arbitrary", "arbitrary", "arbitrary"), vmem),
        name="attend%d" % nsrc,
    )(q, *ks, *vs)


def _conv_kernel(y_ref, w_ref, b_ref, lg_ref, lb_ref, o_ref, pad_ref, *, seq):
    halo = 2 * SUBLANES
    pad_ref[0:halo, :] = jnp.zeros((halo, CONV_CH), F32)
    pad_ref[halo + seq:2 * halo + seq, :] = jnp.zeros((halo, CONV_CH), F32)
    pad_ref[halo:halo + seq, :] = y_ref[...]
    rows = CONV_ROWS
    span = rows + 3 * SUBLANES

    def chunk(r, carry):
        base = pl.multiple_of(r * rows, rows)
        win = pad_ref[pl.ds(base, rows + 2 * halo), :]
        acc = jnp.zeros((rows, CONV_CH), F32)
        for ph in range(SUBLANES):
            offs = [t for t in range(1, CONV_WIDTH + 1) if t % SUBLANES == ph]
            if not offs:
                continue
            shifted = win[ph:ph + span, :]
            for t in offs:
                a0 = t - ph
                acc = acc + shifted[a0:a0 + rows, :] * w_ref[t - 1:t, :]
        acc = acc + b_ref[...]
        mu = jnp.mean(acc, axis=-1, keepdims=True)
        cen = acc - mu
        var = jnp.mean(cen * cen, axis=-1, keepdims=True)
        yn = cen * lax.rsqrt(var + EPS) * lg_ref[...] + lb_ref[...]
        o_ref[pl.ds(base, rows), :] = (yn * _sigmoid(yn)).astype(BF16)
        return carry

    lax.fori_loop(0, seq // rows, chunk, 0)


def _conv(y, p):
    B, S, _ = y.shape
    c2 = lambda b: (0, 0)
    return pl.pallas_call(
        functools.partial(_conv_kernel, seq=S),
        grid=(B,),
        in_specs=[pl.BlockSpec((None, S, CONV_CH), lambda b: (b, 0, 0)),
                  pl.BlockSpec((CONV_WIDTH, CONV_CH), c2),
                  pl.BlockSpec((1, CONV_CH), c2), pl.BlockSpec((1, CONV_CH), c2), pl.BlockSpec((1, CONV_CH), c2)],
        out_specs=pl.BlockSpec((None, S, CONV_CH), lambda b: (b, 0, 0)),
        out_shape=jax.ShapeDtypeStruct((B, S, CONV_CH), BF16),
        scratch_shapes=[pltpu.VMEM((S + 4 * SUBLANES, CONV_CH), F32)],
        compiler_params=_cparams(("arbitrary",), 8 * S * CONV_CH * 4 + (8 << 20)),
        name="conv",
    )(y, p["conv_w"], p["conv_b"], p["conv_ln_g"], p["conv_ln_b"])


def _fourier_kernel(cs_ref, zz_ref, o_ref, *, seq):
    acc = jnp.dot(cs_ref[:, 0:seq], zz_ref[:, 0:FOURIER_CH], preferred_element_type=F32)
    acc = acc + jnp.dot(cs_ref[:, seq:2 * seq], zz_ref[:, FOURIER_CH:2 * FOURIER_CH], preferred_element_type=F32)
    o_ref[...] = acc.astype(BF16)


def _fourier(zz, cs):
    B, S, _ = zz.shape
    tr = min(TR_FOURIER, S)
    return pl.pallas_call(
        functools.partial(_fourier_kernel, seq=S),
        grid=(S // tr, B),
        in_specs=[pl.BlockSpec((tr, 2 * S), lambda r, b: (r, 0)),
                  pl.BlockSpec((None, S, 2 * FOURIER_CH), lambda r, b: (b, 0, 0))],
        out_specs=pl.BlockSpec((None, tr, FOURIER_CH), lambda r, b: (b, r, 0)),
        out_shape=jax.ShapeDtypeStruct((B, S, FOURIER_CH), BF16),
        compiler_params=_cparams(("arbitrary", "arbitrary"), 2 * (tr * 2 * S * 2 + S * 512 * 2) + (8 << 20)),
        name="fourier",
    )(cs, zz)


def _dft_tables(seq):
    norm = seq ** -0.5
    if seq <= 256:
        ks = (np.arange(seq)[:, None] * np.arange(seq)[None, :]) % seq
        ang = 2.0 * np.pi * ks / seq
        tab = np.concatenate([np.cos(ang), -np.sin(ang)], axis=1) * norm
        return jnp.asarray(tab, F32).astype(BF16)
    inner = 64
    outer = seq // inner
    k = np.arange(seq)[:, None]
    ang_a = 2.0 * np.pi * ((k * np.arange(outer)[None, :]) % outer) / outer
    ang_b = 2.0 * np.pi * ((k * np.arange(inner)[None, :]) % seq) / seq
    ca, sa = jnp.asarray(np.cos(ang_a) * norm, F32)[:, :, None], jnp.asarray(np.sin(ang_a) * norm, F32)[:, :, None]
    cb, sb = jnp.asarray(np.cos(ang_b), F32)[:, None, :], jnp.asarray(np.sin(ang_b), F32)[:, None, :]
    cos = (ca * cb - sa * sb).reshape(seq, seq)
    sin = (sa * cb + ca * sb).reshape(seq, seq)
    return jnp.concatenate([cos, -sin], axis=1).astype(BF16)


def _channel_dft():
    n = FOURIER_GROUP_DIM
    ang = 2.0 * np.pi * ((np.arange(n)[:, None] * np.arange(n)[None, :]) % n) / n
    eye = np.eye(FOURIER_GROUPS)
    bd = np.concatenate([np.kron(eye, np.cos(ang)), np.kron(eye, np.sin(ang))], axis=1) * n ** -0.5
    return jnp.asarray(bd, F32).astype(BF16)


def _mixout_kernel(*refs, moe):
    if moe:
        (attn_ref, conv_ref, four_ref, x_ref, mod_ref, wout_ref, g_ref, rw_ref, tri_ref,
         x1_out, h_out, route_out, cnt_out) = refs
    else:
        attn_ref, conv_ref, four_ref, x_ref, mod_ref, wout_ref, g_ref, x1_out, h_out = refs
    mix = jnp.dot(attn_ref[...], wout_ref[0:ATTN_OUT, :], preferred_element_type=F32)
    mix = mix + jnp.dot(conv_ref[...], wout_ref[ATTN_OUT:ATTN_OUT + CONV_CH, :], preferred_element_type=F32)
    mix = mix + jnp.dot(four_ref[...], wout_ref[ATTN_OUT + CONV_CH:, :], preferred_element_type=F32)
    x1 = x_ref[...] + mod_ref[2:3, :] * mix
    x1_out[...] = x1
    h = _rms(x1, g_ref[...]) * (1.0 + mod_ref[4:5, :]) + mod_ref[3:4, :]
    if not moe:
        h_out[...] = h.astype(BF16)
        return
    h_out[...] = h
    tm = h.shape[0]
    logits = jnp.dot(h, rw_ref[...], preferred_element_type=F32, precision=lax.Precision.HIGHEST)
    lane = lax.broadcasted_iota(jnp.int32, (tm, LANES), 1).astype(F32)
    neg = jnp.float32(-jnp.inf)
    lm = jnp.where(lane < N_EXPERTS, logits, neg)
    m1 = jnp.max(lm, axis=-1, keepdims=True)
    i1 = jnp.min(jnp.where(lm == m1, lane, float(LANES)), axis=-1, keepdims=True)
    lm2 = jnp.where(lane == i1, neg, lm)
    m2 = jnp.max(lm2, axis=-1, keepdims=True)
    i2 = jnp.min(jnp.where(lm2 == m2, lane, float(LANES)), axis=-1, keepdims=True)
    t = jnp.exp(m2 - m1)
    w1 = 1.0 / (1.0 + t)
    w2 = t / (1.0 + t)
    oh1 = lane == i1
    oh2 = lane == i2
    tri = tri_ref[...]
    c1 = jnp.dot(tri, jnp.where(oh1, 1.0, 0.0).astype(BF16), preferred_element_type=F32)
    c2 = jnp.dot(tri, jnp.where(oh2, 1.0, 0.0).astype(BF16), preferred_element_type=F32)
    r1 = jnp.sum(jnp.where(oh1, c1, 0.0), axis=-1, keepdims=True)
    r2 = jnp.sum(jnp.where(oh2, c2, 0.0), axis=-1, keepdims=True)
    vals = [i1, i2, w1, w2, r1, r2]
    route = jnp.zeros((tm, LANES), F32)
    for idx, v in enumerate(vals):
        route = jnp.where(lane == idx, v, route)
    route_out[...] = route
    n1 = jnp.sum(jnp.where(oh1, 1.0, 0.0), axis=0, keepdims=True)
    n2 = jnp.sum(jnp.where(oh2, 1.0, 0.0), axis=0, keepdims=True)
    row = lax.broadcasted_iota(jnp.int32, (SUBLANES, LANES), 0)
    cnt_out[...] = jnp.where(row == 0, n1, jnp.where(row == 1, n2, 0.0))


def _mix_out(attn, conv, four, xs, mod, p, *, moe):
    B, S, D = xs.shape
    tm = min(TM_OUT, S)
    bm = mod.shape[0]
    mod_map = (lambda b, i: (b, 0, 0)) if bm > 1 else (lambda b, i: (0, 0, 0))
    c2 = lambda b, i: (0, 0)
    tok = lambda w: pl.BlockSpec((None, tm, w), lambda b, i: (b, i, 0))
    args = [attn, conv, four, xs, mod, p["w_out"], p["ffn_norm_g"]]
    specs = [tok(ATTN_OUT), tok(CONV_CH), tok(FOURIER_CH), tok(D), pl.BlockSpec((None, N_MOD, D), mod_map),
             pl.BlockSpec((D, D), c2), pl.BlockSpec((1, D), c2)]
    out_shapes = [jax.ShapeDtypeStruct((B, S, D), F32), jax.ShapeDtypeStruct((B, S, D), F32 if moe else BF16)]
    out_specs = [tok(D), tok(D)]
    if moe:
        tri = jnp.asarray(np.tril(np.ones((tm, tm), np.float32), -1), BF16)
        args += [p["router_w"], tri]
        specs += [pl.BlockSpec((D, LANES), c2), pl.BlockSpec((tm, tm), c2)]
        out_shapes += [jax.ShapeDtypeStruct((B, S, LANES), F32),
                       jax.ShapeDtypeStruct((B, S // tm, SUBLANES, LANES), F32)]
        out_specs += [tok(LANES), pl.BlockSpec((None, None, SUBLANES, LANES), lambda b, i: (b, i, 0, 0))]
    return pl.pallas_call(
        functools.partial(_mixout_kernel, moe=moe),
        grid=(B, S // tm),
        in_specs=specs,
        out_specs=out_specs,
        out_shape=out_shapes,
        compiler_params=_cparams(("arbitrary", "arbitrary"), 2 * D * D * 2 + 24 * tm * D * 4 + (8 << 20)),
        name="mix_out_moe" if moe else "mix_out",
    )(*args)


def _swiglu_tile(hb, wg_ref, wu_ref, wd_ref):
    acc = jnp.zeros((hb.shape[0], D_MODEL), F32)
    for c in range(D_FF // FF_CHUNK):
        sl = slice(c * FF_CHUNK, (c + 1) * FF_CHUNK)
        g = jnp.dot(hb, wg_ref[:, sl], preferred_element_type=F32)
        u = jnp.dot(hb, wu_ref[:, sl], preferred_element_type=F32)
        a = (g * _sigmoid(g) * u).astype(BF16)
        acc = acc + jnp.dot(a, wd_ref[sl, :], preferred_element_type=F32)
    return acc


def _ffn_kernel(h_ref, x1_ref, mod_ref, wg_ref, wu_ref, wd_ref, o_ref):
    acc = _swiglu_tile(h_ref[...], wg_ref, wu_ref, wd_ref)
    o_ref[...] = x1_ref[...] + mod_ref[5:6, :] * acc


def _ffn(h, x1, mod, p):
    B, S, D = x1.shape
    tm = min(TM_FFN, S)
    bm = mod.shape[0]
    mod_map = (lambda b, i: (b, 0, 0)) if bm > 1 else (lambda b, i: (0, 0, 0))
    c2 = lambda b, i: (0, 0)
    tok = pl.BlockSpec((None, tm, D), lambda b, i: (b, i, 0))
    vmem = 2 * 3 * D * D_FF * 2 + 12 * tm * D * 4 + (6 << 20)
    return pl.pallas_call(
        _ffn_kernel,
        grid=(B, S // tm),
        in_specs=[tok, tok, pl.BlockSpec((None, N_MOD, D), mod_map),
                  pl.BlockSpec((D, D_FF), c2), pl.BlockSpec((D, D_FF), c2), pl.BlockSpec((D_FF, D), c2)],
        out_specs=tok,
        out_shape=jax.ShapeDtypeStruct((B, S, D), F32),
        compiler_params=_cparams(("arbitrary", "arbitrary"), vmem),
        name="ffn",
    )(h, x1, mod, p["ffn_wg"], p["ffn_wu"], p["ffn_wd"])


def _row_copy(src_ref, src_row, dst_ref, dst_row, sem):
    return pltpu.make_async_copy(src_ref.at[pl.ds(src_row, 1), :], dst_ref.at[pl.ds(dst_row, 1), :], sem)


def _dispatch_kernel(slot_ref, pad_ref, h_ref, xs_ref, zero_ref, sem, *, tm, npad):
    zero_ref[...] = jnp.zeros(zero_ref.shape, F32)

    def start(r, carry):
        for k in range(2):
            _row_copy(h_ref, r, xs_ref, slot_ref[0, 0, k * tm + r], sem).start()
        return carry

    def wait(r, carry):
        for k in range(2):
            _row_copy(h_ref, r, xs_ref, slot_ref[0, 0, k * tm + r], sem).wait()
        return carry

    def start_pad(j, carry):
        s = pad_ref[0, 0, j]

        @pl.when(s >= 0)
        def _():
            _row_copy(zero_ref, 0, xs_ref, s, sem).start()
        return carry

    def wait_pad(j, carry):
        s = pad_ref[0, 0, j]

        @pl.when(s >= 0)
        def _():
            _row_copy(zero_ref, 0, xs_ref, s, sem).wait()
        return carry

    lax.fori_loop(0, tm, start, 0, unroll=8)
    lax.fori_loop(0, npad, start_pad, 0, unroll=8)
    lax.fori_loop(0, tm, wait, 0, unroll=8)
    lax.fori_loop(0, npad, wait_pad, 0, unroll=8)


def _dispatch(h2, slots, pad_slots, n_rows):
    n, D = h2.shape
    tm = TM_ROUTE
    npad = pad_slots.shape[-1]
    return pl.pallas_call(
        functools.partial(_dispatch_kernel, tm=tm, npad=npad),
        grid=(n // tm,),
        in_specs=[pl.BlockSpec((1, 1, 2 * tm), lambda i: (i, 0, 0), memory_space=pltpu.SMEM),
                  pl.BlockSpec((1, 1, npad), lambda i: (i, 0, 0), memory_space=pltpu.SMEM),
                  pl.BlockSpec((tm, D), lambda i: (i, 0))],
        out_specs=pl.BlockSpec(memory_space=pl.ANY),
        out_shape=jax.ShapeDtypeStruct((n_rows, D), F32),
        scratch_shapes=[pltpu.VMEM((SUBLANES, D), F32), pltpu.SemaphoreType.DMA(())],
        compiler_params=_cparams(("arbitrary",), 4 * tm * D * 4 + (4 << 20)),
        name="dispatch",
    )(slots, pad_slots, h2)


def _experts_kernel(te_ref, nt_ref, xs_ref, wg_ref, wu_ref, wd_ref, o_ref):
    t = pl.program_id(0)

    @pl.when(t < nt_ref[0])
    def _():
        o_ref[...] = _swiglu_tile(xs_ref[...].astype(BF16), wg_ref, wu_ref, wd_ref)

    @pl.when(t >= nt_ref[0])
    def _():
        o_ref[...] = jnp.zeros(o_ref.shape, F32)


def _experts(xs, tile_expert, n_tiles_used, p):
    n_slots, D = xs.shape
    tm = TM_MOE
    wmap_in = lambda t, te, nt: (te[t], 0, 0)
    tok = pl.BlockSpec((tm, D), lambda t, te, nt: (t, 0))
    tok_in = pl.BlockSpec((tm, D), lambda t, te, nt: (jnp.minimum(t, nt[0] - 1), 0))
    vmem = 2 * 3 * D * D_FF * 2 + 12 * tm * D * 4 + (6 << 20)
    return pl.pallas_call(
        _experts_kernel,
        grid_spec=pltpu.PrefetchScalarGridSpec(
            num_scalar_prefetch=2,
            grid=(n_slots // tm,),
            in_specs=[tok_in,
                      pl.BlockSpec((None, D, D_FF), wmap_in),
                      pl.BlockSpec((None, D, D_FF), wmap_in),
                      pl.BlockSpec((None, D_FF, D), wmap_in)],
            out_specs=tok),
        out_shape=jax.ShapeDtypeStruct((n_slots, D), F32),
        compiler_params=_cparams(("arbitrary",), vmem),
        name="experts",
    )(tile_expert, n_tiles_used, xs, p["moe_wg"], p["moe_wu"], p["moe_wd"])


def _combine_kernel(slot_ref, x1_ref, route_ref, mod_ref, ys_ref, o_ref, buf_ref, sem, *, tm):
    def start(r, carry):
        for k in range(2):
            _row_copy(ys_ref, slot_ref[0, 0, k * tm + r], buf_ref.at[k], r, sem).start()
        return carry

    def wait(r, carry):
        for k in range(2):
            _row_copy(ys_ref, slot_ref[0, 0, k * tm + r], buf_ref.at[k], r, sem).wait()
        return carry

    lax.fori_loop(0, tm, start, 0, unroll=8)
    lax.fori_loop(0, tm, wait, 0, unroll=8)
    route = route_ref[...]
    w1 = route[:, 2:3]
    w2 = route[:, 3:4]
    o_ref[...] = x1_ref[...] + mod_ref[5:6, :] * (w1 * buf_ref[0] + w2 * buf_ref[1])


def _combine(x1, route, mod, ys, slots):
    B, S, D = x1.shape
    tm = TM_ROUTE
    per = S // tm
    tok = lambda w: pl.BlockSpec((None, tm, w), lambda i: (i // per, i % per, 0))
    return pl.pallas_call(
        functools.partial(_combine_kernel, tm=tm),
        grid=(B * per,),
        in_specs=[pl.BlockSpec((1, 1, 2 * tm), lambda i: (i, 0, 0), memory_space=pltpu.SMEM),
                  tok(D), tok(LANES), pl.BlockSpec((None, N_MOD, D), lambda i: (i // per, 0, 0)),
                  pl.BlockSpec(memory_space=pl.ANY)],
        out_specs=tok(D),
        out_shape=jax.ShapeDtypeStruct((B, S, D), F32),
        scratch_shapes=[pltpu.VMEM((2, tm, D), F32), pltpu.SemaphoreType.DMA(())],
        compiler_params=_cparams(("arbitrary",), 10 * tm * D * 4 + (4 << 20)),
        name="combine",
    )(slots, x1, route, mod, ys)


def _moe(h, x1, route, cnt, mod, p):
    B, S, D = x1.shape
    n = B * S
    tmm = TM_MOE
    n_tiles = 2 * n // tmm + N_EXPERTS
    n_slots = n_tiles * tmm
    r2 = route.reshape(n, LANES)
    e = r2[:, 0:2].astype(jnp.int32)
    rank = r2[:, 4:6].astype(jnp.int32)
    counts = cnt[:, :, 0:2, 0:N_EXPERTS].astype(jnp.int32).reshape(-1, 2, N_EXPERTS)
    tile_tot = counts.sum(axis=0)
    n_e = tile_tot.sum(axis=0)
    base = jnp.cumsum(counts, axis=0) - counts
    base = base + jnp.array([0, 1], jnp.int32)[None, :, None] * tile_tot[0][None, None, :]
    tiles_e = (n_e + tmm - 1) // tmm
    pstart = (jnp.cumsum(tiles_e) - tiles_e) * tmm
    tok_tile = jnp.arange(n, dtype=jnp.int32) // TM_OUT
    onehot = e[:, :, None] == jnp.arange(N_EXPERTS, dtype=jnp.int32)[None, None, :]
    per_tok = base[tok_tile] + pstart[None, None, :]
    slot = jnp.sum(jnp.where(onehot, per_tok, 0), axis=-1) + rank
    slots = slot.reshape(n // TM_ROUTE, TM_ROUTE, 2).transpose(0, 2, 1).reshape(n // TM_ROUTE, 1, 2 * TM_ROUTE)
    tile_end = jnp.cumsum(tiles_e)
    n_used = tile_end[-1]
    t_idx = jnp.minimum(jnp.arange(n_tiles, dtype=jnp.int32), n_used - 1)
    tile_expert = jnp.sum(t_idx[:, None] >= tile_end[None, :], axis=-1).astype(jnp.int32)
    tile_expert = jnp.minimum(tile_expert, N_EXPERTS - 1)

    pe = jnp.arange(N_EXPERTS * tmm, dtype=jnp.int32) // tmm
    pq = jnp.arange(N_EXPERTS * tmm, dtype=jnp.int32) % tmm
    n_pad_e = tiles_e * tmm - n_e
    pad_slots = jnp.where(pq < n_pad_e[pe], pstart[pe] + n_e[pe] + pq, -1).astype(jnp.int32)
    steps = n // TM_ROUTE
    assert (N_EXPERTS * tmm) % steps == 0
    pad_slots = pad_slots.reshape(steps, 1, (N_EXPERTS * tmm) // steps)

    xs = _dispatch(h.reshape(n, D), slots, pad_slots, n_slots)
    ys = _experts(xs, tile_expert, n_used.reshape(1).astype(jnp.int32), p)
    return _combine(x1, route, mod, ys, slots)


def _pad_heads(w, width):
    lead = w.shape[:-1]
    w = w.reshape(lead + (N_HEADS, width))
    w = jnp.pad(w, [(0, 0)] * len(lead) + [(0, 0), (0, HEAD_PAD - width)])
    return w.reshape(lead + (N_HEADS * HEAD_PAD,))


def _layer_params(l, w_in, q_lat_g, kv_lat_g, w_uq, w_ukv, q_norm_g, k_norm_g, conv_w, conv_b, conv_ln_g,
                  conv_ln_b, w_out, ffn_norm_g):
    wi = w_in[l]
    kr = jnp.pad(wi[:, OFF_KR:OFF_CONV], ((0, 0), (QK_NOPE, HEAD_PAD - QK_HEAD)))
    ckv = wi[:, OFF_CKV:OFF_KR]
    w_full = jnp.concatenate([wi[:, OFF_CQ:OFF_CKV], ckv, wi[:, OFF_CONV:OFF_FOUR], wi[:, OFF_FOUR:IN_COLS], kr],
                             axis=1)
    w_kv = jnp.concatenate([ckv, kr], axis=1)
    ukv = w_ukv[l].reshape(KV_LORA, N_HEADS, QK_NOPE + V_HEAD)
    uk = _pad_heads(ukv[:, :, :QK_NOPE].reshape(KV_LORA, N_HEADS * QK_NOPE), QK_NOPE)
    uv = ukv[:, :, QK_NOPE:].reshape(KV_LORA, ATTN_OUT)
    pad_g = lambda g: jnp.pad(g, (0, HEAD_PAD - QK_HEAD)).reshape(1, HEAD_PAD)
    return {
        "w_in_full": w_full.astype(BF16),
        "w_in_kv": w_kv.astype(BF16),
        "q_lat_g": q_lat_g[l].reshape(1, Q_LORA),
        "kv_lat_g": kv_lat_g[l].reshape(1, KV_LORA),
        "w_uq": _pad_heads(w_uq[l], QK_HEAD).astype(BF16),
        "w_ukv": jnp.concatenate([uk, uv], axis=1).astype(BF16),
        "q_norm_g": pad_g(q_norm_g[l]) * (QK_HEAD ** -0.5),
        "k_norm_g": pad_g(k_norm_g[l]),
        "conv_w": conv_w[l],
        "conv_b": conv_b[l].reshape(1, CONV_CH),
        "conv_ln_g": conv_ln_g[l].reshape(1, CONV_CH),
        "conv_ln_b": conv_ln_b[l].reshape(1, CONV_CH),
        "w_out": w_out[l].astype(BF16),
        "ffn_norm_g": ffn_norm_g[l].reshape(1, D_MODEL),
        "bd": _channel_dft(),
    }


def _rope_tables(seq):
    rows = seq // GRID_W
    row = jnp.repeat(jnp.arange(rows, dtype=F32), GRID_W)
    col = jnp.tile(jnp.arange(GRID_W, dtype=F32), rows)
    n_freq = QK_ROPE // 4
    inv = ROPE_BASE ** (-jnp.arange(n_freq, dtype=F32) / n_freq)
    ang = jnp.concatenate([row[:, None] * inv, col[:, None] * inv], axis=-1)
    cos, sin = jnp.cos(ang), jnp.sin(ang)
    half = QK_ROPE // 2
    ones = jnp.ones((seq, QK_NOPE), F32)
    zeros = lambda w: jnp.zeros((seq, w), F32)
    tail = HEAD_PAD - QK_HEAD
    cos_t = jnp.concatenate([ones, cos, cos, zeros(tail)], axis=1)
    sa_t = jnp.concatenate([zeros(QK_NOPE + half), sin, zeros(tail)], axis=1)
    sb_t = jnp.concatenate([zeros(QK_NOPE), -sin, zeros(half + tail)], axis=1)
    return cos_t, sa_t, sb_t


def _mixer(xs, mod, p, rope_tabs, k_extra, v_extra, cs):
    q, k, v, y, zz = _front(xs, mod, p["mix_norm_g"], p, rope_tabs, full=True)
    ks, vs = [k], [v]
    if k_extra is not None:
        ks.append(k_extra)
        vs.append(v_extra)
    attn = _attend(q, ks, vs)
    conv = _conv(y, p)
    four = _fourier(zz, cs)
    return attn, conv, four, k, v


def kernel(x, c, ctx, c_ctx, ada_w, ada_b, mix_norm_g, ffn_norm_g, w_in, q_lat_g, kv_lat_g, w_uq, w_ukv, q_norm_g,
           k_norm_g, conv_w, conv_b, conv_ln_g, conv_ln_b, w_out, ffn_w_gate, ffn_w_up, ffn_w_down, router_w,
           moe_w_gate, moe_w_up, moe_w_down):
    B, S, D = x.shape
    T = ctx.shape[1]
    assert (D, DEPTH) == (D_MODEL, ada_w.shape[0]) and S % GRID_W == 0

    cc = jnp.zeros((2 * SUBLANES, D), F32).at[:B].set(c).at[B].set(c_ctx)
    mods = _modulation(cc, ada_w, ada_b).reshape(DEPTH, 2 * SUBLANES, N_MOD, D)
    rope_tabs = _rope_tables(S)
    cs_x = _dft_tables(S)
    cs_c = _dft_tables(T)

    for l in range(DEPTH):
        last = l == DEPTH - 1
        p = _layer_params(l, w_in, q_lat_g, kv_lat_g, w_uq, w_ukv, q_norm_g, k_norm_g, conv_w, conv_b, conv_ln_g,
                          conv_ln_b, w_out, ffn_norm_g)
        p["mix_norm_g"] = mix_norm_g[l].reshape(1, D)
        i = l // 2
        moe = l % 2 == 1
        if moe:
            p["router_w"] = jnp.pad(router_w[i], ((0, 0), (0, LANES - N_EXPERTS)))
            p["moe_wg"] = moe_w_gate[i].astype(BF16)
            p["moe_wu"] = moe_w_up[i].astype(BF16)
            p["moe_wd"] = moe_w_down[i].astype(BF16)
        else:
            p["ffn_wg"] = ffn_w_gate[i].astype(BF16)
            p["ffn_wu"] = ffn_w_up[i].astype(BF16)
            p["ffn_wd"] = ffn_w_down[i].astype(BF16)
        mod_x = mods[l, :B]
        mod_c = mods[l, B:B + 1]

        def channel_mixer(attn, conv, four, xs, mod):
            if moe:
                x1, h, route, cnt = _mix_out(attn, conv, four, xs, mod, p, moe=True)
                return _moe(h, x1, route, cnt, mod if mod.shape[0] > 1 else jnp.broadcast_to(mod, (B,) + mod.shape[1:]), p)
            x1, h = _mix_out(attn, conv, four, xs, mod, p, moe=False)
            return _ffn(h, x1, mod, p)

        if last:
            k_c, v_c = _front(ctx, mod_c, p["mix_norm_g"], p, None, full=False)
        else:
            attn_c, conv_c, four_c, k_c, v_c = _mixer(ctx, mod_c, p, None, None, None, cs_c)
            ctx_next = channel_mixer(attn_c, conv_c, four_c, ctx, mod_c)

        attn_x, conv_x, four_x, _, _ = _mixer(x, mod_x, p, rope_tabs, k_c, v_c, cs_x)
        x = channel_mixer(attn_x, conv_x, four_x, x, mod_x)
        if not last:
            ctx = ctx_next
    return x
```

```python
import functools
import math

import numpy as np
import jax
import jax.numpy as jnp
from jax import lax
from jax.experimental import pallas as pl
from jax.experimental.pallas import tpu as pltpu

F32 = jnp.float32
BF16 = jnp.bfloat16

D_MODEL = 1024
DEPTH = 2
GRID_W = 64
N_HEADS = 8
QK_NOPE = 64
QK_ROPE = 32
QK_HEAD = QK_NOPE + QK_ROPE
V_HEAD = 64
Q_LORA = 384
KV_LORA = 256
ROPE_BASE = 10000.0
CONV_CH = 256
CONV_WIDTH = 31
CONV_PAD = (CONV_WIDTH - 1) // 2
FOURIER_GROUPS = 4
FOURIER_GROUP_DIM = 64
FOURIER_CH = FOURIER_GROUPS * FOURIER_GROUP_DIM
ATTN_OUT = N_HEADS * V_HEAD
OFF_CQ = 0
OFF_CKV = OFF_CQ + Q_LORA
OFF_KR = OFF_CKV + KV_LORA
OFF_CONV = OFF_KR + QK_ROPE
OFF_FOUR = OFF_CONV + 2 * CONV_CH
IN_COLS = OFF_FOUR + FOURIER_CH
D_FF = 2816
N_EXPERTS = 8
N_MOD = 6
EPS = 1e-6

LANES = 128
SUBLANES = 8
HEAD_PAD = LANES
VMEM_CAP = 56 * 1024 * 1024
FF_CHUNK = 256

SUB_ROWS = 256
TM_FRONT = 512
TQ_ATTN = 512
TK_ATTN = 256
TM_OUT = 512
TM_FFN = 512
TM_MOE = 512
MOE_FF_PARTS = 2
TM_ROUTE = 256
CONV_ROWS = 128
TR_FOURIER = 512


def _cparams(sem, vmem_bytes):
    return pltpu.CompilerParams(dimension_semantics=sem, vmem_limit_bytes=int(min(VMEM_CAP, vmem_bytes)))


def _rms(v, g):
    return v * lax.rsqrt(jnp.mean(v * v, axis=-1, keepdims=True) + EPS) * g


def _sigmoid(v):
    return 1.0 / (1.0 + jnp.exp(-v))


def _mod_kernel(c_ref, w_ref, b_ref, o_ref):
    c = c_ref[...]
    s = (c * _sigmoid(c)).astype(BF16)
    o_ref[...] = jnp.dot(s, w_ref[...].astype(BF16), preferred_element_type=F32) + b_ref[...]


def _modulation(cc, ada_w, ada_b):
    rows = cc.shape[0]
    tn = 1536
    n_out = N_MOD * D_MODEL
    return pl.pallas_call(
        _mod_kernel,
        grid=(DEPTH, n_out // tn),
        in_specs=[
            pl.BlockSpec((rows, D_MODEL), lambda l, j: (0, 0)),
            pl.BlockSpec((None, D_MODEL, tn), lambda l, j: (l, 0, j)),
            pl.BlockSpec((None, 1, tn), lambda l, j: (l, 0, j)),
        ],
        out_specs=pl.BlockSpec((None, rows, tn), lambda l, j: (l, 0, j)),
        out_shape=jax.ShapeDtypeStruct((DEPTH, rows, n_out), F32),
        compiler_params=_cparams(("arbitrary", "arbitrary"), 4 * D_MODEL * tn * 4),
        name="modulation",
    )(cc, ada_w, ada_b.reshape(DEPTH, 1, n_out))


def _front_kernel(*refs, full, rope):
    it = iter(refs)
    x_ref, mod_ref, g_ref, win_ref = next(it), next(it), next(it), next(it)
    if full:
        qlg_ref, wuq_ref = next(it), next(it)
    kvlg_ref, wukv_ref, vones_ref, onesh_ref, expand_ref = next(it), next(it), next(it), next(it), next(it)
    if rope:
        cq_ref, sq_ref, ck_ref, sk_ref = next(it), next(it), next(it), next(it)
    else:
        qg_ref = next(it) if full else None
        kg_ref = next(it)
    if full:
        bd_ref = next(it)
        q_out = next(it)
    k_out, v_out = next(it), next(it)
    if full:
        y_out, zz_out = next(it), next(it)

    shift = mod_ref[0:1, :]
    scale = mod_ref[1:2, :]
    hw = N_HEADS * HEAD_PAD

    def head_inv_rms(raw):
        ss = jnp.dot((raw * raw).astype(BF16), onesh_ref[...], preferred_element_type=F32)
        rs = lax.rsqrt(ss * (1.0 / QK_HEAD) + EPS)
        rs_hi = rs.astype(BF16)
        rs_lo = (rs - rs_hi.astype(F32)).astype(BF16)
        return jnp.dot(jnp.concatenate([rs_hi, rs_lo], axis=1), expand_ref[...], preferred_element_type=F32)

    def sub_block(rows):
        x = x_ref[rows, :]
        h = _rms(x, g_ref[...]) * (1.0 + scale) + shift
        cols = jnp.dot(h.astype(BF16), win_ref[...], preferred_element_type=F32)

        o = 0
        if full:
            cq = cols[:, 0:Q_LORA]
            o = Q_LORA
            qall = jnp.dot(_rms(cq, qlg_ref[...]).astype(BF16), wuq_ref[...], preferred_element_type=F32)
            rsb = head_inv_rms(qall[:, 0:hw])
            for hd in range(N_HEADS):
                sl = slice(hd * HEAD_PAD, (hd + 1) * HEAD_PAD)
                if rope:
                    val = qall[:, sl] * cq_ref[rows, :] + qall[:, hw + hd * HEAD_PAD:hw + (hd + 1) * HEAD_PAD] * \
                        sq_ref[rows, :]
                else:
                    val = qall[:, sl] * qg_ref[...]
                q_out[rows, sl] = (val * rsb[:, sl]).astype(BF16)

        ckv = cols[:, o:o + KV_LORA]
        o += KV_LORA
        kv = jnp.dot(_rms(ckv, kvlg_ref[...]).astype(BF16), wukv_ref[...], preferred_element_type=F32)
        if full:
            a = cols[:, o:o + CONV_CH]
            gt = cols[:, o + CONV_CH:o + 2 * CONV_CH]
            y_out[rows, :] = a * _sigmoid(gt)
            o += 2 * CONV_CH
            z = cols[:, o:o + FOURIER_CH]
            o += FOURIER_CH
            zz_out[rows, :] = jnp.dot(z.astype(BF16), bd_ref[...], preferred_element_type=F32).astype(BF16)
        krb = cols[:, o:o + HEAD_PAD]
        kraw = [kv[:, hd * HEAD_PAD:(hd + 1) * HEAD_PAD] + krb for hd in range(N_HEADS)]
        rsb = head_inv_rms(jnp.concatenate(kraw, axis=1))
        if rope:
            kpart = cols[:, o + HEAD_PAD:o + 2 * HEAD_PAD] * sk_ref[rows, :]
        for hd in range(N_HEADS):
            sl = slice(hd * HEAD_PAD, (hd + 1) * HEAD_PAD)
            val = kraw[hd] * ck_ref[rows, :] + kpart if rope else kraw[hd] * kg_ref[...]
            k_out[rows, sl] = (val * rsb[:, sl]).astype(BF16)
        v_out[rows, :] = (kv[:, hw:] + vones_ref[...]).astype(BF16)

    tm = x_ref.shape[0]
    sub = min(SUB_ROWS, tm)
    for sb in range(tm // sub):
        sub_block(slice(sb * sub, (sb + 1) * sub))


def _front(xs, mod, norm_g, p, rope_tabs, *, full):
    B, S, D = xs.shape
    tm = min(TM_FRONT, S)
    rope = rope_tabs is not None
    assert full or not rope
    w_in = (p["w_in_rope"] if rope else p["w_in_full"]) if full else p["w_in_kv"]
    w_uq = p["w_uq_rope"] if rope else p["w_uq"]
    ncol = w_in.shape[1]
    bm = mod.shape[0]
    mod_map = (lambda b, i: (b, 0, 0)) if bm > 1 else (lambda b, i: (0, 0, 0))
    const2 = lambda b, i: (0, 0)
    hw = N_HEADS * HEAD_PAD

    args = [xs, mod, norm_g, w_in]
    specs = [
        pl.BlockSpec((None, tm, D), lambda b, i: (b, i, 0)),
        pl.BlockSpec((None, N_MOD, D), mod_map),
        pl.BlockSpec((1, D), const2),
        pl.BlockSpec((D, ncol), const2),
    ]
    if full:
        args += [p["q_lat_g"], w_uq]
        specs += [pl.BlockSpec((1, Q_LORA), const2), pl.BlockSpec((Q_LORA, w_uq.shape[1]), const2)]
    args += [p["kv_lat_g"], p["w_ukv"], p["v_ones"], p["ones_h"], p["expand_h"]]
    specs += [pl.BlockSpec((1, KV_LORA), const2),
              pl.BlockSpec((KV_LORA, 2 * hw), const2),
              pl.BlockSpec((1, hw), const2),
              pl.BlockSpec((hw, LANES), const2),
              pl.BlockSpec((2 * LANES, hw), const2)]
    if rope:
        args += list(rope_tabs)
        specs += [pl.BlockSpec((tm, HEAD_PAD), lambda b, i: (i, 0))] * 4
    else:
        if full:
            args += [p["q_norm_g"]]
            specs += [pl.BlockSpec((1, HEAD_PAD), const2)]
        args += [p["k_norm_g"]]
        specs += [pl.BlockSpec((1, HEAD_PAD), const2)]
    if full:
        args += [p["bd"]]
        specs += [pl.BlockSpec((FOURIER_CH, 2 * FOURIER_CH), const2)]

    out_shapes, out_specs = [], []

    def add_out(width, dtype):
        out_shapes.append(jax.ShapeDtypeStruct((B, S, width), dtype))
        out_specs.append(pl.BlockSpec((None, tm, width), lambda b, i: (b, i, 0)))

    if full:
        add_out(N_HEADS * HEAD_PAD, BF16)
    add_out(N_HEADS * HEAD_PAD, BF16)
    add_out(N_HEADS * HEAD_PAD, BF16)
    if full:
        add_out(CONV_CH, F32)
        add_out(2 * FOURIER_CH, BF16)

    vmem = 2 * (D * ncol * 2 + Q_LORA * 1024 * 2 + KV_LORA * 2048 * 2) + 28 * tm * D * 4
    return pl.pallas_call(
        functools.partial(_front_kernel, full=full, rope=rope),
        grid=(B, S // tm),
        in_specs=specs,
        out_specs=out_specs,
        out_shape=out_shapes,
        compiler_params=_cparams(("arbitrary", "arbitrary"), vmem),
        name="front_full" if full else "front_kv",
    )(*args)


def _attn_kernel(*refs, nsrc):
    q_ref = refs[0]
    k_refs = refs[1:1 + nsrc]
    v_refs = refs[1 + nsrc:1 + 2 * nsrc]
    o_ref = refs[-1]
    chunks = []
    for k_ref, v_ref in zip(k_refs, v_refs):
        for s0 in range(0, k_ref.shape[0], TK_ATTN):
            chunks.append((k_ref, v_ref, s0, min(TK_ATTN, k_ref.shape[0] - s0)))
    state = [None, None]
    for k_ref, v_ref, s0, sz in chunks:
        for hh in range(2):
            sl = slice(hh * HEAD_PAD, (hh + 1) * HEAD_PAD)
            s = lax.dot_general(q_ref[:, sl], k_ref[s0:s0 + sz, sl], (((1,), (1,)), ((), ())),
                                preferred_element_type=F32)
            m = jnp.max(s, axis=-1, keepdims=True)
            if state[hh] is not None:
                m_old, acc_old = state[hh]
                m = jnp.maximum(m_old, m)
            pv = jnp.dot(jnp.exp2((s - m).astype(BF16)), v_ref[s0:s0 + sz, sl], preferred_element_type=F32)
            if state[hh] is not None:
                pv = pv + jnp.exp2(m_old - m) * acc_old
            state[hh] = (m, pv)
    accs = [state[0][1], state[1][1]]
    lane = lax.broadcasted_iota(jnp.int32, accs[0].shape, 1)
    lo = accs[0] / pltpu.roll(accs[0], V_HEAD, 1)
    hi = pltpu.roll(accs[1], V_HEAD, 1) / accs[1]
    o_ref[...] = jnp.where(lane < V_HEAD, lo, hi).astype(BF16)


def _attend(q, ks, vs):
    B, S, _ = q.shape
    tq = min(TQ_ATTN, S)
    nsrc = len(ks)
    specs = [pl.BlockSpec((None, tq, 2 * HEAD_PAD), lambda b, hp, i: (b, i, hp))]
    for kv in list(ks) + list(vs):
        specs.append(pl.BlockSpec((None, kv.shape[1], 2 * HEAD_PAD), lambda b, hp, i: (b, 0, hp)))
    t_all = sum(k.shape[1] for k in ks)
    vmem = 2 * (t_all * 512 * 2) * 2 + 8 * tq * t_all * 4 + (8 << 20)
    return pl.pallas_call(
        functools.partial(_attn_kernel, nsrc=nsrc),
        grid=(B, N_HEADS // 2, S // tq),
        in_specs=specs,
        out_specs=pl.BlockSpec((None, tq, 2 * V_HEAD), lambda b, hp, i: (b, i, hp)),
        out_shape=jax.ShapeDtypeStruct((B, S, ATTN_OUT), BF16),
        compiler_params=_cparams(("arbitrary", "arbitrary", "arbitrary"), vmem),
        name="attend%d" % nsrc,
    )(q, *ks, *vs)


def _conv_kernel(y_ref, w_ref, b_ref, lg_ref, lb_ref, o_ref, pad_ref, *, seq):
    halo = 2 * SUBLANES
    pad_ref[0:halo, :] = jnp.zeros((halo, CONV_CH), F32)
    pad_ref[halo + seq:2 * halo + seq, :] = jnp.zeros((halo, CONV_CH), F32)
    pad_ref[halo:halo + seq, :] = y_ref[...]
    rows = CONV_ROWS
    span = rows + 3 * SUBLANES

    def chunk(r, carry):
        base = pl.multiple_of(r * rows, rows)
        win = pad_ref[pl.ds(base, rows + 2 * halo), :]
        acc = jnp.zeros((rows, CONV_CH), F32)
        for ph in range(SUBLANES):
            offs = [t for t in range(1, CONV_WIDTH + 1) if t % SUBLANES == ph]
            if not offs:
                continue
            shifted = win[ph:ph + span, :]
            for t in offs:
                a0 = t - ph
                acc = acc + shifted[a0:a0 + rows, :] * w_ref[t - 1:t, :]
        acc = acc + b_ref[...]
        mu = jnp.mean(acc, axis=-1, keepdims=True)
        cen = acc - mu
        var = jnp.mean(cen * cen, axis=-1, keepdims=True)
        yn = cen * lax.rsqrt(var + EPS) * lg_ref[...] + lb_ref[...]
        o_ref[pl.ds(base, rows), :] = (yn * _sigmoid(yn)).astype(BF16)
        return carry

    lax.fori_loop(0, seq // rows, chunk, 0)


def _conv(y, p):
    B, S, _ = y.shape
    c2 = lambda b: (0, 0)
    return pl.pallas_call(
        functools.partial(_conv_kernel, seq=S),
        grid=(B,),
        in_specs=[pl.BlockSpec((None, S, CONV_CH), lambda b: (b, 0, 0)),
                  pl.BlockSpec((CONV_WIDTH, CONV_CH), c2),
                  pl.BlockSpec((1, CONV_CH), c2), pl.BlockSpec((1, CONV_CH), c2), pl.BlockSpec((1, CONV_CH), c2)],
        out_specs=pl.BlockSpec((None, S, CONV_CH), lambda b: (b, 0, 0)),
        out_shape=jax.ShapeDtypeStruct((B, S, CONV_CH), BF16),
        scratch_shapes=[pltpu.VMEM((S + 4 * SUBLANES, CONV_CH), F32)],
        compiler_params=_cparams(("arbitrary",), 8 * S * CONV_CH * 4 + (8 << 20)),
        name="conv",
    )(y, p["conv_w"], p["conv_b"], p["conv_ln_g"], p["conv_ln_b"])


def _fourier_kernel(cs_ref, zz_ref, o_ref, *, seq):
    acc = jnp.dot(cs_ref[:, 0:seq], zz_ref[:, 0:FOURIER_CH], preferred_element_type=F32)
    acc = acc + jnp.dot(cs_ref[:, seq:2 * seq], zz_ref[:, FOURIER_CH:2 * FOURIER_CH], preferred_element_type=F32)
    o_ref[...] = acc.astype(BF16)


def _fourier(zz, cs):
    B, S, _ = zz.shape
    tr = min(TR_FOURIER, S)
    return pl.pallas_call(
        functools.partial(_fourier_kernel, seq=S),
        grid=(S // tr, B),
        in_specs=[pl.BlockSpec((tr, 2 * S), lambda r, b: (r, 0)),
                  pl.BlockSpec((None, S, 2 * FOURIER_CH), lambda r, b: (b, 0, 0))],
        out_specs=pl.BlockSpec((None, tr, FOURIER_CH), lambda r, b: (b, r, 0)),
        out_shape=jax.ShapeDtypeStruct((B, S, FOURIER_CH), BF16),
        compiler_params=_cparams(("arbitrary", "arbitrary"), 2 * (tr * 2 * S * 2 + S * 512 * 2) + (8 << 20)),
        name="fourier",
    )(cs, zz)


def _dft_tables(seq):
    norm = seq ** -0.5
    if seq <= 256:
        ks = (np.arange(seq)[:, None] * np.arange(seq)[None, :]) % seq
        ang = 2.0 * np.pi * ks / seq
        tab = np.concatenate([np.cos(ang), -np.sin(ang)], axis=1) * norm
        return jnp.asarray(tab, F32).astype(BF16)
    inner = 64
    outer = seq // inner
    k = np.arange(seq)[:, None]
    ang_a = 2.0 * np.pi * ((k * np.arange(outer)[None, :]) % outer) / outer
    ang_b = 2.0 * np.pi * ((k * np.arange(inner)[None, :]) % seq) / seq
    ca, sa = jnp.asarray(np.cos(ang_a) * norm, F32)[:, :, None], jnp.asarray(np.sin(ang_a) * norm, F32)[:, :, None]
    cb, sb = jnp.asarray(np.cos(ang_b), F32)[:, None, :], jnp.asarray(np.sin(ang_b), F32)[:, None, :]
    cos = (ca * cb - sa * sb).reshape(seq, seq)
    sin = (sa * cb + ca * sb).reshape(seq, seq)
    return jnp.concatenate([cos, -sin], axis=1).astype(BF16)


def _channel_dft():
    n = FOURIER_GROUP_DIM
    ang = 2.0 * np.pi * ((np.arange(n)[:, None] * np.arange(n)[None, :]) % n) / n
    eye = np.eye(FOURIER_GROUPS)
    bd = np.concatenate([np.kron(eye, np.cos(ang)), np.kron(eye, np.sin(ang))], axis=1) * n ** -0.5
    return jnp.asarray(bd, F32).astype(BF16)


def _mixout_kernel(*refs, moe):
    if moe:
        (attn_ref, conv_ref, four_ref, x_ref, mod_ref, wout_ref, g_ref, rw_ref, tri_ref,
         x1_out, h_out, route_out, cnt_out) = refs
    else:
        attn_ref, conv_ref, four_ref, x_ref, mod_ref, wout_ref, g_ref, x1_out, h_out = refs
    tm = x_ref.shape[0]
    sub = min(SUB_ROWS, tm)
    for sb in range(tm // sub):
        rows = slice(sb * sub, (sb + 1) * sub)
        mix = jnp.dot(attn_ref[rows, :], wout_ref[0:ATTN_OUT, :], preferred_element_type=F32)
        mix = mix + jnp.dot(conv_ref[rows, :], wout_ref[ATTN_OUT:ATTN_OUT + CONV_CH, :],
                            preferred_element_type=F32)
        mix = mix + jnp.dot(four_ref[rows, :], wout_ref[ATTN_OUT + CONV_CH:, :], preferred_element_type=F32)
        x1 = x_ref[rows, :] + mod_ref[2:3, :] * mix
        x1_out[rows, :] = x1
        h = _rms(x1, g_ref[...]) * (1.0 + mod_ref[4:5, :]) + mod_ref[3:4, :]
        if moe:
            h_out[rows, :] = h
            route, cnt = _route(h, rw_ref, tri_ref)
            route_out[rows, :] = route
            cnt_out[sb] = cnt
        else:
            h_out[rows, :] = h.astype(BF16)


def _route(h, rw_ref, tri_ref):
    tm = h.shape[0]
    h_hi = h.astype(BF16)
    h_lo = (h - h_hi.astype(F32)).astype(BF16)
    hw = jnp.dot(h_hi, rw_ref[...], preferred_element_type=F32)
    logits = hw[:, 0:LANES] + hw[:, LANES:] + jnp.dot(h_lo, rw_ref[:, 0:LANES], preferred_element_type=F32)
    lane = lax.broadcasted_iota(jnp.int32, (tm, LANES), 1).astype(F32)
    neg = jnp.float32(-jnp.inf)
    lm = jnp.where(lane < N_EXPERTS, logits, neg)
    m1 = jnp.max(lm, axis=-1, keepdims=True)
    i1 = jnp.min(jnp.where(lm == m1, lane, float(LANES)), axis=-1, keepdims=True)
    lm2 = jnp.where(lane == i1, neg, lm)
    m2 = jnp.max(lm2, axis=-1, keepdims=True)
    i2 = jnp.min(jnp.where(lm2 == m2, lane, float(LANES)), axis=-1, keepdims=True)
    t = jnp.exp(m2 - m1)
    w1 = 1.0 / (1.0 + t)
    w2 = t / (1.0 + t)
    oh1 = lane == i1
    oh2 = lane == i2
    tri = tri_ref[...]
    c1 = jnp.dot(tri, jnp.where(oh1, 1.0, 0.0).astype(BF16), preferred_element_type=F32)
    c2 = jnp.dot(tri, jnp.where(oh2, 1.0, 0.0).astype(BF16), preferred_element_type=F32)
    r1 = jnp.sum(jnp.where(oh1, c1, 0.0), axis=-1, keepdims=True)
    r2 = jnp.sum(jnp.where(oh2, c2, 0.0), axis=-1, keepdims=True)
    vals = [i1, i2, w1, w2, r1, r2]
    route = jnp.zeros((tm, LANES), F32)
    for idx, v in enumerate(vals):
        route = jnp.where(lane == idx, v, route)
    n1 = jnp.sum(jnp.where(oh1, 1.0, 0.0), axis=0, keepdims=True)
    n2 = jnp.sum(jnp.where(oh2, 1.0, 0.0), axis=0, keepdims=True)
    row = lax.broadcasted_iota(jnp.int32, (SUBLANES, LANES), 0)
    return route, jnp.where(row == 0, n1, jnp.where(row == 1, n2, 0.0))


def _mix_out(attn, conv, four, xs, mod, p, *, moe):
    B, S, D = xs.shape
    tm = min(TM_OUT, S)
    sub = min(SUB_ROWS, tm)
    bm = mod.shape[0]
    mod_map = (lambda b, i: (b, 0, 0)) if bm > 1 else (lambda b, i: (0, 0, 0))
    c2 = lambda b, i: (0, 0)
    tok = lambda w: pl.BlockSpec((None, tm, w), lambda b, i: (b, i, 0))
    args = [attn, conv, four, xs, mod, p["w_out"], p["ffn_norm_g"]]
    specs = [tok(ATTN_OUT), tok(CONV_CH), tok(FOURIER_CH), tok(D), pl.BlockSpec((None, N_MOD, D), mod_map),
             pl.BlockSpec((D, D), c2), pl.BlockSpec((1, D), c2)]
    out_shapes = [jax.ShapeDtypeStruct((B, S, D), F32), jax.ShapeDtypeStruct((B, S, D), F32 if moe else BF16)]
    out_specs = [tok(D), tok(D)]
    if moe:
        tri = jnp.asarray(np.tril(np.ones((sub, sub), np.float32), -1), BF16)
        args += [p["router_w"], tri]
        specs += [pl.BlockSpec((D, 2 * LANES), c2), pl.BlockSpec((sub, sub), c2)]
        out_shapes += [jax.ShapeDtypeStruct((B, S, LANES), F32),
                       jax.ShapeDtypeStruct((B, S // sub, SUBLANES, LANES), F32)]
        out_specs += [tok(LANES), pl.BlockSpec((None, tm // sub, SUBLANES, LANES), lambda b, i: (b, i, 0, 0))]
    return pl.pallas_call(
        functools.partial(_mixout_kernel, moe=moe),
        grid=(B, S // tm),
        in_specs=specs,
        out_specs=out_specs,
        out_shape=out_shapes,
        compiler_params=_cparams(("arbitrary", "arbitrary"), 2 * D * D * 2 + 24 * tm * D * 4 + (8 << 20)),
        name="mix_out_moe" if moe else "mix_out",
    )(*args)


def _swiglu_tile(hb, wg_ref, wu_ref, wd_ref):
    ff = wg_ref.shape[1]
    acc = jnp.zeros((hb.shape[0], D_MODEL), F32)
    for c0 in range(0, ff, FF_CHUNK):
        sl = slice(c0, min(c0 + FF_CHUNK, ff))
        g = jnp.dot(hb, wg_ref[:, sl].astype(BF16), preferred_element_type=F32)
        u = jnp.dot(hb, wu_ref[:, sl].astype(BF16), preferred_element_type=F32)
        a = (g * _sigmoid(g) * u).astype(BF16)
        acc = acc + jnp.dot(a, wd_ref[sl, :].astype(BF16), preferred_element_type=F32)
    return acc


def _ffn_kernel(h_ref, x1_ref, mod_ref, wg_ref, wu_ref, wd_ref, o_ref):
    acc = _swiglu_tile(h_ref[...], wg_ref, wu_ref, wd_ref)
    o_ref[...] = x1_ref[...] + mod_ref[5:6, :] * acc


def _ffn(h, x1, mod, p):
    B, S, D = x1.shape
    tm = min(TM_FFN, S)
    bm = mod.shape[0]
    mod_map = (lambda b, i: (b, 0, 0)) if bm > 1 else (lambda b, i: (0, 0, 0))
    c2 = lambda b, i: (0, 0)
    tok = pl.BlockSpec((None, tm, D), lambda b, i: (b, i, 0))
    vmem = 2 * 3 * D * D_FF * 2 + 12 * tm * D * 4 + (6 << 20)
    return pl.pallas_call(
        _ffn_kernel,
        grid=(B, S // tm),
        in_specs=[tok, tok, pl.BlockSpec((None, N_MOD, D), mod_map),
                  pl.BlockSpec((D, D_FF), c2), pl.BlockSpec((D, D_FF), c2), pl.BlockSpec((D_FF, D), c2)],
        out_specs=tok,
        out_shape=jax.ShapeDtypeStruct((B, S, D), F32),
        compiler_params=_cparams(("arbitrary", "arbitrary"), vmem),
        name="ffn",
    )(h, x1, mod, p["ffn_wg"], p["ffn_wu"], p["ffn_wd"])


def _row_copy(src_ref, src_row, dst_ref, dst_row, sem):
    return pltpu.make_async_copy(src_ref.at[pl.ds(src_row, 1), :], dst_ref.at[pl.ds(dst_row, 1), :], sem)


def _dispatch_kernel(slot_ref, pad_ref, h_ref, xs_ref, zero_ref, sem, *, tm, npad):
    zero_ref[...] = jnp.zeros(zero_ref.shape, F32)

    def start(r, carry):
        for k in range(2):
            _row_copy(h_ref, r, xs_ref, slot_ref[0, 0, k * tm + r], sem).start(priority=k)
        return carry

    def wait(r, carry):
        for k in range(2):
            _row_copy(h_ref, r, xs_ref, slot_ref[0, 0, k * tm + r], sem).wait()
        return carry

    def start_pad(j, carry):
        s = pad_ref[0, 0, j]

        @pl.when(s >= 0)
        def _():
            _row_copy(zero_ref, 0, xs_ref, s, sem).start()
        return carry

    def wait_pad(j, carry):
        s = pad_ref[0, 0, j]

        @pl.when(s >= 0)
        def _():
            _row_copy(zero_ref, 0, xs_ref, s, sem).wait()
        return carry

    lax.fori_loop(0, tm, start, 0, unroll=8)
    lax.fori_loop(0, npad, start_pad, 0, unroll=8)
    lax.fori_loop(0, tm, wait, 0, unroll=8)
    lax.fori_loop(0, npad, wait_pad, 0, unroll=8)


def _dispatch(h2, slots, pad_slots, n_rows):
    n, D = h2.shape
    tm = TM_ROUTE
    npad = pad_slots.shape[-1]
    return pl.pallas_call(
        functools.partial(_dispatch_kernel, tm=tm, npad=npad),
        grid=(n // tm,),
        in_specs=[pl.BlockSpec((1, 1, 2 * tm), lambda i: (i, 0, 0), memory_space=pltpu.SMEM),
                  pl.BlockSpec((1, 1, npad), lambda i: (i, 0, 0), memory_space=pltpu.SMEM),
                  pl.BlockSpec((tm, D), lambda i: (i, 0))],
        out_specs=pl.BlockSpec(memory_space=pl.ANY),
        out_shape=jax.ShapeDtypeStruct((n_rows, D), F32),
        scratch_shapes=[pltpu.VMEM((SUBLANES, D), F32), pltpu.SemaphoreType.DMA(())],
        compiler_params=_cparams(("arbitrary",), 4 * tm * D * 4 + (4 << 20)),
        name="dispatch",
    )(slots, pad_slots, h2)


def _experts_kernel(te_ref, nt_ref, xs_ref, wg_ref, wu_ref, wd_ref, *rest):
    o_ref = rest[-1]
    prev_ref = rest[0] if len(rest) == 2 else None
    t = pl.program_id(0)

    @pl.when(t < nt_ref[0])
    def _():
        part = _swiglu_tile(xs_ref[...].astype(BF16), wg_ref, wu_ref, wd_ref)
        o_ref[...] = part if prev_ref is None else prev_ref[...] + part

    @pl.when(t >= nt_ref[0])
    def _():
        o_ref[...] = jnp.zeros(o_ref.shape, F32)


def _experts(xs, tile_expert, n_tiles_used, p):
    n_slots, D = xs.shape
    tm = TM_MOE
    ffp = D_FF // MOE_FF_PARTS
    tok = pl.BlockSpec((tm, D), lambda t, te, nt: (t, 0))
    tok_in = pl.BlockSpec((tm, D), lambda t, te, nt: (jnp.minimum(t, nt[0] - 1), 0))
    vmem = 2 * 3 * D * ffp * 4 + 12 * tm * D * 4 + (8 << 20)
    ys = None
    for part in range(MOE_FF_PARTS):
        col_map = lambda t, te, nt, part=part: (te[t], 0, part)
        row_map = lambda t, te, nt, part=part: (te[t], part, 0)
        args = [tile_expert, n_tiles_used, xs, p["moe_wg"], p["moe_wu"], p["moe_wd"]]
        specs = [tok_in,
                 pl.BlockSpec((None, D, ffp), col_map),
                 pl.BlockSpec((None, D, ffp), col_map),
                 pl.BlockSpec((None, ffp, D), row_map)]
        aliases = {}
        if ys is not None:
            args.append(ys)
            specs.append(tok)
            aliases = {len(args) - 1: 0}
        ys = pl.pallas_call(
            _experts_kernel,
            grid_spec=pltpu.PrefetchScalarGridSpec(
                num_scalar_prefetch=2,
                grid=(n_slots // tm,),
                in_specs=specs,
                out_specs=tok),
            out_shape=jax.ShapeDtypeStruct((n_slots, D), F32),
            input_output_aliases=aliases,
            compiler_params=_cparams(("arbitrary",), vmem),
            name="experts%d" % part,
        )(*args)
    return ys


def _combine_kernel(slot_ref, x1_ref, route_ref, mod_ref, ys_ref, o_ref, buf_ref, sem, *, tm):
    def start(r, carry):
        for k in range(2):
            _row_copy(ys_ref, slot_ref[0, 0, k * tm + r], buf_ref.at[k], r, sem).start(priority=k)
        return carry

    def wait(r, carry):
        for k in range(2):
            _row_copy(ys_ref, slot_ref[0, 0, k * tm + r], buf_ref.at[k], r, sem).wait()
        return carry

    lax.fori_loop(0, tm, start, 0, unroll=8)
    lax.fori_loop(0, tm, wait, 0, unroll=8)
    route = route_ref[...]
    w1 = route[:, 2:3]
    w2 = route[:, 3:4]
    o_ref[...] = x1_ref[...] + mod_ref[5:6, :] * (w1 * buf_ref[0] + w2 * buf_ref[1])


def _combine(x1, route, mod, ys, slots):
    B, S, D = x1.shape
    tm = TM_ROUTE
    per = S // tm
    tok = lambda w: pl.BlockSpec((None, tm, w), lambda i: (i // per, i % per, 0))
    return pl.pallas_call(
        functools.partial(_combine_kernel, tm=tm),
        grid=(B * per,),
        in_specs=[pl.BlockSpec((1, 1, 2 * tm), lambda i: (i, 0, 0), memory_space=pltpu.SMEM),
                  tok(D), tok(LANES), pl.BlockSpec((None, N_MOD, D), lambda i: (i // per, 0, 0)),
                  pl.BlockSpec(memory_space=pl.ANY)],
        out_specs=tok(D),
        out_shape=jax.ShapeDtypeStruct((B, S, D), F32),
        scratch_shapes=[pltpu.VMEM((2, tm, D), F32), pltpu.SemaphoreType.DMA(())],
        compiler_params=_cparams(("arbitrary",), 10 * tm * D * 4 + (4 << 20)),
        name="combine",
    )(slots, x1, route, mod, ys)


def _moe(h, x1, route, cnt, mod, p):
    B, S, D = x1.shape
    n = B * S
    sub = n // (cnt.shape[0] * cnt.shape[1])
    tmm = TM_MOE
    n_tiles = 2 * n // tmm + N_EXPERTS
    n_slots = n_tiles * tmm
    r2 = route.reshape(n, LANES)
    e = r2[:, 0:2].astype(jnp.int32)
    rank = r2[:, 4:6].astype(jnp.int32)
    counts = cnt[:, :, 0:2, 0:N_EXPERTS].astype(jnp.int32).reshape(-1, 2, N_EXPERTS)
    tile_tot = counts.sum(axis=0)
    n_e = tile_tot.sum(axis=0)
    base = jnp.cumsum(counts, axis=0) - counts
    base = base + jnp.array([0, 1], jnp.int32)[None, :, None] * tile_tot[0][None, None, :]
    tiles_e = (n_e + tmm - 1) // tmm
    pstart = (jnp.cumsum(tiles_e) - tiles_e) * tmm
    onehot = e[:, :, None] == jnp.arange(N_EXPERTS, dtype=jnp.int32)[None, None, :]
    per_tok = jnp.repeat(base + pstart[None, None, :], sub, axis=0)
    slot = jnp.sum(jnp.where(onehot, per_tok, 0), axis=-1) + rank
    slots = slot.reshape(n // TM_ROUTE, TM_ROUTE, 2).transpose(0, 2, 1).reshape(n // TM_ROUTE, 1, 2 * TM_ROUTE)
    tile_end = jnp.cumsum(tiles_e)
    n_used = tile_end[-1]
    t_idx = jnp.minimum(jnp.arange(n_tiles, dtype=jnp.int32), n_used - 1)
    tile_expert = jnp.sum(t_idx[:, None] >= tile_end[None, :], axis=-1).astype(jnp.int32)
    tile_expert = jnp.minimum(tile_expert, N_EXPERTS - 1)

    pe = jnp.arange(N_EXPERTS * tmm, dtype=jnp.int32) // tmm
    pq = jnp.arange(N_EXPERTS * tmm, dtype=jnp.int32) % tmm
    n_pad_e = tiles_e * tmm - n_e
    pad_slots = jnp.where(pq < n_pad_e[pe], pstart[pe] + n_e[pe] + pq, -1).astype(jnp.int32)
    steps = n // TM_ROUTE
    assert (N_EXPERTS * tmm) % steps == 0
    pad_slots = pad_slots.reshape(steps, 1, (N_EXPERTS * tmm) // steps)

    xs = _dispatch(h.reshape(n, D), slots, pad_slots, n_slots)
    ys = _experts(xs, tile_expert, n_used.reshape(1).astype(jnp.int32), p)
    return _combine(x1, route, mod, ys, slots)


def _pad_heads(w, width):
    lead = w.shape[:-1]
    w = w.reshape(lead + (N_HEADS, width))
    w = jnp.pad(w, [(0, 0)] * len(lead) + [(0, 0), (0, HEAD_PAD - width)])
    return w.reshape(lead + (N_HEADS * HEAD_PAD,))


def _layer_params(l, w_in, q_lat_g, kv_lat_g, w_uq, w_ukv, q_norm_g, k_norm_g, conv_w, conv_b, conv_ln_g,
                  conv_ln_b, w_out, ffn_norm_g):
    wi = w_in[l]
    kr = jnp.pad(wi[:, OFF_KR:OFF_CONV], ((0, 0), (QK_NOPE, HEAD_PAD - QK_HEAD)))
    ckv = wi[:, OFF_CKV:OFF_KR]
    w_full = jnp.concatenate([wi[:, OFF_CQ:OFF_CKV], ckv, wi[:, OFF_CONV:OFF_FOUR], wi[:, OFF_FOUR:IN_COLS], kr],
                             axis=1)
    w_kv = jnp.concatenate([ckv, kr], axis=1)
    ukv = w_ukv[l].reshape(KV_LORA, N_HEADS, QK_NOPE + V_HEAD)
    uk = _pad_heads(ukv[:, :, :QK_NOPE].reshape(KV_LORA, N_HEADS * QK_NOPE), QK_NOPE)
    uv = _pad_heads(ukv[:, :, QK_NOPE:].reshape(KV_LORA, ATTN_OUT), V_HEAD)
    pad_g = lambda g: jnp.pad(g, (0, HEAD_PAD - QK_HEAD)).reshape(1, HEAD_PAD)
    uq = _pad_heads(w_uq[l], QK_HEAD)
    hw = N_HEADS * HEAD_PAD
    head_of_col = np.arange(hw)[:, None] // HEAD_PAD == np.arange(LANES)[None, :]
    expand = np.concatenate([head_of_col.T, head_of_col.T], axis=0)
    return {
        "w_in_full": w_full.astype(BF16),
        "w_in_rope": jnp.concatenate([w_full, _rot_partner(kr)], axis=1).astype(BF16),
        "w_in_kv": w_kv.astype(BF16),
        "q_lat_g": q_lat_g[l].reshape(1, Q_LORA),
        "kv_lat_g": kv_lat_g[l].reshape(1, KV_LORA),
        "w_uq": uq.astype(BF16),
        "w_uq_rope": jnp.concatenate([uq, _rot_partner(uq)], axis=1).astype(BF16),
        "w_ukv": jnp.concatenate([uk, uv], axis=1).astype(BF16),
        "q_norm_g": pad_g(q_norm_g[l]) * (QK_HEAD ** -0.5 * math.log2(math.e)),
        "v_ones": jnp.tile(jnp.concatenate([jnp.zeros((V_HEAD,), F32), jnp.ones((HEAD_PAD - V_HEAD,), F32)]),
                           N_HEADS).reshape(1, hw),
        "ones_h": jnp.asarray(head_of_col, F32).astype(BF16),
        "expand_h": jnp.asarray(expand, F32).astype(BF16),
        "k_norm_g": pad_g(k_norm_g[l]),
        "conv_w": conv_w[l],
        "conv_b": conv_b[l].reshape(1, CONV_CH),
        "conv_ln_g": conv_ln_g[l].reshape(1, CONV_CH),
        "conv_ln_b": conv_ln_b[l].reshape(1, CONV_CH),
        "w_out": w_out[l].astype(BF16),
        "ffn_norm_g": ffn_norm_g[l].reshape(1, D_MODEL),
        "bd": _channel_dft(),
    }


def _rot_partner(w):
    lead = w.shape[:-1]
    half = QK_ROPE // 2
    w3 = w.reshape(lead + (-1, HEAD_PAD))
    z = jnp.zeros_like(w3)
    out = jnp.concatenate([z[..., :QK_NOPE], w3[..., QK_NOPE + half:QK_HEAD], w3[..., QK_NOPE:QK_NOPE + half],
                           z[..., QK_HEAD:]], axis=-1)
    return out.reshape(w.shape)


def _rope_tables(seq, gq, gk):
    rows = seq // GRID_W
    row = jnp.repeat(jnp.arange(rows, dtype=F32), GRID_W)
    col = jnp.tile(jnp.arange(GRID_W, dtype=F32), rows)
    n_freq = QK_ROPE // 4
    inv = ROPE_BASE ** (-jnp.arange(n_freq, dtype=F32) / n_freq)
    ang = jnp.concatenate([row[:, None] * inv, col[:, None] * inv], axis=-1)
    cos, sin = jnp.cos(ang), jnp.sin(ang)
    ones = jnp.ones((seq, QK_NOPE), F32)
    tail = jnp.zeros((seq, HEAD_PAD - QK_HEAD), F32)
    cos_t = jnp.concatenate([ones, cos, cos, tail], axis=1)
    sin_t = jnp.concatenate([jnp.zeros((seq, QK_NOPE), F32), -sin, sin, tail], axis=1)
    return (cos_t * gq, sin_t * _rot_partner(gq), cos_t * gk, sin_t * _rot_partner(gk))


def _mixer(xs, mod, p, rope_tabs, k_extra, v_extra, cs):
    q, k, v, y, zz = _front(xs, mod, p["mix_norm_g"], p, rope_tabs, full=True)
    ks, vs = [k], [v]
    if k_extra is not None:
        ks.append(k_extra)
        vs.append(v_extra)
    attn = _attend(q, ks, vs)
    conv = _conv(y, p)
    four = _fourier(zz, cs)
    return attn, conv, four, k, v


def kernel(x, c, ctx, c_ctx, ada_w, ada_b, mix_norm_g, ffn_norm_g, w_in, q_lat_g, kv_lat_g, w_uq, w_ukv, q_norm_g,
           k_norm_g, conv_w, conv_b, conv_ln_g, conv_ln_b, w_out, ffn_w_gate, ffn_w_up, ffn_w_down, router_w,
           moe_w_gate, moe_w_up, moe_w_down):
    B, S, D = x.shape
    T = ctx.shape[1]
    assert (D, DEPTH) == (D_MODEL, ada_w.shape[0]) and S % GRID_W == 0

    cc = jnp.zeros((2 * SUBLANES, D), F32).at[:B].set(c).at[B].set(c_ctx)
    mods = _modulation(cc, ada_w, ada_b).reshape(DEPTH, 2 * SUBLANES, N_MOD, D)
    cs_x = _dft_tables(S)
    cs_c = _dft_tables(T)

    for l in range(DEPTH):
        last = l == DEPTH - 1
        p = _layer_params(l, w_in, q_lat_g, kv_lat_g, w_uq, w_ukv, q_norm_g, k_norm_g, conv_w, conv_b, conv_ln_g,
                          conv_ln_b, w_out, ffn_norm_g)
        p["mix_norm_g"] = mix_norm_g[l].reshape(1, D)
        rope_tabs = _rope_tables(S, p["q_norm_g"], p["k_norm_g"])
        i = l // 2
        moe = l % 2 == 1
        if moe:
            rw = jnp.pad(router_w[i], ((0, 0), (0, LANES - N_EXPERTS)))
            rw_hi = rw.astype(BF16)
            p["router_w"] = jnp.concatenate([rw_hi, (rw - rw_hi.astype(F32)).astype(BF16)], axis=1)
            p["moe_wg"] = moe_w_gate[i]
            p["moe_wu"] = moe_w_up[i]
            p["moe_wd"] = moe_w_down[i]
        else:
            p["ffn_wg"] = ffn_w_gate[i].astype(BF16)
            p["ffn_wu"] = ffn_w_up[i].astype(BF16)
            p["ffn_wd"] = ffn_w_down[i].astype(BF16)
        mod_x = mods[l, :B]
        mod_c = mods[l, B:B + 1]

        def channel_mixer(attn, conv, four, xs, mod):
            if moe:
                x1, h, route, cnt = _mix_out(attn, conv, four, xs, mod, p, moe=True)
                return _moe(h, x1, route, cnt, mod if mod.shape[0] > 1 else jnp.broadcast_to(mod, (B,) + mod.shape[1:]), p)
            x1, h = _mix_out(attn, conv, four, xs, mod, p, moe=False)
            return _ffn(h, x1, mod, p)

        if last:
            k_c, v_c = _front(ctx, mod_c, p["mix_norm_g"], p, None, full=False)
        else:
            attn_c, conv_c, four_c, k_c, v_c = _mixer(ctx, mod_c, p, None, None, None, cs_c)
            ctx_next = channel_mixer(attn_c, conv_c, four_c, ctx, mod_c)

        attn_x, conv_x, four_x, _, _ = _mixer(x, mod_x, p, rope_tabs, k_c, v_c, cs_x)
        x = channel_mixer(attn_x, conv_x, four_x, x, mod_x)
        if not last:
            ctx = ctx_next
    return x
```

```python
import functools
import math

import numpy as np
import jax
import jax.numpy as jnp
from jax import lax
from jax.experimental import pallas as pl
from jax.experimental.pallas import tpu as pltpu

F32 = jnp.float32
BF16 = jnp.bfloat16

D_MODEL = 1024
DEPTH = 2
GRID_W = 64
N_HEADS = 8
QK_NOPE = 64
QK_ROPE = 32
QK_HEAD = QK_NOPE + QK_ROPE
V_HEAD = 64
Q_LORA = 384
KV_LORA = 256
ROPE_BASE = 10000.0
CONV_CH = 256
CONV_WIDTH = 31
CONV_PAD = (CONV_WIDTH - 1) // 2
FOURIER_GROUPS = 4
FOURIER_GROUP_DIM = 64
FOURIER_CH = FOURIER_GROUPS * FOURIER_GROUP_DIM
ATTN_OUT = N_HEADS * V_HEAD
OFF_CQ = 0
OFF_CKV = OFF_CQ + Q_LORA
OFF_KR = OFF_CKV + KV_LORA
OFF_CONV = OFF_KR + QK_ROPE
OFF_FOUR = OFF_CONV + 2 * CONV_CH
IN_COLS = OFF_FOUR + FOURIER_CH
D_FF = 2816
N_EXPERTS = 8
N_MOD = 6
EPS = 1e-6

LANES = 128
SUBLANES = 8
HEAD_PAD = LANES
VMEM_CAP = 56 * 1024 * 1024
FF_CHUNK = 256

SUB_ROWS = 256
TM_FRONT = 512
TQ_ATTN = 1024
TQ_SUB = 512
TK_ATTN = 256
TM_OUT = 512
TM_FFN = 512
TM_MOE = 512
MOE_FF_PARTS = 1
TM_ROUTE = 512
CONV_ROWS = 128
TR_FOURIER = 512


def _cparams(sem, vmem_bytes):
    return pltpu.CompilerParams(dimension_semantics=sem, vmem_limit_bytes=int(min(VMEM_CAP, vmem_bytes)))


def _rms(v, g):
    return v * lax.rsqrt(jnp.mean(v * v, axis=-1, keepdims=True) + EPS) * g


def _sigmoid(v):
    return 1.0 / (1.0 + jnp.exp(-v))


def _mod_kernel(c_ref, w_ref, b_ref, o_ref):
    c = c_ref[...]
    s = (c * _sigmoid(c)).astype(BF16)
    o_ref[...] = jnp.dot(s, w_ref[...].astype(BF16), preferred_element_type=F32) + b_ref[...]


def _modulation(cc, ada_w, ada_b):
    rows = cc.shape[0]
    tn = 1536
    n_out = N_MOD * D_MODEL
    return pl.pallas_call(
        _mod_kernel,
        grid=(DEPTH, n_out // tn),
        in_specs=[
            pl.BlockSpec((rows, D_MODEL), lambda l, j: (0, 0)),
            pl.BlockSpec((None, D_MODEL, tn), lambda l, j: (l, 0, j)),
            pl.BlockSpec((None, 1, tn), lambda l, j: (l, 0, j)),
        ],
        out_specs=pl.BlockSpec((None, rows, tn), lambda l, j: (l, 0, j)),
        out_shape=jax.ShapeDtypeStruct((DEPTH, rows, n_out), F32),
        compiler_params=_cparams(("arbitrary", "arbitrary"), 4 * D_MODEL * tn * 4),
        name="modulation",
    )(cc, ada_w, ada_b.reshape(DEPTH, 1, n_out))


def _front_kernel(*refs, full, rope):
    it = iter(refs)
    x_ref, mod_ref, g_ref, win_ref = next(it), next(it), next(it), next(it)
    if full:
        qlg_ref, wuq_ref = next(it), next(it)
    kvlg_ref, wukv_ref, vones_ref, onesh_ref, expand_ref = next(it), next(it), next(it), next(it), next(it)
    if rope:
        cq_ref, sq_ref, ck_ref, sk_ref = next(it), next(it), next(it), next(it)
    else:
        qg_ref = next(it) if full else None
        kg_ref = next(it)
    if full:
        bd_ref = next(it)
        q_out = next(it)
    k_out, v_out = next(it), next(it)
    if full:
        y_out, zz_out = next(it), next(it)

    shift = mod_ref[0:1, :]
    scale = mod_ref[1:2, :]
    hw = N_HEADS * HEAD_PAD

    def head_inv_rms(raw):
        ss = jnp.dot((raw * raw).astype(BF16), onesh_ref[...], preferred_element_type=F32)
        rs = lax.rsqrt(ss * (1.0 / QK_HEAD) + EPS)
        rs_hi = rs.astype(BF16)
        rs_lo = (rs - rs_hi.astype(F32)).astype(BF16)
        return jnp.dot(jnp.concatenate([rs_hi, rs_lo], axis=1), expand_ref[...], preferred_element_type=F32)

    def sub_block(rows):
        x = x_ref[rows, :]
        h = _rms(x, g_ref[...]) * (1.0 + scale) + shift
        cols = jnp.dot(h.astype(BF16), win_ref[...], preferred_element_type=F32)

        o = 0
        if full:
            cq = cols[:, 0:Q_LORA]
            o = Q_LORA
            qall = jnp.dot(_rms(cq, qlg_ref[...]).astype(BF16), wuq_ref[...], preferred_element_type=F32)
            rsb = head_inv_rms(qall[:, 0:hw])
            for hd in range(N_HEADS):
                sl = slice(hd * HEAD_PAD, (hd + 1) * HEAD_PAD)
                if rope:
                    val = qall[:, sl] * cq_ref[rows, :] + qall[:, hw + hd * HEAD_PAD:hw + (hd + 1) * HEAD_PAD] * \
                        sq_ref[rows, :]
                else:
                    val = qall[:, sl] * qg_ref[...]
                q_out[rows, sl] = (val * rsb[:, sl]).astype(BF16)

        ckv = cols[:, o:o + KV_LORA]
        o += KV_LORA
        kv = jnp.dot(_rms(ckv, kvlg_ref[...]).astype(BF16), wukv_ref[...], preferred_element_type=F32)
        if full:
            a = cols[:, o:o + CONV_CH]
            gt = cols[:, o + CONV_CH:o + 2 * CONV_CH]
            y_out[rows, :] = a * _sigmoid(gt)
            o += 2 * CONV_CH
            z = cols[:, o:o + FOURIER_CH]
            o += FOURIER_CH
            zz_out[rows, :] = jnp.dot(z.astype(BF16), bd_ref[...], preferred_element_type=F32).astype(BF16)
        krb = cols[:, o:o + HEAD_PAD]
        kraw = [kv[:, hd * HEAD_PAD:(hd + 1) * HEAD_PAD] + krb for hd in range(N_HEADS)]
        rsb = head_inv_rms(jnp.concatenate(kraw, axis=1))
        if rope:
            kpart = cols[:, o + HEAD_PAD:o + 2 * HEAD_PAD] * sk_ref[rows, :]
        for hd in range(N_HEADS):
            sl = slice(hd * HEAD_PAD, (hd + 1) * HEAD_PAD)
            val = kraw[hd] * ck_ref[rows, :] + kpart if rope else kraw[hd] * kg_ref[...]
            k_out[rows, sl] = (val * rsb[:, sl]).astype(BF16)
        v_out[rows, :] = (kv[:, hw:] + vones_ref[...]).astype(BF16)

    tm = x_ref.shape[0]
    sub = min(SUB_ROWS, tm)
    for sb in range(tm // sub):
        sub_block(slice(sb * sub, (sb + 1) * sub))


def _front(xs, mod, norm_g, p, rope_tabs, *, full):
    B, S, D = xs.shape
    tm = min(TM_FRONT, S)
    rope = rope_tabs is not None
    assert full or not rope
    w_in = (p["w_in_rope"] if rope else p["w_in_full"]) if full else p["w_in_kv"]
    w_uq = p["w_uq_rope"] if rope else p["w_uq"]
    ncol = w_in.shape[1]
    bm = mod.shape[0]
    mod_map = (lambda b, i: (b, 0, 0)) if bm > 1 else (lambda b, i: (0, 0, 0))
    const2 = lambda b, i: (0, 0)
    hw = N_HEADS * HEAD_PAD

    args = [xs, mod, norm_g, w_in]
    specs = [
        pl.BlockSpec((None, tm, D), lambda b, i: (b, i, 0)),
        pl.BlockSpec((None, N_MOD, D), mod_map),
        pl.BlockSpec((1, D), const2),
        pl.BlockSpec((D, ncol), const2),
    ]
    if full:
        args += [p["q_lat_g"], w_uq]
        specs += [pl.BlockSpec((1, Q_LORA), const2), pl.BlockSpec((Q_LORA, w_uq.shape[1]), const2)]
    args += [p["kv_lat_g"], p["w_ukv"], p["v_ones"], p["ones_h"], p["expand_h"]]
    specs += [pl.BlockSpec((1, KV_LORA), const2),
              pl.BlockSpec((KV_LORA, 2 * hw), const2),
              pl.BlockSpec((1, hw), const2),
              pl.BlockSpec((hw, LANES), const2),
              pl.BlockSpec((2 * LANES, hw), const2)]
    if rope:
        args += list(rope_tabs)
        specs += [pl.BlockSpec((tm, HEAD_PAD), lambda b, i: (i, 0))] * 4
    else:
        if full:
            args += [p["q_norm_g"]]
            specs += [pl.BlockSpec((1, HEAD_PAD), const2)]
        args += [p["k_norm_g"]]
        specs += [pl.BlockSpec((1, HEAD_PAD), const2)]
    if full:
        args += [p["bd"]]
        specs += [pl.BlockSpec((FOURIER_CH, 2 * FOURIER_CH), const2)]

    out_shapes, out_specs = [], []

    def add_out(width, dtype):
        out_shapes.append(jax.ShapeDtypeStruct((B, S, width), dtype))
        out_specs.append(pl.BlockSpec((None, tm, width), lambda b, i: (b, i, 0)))

    if full:
        add_out(N_HEADS * HEAD_PAD, BF16)
    add_out(N_HEADS * HEAD_PAD, BF16)
    add_out(N_HEADS * HEAD_PAD, BF16)
    if full:
        add_out(CONV_CH, F32)
        add_out(2 * FOURIER_CH, BF16)

    vmem = 2 * (D * ncol * 2 + Q_LORA * 1024 * 2 + KV_LORA * 2048 * 2) + 28 * tm * D * 4
    return pl.pallas_call(
        functools.partial(_front_kernel, full=full, rope=rope),
        grid=(B, S // tm),
        in_specs=specs,
        out_specs=out_specs,
        out_shape=out_shapes,
        compiler_params=_cparams(("arbitrary", "arbitrary"), vmem),
        name="front_full" if full else "front_kv",
    )(*args)


def _attn_kernel(*refs, nsrc, ncast):
    q_ref = refs[0]
    k_refs = refs[1:1 + nsrc]
    v_refs = refs[1 + nsrc:1 + 2 * nsrc]
    cast_in = refs[1 + 2 * nsrc:1 + 2 * nsrc + ncast]
    o_ref = refs[1 + 2 * nsrc + ncast]
    cast_out = refs[2 + 2 * nsrc + ncast:]
    chunks = []
    for k_ref, v_ref in zip(k_refs, v_refs):
        for s0 in range(0, k_ref.shape[0], TK_ATTN):
            chunks.append((k_ref, v_ref, s0, min(TK_ATTN, k_ref.shape[0] - s0)))
    tq = q_ref.shape[0]
    sub = min(TQ_SUB, tq)
    for r0 in range(0, tq, sub):
        rows = slice(r0, r0 + sub)
        state = [None, None]
        for k_ref, v_ref, s0, sz in chunks:
            for hh in range(2):
                sl = slice(hh * HEAD_PAD, (hh + 1) * HEAD_PAD)
                s = lax.dot_general(q_ref[rows, sl], k_ref[s0:s0 + sz, sl], (((1,), (1,)), ((), ())),
                                    preferred_element_type=F32)
                m = jnp.max(s, axis=-1, keepdims=True)
                if state[hh] is not None:
                    m_old, acc_old = state[hh]
                    m = jnp.maximum(m_old, m)
                pv = jnp.dot(jnp.exp2((s - m).astype(BF16)), v_ref[s0:s0 + sz, sl], preferred_element_type=F32)
                if state[hh] is not None:
                    pv = pv + jnp.exp2(m_old - m) * acc_old
                state[hh] = (m, pv)
        accs = [state[0][1], state[1][1]]
        lane = lax.broadcasted_iota(jnp.int32, accs[0].shape, 1)
        lo = accs[0] / pltpu.roll(accs[0], V_HEAD, 1)
        hi = pltpu.roll(accs[1], V_HEAD, 1) / accs[1]
        o_ref[rows, :] = jnp.where(lane < V_HEAD, lo, hi).astype(BF16)
    for src, dst in zip(cast_in, cast_out):
        dst[...] = src[...].astype(BF16)


def _attend(q, ks, vs, cast=()):
    B, S, _ = q.shape
    tq = min(TQ_ATTN, S)
    nsrc = len(ks)
    nq = S // tq
    steps = B * (N_HEADS // 2) * nq
    specs = [pl.BlockSpec((None, tq, 2 * HEAD_PAD), lambda b, hp, i: (b, i, hp))]
    for kv in list(ks) + list(vs):
        specs.append(pl.BlockSpec((None, kv.shape[1], 2 * HEAD_PAD), lambda b, hp, i: (b, 0, hp)))
    out_specs = [pl.BlockSpec((None, tq, 2 * V_HEAD), lambda b, hp, i: (b, i, hp))]
    out_shapes = [jax.ShapeDtypeStruct((B, S, ATTN_OUT), BF16)]
    cast_bytes = 0
    for w in cast:
        rows, cols = w.shape
        assert rows % (steps * 2 * SUBLANES) == 0
        spec = pl.BlockSpec((rows // steps, cols), lambda b, hp, i: ((b * (N_HEADS // 2) + hp) * nq + i, 0))
        specs.append(spec)
        out_specs.append(spec)
        out_shapes.append(jax.ShapeDtypeStruct(w.shape, BF16))
        cast_bytes += 2 * (rows // steps) * cols * 6
    t_all = sum(k.shape[1] for k in ks)
    vmem = 2 * (t_all * 512 * 2) * 2 + 16 * min(tq, TQ_SUB) * TK_ATTN * 4 + cast_bytes + (12 << 20)
    outs = pl.pallas_call(
        functools.partial(_attn_kernel, nsrc=nsrc, ncast=len(cast)),
        grid=(B, N_HEADS // 2, nq),
        in_specs=specs,
        out_specs=out_specs,
        out_shape=out_shapes,
        compiler_params=_cparams(("arbitrary", "arbitrary", "arbitrary"), vmem),
        name="attend%d" % nsrc,
    )(q, *ks, *vs, *cast)
    return outs[0] if not cast else outs


def _conv_kernel(y_ref, w_ref, b_ref, lg_ref, lb_ref, o_ref, pad_ref, *, seq):
    halo = 2 * SUBLANES
    pad_ref[0:halo, :] = jnp.zeros((halo, CONV_CH), F32)
    pad_ref[halo + seq:2 * halo + seq, :] = jnp.zeros((halo, CONV_CH), F32)
    pad_ref[halo:halo + seq, :] = y_ref[...]
    rows = CONV_ROWS
    for base in range(0, seq, rows):
        acc = jnp.zeros((rows, CONV_CH), F32)
        for j in range(CONV_WIDTH):
            off = base + halo - CONV_PAD + j
            acc = acc + pad_ref[off:off + rows, :] * w_ref[j:j + 1, :]
        acc = acc + b_ref[...]
        mu = jnp.mean(acc, axis=-1, keepdims=True)
        cen = acc - mu
        var = jnp.mean(cen * cen, axis=-1, keepdims=True)
        yn = cen * lax.rsqrt(var + EPS) * lg_ref[...] + lb_ref[...]
        o_ref[base:base + rows, :] = (yn * _sigmoid(yn)).astype(BF16)


def _conv(y, p):
    B, S, _ = y.shape
    c2 = lambda b: (0, 0)
    return pl.pallas_call(
        functools.partial(_conv_kernel, seq=S),
        grid=(B,),
        in_specs=[pl.BlockSpec((None, S, CONV_CH), lambda b: (b, 0, 0)),
                  pl.BlockSpec((CONV_WIDTH, CONV_CH), c2),
                  pl.BlockSpec((1, CONV_CH), c2), pl.BlockSpec((1, CONV_CH), c2), pl.BlockSpec((1, CONV_CH), c2)],
        out_specs=pl.BlockSpec((None, S, CONV_CH), lambda b: (b, 0, 0)),
        out_shape=jax.ShapeDtypeStruct((B, S, CONV_CH), BF16),
        scratch_shapes=[pltpu.VMEM((S + 4 * SUBLANES, CONV_CH), F32)],
        compiler_params=_cparams(("arbitrary",), 8 * S * CONV_CH * 4 + (8 << 20)),
        name="conv",
    )(y, p["conv_w"], p["conv_b"], p["conv_ln_g"], p["conv_ln_b"])


def _fourier_kernel(cs_ref, zz_ref, o_ref, *, seq):
    acc = jnp.dot(cs_ref[:, 0:seq], zz_ref[:, 0:FOURIER_CH], preferred_element_type=F32)
    acc = acc + jnp.dot(cs_ref[:, seq:2 * seq], zz_ref[:, FOURIER_CH:2 * FOURIER_CH], preferred_element_type=F32)
    o_ref[...] = acc.astype(BF16)


def _fourier(zz, cs):
    B, S, _ = zz.shape
    tr = min(TR_FOURIER, S)
    return pl.pallas_call(
        functools.partial(_fourier_kernel, seq=S),
        grid=(S // tr, B),
        in_specs=[pl.BlockSpec((tr, 2 * S), lambda r, b: (r, 0)),
                  pl.BlockSpec((None, S, 2 * FOURIER_CH), lambda r, b: (b, 0, 0))],
        out_specs=pl.BlockSpec((None, tr, FOURIER_CH), lambda r, b: (b, r, 0)),
        out_shape=jax.ShapeDtypeStruct((B, S, FOURIER_CH), BF16),
        compiler_params=_cparams(("arbitrary", "arbitrary"), 2 * (tr * 2 * S * 2 + S * 512 * 2) + (8 << 20)),
        name="fourier",
    )(cs, zz)


def _dft_tables(seq):
    norm = seq ** -0.5
    if seq <= 256:
        ks = (np.arange(seq)[:, None] * np.arange(seq)[None, :]) % seq
        ang = 2.0 * np.pi * ks / seq
        tab = np.concatenate([np.cos(ang), -np.sin(ang)], axis=1) * norm
        return jnp.asarray(tab, F32).astype(BF16)
    inner = 64
    outer = seq // inner
    k = np.arange(seq)[:, None]
    ang_a = 2.0 * np.pi * ((k * np.arange(outer)[None, :]) % outer) / outer
    ang_b = 2.0 * np.pi * ((k * np.arange(inner)[None, :]) % seq) / seq
    ca, sa = jnp.asarray(np.cos(ang_a) * norm, F32)[:, :, None], jnp.asarray(np.sin(ang_a) * norm, F32)[:, :, None]
    cb, sb = jnp.asarray(np.cos(ang_b), F32)[:, None, :], jnp.asarray(np.sin(ang_b), F32)[:, None, :]
    cos = (ca * cb - sa * sb).reshape(seq, seq)
    sin = (sa * cb + ca * sb).reshape(seq, seq)
    return jnp.concatenate([cos, -sin], axis=1).astype(BF16)


def _channel_dft():
    n = FOURIER_GROUP_DIM
    ang = 2.0 * np.pi * ((np.arange(n)[:, None] * np.arange(n)[None, :]) % n) / n
    eye = np.eye(FOURIER_GROUPS)
    bd = np.concatenate([np.kron(eye, np.cos(ang)), np.kron(eye, np.sin(ang))], axis=1) * n ** -0.5
    return jnp.asarray(bd, F32).astype(BF16)


def _mixout_kernel(*refs, moe):
    if moe:
        (attn_ref, conv_ref, four_ref, x_ref, mod_ref, wout_ref, g_ref, rw_ref, tri_ref,
         x1_out, h_out, route_out, cnt_out) = refs
    else:
        attn_ref, conv_ref, four_ref, x_ref, mod_ref, wout_ref, g_ref, x1_out, h_out = refs
    tm = x_ref.shape[0]
    sub = min(SUB_ROWS, tm)
    for sb in range(tm // sub):
        rows = slice(sb * sub, (sb + 1) * sub)
        mix = jnp.dot(attn_ref[rows, :], wout_ref[0:ATTN_OUT, :], preferred_element_type=F32)
        mix = mix + jnp.dot(conv_ref[rows, :], wout_ref[ATTN_OUT:ATTN_OUT + CONV_CH, :],
                            preferred_element_type=F32)
        mix = mix + jnp.dot(four_ref[rows, :], wout_ref[ATTN_OUT + CONV_CH:, :], preferred_element_type=F32)
        x1 = x_ref[rows, :] + mod_ref[2:3, :] * mix
        x1_out[rows, :] = x1
        h = _rms(x1, g_ref[...]) * (1.0 + mod_ref[4:5, :]) + mod_ref[3:4, :]
        if moe:
            h_out[rows, :] = h
            route, cnt = _route(h, rw_ref, tri_ref)
            route_out[rows, :] = route
            cnt_out[sb] = cnt
        else:
            h_out[rows, :] = h.astype(BF16)


def _route(h, rw_ref, tri_ref):
    tm = h.shape[0]
    h_hi = h.astype(BF16)
    h_lo = (h - h_hi.astype(F32)).astype(BF16)
    hw = jnp.dot(h_hi, rw_ref[...], preferred_element_type=F32)
    logits = hw[:, 0:LANES] + hw[:, LANES:] + jnp.dot(h_lo, rw_ref[:, 0:LANES], preferred_element_type=F32)
    lane = lax.broadcasted_iota(jnp.int32, (tm, LANES), 1).astype(F32)
    neg = jnp.float32(-jnp.inf)
    lm = jnp.where(lane < N_EXPERTS, logits, neg)
    m1 = jnp.max(lm, axis=-1, keepdims=True)
    i1 = jnp.min(jnp.where(lm == m1, lane, float(LANES)), axis=-1, keepdims=True)
    lm2 = jnp.where(lane == i1, neg, lm)
    m2 = jnp.max(lm2, axis=-1, keepdims=True)
    i2 = jnp.min(jnp.where(lm2 == m2, lane, float(LANES)), axis=-1, keepdims=True)
    t = jnp.exp(m2 - m1)
    w1 = 1.0 / (1.0 + t)
    w2 = t / (1.0 + t)
    oh1 = lane == i1
    oh2 = lane == i2
    tri = tri_ref[...]
    c1 = jnp.dot(tri, jnp.where(oh1, 1.0, 0.0).astype(BF16), preferred_element_type=F32)
    c2 = jnp.dot(tri, jnp.where(oh2, 1.0, 0.0).astype(BF16), preferred_element_type=F32)
    r1 = jnp.sum(jnp.where(oh1, c1, 0.0), axis=-1, keepdims=True)
    r2 = jnp.sum(jnp.where(oh2, c2, 0.0), axis=-1, keepdims=True)
    vals = [i1, i2, w1, w2, r1, r2]
    route = jnp.zeros((tm, LANES), F32)
    for idx, v in enumerate(vals):
        route = jnp.where(lane == idx, v, route)
    n1 = jnp.sum(jnp.where(oh1, 1.0, 0.0), axis=0, keepdims=True)
    n2 = jnp.sum(jnp.where(oh2, 1.0, 0.0), axis=0, keepdims=True)
    row = lax.broadcasted_iota(jnp.int32, (SUBLANES, LANES), 0)
    return route, jnp.where(row == 0, n1, jnp.where(row == 1, n2, 0.0))


def _mix_out(attn, conv, four, xs, mod, p, *, moe):
    B, S, D = xs.shape
    tm = min(TM_OUT, S)
    sub = min(SUB_ROWS, tm)
    bm = mod.shape[0]
    mod_map = (lambda b, i: (b, 0, 0)) if bm > 1 else (lambda b, i: (0, 0, 0))
    c2 = lambda b, i: (0, 0)
    tok = lambda w: pl.BlockSpec((None, tm, w), lambda b, i: (b, i, 0))
    args = [attn, conv, four, xs, mod, p["w_out"], p["ffn_norm_g"]]
    specs = [tok(ATTN_OUT), tok(CONV_CH), tok(FOURIER_CH), tok(D), pl.BlockSpec((None, N_MOD, D), mod_map),
             pl.BlockSpec((D, D), c2), pl.BlockSpec((1, D), c2)]
    out_shapes = [jax.ShapeDtypeStruct((B, S, D), F32), jax.ShapeDtypeStruct((B, S, D), F32 if moe else BF16)]
    out_specs = [tok(D), tok(D)]
    if moe:
        tri = jnp.asarray(np.tril(np.ones((sub, sub), np.float32), -1), BF16)
        args += [p["router_w"], tri]
        specs += [pl.BlockSpec((D, 2 * LANES), c2), pl.BlockSpec((sub, sub), c2)]
        out_shapes += [jax.ShapeDtypeStruct((B, S, LANES), F32),
                       jax.ShapeDtypeStruct((B, S // sub, SUBLANES, LANES), F32)]
        out_specs += [tok(LANES), pl.BlockSpec((None, tm // sub, SUBLANES, LANES), lambda b, i: (b, i, 0, 0))]
    return pl.pallas_call(
        functools.partial(_mixout_kernel, moe=moe),
        grid=(B, S // tm),
        in_specs=specs,
        out_specs=out_specs,
        out_shape=out_shapes,
        compiler_params=_cparams(("arbitrary", "arbitrary"), 2 * D * D * 2 + 24 * tm * D * 4 + (8 << 20)),
        name="mix_out_moe" if moe else "mix_out",
    )(*args)


def _swiglu_tile(hb, wg_ref, wu_ref, wd_ref):
    ff = wg_ref.shape[1]
    acc = jnp.zeros((hb.shape[0], D_MODEL), F32)
    for c0 in range(0, ff, FF_CHUNK):
        sl = slice(c0, min(c0 + FF_CHUNK, ff))
        g = jnp.dot(hb, wg_ref[:, sl].astype(BF16), preferred_element_type=F32)
        u = jnp.dot(hb, wu_ref[:, sl].astype(BF16), preferred_element_type=F32)
        a = (g * _sigmoid(g) * u).astype(BF16)
        acc = acc + jnp.dot(a, wd_ref[sl, :].astype(BF16), preferred_element_type=F32)
    return acc


def _ffn_kernel(h_ref, x1_ref, mod_ref, wg_ref, wu_ref, wd_ref, o_ref):
    acc = _swiglu_tile(h_ref[...], wg_ref, wu_ref, wd_ref)
    o_ref[...] = x1_ref[...] + mod_ref[5:6, :] * acc


def _ffn(h, x1, mod, p):
    B, S, D = x1.shape
    tm = min(TM_FFN, S)
    bm = mod.shape[0]
    mod_map = (lambda b, i: (b, 0, 0)) if bm > 1 else (lambda b, i: (0, 0, 0))
    c2 = lambda b, i: (0, 0)
    tok = pl.BlockSpec((None, tm, D), lambda b, i: (b, i, 0))
    vmem = 2 * 3 * D * D_FF * 2 + 12 * tm * D * 4 + (6 << 20)
    return pl.pallas_call(
        _ffn_kernel,
        grid=(B, S // tm),
        in_specs=[tok, tok, pl.BlockSpec((None, N_MOD, D), mod_map),
                  pl.BlockSpec((D, D_FF), c2), pl.BlockSpec((D, D_FF), c2), pl.BlockSpec((D_FF, D), c2)],
        out_specs=tok,
        out_shape=jax.ShapeDtypeStruct((B, S, D), F32),
        compiler_params=_cparams(("arbitrary", "arbitrary"), vmem),
        name="ffn",
    )(h, x1, mod, p["ffn_wg"], p["ffn_wu"], p["ffn_wd"])


def _row_copy(src_ref, src_row, dst_ref, dst_row, sem):
    return pltpu.make_async_copy(src_ref.at[pl.ds(src_row, 1), :], dst_ref.at[pl.ds(dst_row, 1), :], sem)


def _dispatch_kernel(slot_ref, pad_ref, h_ref, xs_ref, hbuf, zero_ref, in_sem, out_sem, *, tm, npad, steps):
    i = pl.program_id(0)

    def fetch(step, slot):
        return pltpu.make_async_copy(h_ref.at[pl.ds(step * tm, tm), :], hbuf.at[slot], in_sem.at[slot])

    def drain(slot):
        for _ in range(2):
            pltpu.make_async_copy(hbuf.at[slot], xs_ref.at[pl.ds(0, tm), :], out_sem.at[slot]).wait()
        pltpu.make_async_copy(hbuf.at[slot, pl.ds(0, npad), :], xs_ref.at[pl.ds(0, npad), :],
                              out_sem.at[slot]).wait()

    @pl.when(i == 0)
    def _():
        zero_ref[...] = jnp.zeros(zero_ref.shape, F32)
        fetch(0, 0).start()

    slot = lax.rem(i, 3)
    fetch(i, slot).wait()

    @pl.when(i + 1 < steps)
    def _():
        fetch(i + 1, lax.rem(i + 1, 3)).start()

    def scatter(r, carry):
        for k in range(2):
            _row_copy(hbuf.at[slot], r, xs_ref, slot_ref[0, 0, k * tm + r], out_sem.at[slot]).start(priority=k)
        return carry

    def scatter_pad(j, carry):
        _row_copy(zero_ref, 0, xs_ref, pad_ref[0, 0, j], out_sem.at[slot]).start()
        return carry

    lax.fori_loop(0, tm, scatter, 0, unroll=8)
    lax.fori_loop(0, npad, scatter_pad, 0, unroll=8)

    @pl.when(i > 0)
    def _():
        drain(lax.rem(i + 2, 3))

    @pl.when(i == steps - 1)
    def _():
        drain(slot)


def _dispatch(h2, slots, pad_slots, n_rows):
    n, D = h2.shape
    tm = TM_ROUTE
    npad = pad_slots.shape[-1]
    steps = n // tm
    return pl.pallas_call(
        functools.partial(_dispatch_kernel, tm=tm, npad=npad, steps=steps),
        grid=(steps,),
        in_specs=[pl.BlockSpec((1, 1, 2 * tm), lambda i: (i, 0, 0), memory_space=pltpu.SMEM),
                  pl.BlockSpec((1, 1, npad), lambda i: (i, 0, 0), memory_space=pltpu.SMEM),
                  pl.BlockSpec(memory_space=pl.ANY)],
        out_specs=pl.BlockSpec(memory_space=pl.ANY),
        out_shape=jax.ShapeDtypeStruct((n_rows, D), F32),
        scratch_shapes=[pltpu.VMEM((3, tm, D), F32), pltpu.VMEM((SUBLANES, D), F32),
                        pltpu.SemaphoreType.DMA((3,)), pltpu.SemaphoreType.DMA((3,))],
        compiler_params=_cparams(("arbitrary",), 3 * tm * D * 4 + (4 << 20)),
        name="dispatch",
    )(slots, pad_slots, h2)


def _experts_kernel(te_ref, nt_ref, xs_ref, wg_ref, wu_ref, wd_ref, *rest):
    o_ref = rest[-1]
    prev_ref = rest[0] if len(rest) == 2 else None
    t = pl.program_id(0)

    @pl.when(t < nt_ref[0])
    def _():
        part = _swiglu_tile(xs_ref[...].astype(BF16), wg_ref, wu_ref, wd_ref)
        o_ref[...] = part if prev_ref is None else prev_ref[...] + part

    @pl.when(t >= nt_ref[0])
    def _():
        o_ref[...] = jnp.zeros(o_ref.shape, F32)


def _experts(xs, tile_expert, n_tiles_used, p):
    D = xs.shape[1]
    tm = TM_MOE
    n_slots = tile_expert.shape[0] * tm
    ffp = D_FF // MOE_FF_PARTS
    tok = pl.BlockSpec((tm, D), lambda t, te, nt: (t, 0))
    tok_in = pl.BlockSpec((tm, D), lambda t, te, nt: (jnp.minimum(t, nt[0] - 1), 0))
    vmem = 2 * 3 * D * ffp * p["moe_wg"].dtype.itemsize + 12 * tm * D * 4 + (6 << 20)
    ys = None
    for part in range(MOE_FF_PARTS):
        col_map = lambda t, te, nt, part=part: (te[t], 0, part)
        row_map = lambda t, te, nt, part=part: (te[t], part, 0)
        args = [tile_expert, n_tiles_used, xs, p["moe_wg"], p["moe_wu"], p["moe_wd"]]
        specs = [tok_in,
                 pl.BlockSpec((None, D, ffp), col_map),
                 pl.BlockSpec((None, D, ffp), col_map),
                 pl.BlockSpec((None, ffp, D), row_map)]
        aliases = {}
        if ys is not None:
            args.append(ys)
            specs.append(tok)
            aliases = {len(args) - 1: 0}
        ys = pl.pallas_call(
            _experts_kernel,
            grid_spec=pltpu.PrefetchScalarGridSpec(
                num_scalar_prefetch=2,
                grid=(n_slots // tm,),
                in_specs=specs,
                out_specs=tok),
            out_shape=jax.ShapeDtypeStruct((n_slots, D), F32),
            input_output_aliases=aliases,
            compiler_params=_cparams(("arbitrary",), vmem),
            name="experts%d" % part,
        )(*args)
    return ys


def _combine_kernel(slot_ref, next_ref, x1_ref, route_ref, mod_ref, ys_ref, o_ref, buf_ref, sem, *, tm, steps):
    i = pl.program_id(0)

    def gather(idx_ref, s):
        def body(r, carry):
            for k in range(2):
                _row_copy(ys_ref, idx_ref[0, 0, k * tm + r], buf_ref.at[s, k], r, sem.at[s]).start(priority=k)
            return carry
        lax.fori_loop(0, tm, body, 0, unroll=8)

    @pl.when(i == 0)
    def _():
        gather(slot_ref, 0)

    @pl.when(i + 1 < steps)
    def _():
        gather(next_ref, lax.rem(i + 1, 2))

    cur = lax.rem(i, 2)
    for k in range(2):
        pltpu.make_async_copy(ys_ref.at[pl.ds(0, tm), :], buf_ref.at[cur, k], sem.at[cur]).wait()
    route = route_ref[...]
    w1 = route[:, 2:3]
    w2 = route[:, 3:4]
    o_ref[...] = x1_ref[...] + mod_ref[5:6, :] * (w1 * buf_ref[cur, 0] + w2 * buf_ref[cur, 1])


def _combine(x1, route, mod, ys, slots):
    B, S, D = x1.shape
    tm = TM_ROUTE
    per = S // tm
    steps = B * per
    tok = lambda w: pl.BlockSpec((None, tm, w), lambda i: (i // per, i % per, 0))
    return pl.pallas_call(
        functools.partial(_combine_kernel, tm=tm, steps=steps),
        grid=(steps,),
        in_specs=[pl.BlockSpec((1, 1, 2 * tm), lambda i: (i, 0, 0), memory_space=pltpu.SMEM),
                  pl.BlockSpec((1, 1, 2 * tm), lambda i: (jnp.minimum(i + 1, steps - 1), 0, 0),
                               memory_space=pltpu.SMEM),
                  tok(D), tok(LANES), pl.BlockSpec((None, N_MOD, D), lambda i: (i // per, 0, 0)),
                  pl.BlockSpec(memory_space=pl.ANY)],
        out_specs=tok(D),
        out_shape=jax.ShapeDtypeStruct((B, S, D), F32),
        scratch_shapes=[pltpu.VMEM((2, 2, tm, D), F32), pltpu.SemaphoreType.DMA((2,))],
        compiler_params=_cparams(("arbitrary",), 12 * tm * D * 4 + (4 << 20)),
        name="combine",
    )(slots, slots, x1, route, mod, ys)


def _moe(h, x1, route, cnt, mod, p):
    B, S, D = x1.shape
    n = B * S
    sub = n // (cnt.shape[0] * cnt.shape[1])
    tmm = TM_MOE
    n_tiles = 2 * n // tmm + N_EXPERTS
    n_slots = n_tiles * tmm
    r2 = route.reshape(n, LANES)
    e = r2[:, 0:2].astype(jnp.int32)
    rank = r2[:, 4:6].astype(jnp.int32)
    counts = cnt[:, :, 0:2, 0:N_EXPERTS].astype(jnp.int32).reshape(-1, 2, N_EXPERTS)
    tile_tot = counts.sum(axis=0)
    n_e = tile_tot.sum(axis=0)
    base = jnp.cumsum(counts, axis=0) - counts
    base = base + jnp.array([0, 1], jnp.int32)[None, :, None] * tile_tot[0][None, None, :]
    tiles_e = (n_e + tmm - 1) // tmm
    pstart = (jnp.cumsum(tiles_e) - tiles_e) * tmm
    onehot = e[:, :, None] == jnp.arange(N_EXPERTS, dtype=jnp.int32)[None, None, :]
    per_tok = jnp.repeat(base + pstart[None, None, :], sub, axis=0)
    slot = jnp.sum(jnp.where(onehot, per_tok, 0), axis=-1) + rank
    slots = slot.reshape(n // TM_ROUTE, TM_ROUTE, 2).transpose(0, 2, 1).reshape(n // TM_ROUTE, 1, 2 * TM_ROUTE)
    tile_end = jnp.cumsum(tiles_e)
    n_used = tile_end[-1]
    t_idx = jnp.minimum(jnp.arange(n_tiles, dtype=jnp.int32), n_used - 1)
    tile_expert = jnp.sum(t_idx[:, None] >= tile_end[None, :], axis=-1).astype(jnp.int32)
    tile_expert = jnp.minimum(tile_expert, N_EXPERTS - 1)

    n_cand = N_EXPERTS * tmm
    cand = jnp.arange(n_cand, dtype=jnp.int32)
    pe, pq = cand // tmm, cand % tmm
    n_pad_e = tiles_e * tmm - n_e
    pad_slots = jnp.where(pq < n_pad_e[pe], pstart[pe] + n_e[pe] + pq, n_slots + cand).astype(jnp.int32)
    steps = n // TM_ROUTE
    assert n_cand % steps == 0
    pad_slots = pad_slots.reshape(steps, 1, n_cand // steps)

    xs = _dispatch(h.reshape(n, D), slots, pad_slots, n_slots + n_cand)
    ys = _experts(xs, tile_expert, n_used.reshape(1).astype(jnp.int32), p)
    return _combine(x1, route, mod, ys, slots)


def _pad_heads(w, width):
    lead = w.shape[:-1]
    w = w.reshape(lead + (N_HEADS, width))
    w = jnp.pad(w, [(0, 0)] * len(lead) + [(0, 0), (0, HEAD_PAD - width)])
    return w.reshape(lead + (N_HEADS * HEAD_PAD,))


def _layer_params(l, w_in, q_lat_g, kv_lat_g, w_uq, w_ukv, q_norm_g, k_norm_g, conv_w, conv_b, conv_ln_g,
                  conv_ln_b, w_out, ffn_norm_g):
    wi = w_in[l]
    kr = jnp.pad(wi[:, OFF_KR:OFF_CONV], ((0, 0), (QK_NOPE, HEAD_PAD - QK_HEAD)))
    ckv = wi[:, OFF_CKV:OFF_KR]
    w_full = jnp.concatenate([wi[:, OFF_CQ:OFF_CKV], ckv, wi[:, OFF_CONV:OFF_FOUR], wi[:, OFF_FOUR:IN_COLS], kr],
                             axis=1)
    w_kv = jnp.concatenate([ckv, kr], axis=1)
    ukv = w_ukv[l].reshape(KV_LORA, N_HEADS, QK_NOPE + V_HEAD)
    uk = _pad_heads(ukv[:, :, :QK_NOPE].reshape(KV_LORA, N_HEADS * QK_NOPE), QK_NOPE)
    uv = _pad_heads(ukv[:, :, QK_NOPE:].reshape(KV_LORA, ATTN_OUT), V_HEAD)
    pad_g = lambda g: jnp.pad(g, (0, HEAD_PAD - QK_HEAD)).reshape(1, HEAD_PAD)
    uq = _pad_heads(w_uq[l], QK_HEAD)
    hw = N_HEADS * HEAD_PAD
    head_of_col = np.arange(hw)[:, None] // HEAD_PAD == np.arange(LANES)[None, :]
    expand = np.concatenate([head_of_col.T, head_of_col.T], axis=0)
    return {
        "w_in_full": w_full.astype(BF16),
        "w_in_rope": jnp.concatenate([w_full, _rot_partner(kr)], axis=1).astype(BF16),
        "w_in_kv": w_kv.astype(BF16),
        "q_lat_g": q_lat_g[l].reshape(1, Q_LORA),
        "kv_lat_g": kv_lat_g[l].reshape(1, KV_LORA),
        "w_uq": uq.astype(BF16),
        "w_uq_rope": jnp.concatenate([uq, _rot_partner(uq)], axis=1).astype(BF16),
        "w_ukv": jnp.concatenate([uk, uv], axis=1).astype(BF16),
        "q_norm_g": pad_g(q_norm_g[l]) * (QK_HEAD ** -0.5 * math.log2(math.e)),
        "v_ones": jnp.tile(jnp.concatenate([jnp.zeros((V_HEAD,), F32), jnp.ones((HEAD_PAD - V_HEAD,), F32)]),
                           N_HEADS).reshape(1, hw),
        "ones_h": jnp.asarray(head_of_col, F32).astype(BF16),
        "expand_h": jnp.asarray(expand, F32).astype(BF16),
        "k_norm_g": pad_g(k_norm_g[l]),
        "conv_w": conv_w[l],
        "conv_b": conv_b[l].reshape(1, CONV_CH),
        "conv_ln_g": conv_ln_g[l].reshape(1, CONV_CH),
        "conv_ln_b": conv_ln_b[l].reshape(1, CONV_CH),
        "w_out": w_out[l].astype(BF16),
        "ffn_norm_g": ffn_norm_g[l].reshape(1, D_MODEL),
        "bd": _channel_dft(),
    }


def _rot_partner(w):
    lead = w.shape[:-1]
    half = QK_ROPE // 2
    w3 = w.reshape(lead + (-1, HEAD_PAD))
    z = jnp.zeros_like(w3)
    out = jnp.concatenate([z[..., :QK_NOPE], w3[..., QK_NOPE + half:QK_HEAD], w3[..., QK_NOPE:QK_NOPE + half],
                           z[..., QK_HEAD:]], axis=-1)
    return out.reshape(w.shape)


def _rope_tables(seq, gq, gk):
    rows = seq // GRID_W
    row = jnp.repeat(jnp.arange(rows, dtype=F32), GRID_W)
    col = jnp.tile(jnp.arange(GRID_W, dtype=F32), rows)
    n_freq = QK_ROPE // 4
    inv = ROPE_BASE ** (-jnp.arange(n_freq, dtype=F32) / n_freq)
    ang = jnp.concatenate([row[:, None] * inv, col[:, None] * inv], axis=-1)
    cos, sin = jnp.cos(ang), jnp.sin(ang)
    ones = jnp.ones((seq, QK_NOPE), F32)
    tail = jnp.zeros((seq, HEAD_PAD - QK_HEAD), F32)
    cos_t = jnp.concatenate([ones, cos, cos, tail], axis=1)
    sin_t = jnp.concatenate([jnp.zeros((seq, QK_NOPE), F32), -sin, sin, tail], axis=1)
    return (cos_t * gq, sin_t * _rot_partner(gq), cos_t * gk, sin_t * _rot_partner(gk))


def _mixer(xs, mod, p, rope_tabs, k_extra, v_extra, cs, cast=()):
    q, k, v, y, zz = _front(xs, mod, p["mix_norm_g"], p, rope_tabs, full=True)
    ks, vs = [k], [v]
    if k_extra is not None:
        ks.append(k_extra)
        vs.append(v_extra)
    attn = _attend(q, ks, vs, cast)
    casted = ()
    if cast:
        attn, casted = attn[0], attn[1:]
    conv = _conv(y, p)
    four = _fourier(zz, cs)
    return attn, conv, four, k, v, casted


def kernel(x, c, ctx, c_ctx, ada_w, ada_b, mix_norm_g, ffn_norm_g, w_in, q_lat_g, kv_lat_g, w_uq, w_ukv, q_norm_g,
           k_norm_g, conv_w, conv_b, conv_ln_g, conv_ln_b, w_out, ffn_w_gate, ffn_w_up, ffn_w_down, router_w,
           moe_w_gate, moe_w_up, moe_w_down):
    B, S, D = x.shape
    T = ctx.shape[1]
    assert (D, DEPTH) == (D_MODEL, ada_w.shape[0]) and S % GRID_W == 0

    cc = jnp.zeros((2 * SUBLANES, D), F32).at[:B].set(c).at[B].set(c_ctx)
    mods = _modulation(cc, ada_w, ada_b).reshape(DEPTH, 2 * SUBLANES, N_MOD, D)
    cs_x = _dft_tables(S)
    cs_c = _dft_tables(T)

    def moe_f32(i):
        return (moe_w_gate[i].reshape(N_EXPERTS * D, D_FF), moe_w_up[i].reshape(N_EXPERTS * D, D_FF),
                moe_w_down[i].reshape(N_EXPERTS * D_FF, D))

    moe_bf16 = {}
    for l in range(DEPTH):
        last = l == DEPTH - 1
        p = _layer_params(l, w_in, q_lat_g, kv_lat_g, w_uq, w_ukv, q_norm_g, k_norm_g, conv_w, conv_b, conv_ln_g,
                          conv_ln_b, w_out, ffn_norm_g)
        p["mix_norm_g"] = mix_norm_g[l].reshape(1, D)
        rope_tabs = _rope_tables(S, p["q_norm_g"], p["k_norm_g"])
        i = l // 2
        moe = l % 2 == 1
        nxt = l + 1 if l % 2 == 0 else l + 2
        cast = moe_f32(nxt // 2) if nxt < DEPTH else ()
        if moe:
            rw = jnp.pad(router_w[i], ((0, 0), (0, LANES - N_EXPERTS)))
            rw_hi = rw.astype(BF16)
            p["router_w"] = jnp.concatenate([rw_hi, (rw - rw_hi.astype(F32)).astype(BF16)], axis=1)
            wg, wu, wd = moe_bf16.pop(i) if i in moe_bf16 else [w.astype(BF16) for w in moe_f32(i)]
            p["moe_wg"] = wg.reshape(N_EXPERTS, D, D_FF)
            p["moe_wu"] = wu.reshape(N_EXPERTS, D, D_FF)
            p["moe_wd"] = wd.reshape(N_EXPERTS, D_FF, D)
        else:
            p["ffn_wg"] = ffn_w_gate[i].astype(BF16)
            p["ffn_wu"] = ffn_w_up[i].astype(BF16)
            p["ffn_wd"] = ffn_w_down[i].astype(BF16)
        mod_x = mods[l, :B]
        mod_c = mods[l, B:B + 1]

        def channel_mixer(attn, conv, four, xs, mod):
            if moe:
                x1, h, route, cnt = _mix_out(attn, conv, four, xs, mod, p, moe=True)
                return _moe(h, x1, route, cnt, mod if mod.shape[0] > 1 else jnp.broadcast_to(mod, (B,) + mod.shape[1:]), p)
            x1, h = _mix_out(attn, conv, four, xs, mod, p, moe=False)
            return _ffn(h, x1, mod, p)

        if last:
            k_c, v_c = _front(ctx, mod_c, p["mix_norm_g"], p, None, full=False)
        else:
            attn_c, conv_c, four_c, k_c, v_c, _ = _mixer(ctx, mod_c, p, None, None, None, cs_c)
            ctx_next = channel_mixer(attn_c, conv_c, four_c, ctx, mod_c)

        attn_x, conv_x, four_x, _, _, casted = _mixer(x, mod_x, p, rope_tabs, k_c, v_c, cs_x, cast)
        if cast:
            moe_bf16[nxt // 2] = casted
        x = channel_mixer(attn_x, conv_x, four_x, x, mod_x)
        if not last:
            ctx = ctx_next
    return x
```

```python
import functools
import math

import numpy as np
import jax
import jax.numpy as jnp
from jax import lax
from jax.experimental import pallas as pl
from jax.experimental.pallas import tpu as pltpu

F32 = jnp.float32
BF16 = jnp.bfloat16

D_MODEL = 1024
DEPTH = 2
GRID_W = 64
N_HEADS = 8
QK_NOPE = 64
QK_ROPE = 32
QK_HEAD = QK_NOPE + QK_ROPE
V_HEAD = 64
Q_LORA = 384
KV_LORA = 256
ROPE_BASE = 10000.0
CONV_CH = 256
CONV_WIDTH = 31
CONV_PAD = (CONV_WIDTH - 1) // 2
FOURIER_GROUPS = 4
FOURIER_GROUP_DIM = 64
FOURIER_CH = FOURIER_GROUPS * FOURIER_GROUP_DIM
ATTN_OUT = N_HEADS * V_HEAD
OFF_CQ = 0
OFF_CKV = OFF_CQ + Q_LORA
OFF_KR = OFF_CKV + KV_LORA
OFF_CONV = OFF_KR + QK_ROPE
OFF_FOUR = OFF_CONV + 2 * CONV_CH
IN_COLS = OFF_FOUR + FOURIER_CH
D_FF = 2816
N_EXPERTS = 8
N_MOD = 6
EPS = 1e-6

LANES = 128
SUBLANES = 8
HEAD_PAD = LANES
VMEM_CAP = 56 * 1024 * 1024
FF_CHUNK = 256

SUB_ROWS = 256
TM_FRONT = 512
TQ_ATTN = 1024
TQ_SUB = 512
TK_ATTN = 256
TM_OUT = 512
TM_FFN = 512
TM_MOE = 512
TM_ROUTE = 512
CONV_ROWS = 128
TR_FOURIER = 512


def _cparams(sem, vmem_bytes):
    return pltpu.CompilerParams(dimension_semantics=sem, vmem_limit_bytes=int(min(VMEM_CAP, vmem_bytes)))


def _rms(v, g):
    return v * lax.rsqrt(jnp.mean(v * v, axis=-1, keepdims=True) + EPS) * g


def _sigmoid(v):
    return 1.0 / (1.0 + jnp.exp(-v))


def _mod_kernel(c_ref, w_ref, b_ref, o_ref):
    c = c_ref[...]
    s = (c * _sigmoid(c)).astype(BF16)
    o_ref[...] = jnp.dot(s, w_ref[...].astype(BF16), preferred_element_type=F32) + b_ref[...]


def _modulation(cc, ada_w, ada_b):
    rows = cc.shape[0]
    tn = 1536
    n_out = N_MOD * D_MODEL
    return pl.pallas_call(
        _mod_kernel,
        grid=(DEPTH, n_out // tn),
        in_specs=[
            pl.BlockSpec((rows, D_MODEL), lambda l, j: (0, 0)),
            pl.BlockSpec((None, D_MODEL, tn), lambda l, j: (l, 0, j)),
            pl.BlockSpec((None, 1, tn), lambda l, j: (l, 0, j)),
        ],
        out_specs=pl.BlockSpec((None, rows, tn), lambda l, j: (l, 0, j)),
        out_shape=jax.ShapeDtypeStruct((DEPTH, rows, n_out), F32),
        compiler_params=_cparams(("arbitrary", "arbitrary"), 4 * D_MODEL * tn * 4),
        name="modulation",
    )(cc, ada_w, ada_b.reshape(DEPTH, 1, n_out))


def _front_kernel(*refs, full, rope):
    it = iter(refs)
    x_ref, mod_ref, g_ref, win_ref = next(it), next(it), next(it), next(it)
    if full:
        qlg_ref, wuq_ref = next(it), next(it)
    kvlg_ref, wukv_ref, vones_ref, onesh_ref, expand_ref = next(it), next(it), next(it), next(it), next(it)
    if rope:
        cq_ref, sq_ref, ck_ref, sk_ref = next(it), next(it), next(it), next(it)
    else:
        qg_ref = next(it) if full else None
        kg_ref = next(it)
    if full:
        bd_ref = next(it)
        q_out = next(it)
    k_out, v_out = next(it), next(it)
    if full:
        y_out, zz_out = next(it), next(it)

    shift = mod_ref[0:1, :]
    scale = mod_ref[1:2, :]
    hw = N_HEADS * HEAD_PAD

    def head_inv_rms(raw):
        ss = jnp.dot((raw * raw).astype(BF16), onesh_ref[...], preferred_element_type=F32)
        rs = lax.rsqrt(ss * (1.0 / QK_HEAD) + EPS)
        rs_hi = rs.astype(BF16)
        rs_lo = (rs - rs_hi.astype(F32)).astype(BF16)
        return jnp.dot(jnp.concatenate([rs_hi, rs_lo], axis=1), expand_ref[...], preferred_element_type=F32)

    def sub_block(rows):
        x = x_ref[rows, :]
        h = _rms(x, g_ref[...]) * (1.0 + scale) + shift
        cols = jnp.dot(h.astype(BF16), win_ref[...], preferred_element_type=F32)

        o = 0
        if full:
            cq = cols[:, 0:Q_LORA]
            o = Q_LORA
            qall = jnp.dot(_rms(cq, qlg_ref[...]).astype(BF16), wuq_ref[...], preferred_element_type=F32)
            rsb = head_inv_rms(qall[:, 0:hw])
            for hd in range(N_HEADS):
                sl = slice(hd * HEAD_PAD, (hd + 1) * HEAD_PAD)
                if rope:
                    val = qall[:, sl] * cq_ref[rows, :] + qall[:, hw + hd * HEAD_PAD:hw + (hd + 1) * HEAD_PAD] * \
                        sq_ref[rows, :]
                else:
                    val = qall[:, sl] * qg_ref[...]
                q_out[rows, sl] = (val * rsb[:, sl]).astype(BF16)

        ckv = cols[:, o:o + KV_LORA]
        o += KV_LORA
        kv = jnp.dot(_rms(ckv, kvlg_ref[...]).astype(BF16), wukv_ref[...], preferred_element_type=F32)
        if full:
            a = cols[:, o:o + CONV_CH]
            gt = cols[:, o + CONV_CH:o + 2 * CONV_CH]
            y_out[rows, :] = a * _sigmoid(gt)
            o += 2 * CONV_CH
            z = cols[:, o:o + FOURIER_CH]
            o += FOURIER_CH
            zz_out[rows, :] = jnp.dot(z.astype(BF16), bd_ref[...], preferred_element_type=F32).astype(BF16)
        krb = cols[:, o:o + HEAD_PAD]
        kraw = [kv[:, hd * HEAD_PAD:(hd + 1) * HEAD_PAD] + krb for hd in range(N_HEADS)]
        rsb = head_inv_rms(jnp.concatenate(kraw, axis=1))
        if rope:
            kpart = cols[:, o + HEAD_PAD:o + 2 * HEAD_PAD] * sk_ref[rows, :]
        for hd in range(N_HEADS):
            sl = slice(hd * HEAD_PAD, (hd + 1) * HEAD_PAD)
            val = kraw[hd] * ck_ref[rows, :] + kpart if rope else kraw[hd] * kg_ref[...]
            k_out[rows, sl] = (val * rsb[:, sl]).astype(BF16)
        v_out[rows, :] = (kv[:, hw:] + vones_ref[...]).astype(BF16)

    tm = x_ref.shape[0]
    sub = min(SUB_ROWS, tm)
    for sb in range(tm // sub):
        sub_block(slice(sb * sub, (sb + 1) * sub))


def _front(xs, mod, norm_g, p, rope_tabs, *, full):
    B, S, D = xs.shape
    tm = min(TM_FRONT, S)
    rope = rope_tabs is not None
    assert full or not rope
    w_in = (p["w_in_rope"] if rope else p["w_in_full"]) if full else p["w_in_kv"]
    w_uq = p["w_uq_rope"] if rope else p["w_uq"]
    ncol = w_in.shape[1]
    bm = mod.shape[0]
    mod_map = (lambda b, i: (b, 0, 0)) if bm > 1 else (lambda b, i: (0, 0, 0))
    const2 = lambda b, i: (0, 0)
    hw = N_HEADS * HEAD_PAD

    args = [xs, mod, norm_g, w_in]
    specs = [
        pl.BlockSpec((None, tm, D), lambda b, i: (b, i, 0)),
        pl.BlockSpec((None, N_MOD, D), mod_map),
        pl.BlockSpec((1, D), const2),
        pl.BlockSpec((D, ncol), const2),
    ]
    if full:
        args += [p["q_lat_g"], w_uq]
        specs += [pl.BlockSpec((1, Q_LORA), const2), pl.BlockSpec((Q_LORA, w_uq.shape[1]), const2)]
    args += [p["kv_lat_g"], p["w_ukv"], p["v_ones"], p["ones_h"], p["expand_h"]]
    specs += [pl.BlockSpec((1, KV_LORA), const2),
              pl.BlockSpec((KV_LORA, 2 * hw), const2),
              pl.BlockSpec((1, hw), const2),
              pl.BlockSpec((hw, LANES), const2),
              pl.BlockSpec((2 * LANES, hw), const2)]
    if rope:
        args += list(rope_tabs)
        specs += [pl.BlockSpec((tm, HEAD_PAD), lambda b, i: (i, 0))] * 4
    else:
        if full:
            args += [p["q_norm_g"]]
            specs += [pl.BlockSpec((1, HEAD_PAD), const2)]
        args += [p["k_norm_g"]]
        specs += [pl.BlockSpec((1, HEAD_PAD), const2)]
    if full:
        args += [p["bd"]]
        specs += [pl.BlockSpec((FOURIER_CH, 2 * FOURIER_CH), const2)]

    out_shapes, out_specs = [], []

    def add_out(width, dtype):
        out_shapes.append(jax.ShapeDtypeStruct((B, S, width), dtype))
        out_specs.append(pl.BlockSpec((None, tm, width), lambda b, i: (b, i, 0)))

    if full:
        add_out(N_HEADS * HEAD_PAD, BF16)
    add_out(N_HEADS * HEAD_PAD, BF16)
    add_out(N_HEADS * HEAD_PAD, BF16)
    if full:
        add_out(CONV_CH, F32)
        add_out(2 * FOURIER_CH, BF16)

    vmem = 2 * (D * ncol * 2 + Q_LORA * 1024 * 2 + KV_LORA * 2048 * 2) + 28 * tm * D * 4
    return pl.pallas_call(
        functools.partial(_front_kernel, full=full, rope=rope),
        grid=(B, S // tm),
        in_specs=specs,
        out_specs=out_specs,
        out_shape=out_shapes,
        compiler_params=_cparams(("arbitrary", "arbitrary"), vmem),
        name="front_full" if full else "front_kv",
    )(*args)


def _attn_kernel(*refs, nsrc, ncast):
    q_ref = refs[0]
    k_refs = refs[1:1 + nsrc]
    v_refs = refs[1 + nsrc:1 + 2 * nsrc]
    cast_in = refs[1 + 2 * nsrc:1 + 2 * nsrc + ncast]
    o_ref = refs[1 + 2 * nsrc + ncast]
    cast_out = refs[2 + 2 * nsrc + ncast:]
    chunks = []
    for k_ref, v_ref in zip(k_refs, v_refs):
        for s0 in range(0, k_ref.shape[0], TK_ATTN):
            chunks.append((k_ref, v_ref, s0, min(TK_ATTN, k_ref.shape[0] - s0)))
    tq = q_ref.shape[0]
    sub = min(TQ_SUB, tq)
    for r0 in range(0, tq, sub):
        rows = slice(r0, r0 + sub)
        state = [None, None]
        for k_ref, v_ref, s0, sz in chunks:
            for hh in range(2):
                sl = slice(hh * HEAD_PAD, (hh + 1) * HEAD_PAD)
                s = lax.dot_general(q_ref[rows, sl], k_ref[s0:s0 + sz, sl], (((1,), (1,)), ((), ())),
                                    preferred_element_type=F32)
                m = jnp.max(s, axis=-1, keepdims=True)
                if state[hh] is not None:
                    m_old, acc_old = state[hh]
                    m = jnp.maximum(m_old, m)
                pv = jnp.dot(jnp.exp2((s - m).astype(BF16)), v_ref[s0:s0 + sz, sl], preferred_element_type=F32)
                if state[hh] is not None:
                    pv = pv + jnp.exp2(m_old - m) * acc_old
                state[hh] = (m, pv)
        accs = [state[0][1], state[1][1]]
        lane = lax.broadcasted_iota(jnp.int32, accs[0].shape, 1)
        lo = accs[0] / pltpu.roll(accs[0], V_HEAD, 1)
        hi = pltpu.roll(accs[1], V_HEAD, 1) / accs[1]
        o_ref[rows, :] = jnp.where(lane < V_HEAD, lo, hi).astype(BF16)
    for src, dst in zip(cast_in, cast_out):
        dst[...] = src[...].astype(BF16)


def _attend(q, ks, vs, cast=()):
    B, S, _ = q.shape
    tq = min(TQ_ATTN, S)
    nsrc = len(ks)
    nq = S // tq
    steps = B * (N_HEADS // 2) * nq
    specs = [pl.BlockSpec((None, tq, 2 * HEAD_PAD), lambda b, hp, i: (b, i, hp))]
    for kv in list(ks) + list(vs):
        specs.append(pl.BlockSpec((None, kv.shape[1], 2 * HEAD_PAD), lambda b, hp, i: (b, 0, hp)))
    out_specs = [pl.BlockSpec((None, tq, 2 * V_HEAD), lambda b, hp, i: (b, i, hp))]
    out_shapes = [jax.ShapeDtypeStruct((B, S, ATTN_OUT), BF16)]
    cast_bytes = 0
    for w in cast:
        rows, cols = w.shape
        assert rows % (steps * 2 * SUBLANES) == 0
        spec = pl.BlockSpec((rows // steps, cols), lambda b, hp, i: ((b * (N_HEADS // 2) + hp) * nq + i, 0))
        specs.append(spec)
        out_specs.append(spec)
        out_shapes.append(jax.ShapeDtypeStruct(w.shape, BF16))
        cast_bytes += 2 * (rows // steps) * cols * 6
    t_all = sum(k.shape[1] for k in ks)
    vmem = 2 * (t_all * 512 * 2) * 2 + 16 * min(tq, TQ_SUB) * TK_ATTN * 4 + cast_bytes + (12 << 20)
    outs = pl.pallas_call(
        functools.partial(_attn_kernel, nsrc=nsrc, ncast=len(cast)),
        grid=(B, N_HEADS // 2, nq),
        in_specs=specs,
        out_specs=out_specs,
        out_shape=out_shapes,
        compiler_params=_cparams(("arbitrary", "arbitrary", "arbitrary"), vmem),
        name="attend%d" % nsrc,
    )(q, *ks, *vs, *cast)
    return outs[0] if not cast else outs


def _conv_kernel(y_ref, w_ref, b_ref, lg_ref, lb_ref, o_ref, pad_ref, *, seq):
    halo = 2 * SUBLANES
    total = seq + 2 * halo
    pad_ref[0, 0:halo, :] = jnp.zeros((halo, CONV_CH), F32)
    pad_ref[0, halo + seq:total, :] = jnp.zeros((halo, CONV_CH), F32)
    pad_ref[0, halo:halo + seq, :] = y_ref[...]
    for ph in range(1, SUBLANES):
        pad_ref[ph, 0:total - SUBLANES, :] = pad_ref[0, ph:ph + total - SUBLANES, :]
    rows = CONV_ROWS
    for base in range(0, seq, rows):
        acc = jnp.zeros((rows, CONV_CH), F32)
        for j in range(CONV_WIDTH):
            off = base + halo - CONV_PAD + j
            ph = off % SUBLANES
            acc = acc + pad_ref[ph, off - ph:off - ph + rows, :] * w_ref[j:j + 1, :]
        acc = acc + b_ref[...]
        mu = jnp.mean(acc, axis=-1, keepdims=True)
        cen = acc - mu
        var = jnp.mean(cen * cen, axis=-1, keepdims=True)
        yn = cen * lax.rsqrt(var + EPS) * lg_ref[...] + lb_ref[...]
        o_ref[base:base + rows, :] = (yn * _sigmoid(yn)).astype(BF16)


def _conv(y, p):
    B, S, _ = y.shape
    c2 = lambda b: (0, 0)
    return pl.pallas_call(
        functools.partial(_conv_kernel, seq=S),
        grid=(B,),
        in_specs=[pl.BlockSpec((None, S, CONV_CH), lambda b: (b, 0, 0)),
                  pl.BlockSpec((CONV_WIDTH, CONV_CH), c2),
                  pl.BlockSpec((1, CONV_CH), c2), pl.BlockSpec((1, CONV_CH), c2), pl.BlockSpec((1, CONV_CH), c2)],
        out_specs=pl.BlockSpec((None, S, CONV_CH), lambda b: (b, 0, 0)),
        out_shape=jax.ShapeDtypeStruct((B, S, CONV_CH), BF16),
        scratch_shapes=[pltpu.VMEM((SUBLANES, S + 4 * SUBLANES, CONV_CH), F32)],
        compiler_params=_cparams(("arbitrary",), (SUBLANES + 6) * (S + 32) * CONV_CH * 4 + (8 << 20)),
        name="conv",
    )(y, p["conv_w"], p["conv_b"], p["conv_ln_g"], p["conv_ln_b"])


def _fourier_kernel(cs_ref, zz_ref, o_ref, *, seq):
    acc = jnp.dot(cs_ref[:, 0:seq], zz_ref[:, 0:FOURIER_CH], preferred_element_type=F32)
    acc = acc + jnp.dot(cs_ref[:, seq:2 * seq], zz_ref[:, FOURIER_CH:2 * FOURIER_CH], preferred_element_type=F32)
    o_ref[...] = acc.astype(BF16)


def _fourier(zz, cs):
    B, S, _ = zz.shape
    tr = min(TR_FOURIER, S)
    return pl.pallas_call(
        functools.partial(_fourier_kernel, seq=S),
        grid=(S // tr, B),
        in_specs=[pl.BlockSpec((tr, 2 * S), lambda r, b: (r, 0)),
                  pl.BlockSpec((None, S, 2 * FOURIER_CH), lambda r, b: (b, 0, 0))],
        out_specs=pl.BlockSpec((None, tr, FOURIER_CH), lambda r, b: (b, r, 0)),
        out_shape=jax.ShapeDtypeStruct((B, S, FOURIER_CH), BF16),
        compiler_params=_cparams(("arbitrary", "arbitrary"), 2 * (tr * 2 * S * 2 + S * 512 * 2) + (8 << 20)),
        name="fourier",
    )(cs, zz)


def _dft_tables(seq):
    norm = seq ** -0.5
    if seq <= 256:
        ks = (np.arange(seq)[:, None] * np.arange(seq)[None, :]) % seq
        ang = 2.0 * np.pi * ks / seq
        tab = np.concatenate([np.cos(ang), -np.sin(ang)], axis=1) * norm
        return jnp.asarray(tab, F32).astype(BF16)
    inner = 64
    outer = seq // inner
    k = np.arange(seq)[:, None]
    ang_a = 2.0 * np.pi * ((k * np.arange(outer)[None, :]) % outer) / outer
    ang_b = 2.0 * np.pi * ((k * np.arange(inner)[None, :]) % seq) / seq
    ca, sa = jnp.asarray(np.cos(ang_a) * norm, F32)[:, :, None], jnp.asarray(np.sin(ang_a) * norm, F32)[:, :, None]
    cb, sb = jnp.asarray(np.cos(ang_b), F32)[:, None, :], jnp.asarray(np.sin(ang_b), F32)[:, None, :]
    cos = (ca * cb - sa * sb).reshape(seq, seq)
    sin = (sa * cb + ca * sb).reshape(seq, seq)
    return jnp.concatenate([cos, -sin], axis=1).astype(BF16)


def _channel_dft():
    n = FOURIER_GROUP_DIM
    ang = 2.0 * np.pi * ((np.arange(n)[:, None] * np.arange(n)[None, :]) % n) / n
    eye = np.eye(FOURIER_GROUPS)
    bd = np.concatenate([np.kron(eye, np.cos(ang)), np.kron(eye, np.sin(ang))], axis=1) * n ** -0.5
    return jnp.asarray(bd, F32).astype(BF16)


def _mixout_kernel(*refs, moe):
    if moe:
        (attn_ref, conv_ref, four_ref, x_ref, mod_ref, wout_ref, g_ref, rw_ref, tri_ref,
         x1_out, h_out, route_out, cnt_out) = refs
    else:
        attn_ref, conv_ref, four_ref, x_ref, mod_ref, wout_ref, g_ref, x1_out, h_out = refs
    tm = x_ref.shape[0]
    sub = min(SUB_ROWS, tm)
    for sb in range(tm // sub):
        rows = slice(sb * sub, (sb + 1) * sub)
        mix = jnp.dot(attn_ref[rows, :], wout_ref[0:ATTN_OUT, :], preferred_element_type=F32)
        mix = mix + jnp.dot(conv_ref[rows, :], wout_ref[ATTN_OUT:ATTN_OUT + CONV_CH, :],
                            preferred_element_type=F32)
        mix = mix + jnp.dot(four_ref[rows, :], wout_ref[ATTN_OUT + CONV_CH:, :], preferred_element_type=F32)
        x1 = x_ref[rows, :] + mod_ref[2:3, :] * mix
        x1_out[rows, :] = x1
        h = _rms(x1, g_ref[...]) * (1.0 + mod_ref[4:5, :]) + mod_ref[3:4, :]
        if moe:
            _store_token_tiles(h_out, sb * sub, h)
            route, cnt = _route(h, rw_ref, tri_ref)
            route_out[rows, :] = route
            cnt_out[sb] = cnt
        else:
            h_out[rows, :] = h.astype(BF16)


def _store_token_tiles(ref, row0, val):
    rows = val.shape[0]
    for j in range(SUBLANES):
        ref[pl.ds(row0 * SUBLANES + j, rows, stride=SUBLANES), :] = val[:, j * LANES:(j + 1) * LANES]


def _load_token_tiles(ref, row0, rows, lead=()):
    idx = tuple(lead)
    return jnp.concatenate([ref[idx + (pl.ds(row0 * SUBLANES + j, rows, stride=SUBLANES), slice(None))]
                            for j in range(SUBLANES)], axis=1)


def _route(h, rw_ref, tri_ref):
    tm = h.shape[0]
    h_hi = h.astype(BF16)
    h_lo = (h - h_hi.astype(F32)).astype(BF16)
    hw = jnp.dot(h_hi, rw_ref[...], preferred_element_type=F32)
    logits = hw[:, 0:LANES] + hw[:, LANES:] + jnp.dot(h_lo, rw_ref[:, 0:LANES], preferred_element_type=F32)
    lane = lax.broadcasted_iota(jnp.int32, (tm, LANES), 1).astype(F32)
    neg = jnp.float32(-jnp.inf)
    lm = jnp.where(lane < N_EXPERTS, logits, neg)
    m1 = jnp.max(lm, axis=-1, keepdims=True)
    i1 = jnp.min(jnp.where(lm == m1, lane, float(LANES)), axis=-1, keepdims=True)
    lm2 = jnp.where(lane == i1, neg, lm)
    m2 = jnp.max(lm2, axis=-1, keepdims=True)
    i2 = jnp.min(jnp.where(lm2 == m2, lane, float(LANES)), axis=-1, keepdims=True)
    t = jnp.exp(m2 - m1)
    w1 = 1.0 / (1.0 + t)
    w2 = t / (1.0 + t)
    oh1 = lane == i1
    oh2 = lane == i2
    tri = tri_ref[...]
    c1 = jnp.dot(tri, jnp.where(oh1, 1.0, 0.0).astype(BF16), preferred_element_type=F32)
    c2 = jnp.dot(tri, jnp.where(oh2, 1.0, 0.0).astype(BF16), preferred_element_type=F32)
    r1 = jnp.sum(jnp.where(oh1, c1, 0.0), axis=-1, keepdims=True)
    r2 = jnp.sum(jnp.where(oh2, c2, 0.0), axis=-1, keepdims=True)
    vals = [i1, i2, w1, w2, r1, r2]
    route = jnp.zeros((tm, LANES), F32)
    for idx, v in enumerate(vals):
        route = jnp.where(lane == idx, v, route)
    n1 = jnp.sum(jnp.where(oh1, 1.0, 0.0), axis=0, keepdims=True)
    n2 = jnp.sum(jnp.where(oh2, 1.0, 0.0), axis=0, keepdims=True)
    row = lax.broadcasted_iota(jnp.int32, (SUBLANES, LANES), 0)
    return route, jnp.where(row == 0, n1, jnp.where(row == 1, n2, 0.0))


def _mix_out(attn, conv, four, xs, mod, p, *, moe):
    B, S, D = xs.shape
    tm = min(TM_OUT, S)
    sub = min(SUB_ROWS, tm)
    bm = mod.shape[0]
    mod_map = (lambda b, i: (b, 0, 0)) if bm > 1 else (lambda b, i: (0, 0, 0))
    c2 = lambda b, i: (0, 0)
    tok = lambda w: pl.BlockSpec((None, tm, w), lambda b, i: (b, i, 0))
    args = [attn, conv, four, xs, mod, p["w_out"], p["ffn_norm_g"]]
    specs = [tok(ATTN_OUT), tok(CONV_CH), tok(FOURIER_CH), tok(D), pl.BlockSpec((None, N_MOD, D), mod_map),
             pl.BlockSpec((D, D), c2), pl.BlockSpec((1, D), c2)]
    out_shapes = [jax.ShapeDtypeStruct((B, S, D), F32), jax.ShapeDtypeStruct((B, S, D), BF16)]
    out_specs = [tok(D), tok(D)]
    if moe:
        per = S // tm
        out_shapes[1] = jax.ShapeDtypeStruct((B * S * SUBLANES, LANES), F32)
        out_specs[1] = pl.BlockSpec((tm * SUBLANES, LANES), lambda b, i: (b * per + i, 0))
        tri = jnp.asarray(np.tril(np.ones((sub, sub), np.float32), -1), BF16)
        args += [p["router_w"], tri]
        specs += [pl.BlockSpec((D, 2 * LANES), c2), pl.BlockSpec((sub, sub), c2)]
        out_shapes += [jax.ShapeDtypeStruct((B, S, LANES), F32),
                       jax.ShapeDtypeStruct((B, S // sub, SUBLANES, LANES), F32)]
        out_specs += [tok(LANES), pl.BlockSpec((None, tm // sub, SUBLANES, LANES), lambda b, i: (b, i, 0, 0))]
    return pl.pallas_call(
        functools.partial(_mixout_kernel, moe=moe),
        grid=(B, S // tm),
        in_specs=specs,
        out_specs=out_specs,
        out_shape=out_shapes,
        compiler_params=_cparams(("arbitrary", "arbitrary"), 2 * D * D * 2 + 24 * tm * D * 4 + (8 << 20)),
        name="mix_out_moe" if moe else "mix_out",
    )(*args)


def _swiglu_tile(hb, wg_ref, wu_ref, wd_ref):
    ff = wg_ref.shape[1]
    acc = jnp.zeros((hb.shape[0], D_MODEL), F32)
    for c0 in range(0, ff, FF_CHUNK):
        sl = slice(c0, min(c0 + FF_CHUNK, ff))
        g = jnp.dot(hb, wg_ref[:, sl].astype(BF16), preferred_element_type=F32)
        u = jnp.dot(hb, wu_ref[:, sl].astype(BF16), preferred_element_type=F32)
        a = (g * _sigmoid(g) * u).astype(BF16)
        acc = acc + jnp.dot(a, wd_ref[sl, :].astype(BF16), preferred_element_type=F32)
    return acc


def _ffn_kernel(h_ref, x1_ref, mod_ref, wg_ref, wu_ref, wd_ref, o_ref):
    acc = _swiglu_tile(h_ref[...], wg_ref, wu_ref, wd_ref)
    o_ref[...] = x1_ref[...] + mod_ref[5:6, :] * acc


def _ffn(h, x1, mod, p):
    B, S, D = x1.shape
    tm = min(TM_FFN, S)
    bm = mod.shape[0]
    mod_map = (lambda b, i: (b, 0, 0)) if bm > 1 else (lambda b, i: (0, 0, 0))
    c2 = lambda b, i: (0, 0)
    tok = pl.BlockSpec((None, tm, D), lambda b, i: (b, i, 0))
    vmem = 2 * 3 * D * D_FF * 2 + 12 * tm * D * 4 + (6 << 20)
    return pl.pallas_call(
        _ffn_kernel,
        grid=(B, S // tm),
        in_specs=[tok, tok, pl.BlockSpec((None, N_MOD, D), mod_map),
                  pl.BlockSpec((D, D_FF), c2), pl.BlockSpec((D, D_FF), c2), pl.BlockSpec((D_FF, D), c2)],
        out_specs=tok,
        out_shape=jax.ShapeDtypeStruct((B, S, D), F32),
        compiler_params=_cparams(("arbitrary", "arbitrary"), vmem),
        name="ffn",
    )(h, x1, mod, p["ffn_wg"], p["ffn_wu"], p["ffn_wd"])


def _token_copy(src_ref, src_row, dst_ref, dst_row, sem):
    return pltpu.make_async_copy(src_ref.at[pl.ds(pl.multiple_of(src_row, SUBLANES), SUBLANES), :],
                                 dst_ref.at[pl.ds(pl.multiple_of(dst_row, SUBLANES), SUBLANES), :], sem)


def _dispatch_kernel(slot_ref, pad_ref, h_ref, xs_ref, hbuf, zero_ref, in_sem, out_sem, *, tm, npad, steps):
    i = pl.program_id(0)

    tr = tm * SUBLANES

    def fetch(step, slot):
        return pltpu.make_async_copy(h_ref.at[pl.ds(step * tr, tr), :], hbuf.at[slot], in_sem.at[slot])

    def drain(slot):
        for _ in range(2):
            pltpu.make_async_copy(hbuf.at[slot], xs_ref.at[pl.ds(0, tr), :], out_sem.at[slot]).wait()
        pltpu.make_async_copy(hbuf.at[slot, pl.ds(0, npad * SUBLANES), :], xs_ref.at[pl.ds(0, npad * SUBLANES), :],
                              out_sem.at[slot]).wait()

    @pl.when(i == 0)
    def _():
        zero_ref[...] = jnp.zeros(zero_ref.shape, F32)
        fetch(0, 0).start()

    slot = lax.rem(i, 3)
    fetch(i, slot).wait()

    @pl.when(i + 1 < steps)
    def _():
        fetch(i + 1, lax.rem(i + 1, 3)).start()

    def scatter(r, carry):
        for k in range(2):
            _token_copy(hbuf.at[slot], r * SUBLANES, xs_ref, slot_ref[0, 0, k * tm + r],
                        out_sem.at[slot]).start(priority=k)
        return carry

    def scatter_pad(j, carry):
        _token_copy(zero_ref, 0, xs_ref, pad_ref[0, 0, j], out_sem.at[slot]).start()
        return carry

    lax.fori_loop(0, tm, scatter, 0, unroll=8)
    lax.fori_loop(0, npad, scatter_pad, 0, unroll=8)

    @pl.when(i > 0)
    def _():
        drain(lax.rem(i + 2, 3))

    @pl.when(i == steps - 1)
    def _():
        drain(slot)


def _dispatch(h_tiles, slots, pad_slots, n_tokens_out):
    n = h_tiles.shape[0] // SUBLANES
    tm = TM_ROUTE
    npad = pad_slots.shape[-1]
    steps = n // tm
    return pl.pallas_call(
        functools.partial(_dispatch_kernel, tm=tm, npad=npad, steps=steps),
        grid=(steps,),
        in_specs=[pl.BlockSpec((1, 1, 2 * tm), lambda i: (i, 0, 0), memory_space=pltpu.SMEM),
                  pl.BlockSpec((1, 1, npad), lambda i: (i, 0, 0), memory_space=pltpu.SMEM),
                  pl.BlockSpec(memory_space=pl.ANY)],
        out_specs=pl.BlockSpec(memory_space=pl.ANY),
        out_shape=jax.ShapeDtypeStruct((n_tokens_out * SUBLANES, LANES), F32),
        scratch_shapes=[pltpu.VMEM((3, tm * SUBLANES, LANES), F32), pltpu.VMEM((SUBLANES, LANES), F32),
                        pltpu.SemaphoreType.DMA((3,)), pltpu.SemaphoreType.DMA((3,))],
        compiler_params=_cparams(("arbitrary",), 3 * tm * D_MODEL * 4 + (4 << 20)),
        name="dispatch",
    )(slots, pad_slots, h_tiles)


def _experts_kernel(te_ref, nt_ref, xs_ref, wg_ref, wu_ref, wd_ref, o_ref, *, tm):
    t = pl.program_id(0)

    @pl.when(t < nt_ref[0])
    def _():
        hb = _load_token_tiles(xs_ref, 0, tm).astype(BF16)
        _store_token_tiles(o_ref, 0, _swiglu_tile(hb, wg_ref, wu_ref, wd_ref))

    @pl.when(t >= nt_ref[0])
    def _():
        o_ref[...] = jnp.zeros(o_ref.shape, F32)


def _experts(xs, tile_expert, n_tiles_used, p):
    D = D_MODEL
    tm = TM_MOE
    n_tiles = tile_expert.shape[0]
    wmap = lambda t, te, nt: (te[t], 0, 0)
    tok = pl.BlockSpec((tm * SUBLANES, LANES), lambda t, te, nt: (t, 0))
    tok_in = pl.BlockSpec((tm * SUBLANES, LANES), lambda t, te, nt: (jnp.minimum(t, nt[0] - 1), 0))
    vmem = 2 * 3 * D * D_FF * p["moe_wg"].dtype.itemsize + 12 * tm * D * 4 + (6 << 20)
    return pl.pallas_call(
        functools.partial(_experts_kernel, tm=tm),
        grid_spec=pltpu.PrefetchScalarGridSpec(
            num_scalar_prefetch=2,
            grid=(n_tiles,),
            in_specs=[tok_in,
                      pl.BlockSpec((None, D, D_FF), wmap),
                      pl.BlockSpec((None, D, D_FF), wmap),
                      pl.BlockSpec((None, D_FF, D), wmap)],
            out_specs=tok),
        out_shape=jax.ShapeDtypeStruct((n_tiles * tm * SUBLANES, LANES), F32),
        compiler_params=_cparams(("arbitrary",), vmem),
        name="experts",
    )(tile_expert, n_tiles_used, xs, p["moe_wg"], p["moe_wu"], p["moe_wd"])


def _combine_kernel(slot_ref, next_ref, x1_ref, route_ref, mod_ref, ys_ref, o_ref, buf_ref, sem, *, tm, steps):
    i = pl.program_id(0)

    def gather(idx_ref, s):
        def body(r, carry):
            for k in range(2):
                _token_copy(ys_ref, idx_ref[0, 0, k * tm + r], buf_ref.at[s, k], r * SUBLANES,
                            sem.at[s]).start(priority=k)
            return carry
        lax.fori_loop(0, tm, body, 0, unroll=8)

    @pl.when(i == 0)
    def _():
        gather(slot_ref, 0)

    @pl.when(i + 1 < steps)
    def _():
        gather(next_ref, lax.rem(i + 1, 2))

    cur = lax.rem(i, 2)
    for k in range(2):
        pltpu.make_async_copy(ys_ref.at[pl.ds(0, tm * SUBLANES), :], buf_ref.at[cur, k], sem.at[cur]).wait()
    route = route_ref[...]
    w1 = route[:, 2:3]
    w2 = route[:, 3:4]
    y0 = _load_token_tiles(buf_ref, 0, tm, lead=(cur, 0))
    y1 = _load_token_tiles(buf_ref, 0, tm, lead=(cur, 1))
    o_ref[...] = x1_ref[...] + mod_ref[5:6, :] * (w1 * y0 + w2 * y1)


def _combine(x1, route, mod, ys, slots):
    B, S, D = x1.shape
    tm = TM_ROUTE
    per = S // tm
    steps = B * per
    tok = lambda w: pl.BlockSpec((None, tm, w), lambda i: (i // per, i % per, 0))
    return pl.pallas_call(
        functools.partial(_combine_kernel, tm=tm, steps=steps),
        grid=(steps,),
        in_specs=[pl.BlockSpec((1, 1, 2 * tm), lambda i: (i, 0, 0), memory_space=pltpu.SMEM),
                  pl.BlockSpec((1, 1, 2 * tm), lambda i: (jnp.minimum(i + 1, steps - 1), 0, 0),
                               memory_space=pltpu.SMEM),
                  tok(D), tok(LANES), pl.BlockSpec((None, N_MOD, D), lambda i: (i // per, 0, 0)),
                  pl.BlockSpec(memory_space=pl.ANY)],
        out_specs=tok(D),
        out_shape=jax.ShapeDtypeStruct((B, S, D), F32),
        scratch_shapes=[pltpu.VMEM((2, 2, tm * SUBLANES, LANES), F32), pltpu.SemaphoreType.DMA((2,))],
        compiler_params=_cparams(("arbitrary",), 12 * tm * D * 4 + (4 << 20)),
        name="combine",
    )(slots, slots, x1, route, mod, ys)


def _moe(h, x1, route, cnt, mod, p):
    B, S, D = x1.shape
    n = B * S
    sub = n // (cnt.shape[0] * cnt.shape[1])
    tmm = TM_MOE
    n_tiles = 2 * n // tmm + N_EXPERTS
    n_slots = n_tiles * tmm
    r2 = route.reshape(n, LANES)
    e = r2[:, 0:2].astype(jnp.int32)
    rank = r2[:, 4:6].astype(jnp.int32)
    counts = cnt[:, :, 0:2, 0:N_EXPERTS].astype(jnp.int32).reshape(-1, 2, N_EXPERTS)
    tile_tot = counts.sum(axis=0)
    n_e = tile_tot.sum(axis=0)
    base = jnp.cumsum(counts, axis=0) - counts
    base = base + jnp.array([0, 1], jnp.int32)[None, :, None] * tile_tot[0][None, None, :]
    tiles_e = (n_e + tmm - 1) // tmm
    pstart = (jnp.cumsum(tiles_e) - tiles_e) * tmm
    onehot = e[:, :, None] == jnp.arange(N_EXPERTS, dtype=jnp.int32)[None, None, :]
    per_tok = jnp.repeat(base + pstart[None, None, :], sub, axis=0)
    slot = jnp.sum(jnp.where(onehot, per_tok, 0), axis=-1) + rank
    slot = slot * SUBLANES
    slots = slot.reshape(n // TM_ROUTE, TM_ROUTE, 2).transpose(0, 2, 1).reshape(n // TM_ROUTE, 1, 2 * TM_ROUTE)
    tile_end = jnp.cumsum(tiles_e)
    n_used = tile_end[-1]
    t_idx = jnp.minimum(jnp.arange(n_tiles, dtype=jnp.int32), n_used - 1)
    tile_expert = jnp.sum(t_idx[:, None] >= tile_end[None, :], axis=-1).astype(jnp.int32)
    tile_expert = jnp.minimum(tile_expert, N_EXPERTS - 1)

    n_cand = N_EXPERTS * tmm
    cand = jnp.arange(n_cand, dtype=jnp.int32)
    pe, pq = cand // tmm, cand % tmm
    n_pad_e = tiles_e * tmm - n_e
    pad_slots = jnp.where(pq < n_pad_e[pe], pstart[pe] + n_e[pe] + pq, n_slots + cand).astype(jnp.int32)
    steps = n // TM_ROUTE
    assert n_cand % steps == 0
    pad_slots = (pad_slots * SUBLANES).reshape(steps, 1, n_cand // steps)

    xs = _dispatch(h, slots, pad_slots, n_slots + n_cand)
    ys = _experts(xs, tile_expert, n_used.reshape(1).astype(jnp.int32), p)
    return _combine(x1, route, mod, ys, slots)


def _pad_heads(w, width):
    lead = w.shape[:-1]
    w = w.reshape(lead + (N_HEADS, width))
    w = jnp.pad(w, [(0, 0)] * len(lead) + [(0, 0), (0, HEAD_PAD - width)])
    return w.reshape(lead + (N_HEADS * HEAD_PAD,))


def _layer_params(l, w_in, q_lat_g, kv_lat_g, w_uq, w_ukv, q_norm_g, k_norm_g, conv_w, conv_b, conv_ln_g,
                  conv_ln_b, w_out, ffn_norm_g):
    wi = w_in[l]
    kr = jnp.pad(wi[:, OFF_KR:OFF_CONV], ((0, 0), (QK_NOPE, HEAD_PAD - QK_HEAD)))
    ckv = wi[:, OFF_CKV:OFF_KR]
    w_full = jnp.concatenate([wi[:, OFF_CQ:OFF_CKV], ckv, wi[:, OFF_CONV:OFF_FOUR], wi[:, OFF_FOUR:IN_COLS], kr],
                             axis=1)
    w_kv = jnp.concatenate([ckv, kr], axis=1)
    ukv = w_ukv[l].reshape(KV_LORA, N_HEADS, QK_NOPE + V_HEAD)
    uk = _pad_heads(ukv[:, :, :QK_NOPE].reshape(KV_LORA, N_HEADS * QK_NOPE), QK_NOPE)
    uv = _pad_heads(ukv[:, :, QK_NOPE:].reshape(KV_LORA, ATTN_OUT), V_HEAD)
    pad_g = lambda g: jnp.pad(g, (0, HEAD_PAD - QK_HEAD)).reshape(1, HEAD_PAD)
    uq = _pad_heads(w_uq[l], QK_HEAD)
    hw = N_HEADS * HEAD_PAD
    head_of_col = np.arange(hw)[:, None] // HEAD_PAD == np.arange(LANES)[None, :]
    expand = np.concatenate([head_of_col.T, head_of_col.T], axis=0)
    return {
        "w_in_full": w_full.astype(BF16),
        "w_in_rope": jnp.concatenate([w_full, _rot_partner(kr)], axis=1).astype(BF16),
        "w_in_kv": w_kv.astype(BF16),
        "q_lat_g": q_lat_g[l].reshape(1, Q_LORA),
        "kv_lat_g": kv_lat_g[l].reshape(1, KV_LORA),
        "w_uq": uq.astype(BF16),
        "w_uq_rope": jnp.concatenate([uq, _rot_partner(uq)], axis=1).astype(BF16),
        "w_ukv": jnp.concatenate([uk, uv], axis=1).astype(BF16),
        "q_norm_g": pad_g(q_norm_g[l]) * (QK_HEAD ** -0.5 * math.log2(math.e)),
        "v_ones": jnp.tile(jnp.concatenate([jnp.zeros((V_HEAD,), F32), jnp.ones((HEAD_PAD - V_HEAD,), F32)]),
                           N_HEADS).reshape(1, hw),
        "ones_h": jnp.asarray(head_of_col, F32).astype(BF16),
        "expand_h": jnp.asarray(expand, F32).astype(BF16),
        "k_norm_g": pad_g(k_norm_g[l]),
        "conv_w": conv_w[l],
        "conv_b": conv_b[l].reshape(1, CONV_CH),
        "conv_ln_g": conv_ln_g[l].reshape(1, CONV_CH),
        "conv_ln_b": conv_ln_b[l].reshape(1, CONV_CH),
        "w_out": w_out[l].astype(BF16),
        "ffn_norm_g": ffn_norm_g[l].reshape(1, D_MODEL),
        "bd": _channel_dft(),
    }


def _rot_partner(w):
    lead = w.shape[:-1]
    half = QK_ROPE // 2
    w3 = w.reshape(lead + (-1, HEAD_PAD))
    z = jnp.zeros_like(w3)
    out = jnp.concatenate([z[..., :QK_NOPE], w3[..., QK_NOPE + half:QK_HEAD], w3[..., QK_NOPE:QK_NOPE + half],
                           z[..., QK_HEAD:]], axis=-1)
    return out.reshape(w.shape)


def _rope_tables(seq, gq, gk):
    rows = seq // GRID_W
    row = jnp.repeat(jnp.arange(rows, dtype=F32), GRID_W)
    col = jnp.tile(jnp.arange(GRID_W, dtype=F32), rows)
    n_freq = QK_ROPE // 4
    inv = ROPE_BASE ** (-jnp.arange(n_freq, dtype=F32) / n_freq)
    ang = jnp.concatenate([row[:, None] * inv, col[:, None] * inv], axis=-1)
    cos, sin = jnp.cos(ang), jnp.sin(ang)
    ones = jnp.ones((seq, QK_NOPE), F32)
    tail = jnp.zeros((seq, HEAD_PAD - QK_HEAD), F32)
    cos_t = jnp.concatenate([ones, cos, cos, tail], axis=1)
    sin_t = jnp.concatenate([jnp.zeros((seq, QK_NOPE), F32), -sin, sin, tail], axis=1)
    return (cos_t * gq, sin_t * _rot_partner(gq), cos_t * gk, sin_t * _rot_partner(gk))


def _mixer(xs, mod, p, rope_tabs, k_extra, v_extra, cs, cast=()):
    q, k, v, y, zz = _front(xs, mod, p["mix_norm_g"], p, rope_tabs, full=True)
    ks, vs = [k], [v]
    if k_extra is not None:
        ks.append(k_extra)
        vs.append(v_extra)
    attn = _attend(q, ks, vs, cast)
    casted = ()
    if cast:
        attn, casted = attn[0], attn[1:]
    conv = _conv(y, p)
    four = _fourier(zz, cs)
    return attn, conv, four, k, v, casted


def kernel(x, c, ctx, c_ctx, ada_w, ada_b, mix_norm_g, ffn_norm_g, w_in, q_lat_g, kv_lat_g, w_uq, w_ukv, q_norm_g,
           k_norm_g, conv_w, conv_b, conv_ln_g, conv_ln_b, w_out, ffn_w_gate, ffn_w_up, ffn_w_down, router_w,
           moe_w_gate, moe_w_up, moe_w_down):
    B, S, D = x.shape
    T = ctx.shape[1]
    assert (D, DEPTH) == (D_MODEL, ada_w.shape[0]) and S % GRID_W == 0

    cc = jnp.zeros((2 * SUBLANES, D), F32).at[:B].set(c).at[B].set(c_ctx)
    mods = _modulation(cc, ada_w, ada_b).reshape(DEPTH, 2 * SUBLANES, N_MOD, D)
    cs_x = _dft_tables(S)
    cs_c = _dft_tables(T)

    def moe_f32(i):
        return (moe_w_gate[i].reshape(N_EXPERTS * D, D_FF), moe_w_up[i].reshape(N_EXPERTS * D, D_FF),
                moe_w_down[i].reshape(N_EXPERTS * D_FF, D))

    moe_bf16 = {}
    for l in range(DEPTH):
        last = l == DEPTH - 1
        p = _layer_params(l, w_in, q_lat_g, kv_lat_g, w_uq, w_ukv, q_norm_g, k_norm_g, conv_w, conv_b, conv_ln_g,
                          conv_ln_b, w_out, ffn_norm_g)
        p["mix_norm_g"] = mix_norm_g[l].reshape(1, D)
        rope_tabs = _rope_tables(S, p["q_norm_g"], p["k_norm_g"])
        i = l // 2
        moe = l % 2 == 1
        nxt = l + 1 if l % 2 == 0 else l + 2
        cast = moe_f32(nxt // 2) if nxt < DEPTH else ()
        if moe:
            rw = jnp.pad(router_w[i], ((0, 0), (0, LANES - N_EXPERTS)))
            rw_hi = rw.astype(BF16)
            p["router_w"] = jnp.concatenate([rw_hi, (rw - rw_hi.astype(F32)).astype(BF16)], axis=1)
            wg, wu, wd = moe_bf16.pop(i) if i in moe_bf16 else [w.astype(BF16) for w in moe_f32(i)]
            p["moe_wg"] = wg.reshape(N_EXPERTS, D, D_FF)
            p["moe_wu"] = wu.reshape(N_EXPERTS, D, D_FF)
            p["moe_wd"] = wd.reshape(N_EXPERTS, D_FF, D)
        else:
            p["ffn_wg"] = ffn_w_gate[i].astype(BF16)
            p["ffn_wu"] = ffn_w_up[i].astype(BF16)
            p["ffn_wd"] = ffn_w_down[i].astype(BF16)
        mod_x = mods[l, :B]
        mod_c = mods[l, B:B + 1]

        def channel_mixer(attn, conv, four, xs, mod):
            if moe:
                x1, h, route, cnt = _mix_out(attn, conv, four, xs, mod, p, moe=True)
                return _moe(h, x1, route, cnt, mod if mod.shape[0] > 1 else jnp.broadcast_to(mod, (B,) + mod.shape[1:]), p)
            x1, h = _mix_out(attn, conv, four, xs, mod, p, moe=False)
            return _ffn(h, x1, mod, p)

        if last:
            k_c, v_c = _front(ctx, mod_c, p["mix_norm_g"], p, None, full=False)
        else:
            attn_c, conv_c, four_c, k_c, v_c, _ = _mixer(ctx, mod_c, p, None, None, None, cs_c)
            ctx_next = channel_mixer(attn_c, conv_c, four_c, ctx, mod_c)

        attn_x, conv_x, four_x, _, _, casted = _mixer(x, mod_x, p, rope_tabs, k_c, v_c, cs_x, cast)
        if cast:
            moe_bf16[nxt // 2] = casted
        x = channel_mixer(attn_x, conv_x, four_x, x, mod_x)
        if not last:
            ctx = ctx_next
    return x
```

```python
import functools
import math

import numpy as np
import jax
import jax.numpy as jnp
from jax import lax
from jax.experimental import pallas as pl
from jax.experimental.pallas import tpu as pltpu

F32 = jnp.float32
BF16 = jnp.bfloat16

D_MODEL = 1024
DEPTH = 2
GRID_W = 64
N_HEADS = 8
QK_NOPE = 64
QK_ROPE = 32
QK_HEAD = QK_NOPE + QK_ROPE
V_HEAD = 64
Q_LORA = 384
KV_LORA = 256
ROPE_BASE = 10000.0
CONV_CH = 256
CONV_WIDTH = 31
CONV_PAD = (CONV_WIDTH - 1) // 2
FOURIER_GROUPS = 4
FOURIER_GROUP_DIM = 64
FOURIER_CH = FOURIER_GROUPS * FOURIER_GROUP_DIM
ATTN_OUT = N_HEADS * V_HEAD
OFF_CQ = 0
OFF_CKV = OFF_CQ + Q_LORA
OFF_KR = OFF_CKV + KV_LORA
OFF_CONV = OFF_KR + QK_ROPE
OFF_FOUR = OFF_CONV + 2 * CONV_CH
IN_COLS = OFF_FOUR + FOURIER_CH
D_FF = 2816
N_EXPERTS = 8
N_MOD = 6
EPS = 1e-6

LANES = 128
SUBLANES = 8
HEAD_PAD = LANES
VMEM_CAP = 56 * 1024 * 1024
FF_CHUNK = 256

SUB_ROWS = 256
TM_FRONT = 512
TQ_ATTN = 1024
TQ_SUB = 512
TK_ATTN = 256
TM_OUT = 512
TM_MOE = 512
TM_ROUTE = 512
CONV_ROWS = 128
TR_FOURIER = 512


def _cparams(sem, vmem_bytes):
    return pltpu.CompilerParams(dimension_semantics=sem, vmem_limit_bytes=int(min(VMEM_CAP, vmem_bytes)))


def _rms(v, g):
    return v * lax.rsqrt(jnp.mean(v * v, axis=-1, keepdims=True) + EPS) * g


def _sigmoid(v):
    return 1.0 / (1.0 + jnp.exp(-v))


def _mod_kernel(c_ref, w_ref, b_ref, o_ref):
    c = c_ref[...]
    s = (c * _sigmoid(c)).astype(BF16)
    o_ref[...] = jnp.dot(s, w_ref[...].astype(BF16), preferred_element_type=F32) + b_ref[...]


def _modulation(cc, ada_w, ada_b):
    rows = cc.shape[0]
    tn = 1536
    n_out = N_MOD * D_MODEL
    return pl.pallas_call(
        _mod_kernel,
        grid=(DEPTH, n_out // tn),
        in_specs=[
            pl.BlockSpec((rows, D_MODEL), lambda l, j: (0, 0)),
            pl.BlockSpec((None, D_MODEL, tn), lambda l, j: (l, 0, j)),
            pl.BlockSpec((None, 1, tn), lambda l, j: (l, 0, j)),
        ],
        out_specs=pl.BlockSpec((None, rows, tn), lambda l, j: (l, 0, j)),
        out_shape=jax.ShapeDtypeStruct((DEPTH, rows, n_out), F32),
        compiler_params=_cparams(("arbitrary", "arbitrary"), 4 * D_MODEL * tn * 4),
        name="modulation",
    )(cc, ada_w, ada_b.reshape(DEPTH, 1, n_out))


def _front_kernel(*refs, full, rope):
    it = iter(refs)
    x_ref, mod_ref, g_ref, win_ref = next(it), next(it), next(it), next(it)
    if full:
        qlg_ref, wuq_ref = next(it), next(it)
    kvlg_ref, wukv_ref, vones_ref, onesh_ref, expand_ref = next(it), next(it), next(it), next(it), next(it)
    if rope:
        cq_ref, sq_ref, ck_ref, sk_ref = next(it), next(it), next(it), next(it)
    else:
        qg_ref = next(it) if full else None
        kg_ref = next(it)
    if full:
        bd_ref = next(it)
        q_out = next(it)
    k_out, v_out = next(it), next(it)
    if full:
        y_out, zz_out = next(it), next(it)

    shift = mod_ref[0:1, :]
    scale = mod_ref[1:2, :]
    hw = N_HEADS * HEAD_PAD

    def head_inv_rms(raw):
        ss = jnp.dot((raw * raw).astype(BF16), onesh_ref[...], preferred_element_type=F32)
        rs = lax.rsqrt(ss * (1.0 / QK_HEAD) + EPS)
        rs_hi = rs.astype(BF16)
        rs_lo = (rs - rs_hi.astype(F32)).astype(BF16)
        return jnp.dot(jnp.concatenate([rs_hi, rs_lo], axis=1), expand_ref[...], preferred_element_type=F32)

    def sub_block(rows):
        x = x_ref[rows, :]
        h = _rms(x, g_ref[...]) * (1.0 + scale) + shift
        cols = jnp.dot(h.astype(BF16), win_ref[...], preferred_element_type=F32)

        o = 0
        if full:
            cq = cols[:, 0:Q_LORA]
            o = Q_LORA
            qall = jnp.dot(_rms(cq, qlg_ref[...]).astype(BF16), wuq_ref[...], preferred_element_type=F32)
            rsb = head_inv_rms(qall)
            for hd in range(N_HEADS):
                sl = slice(hd * HEAD_PAD, (hd + 1) * HEAD_PAD)
                if rope:
                    val = qall[:, sl] * cq_ref[rows, :] + pltpu.roll(qall[:, sl], HEAD_PAD - QK_ROPE, 1) * \
                        sq_ref[rows, :]
                else:
                    val = qall[:, sl] * qg_ref[...]
                q_out[rows, sl] = (val * rsb[:, sl]).astype(BF16)

        ckv = cols[:, o:o + KV_LORA]
        o += KV_LORA
        kv = jnp.dot(_rms(ckv, kvlg_ref[...]).astype(BF16), wukv_ref[...], preferred_element_type=F32)
        if full:
            a = cols[:, o:o + CONV_CH]
            gt = cols[:, o + CONV_CH:o + 2 * CONV_CH]
            y_out[rows, :] = a * _sigmoid(gt)
            o += 2 * CONV_CH
            z = cols[:, o:o + FOURIER_CH]
            o += FOURIER_CH
            zz_out[rows, :] = jnp.dot(z.astype(BF16), bd_ref[...], preferred_element_type=F32).astype(BF16)
        krb = cols[:, o:o + HEAD_PAD]
        kraw = [kv[:, hd * HEAD_PAD:(hd + 1) * HEAD_PAD] + krb for hd in range(N_HEADS)]
        rsb = head_inv_rms(jnp.concatenate(kraw, axis=1))
        if rope:
            kpart = pltpu.roll(krb, HEAD_PAD - QK_ROPE, 1) * sk_ref[rows, :]
        for hd in range(N_HEADS):
            sl = slice(hd * HEAD_PAD, (hd + 1) * HEAD_PAD)
            val = kraw[hd] * ck_ref[rows, :] + kpart if rope else kraw[hd] * kg_ref[...]
            k_out[rows, sl] = (val * rsb[:, sl]).astype(BF16)
        v_out[rows, :] = (kv[:, hw:] + vones_ref[...]).astype(BF16)

    tm = x_ref.shape[0]
    sub = min(SUB_ROWS, tm)
    for sb in range(tm // sub):
        sub_block(slice(sb * sub, (sb + 1) * sub))


def _front(xs, mod, norm_g, p, rope_tabs, *, full):
    B, S, D = xs.shape
    tm = min(TM_FRONT, S)
    rope = rope_tabs is not None
    assert full or not rope
    w_in = p["w_in_full"] if full else p["w_in_kv"]
    w_uq = p["w_uq"]
    ncol = w_in.shape[1]
    bm = mod.shape[0]
    mod_map = (lambda b, i: (b, 0, 0)) if bm > 1 else (lambda b, i: (0, 0, 0))
    const2 = lambda b, i: (0, 0)
    hw = N_HEADS * HEAD_PAD

    args = [xs, mod, norm_g, w_in]
    specs = [
        pl.BlockSpec((None, tm, D), lambda b, i: (b, i, 0)),
        pl.BlockSpec((None, N_MOD, D), mod_map),
        pl.BlockSpec((1, D), const2),
        pl.BlockSpec((D, ncol), const2),
    ]
    if full:
        args += [p["q_lat_g"], w_uq]
        specs += [pl.BlockSpec((1, Q_LORA), const2), pl.BlockSpec((Q_LORA, w_uq.shape[1]), const2)]
    args += [p["kv_lat_g"], p["w_ukv"], p["v_ones"], p["ones_h"], p["expand_h"]]
    specs += [pl.BlockSpec((1, KV_LORA), const2),
              pl.BlockSpec((KV_LORA, 2 * hw), const2),
              pl.BlockSpec((1, hw), const2),
              pl.BlockSpec((hw, LANES), const2),
              pl.BlockSpec((2 * LANES, hw), const2)]
    if rope:
        args += list(rope_tabs)
        specs += [pl.BlockSpec((tm, HEAD_PAD), lambda b, i: (i, 0))] * 4
    else:
        if full:
            args += [p["q_norm_g"]]
            specs += [pl.BlockSpec((1, HEAD_PAD), const2)]
        args += [p["k_norm_g"]]
        specs += [pl.BlockSpec((1, HEAD_PAD), const2)]
    if full:
        args += [p["bd"]]
        specs += [pl.BlockSpec((FOURIER_CH, 2 * FOURIER_CH), const2)]

    out_shapes, out_specs = [], []

    def add_out(width, dtype):
        out_shapes.append(jax.ShapeDtypeStruct((B, S, width), dtype))
        out_specs.append(pl.BlockSpec((None, tm, width), lambda b, i: (b, i, 0)))

    if full:
        add_out(N_HEADS * HEAD_PAD, BF16)
    add_out(N_HEADS * HEAD_PAD, BF16)
    add_out(N_HEADS * HEAD_PAD, BF16)
    if full:
        add_out(CONV_CH, F32)
        add_out(2 * FOURIER_CH, BF16)

    vmem = 2 * (D * ncol * 2 + Q_LORA * 1024 * 2 + KV_LORA * 2048 * 2) + 28 * tm * D * 4
    return pl.pallas_call(
        functools.partial(_front_kernel, full=full, rope=rope),
        grid=(B, S // tm),
        in_specs=specs,
        out_specs=out_specs,
        out_shape=out_shapes,
        compiler_params=_cparams(("arbitrary", "arbitrary"), vmem),
        name="front_full" if full else "front_kv",
    )(*args)


def _attn_kernel(*refs, nsrc, ncast):
    q_ref = refs[0]
    k_refs = refs[1:1 + nsrc]
    v_refs = refs[1 + nsrc:1 + 2 * nsrc]
    cast_in = refs[1 + 2 * nsrc:1 + 2 * nsrc + ncast]
    o_ref = refs[1 + 2 * nsrc + ncast]
    cast_out = refs[2 + 2 * nsrc + ncast:]
    chunks = []
    for k_ref, v_ref in zip(k_refs, v_refs):
        for s0 in range(0, k_ref.shape[0], TK_ATTN):
            chunks.append((k_ref, v_ref, s0, min(TK_ATTN, k_ref.shape[0] - s0)))
    tq = q_ref.shape[0]
    sub = min(TQ_SUB, tq)
    for r0 in range(0, tq, sub):
        rows = slice(r0, r0 + sub)
        state = [None, None]
        for k_ref, v_ref, s0, sz in chunks:
            for hh in range(2):
                sl = slice(hh * HEAD_PAD, (hh + 1) * HEAD_PAD)
                s = lax.dot_general(q_ref[rows, sl], k_ref[s0:s0 + sz, sl], (((1,), (1,)), ((), ())),
                                    preferred_element_type=F32)
                m = jnp.max(s, axis=-1, keepdims=True)
                if state[hh] is not None:
                    m_old, acc_old = state[hh]
                    m = jnp.maximum(m_old, m)
                pv = jnp.dot(jnp.exp2((s - m).astype(BF16)), v_ref[s0:s0 + sz, sl], preferred_element_type=F32)
                if state[hh] is not None:
                    pv = pv + jnp.exp2(m_old - m) * acc_old
                state[hh] = (m, pv)
        accs = [state[0][1], state[1][1]]
        lane = lax.broadcasted_iota(jnp.int32, accs[0].shape, 1)
        lo = accs[0] / pltpu.roll(accs[0], V_HEAD, 1)
        hi = pltpu.roll(accs[1], V_HEAD, 1) / accs[1]
        o_ref[rows, :] = jnp.where(lane < V_HEAD, lo, hi).astype(BF16)
    for src, dst in zip(cast_in, cast_out):
        dst[...] = src[...].astype(BF16)


def _attend(q, ks, vs, cast=()):
    B, S, _ = q.shape
    tq = min(TQ_ATTN, S)
    nsrc = len(ks)
    nq = S // tq
    steps = B * (N_HEADS // 2) * nq
    specs = [pl.BlockSpec((None, tq, 2 * HEAD_PAD), lambda b, hp, i: (b, i, hp))]
    for kv in list(ks) + list(vs):
        specs.append(pl.BlockSpec((None, kv.shape[1], 2 * HEAD_PAD), lambda b, hp, i: (b, 0, hp)))
    out_specs = [pl.BlockSpec((None, tq, 2 * V_HEAD), lambda b, hp, i: (b, i, hp))]
    out_shapes = [jax.ShapeDtypeStruct((B, S, ATTN_OUT), BF16)]
    cast_bytes = 0
    for w in cast:
        rows, cols = w.shape
        assert rows % (steps * 2 * SUBLANES) == 0
        spec = pl.BlockSpec((rows // steps, cols), lambda b, hp, i: ((b * (N_HEADS // 2) + hp) * nq + i, 0))
        specs.append(spec)
        out_specs.append(spec)
        out_shapes.append(jax.ShapeDtypeStruct(w.shape, BF16))
        cast_bytes += 2 * (rows // steps) * cols * 6
    t_all = sum(k.shape[1] for k in ks)
    vmem = 2 * (t_all * 512 * 2) * 2 + 16 * min(tq, TQ_SUB) * TK_ATTN * 4 + cast_bytes + (12 << 20)
    outs = pl.pallas_call(
        functools.partial(_attn_kernel, nsrc=nsrc, ncast=len(cast)),
        grid=(B, N_HEADS // 2, nq),
        in_specs=specs,
        out_specs=out_specs,
        out_shape=out_shapes,
        compiler_params=_cparams(("arbitrary", "arbitrary", "arbitrary"), vmem),
        name="attend%d" % nsrc,
    )(q, *ks, *vs, *cast)
    return outs[0] if not cast else outs


def _conv_kernel(y_ref, w_ref, b_ref, lg_ref, lb_ref, o_ref, pad_ref, *, seq):
    halo = 2 * SUBLANES
    total = seq + 2 * halo
    pad_ref[0, 0:halo, :] = jnp.zeros((halo, CONV_CH), F32)
    pad_ref[0, halo + seq:total, :] = jnp.zeros((halo, CONV_CH), F32)
    pad_ref[0, halo:halo + seq, :] = y_ref[...]
    for ph in range(1, SUBLANES):
        pad_ref[ph, 0:total - SUBLANES, :] = pad_ref[0, ph:ph + total - SUBLANES, :]
    rows = CONV_ROWS
    for base in range(0, seq, rows):
        acc = jnp.zeros((rows, CONV_CH), F32)
        for j in range(CONV_WIDTH):
            off = base + halo - CONV_PAD + j
            ph = off % SUBLANES
            acc = acc + pad_ref[ph, off - ph:off - ph + rows, :] * w_ref[j:j + 1, :]
        acc = acc + b_ref[...]
        mu = jnp.mean(acc, axis=-1, keepdims=True)
        cen = acc - mu
        var = jnp.mean(cen * cen, axis=-1, keepdims=True)
        yn = cen * lax.rsqrt(var + EPS) * lg_ref[...] + lb_ref[...]
        o_ref[base:base + rows, :] = (yn * _sigmoid(yn)).astype(BF16)


def _conv(y, p):
    B, S, _ = y.shape
    c2 = lambda b: (0, 0)
    return pl.pallas_call(
        functools.partial(_conv_kernel, seq=S),
        grid=(B,),
        in_specs=[pl.BlockSpec((None, S, CONV_CH), lambda b: (b, 0, 0)),
                  pl.BlockSpec((CONV_WIDTH, CONV_CH), c2),
                  pl.BlockSpec((1, CONV_CH), c2), pl.BlockSpec((1, CONV_CH), c2), pl.BlockSpec((1, CONV_CH), c2)],
        out_specs=pl.BlockSpec((None, S, CONV_CH), lambda b: (b, 0, 0)),
        out_shape=jax.ShapeDtypeStruct((B, S, CONV_CH), BF16),
        scratch_shapes=[pltpu.VMEM((SUBLANES, S + 4 * SUBLANES, CONV_CH), F32)],
        compiler_params=_cparams(("arbitrary",), (SUBLANES + 6) * (S + 32) * CONV_CH * 4 + (8 << 20)),
        name="conv",
    )(y, p["conv_w"], p["conv_b"], p["conv_ln_g"], p["conv_ln_b"])


def _fourier_kernel(cs_ref, zz_ref, o_ref, csb_ref, *, seq):
    @pl.when(pl.program_id(1) == 0)
    def _():
        csb_ref[...] = cs_ref[...].astype(BF16)

    acc = jnp.dot(csb_ref[:, 0:seq], zz_ref[:, 0:FOURIER_CH], preferred_element_type=F32)
    acc = acc + jnp.dot(csb_ref[:, seq:2 * seq], zz_ref[:, FOURIER_CH:2 * FOURIER_CH], preferred_element_type=F32)
    o_ref[...] = acc.astype(BF16)


def _fourier(zz, cs):
    B, S, _ = zz.shape
    tr = min(TR_FOURIER, S)
    return pl.pallas_call(
        functools.partial(_fourier_kernel, seq=S),
        grid=(S // tr, B),
        in_specs=[pl.BlockSpec((tr, 2 * S), lambda r, b: (r, 0)),
                  pl.BlockSpec((None, S, 2 * FOURIER_CH), lambda r, b: (b, 0, 0))],
        out_specs=pl.BlockSpec((None, tr, FOURIER_CH), lambda r, b: (b, r, 0)),
        out_shape=jax.ShapeDtypeStruct((B, S, FOURIER_CH), BF16),
        scratch_shapes=[pltpu.VMEM((tr, 2 * S), BF16)],
        compiler_params=_cparams(("arbitrary", "arbitrary"), tr * 2 * S * 10 + 2 * S * 512 * 2 + (8 << 20)),
        name="fourier",
    )(cs, zz)


def _dft_tables(seq):
    norm = seq ** -0.5
    ks = (np.arange(seq)[:, None] * np.arange(seq)[None, :]) % seq
    ang = 2.0 * np.pi * ks / seq
    tab = np.concatenate([np.cos(ang), -np.sin(ang)], axis=1) * norm
    return jnp.asarray(tab, F32)


def _channel_dft():
    n = FOURIER_GROUP_DIM
    ang = 2.0 * np.pi * ((np.arange(n)[:, None] * np.arange(n)[None, :]) % n) / n
    eye = np.eye(FOURIER_GROUPS)
    bd = np.concatenate([np.kron(eye, np.cos(ang)), np.kron(eye, np.sin(ang))], axis=1) * n ** -0.5
    return jnp.asarray(bd, F32).astype(BF16)


def _mixout_kernel(*refs, moe):
    if moe:
        (attn_ref, conv_ref, four_ref, x_ref, mod_ref, wout_ref, g_ref, rw_ref, tri_ref,
         x1_out, h_out, route_out, cnt_out) = refs
    else:
        (attn_ref, conv_ref, four_ref, x_ref, mod_ref, wout_ref, g_ref, wg_ref, wu_ref, wd_ref,
         x1_out, h_out) = refs
    tm = x_ref.shape[0]
    sub = min(SUB_ROWS, tm)
    for sb in range(tm // sub):
        rows = slice(sb * sub, (sb + 1) * sub)
        mix = jnp.dot(attn_ref[rows, :], wout_ref[0:ATTN_OUT, :], preferred_element_type=F32)
        mix = mix + jnp.dot(conv_ref[rows, :], wout_ref[ATTN_OUT:ATTN_OUT + CONV_CH, :],
                            preferred_element_type=F32)
        mix = mix + jnp.dot(four_ref[rows, :], wout_ref[ATTN_OUT + CONV_CH:, :], preferred_element_type=F32)
        x1 = x_ref[rows, :] + mod_ref[2:3, :] * mix
        x1_out[rows, :] = x1
        h = _rms(x1, g_ref[...]) * (1.0 + mod_ref[4:5, :]) + mod_ref[3:4, :]
        if moe:
            _store_token_tiles(h_out, sb * sub, h)
            route, cnt = _route(h, rw_ref, tri_ref)
            route_out[rows, :] = route
            cnt_out[sb] = cnt
        else:
            h_out[rows, :] = h.astype(BF16)
    if not moe:
        x1_out[...] = x1_out[...] + mod_ref[5:6, :] * _swiglu_tile(h_out[...], wg_ref, wu_ref, wd_ref)


def _store_token_tiles(ref, row0, val):
    rows = val.shape[0]
    for j in range(SUBLANES):
        ref[pl.ds(row0 * SUBLANES + j, rows, stride=SUBLANES), :] = val[:, j * LANES:(j + 1) * LANES]


def _load_token_tiles(ref, row0, rows, lead=()):
    idx = tuple(lead)
    return jnp.concatenate([ref[idx + (pl.ds(row0 * SUBLANES + j, rows, stride=SUBLANES), slice(None))]
                            for j in range(SUBLANES)], axis=1)


def _route(h, rw_ref, tri_ref):
    tm = h.shape[0]
    h_hi = h.astype(BF16)
    h_lo = (h - h_hi.astype(F32)).astype(BF16)
    hw = jnp.dot(h_hi, rw_ref[...], preferred_element_type=F32)
    logits = hw[:, 0:LANES] + hw[:, LANES:] + jnp.dot(h_lo, rw_ref[:, 0:LANES], preferred_element_type=F32)
    lane = lax.broadcasted_iota(jnp.int32, (tm, LANES), 1).astype(F32)
    neg = jnp.float32(-jnp.inf)
    lm = jnp.where(lane < N_EXPERTS, logits, neg)
    m1 = jnp.max(lm, axis=-1, keepdims=True)
    i1 = jnp.min(jnp.where(lm == m1, lane, float(LANES)), axis=-1, keepdims=True)
    lm2 = jnp.where(lane == i1, neg, lm)
    m2 = jnp.max(lm2, axis=-1, keepdims=True)
    i2 = jnp.min(jnp.where(lm2 == m2, lane, float(LANES)), axis=-1, keepdims=True)
    t = jnp.exp(m2 - m1)
    w1 = 1.0 / (1.0 + t)
    w2 = t / (1.0 + t)
    oh1 = lane == i1
    oh2 = lane == i2
    tri = tri_ref[...]
    c1 = jnp.dot(tri, jnp.where(oh1, 1.0, 0.0).astype(BF16), preferred_element_type=F32)
    c2 = jnp.dot(tri, jnp.where(oh2, 1.0, 0.0).astype(BF16), preferred_element_type=F32)
    r1 = jnp.sum(jnp.where(oh1, c1, 0.0), axis=-1, keepdims=True)
    r2 = jnp.sum(jnp.where(oh2, c2, 0.0), axis=-1, keepdims=True)
    vals = [i1, i2, w1, w2, r1, r2]
    route = jnp.zeros((tm, LANES), F32)
    for idx, v in enumerate(vals):
        route = jnp.where(lane == idx, v, route)
    n1 = jnp.sum(jnp.where(oh1, 1.0, 0.0), axis=0, keepdims=True)
    n2 = jnp.sum(jnp.where(oh2, 1.0, 0.0), axis=0, keepdims=True)
    row = lax.broadcasted_iota(jnp.int32, (SUBLANES, LANES), 0)
    return route, jnp.where(row == 0, n1, jnp.where(row == 1, n2, 0.0))


def _mix_out(attn, conv, four, xs, mod, p, *, moe):
    B, S, D = xs.shape
    tm = min(TM_OUT, S)
    sub = min(SUB_ROWS, tm)
    bm = mod.shape[0]
    mod_map = (lambda b, i: (b, 0, 0)) if bm > 1 else (lambda b, i: (0, 0, 0))
    c2 = lambda b, i: (0, 0)
    tok = lambda w: pl.BlockSpec((None, tm, w), lambda b, i: (b, i, 0))
    args = [attn, conv, four, xs, mod, p["w_out"], p["ffn_norm_g"]]
    specs = [tok(ATTN_OUT), tok(CONV_CH), tok(FOURIER_CH), tok(D), pl.BlockSpec((None, N_MOD, D), mod_map),
             pl.BlockSpec((D, D), c2), pl.BlockSpec((1, D), c2)]
    out_shapes = [jax.ShapeDtypeStruct((B, S, D), F32)]
    out_specs = [tok(D)]
    scratch = []
    vmem = 2 * D * D * 2 + 24 * tm * D * 4 + (8 << 20)
    if not moe:
        resident = lambda shape: pl.BlockSpec(shape, c2, pipeline_mode=pl.Buffered(1))
        args += [p["ffn_wg"], p["ffn_wu"], p["ffn_wd"]]
        specs += [resident((D, D_FF)), resident((D, D_FF)), resident((D_FF, D))]
        scratch = [pltpu.VMEM((tm, D), BF16)]
        vmem += 3 * D * D_FF * 2
    else:
        out_shapes.append(None)
        out_specs.append(None)
        per = S // tm
        out_shapes[1] = jax.ShapeDtypeStruct((B * S * SUBLANES, LANES), F32)
        out_specs[1] = pl.BlockSpec((tm * SUBLANES, LANES), lambda b, i: (b * per + i, 0))
        tri = jnp.asarray(np.tril(np.ones((sub, sub), np.float32), -1), BF16)
        args += [p["router_w"], tri]
        specs += [pl.BlockSpec((D, 2 * LANES), c2), pl.BlockSpec((sub, sub), c2)]
        out_shapes += [jax.ShapeDtypeStruct((B, S, LANES), F32),
                       jax.ShapeDtypeStruct((B, S // sub, SUBLANES, LANES), F32)]
        out_specs += [tok(LANES), pl.BlockSpec((None, tm // sub, SUBLANES, LANES), lambda b, i: (b, i, 0, 0))]
    outs = pl.pallas_call(
        functools.partial(_mixout_kernel, moe=moe),
        grid=(B, S // tm),
        in_specs=specs,
        out_specs=out_specs,
        out_shape=out_shapes,
        scratch_shapes=scratch,
        compiler_params=_cparams(("arbitrary", "arbitrary"), vmem),
        name="mix_out_moe" if moe else "mix_ffn",
    )(*args)
    return outs if moe else outs[0]


def _swiglu_tile(hb, wg_ref, wu_ref, wd_ref):
    ff = wg_ref.shape[1]
    acc = jnp.zeros((hb.shape[0], D_MODEL), F32)
    for c0 in range(0, ff, FF_CHUNK):
        sl = slice(c0, min(c0 + FF_CHUNK, ff))
        g = jnp.dot(hb, wg_ref[:, sl].astype(BF16), preferred_element_type=F32)
        u = jnp.dot(hb, wu_ref[:, sl].astype(BF16), preferred_element_type=F32)
        a = (g * _sigmoid(g) * u).astype(BF16)
        acc = acc + jnp.dot(a, wd_ref[sl, :].astype(BF16), preferred_element_type=F32)
    return acc


def _token_copy(src_ref, src_row, dst_ref, dst_row, sem):
    return pltpu.make_async_copy(src_ref.at[pl.ds(pl.multiple_of(src_row, SUBLANES), SUBLANES), :],
                                 dst_ref.at[pl.ds(pl.multiple_of(dst_row, SUBLANES), SUBLANES), :], sem)


def _dispatch_kernel(slot_ref, pad_ref, h_ref, xs_ref, hbuf, zero_ref, in_sem, out_sem, *, tm, npad, steps):
    i = pl.program_id(0)

    tr = tm * SUBLANES

    def fetch(step, slot):
        return pltpu.make_async_copy(h_ref.at[pl.ds(step * tr, tr), :], hbuf.at[slot], in_sem.at[slot])

    def drain(slot):
        for _ in range(2):
            pltpu.make_async_copy(hbuf.at[slot], xs_ref.at[pl.ds(0, tr), :], out_sem.at[slot]).wait()
        pltpu.make_async_copy(hbuf.at[slot, pl.ds(0, npad * SUBLANES), :], xs_ref.at[pl.ds(0, npad * SUBLANES), :],
                              out_sem.at[slot]).wait()

    @pl.when(i == 0)
    def _():
        zero_ref[...] = jnp.zeros(zero_ref.shape, F32)
        fetch(0, 0).start()

    slot = lax.rem(i, 3)
    fetch(i, slot).wait()

    @pl.when(i + 1 < steps)
    def _():
        fetch(i + 1, lax.rem(i + 1, 3)).start()

    def scatter(r, carry):
        for k in range(2):
            _token_copy(hbuf.at[slot], r * SUBLANES, xs_ref, slot_ref[0, 0, k * tm + r],
                        out_sem.at[slot]).start(priority=k)
        return carry

    def scatter_pad(j, carry):
        _token_copy(zero_ref, 0, xs_ref, pad_ref[0, 0, j], out_sem.at[slot]).start()
        return carry

    lax.fori_loop(0, tm, scatter, 0, unroll=8)
    lax.fori_loop(0, npad, scatter_pad, 0, unroll=8)

    @pl.when(i > 0)
    def _():
        drain(lax.rem(i + 2, 3))

    @pl.when(i == steps - 1)
    def _():
        drain(slot)


def _dispatch(h_tiles, slots, pad_slots, n_tokens_out):
    n = h_tiles.shape[0] // SUBLANES
    tm = TM_ROUTE
    npad = pad_slots.shape[-1]
    steps = n // tm
    return pl.pallas_call(
        functools.partial(_dispatch_kernel, tm=tm, npad=npad, steps=steps),
        grid=(steps,),
        in_specs=[pl.BlockSpec((1, 1, 2 * tm), lambda i: (i, 0, 0), memory_space=pltpu.SMEM),
                  pl.BlockSpec((1, 1, npad), lambda i: (i, 0, 0), memory_space=pltpu.SMEM),
                  pl.BlockSpec(memory_space=pl.ANY)],
        out_specs=pl.BlockSpec(memory_space=pl.ANY),
        out_shape=jax.ShapeDtypeStruct((n_tokens_out * SUBLANES, LANES), F32),
        scratch_shapes=[pltpu.VMEM((3, tm * SUBLANES, LANES), F32), pltpu.VMEM((SUBLANES, LANES), F32),
                        pltpu.SemaphoreType.DMA((3,)), pltpu.SemaphoreType.DMA((3,))],
        compiler_params=_cparams(("arbitrary",), 3 * tm * D_MODEL * 4 + (4 << 20)),
        name="dispatch",
    )(slots, pad_slots, h_tiles)


def _experts_kernel(te_ref, nt_ref, xs_ref, wg_ref, wu_ref, wd_ref, o_ref, *, tm):
    t = pl.program_id(0)

    @pl.when(t < nt_ref[0])
    def _():
        hb = _load_token_tiles(xs_ref, 0, tm).astype(BF16)
        _store_token_tiles(o_ref, 0, _swiglu_tile(hb, wg_ref, wu_ref, wd_ref))

    @pl.when(t >= nt_ref[0])
    def _():
        o_ref[...] = jnp.zeros(o_ref.shape, F32)


def _experts(xs, tile_expert, n_tiles_used, p):
    D = D_MODEL
    tm = TM_MOE
    n_tiles = tile_expert.shape[0]
    wmap = lambda t, te, nt: (te[t], 0, 0)
    tok = pl.BlockSpec((tm * SUBLANES, LANES), lambda t, te, nt: (t, 0))
    tok_in = pl.BlockSpec((tm * SUBLANES, LANES), lambda t, te, nt: (jnp.minimum(t, nt[0] - 1), 0))
    vmem = 2 * 3 * D * D_FF * p["moe_wg"].dtype.itemsize + 12 * tm * D * 4 + (6 << 20)
    return pl.pallas_call(
        functools.partial(_experts_kernel, tm=tm),
        grid_spec=pltpu.PrefetchScalarGridSpec(
            num_scalar_prefetch=2,
            grid=(n_tiles,),
            in_specs=[tok_in,
                      pl.BlockSpec((None, D, D_FF), wmap),
                      pl.BlockSpec((None, D, D_FF), wmap),
                      pl.BlockSpec((None, D_FF, D), wmap)],
            out_specs=tok),
        out_shape=jax.ShapeDtypeStruct((n_tiles * tm * SUBLANES, LANES), F32),
        compiler_params=_cparams(("arbitrary",), vmem),
        name="experts",
    )(tile_expert, n_tiles_used, xs, p["moe_wg"], p["moe_wu"], p["moe_wd"])


def _combine_kernel(slot_ref, next_ref, x1_ref, route_ref, mod_ref, ys_ref, o_ref, buf_ref, sem, *, tm, steps):
    i = pl.program_id(0)

    def gather(idx_ref, s):
        def body(r, carry):
            for k in range(2):
                _token_copy(ys_ref, idx_ref[0, 0, k * tm + r], buf_ref.at[s, k], r * SUBLANES,
                            sem.at[s]).start(priority=k)
            return carry
        lax.fori_loop(0, tm, body, 0, unroll=8)

    @pl.when(i == 0)
    def _():
        gather(slot_ref, 0)

    @pl.when(i + 1 < steps)
    def _():
        gather(next_ref, lax.rem(i + 1, 2))

    cur = lax.rem(i, 2)
    for k in range(2):
        pltpu.make_async_copy(ys_ref.at[pl.ds(0, tm * SUBLANES), :], buf_ref.at[cur, k], sem.at[cur]).wait()
    route = route_ref[...]
    w1 = route[:, 2:3]
    w2 = route[:, 3:4]
    y0 = _load_token_tiles(buf_ref, 0, tm, lead=(cur, 0))
    y1 = _load_token_tiles(buf_ref, 0, tm, lead=(cur, 1))
    o_ref[...] = x1_ref[...] + mod_ref[5:6, :] * (w1 * y0 + w2 * y1)


def _combine(x1, route, mod, ys, slots):
    B, S, D = x1.shape
    tm = TM_ROUTE
    per = S // tm
    steps = B * per
    tok = lambda w: pl.BlockSpec((None, tm, w), lambda i: (i // per, i % per, 0))
    return pl.pallas_call(
        functools.partial(_combine_kernel, tm=tm, steps=steps),
        grid=(steps,),
        in_specs=[pl.BlockSpec((1, 1, 2 * tm), lambda i: (i, 0, 0), memory_space=pltpu.SMEM),
                  pl.BlockSpec((1, 1, 2 * tm), lambda i: (jnp.minimum(i + 1, steps - 1), 0, 0),
                               memory_space=pltpu.SMEM),
                  tok(D), tok(LANES), pl.BlockSpec((None, N_MOD, D), lambda i: (i // per, 0, 0)),
                  pl.BlockSpec(memory_space=pl.ANY)],
        out_specs=tok(D),
        out_shape=jax.ShapeDtypeStruct((B, S, D), F32),
        scratch_shapes=[pltpu.VMEM((2, 2, tm * SUBLANES, LANES), F32), pltpu.SemaphoreType.DMA((2,))],
        compiler_params=_cparams(("arbitrary",), 12 * tm * D * 4 + (4 << 20)),
        name="combine",
    )(slots, slots, x1, route, mod, ys)


def _moe(h, x1, route, cnt, mod, p):
    B, S, D = x1.shape
    n = B * S
    sub = n // (cnt.shape[0] * cnt.shape[1])
    tmm = TM_MOE
    n_tiles = 2 * n // tmm + N_EXPERTS
    n_slots = n_tiles * tmm
    r2 = route.reshape(n, LANES)
    e = r2[:, 0:2].astype(jnp.int32)
    rank = r2[:, 4:6].astype(jnp.int32)
    counts = cnt[:, :, 0:2, 0:N_EXPERTS].astype(jnp.int32).reshape(-1, 2, N_EXPERTS)
    tile_tot = counts.sum(axis=0)
    n_e = tile_tot.sum(axis=0)
    base = jnp.cumsum(counts, axis=0) - counts
    base = base + jnp.array([0, 1], jnp.int32)[None, :, None] * tile_tot[0][None, None, :]
    tiles_e = (n_e + tmm - 1) // tmm
    pstart = (jnp.cumsum(tiles_e) - tiles_e) * tmm
    onehot = e[:, :, None] == jnp.arange(N_EXPERTS, dtype=jnp.int32)[None, None, :]
    per_tok = jnp.repeat(base + pstart[None, None, :], sub, axis=0)
    slot = jnp.sum(jnp.where(onehot, per_tok, 0), axis=-1) + rank
    slot = slot * SUBLANES
    slots = slot.reshape(n // TM_ROUTE, TM_ROUTE, 2).transpose(0, 2, 1).reshape(n // TM_ROUTE, 1, 2 * TM_ROUTE)
    tile_end = jnp.cumsum(tiles_e)
    n_used = tile_end[-1]
    t_idx = jnp.minimum(jnp.arange(n_tiles, dtype=jnp.int32), n_used - 1)
    tile_expert = jnp.sum(t_idx[:, None] >= tile_end[None, :], axis=-1).astype(jnp.int32)
    tile_expert = jnp.minimum(tile_expert, N_EXPERTS - 1)

    n_cand = N_EXPERTS * tmm
    cand = jnp.arange(n_cand, dtype=jnp.int32)
    pe, pq = cand // tmm, cand % tmm
    n_pad_e = tiles_e * tmm - n_e
    pad_slots = jnp.where(pq < n_pad_e[pe], pstart[pe] + n_e[pe] + pq, n_slots + cand).astype(jnp.int32)
    steps = n // TM_ROUTE
    assert n_cand % steps == 0
    pad_slots = (pad_slots * SUBLANES).reshape(steps, 1, n_cand // steps)

    xs = _dispatch(h, slots, pad_slots, n_slots + n_cand)
    ys = _experts(xs, tile_expert, n_used.reshape(1).astype(jnp.int32), p)
    return _combine(x1, route, mod, ys, slots)


def _pad_heads(w, width):
    lead = w.shape[:-1]
    w = w.reshape(lead + (N_HEADS, width))
    w = jnp.pad(w, [(0, 0)] * len(lead) + [(0, 0), (0, HEAD_PAD - width)])
    return w.reshape(lead + (N_HEADS * HEAD_PAD,))


def _layer_params(l, w_in, q_lat_g, kv_lat_g, w_uq, w_ukv, q_norm_g, k_norm_g, conv_w, conv_b, conv_ln_g,
                  conv_ln_b, w_out, ffn_norm_g):
    wi = w_in[l]
    kr = _with_partner(jnp.pad(wi[:, OFF_KR:OFF_CONV], ((0, 0), (QK_NOPE, HEAD_PAD - QK_HEAD))))
    ckv = wi[:, OFF_CKV:OFF_KR]
    w_full = jnp.concatenate([wi[:, OFF_CQ:OFF_CKV], ckv, wi[:, OFF_CONV:OFF_FOUR], wi[:, OFF_FOUR:IN_COLS], kr],
                             axis=1)
    w_kv = jnp.concatenate([ckv, kr], axis=1)
    ukv = w_ukv[l].reshape(KV_LORA, N_HEADS, QK_NOPE + V_HEAD)
    uk = _pad_heads(ukv[:, :, :QK_NOPE].reshape(KV_LORA, N_HEADS * QK_NOPE), QK_NOPE)
    uv = _pad_heads(ukv[:, :, QK_NOPE:].reshape(KV_LORA, ATTN_OUT), V_HEAD)
    pad_g = lambda g: jnp.pad(g, (0, HEAD_PAD - QK_HEAD)).reshape(1, HEAD_PAD)
    uq = _with_partner(_pad_heads(w_uq[l], QK_HEAD))
    hw = N_HEADS * HEAD_PAD
    col = np.arange(hw)[:, None]
    head_of_col = col // HEAD_PAD == np.arange(LANES)[None, :]
    expand = np.concatenate([head_of_col.T, head_of_col.T], axis=0)
    return {
        "w_in_full": w_full.astype(BF16),
        "w_in_kv": w_kv.astype(BF16),
        "q_lat_g": q_lat_g[l].reshape(1, Q_LORA),
        "kv_lat_g": kv_lat_g[l].reshape(1, KV_LORA),
        "w_uq": uq.astype(BF16),
        "w_ukv": jnp.concatenate([uk, uv], axis=1).astype(BF16),
        "q_norm_g": pad_g(q_norm_g[l]) * (QK_HEAD ** -0.5 * math.log2(math.e)),
        "v_ones": jnp.tile(jnp.concatenate([jnp.zeros((V_HEAD,), F32), jnp.ones((HEAD_PAD - V_HEAD,), F32)]),
                           N_HEADS).reshape(1, hw),
        "ones_h": jnp.asarray(head_of_col & (col % HEAD_PAD < QK_HEAD), F32).astype(BF16),
        "expand_h": jnp.asarray(expand, F32).astype(BF16),
        "k_norm_g": pad_g(k_norm_g[l]),
        "conv_w": conv_w[l],
        "conv_b": conv_b[l].reshape(1, CONV_CH),
        "conv_ln_g": conv_ln_g[l].reshape(1, CONV_CH),
        "conv_ln_b": conv_ln_b[l].reshape(1, CONV_CH),
        "w_out": w_out[l].astype(BF16),
        "ffn_norm_g": ffn_norm_g[l].reshape(1, D_MODEL),
        "bd": _channel_dft(),
    }


def _with_partner(w):
    lead = w.shape[:-1]
    half = QK_ROPE // 2
    w3 = w.reshape(lead + (-1, HEAD_PAD))
    out = jnp.concatenate([w3[..., :QK_HEAD], w3[..., QK_NOPE + half:QK_HEAD], w3[..., QK_NOPE:QK_NOPE + half]],
                          axis=-1)
    return out.reshape(w.shape)


def _rot_partner(w):
    lead = w.shape[:-1]
    half = QK_ROPE // 2
    w3 = w.reshape(lead + (-1, HEAD_PAD))
    z = jnp.zeros_like(w3)
    out = jnp.concatenate([z[..., :QK_NOPE], w3[..., QK_NOPE + half:QK_HEAD], w3[..., QK_NOPE:QK_NOPE + half],
                           z[..., QK_HEAD:]], axis=-1)
    return out.reshape(w.shape)


def _rope_tables(seq, gq, gk):
    rows = seq // GRID_W
    row = np.repeat(np.arange(rows, dtype=np.float64), GRID_W)
    col = np.tile(np.arange(GRID_W, dtype=np.float64), rows)
    n_freq = QK_ROPE // 4
    inv = ROPE_BASE ** (-np.arange(n_freq, dtype=np.float64) / n_freq)
    ang = np.concatenate([row[:, None] * inv, col[:, None] * inv], axis=-1)
    cos, sin = np.cos(ang), np.sin(ang)
    ones = np.ones((seq, QK_NOPE))
    tail = np.zeros((seq, HEAD_PAD - QK_HEAD))
    cos_t = jnp.asarray(np.concatenate([ones, cos, cos, tail], axis=1), F32)
    sin_t = jnp.asarray(np.concatenate([np.zeros((seq, QK_NOPE)), -sin, sin, tail], axis=1), F32)
    return (cos_t * gq, sin_t * _rot_partner(gq), cos_t * gk, sin_t * _rot_partner(gk))


def _mixer(xs, mod, p, rope_tabs, k_extra, v_extra, cs, cast=()):
    q, k, v, y, zz = _front(xs, mod, p["mix_norm_g"], p, rope_tabs, full=True)
    ks, vs = [k], [v]
    if k_extra is not None:
        ks.append(k_extra)
        vs.append(v_extra)
    attn = _attend(q, ks, vs, cast)
    casted = ()
    if cast:
        attn, casted = attn[0], attn[1:]
    conv = _conv(y, p)
    four = _fourier(zz, cs)
    return attn, conv, four, k, v, casted


def kernel(x, c, ctx, c_ctx, ada_w, ada_b, mix_norm_g, ffn_norm_g, w_in, q_lat_g, kv_lat_g, w_uq, w_ukv, q_norm_g,
           k_norm_g, conv_w, conv_b, conv_ln_g, conv_ln_b, w_out, ffn_w_gate, ffn_w_up, ffn_w_down, router_w,
           moe_w_gate, moe_w_up, moe_w_down):
    B, S, D = x.shape
    T = ctx.shape[1]
    assert (D, DEPTH) == (D_MODEL, ada_w.shape[0]) and S % GRID_W == 0

    cc = jnp.concatenate([c, c_ctx[None, :], jnp.zeros((2 * SUBLANES - B - 1, D), F32)], axis=0)
    mods = _modulation(cc, ada_w, ada_b).reshape(DEPTH, 2 * SUBLANES, N_MOD, D)
    cs_x = _dft_tables(S)
    cs_c = _dft_tables(T)

    def moe_f32(i):
        return (moe_w_gate[i].reshape(N_EXPERTS * D, D_FF), moe_w_up[i].reshape(N_EXPERTS * D, D_FF),
                moe_w_down[i].reshape(N_EXPERTS * D_FF, D))

    moe_bf16 = {}
    for l in range(DEPTH):
        last = l == DEPTH - 1
        p = _layer_params(l, w_in, q_lat_g, kv_lat_g, w_uq, w_ukv, q_norm_g, k_norm_g, conv_w, conv_b, conv_ln_g,
                          conv_ln_b, w_out, ffn_norm_g)
        p["mix_norm_g"] = mix_norm_g[l].reshape(1, D)
        rope_tabs = _rope_tables(S, p["q_norm_g"], p["k_norm_g"])
        i = l // 2
        moe = l % 2 == 1
        nxt = l + 1 if l % 2 == 0 else l + 2
        cast = moe_f32(nxt // 2) if nxt < DEPTH else ()
        if moe:
            rw = jnp.pad(router_w[i], ((0, 0), (0, LANES - N_EXPERTS)))
            rw_hi = rw.astype(BF16)
            p["router_w"] = jnp.concatenate([rw_hi, (rw - rw_hi.astype(F32)).astype(BF16)], axis=1)
            wg, wu, wd = moe_bf16.pop(i) if i in moe_bf16 else [w.astype(BF16) for w in moe_f32(i)]
            p["moe_wg"] = wg.reshape(N_EXPERTS, D, D_FF)
            p["moe_wu"] = wu.reshape(N_EXPERTS, D, D_FF)
            p["moe_wd"] = wd.reshape(N_EXPERTS, D_FF, D)
        else:
            p["ffn_wg"] = ffn_w_gate[i].astype(BF16)
            p["ffn_wu"] = ffn_w_up[i].astype(BF16)
            p["ffn_wd"] = ffn_w_down[i].astype(BF16)
        mod_x = mods[l, :B]
        mod_c = mods[l, B:B + 1]

        def channel_mixer(attn, conv, four, xs, mod):
            if moe:
                x1, h, route, cnt = _mix_out(attn, conv, four, xs, mod, p, moe=True)
                return _moe(h, x1, route, cnt, mod if mod.shape[0] > 1 else jnp.broadcast_to(mod, (B,) + mod.shape[1:]), p)
            return _mix_out(attn, conv, four, xs, mod, p, moe=False)

        if last:
            k_c, v_c = _front(ctx, mod_c, p["mix_norm_g"], p, None, full=False)
        else:
            attn_c, conv_c, four_c, k_c, v_c, _ = _mixer(ctx, mod_c, p, None, None, None, cs_c)
            ctx_next = channel_mixer(attn_c, conv_c, four_c, ctx, mod_c)

        attn_x, conv_x, four_x, _, _, casted = _mixer(x, mod_x, p, rope_tabs, k_c, v_c, cs_x, cast)
        if cast:
            moe_bf16[nxt // 2] = casted
        x = channel_mixer(attn_x, conv_x, four_x, x, mod_x)
        if not last:
            ctx = ctx_next
    return x
```

```python
import functools
import math

import numpy as np
import jax
import jax.numpy as jnp
from jax import lax
from jax.experimental import pallas as pl
from jax.experimental.pallas import tpu as pltpu

F32 = jnp.float32
BF16 = jnp.bfloat16

D_MODEL = 1024
DEPTH = 2
GRID_W = 64
N_HEADS = 8
QK_NOPE = 64
QK_ROPE = 32
QK_HEAD = QK_NOPE + QK_ROPE
V_HEAD = 64
Q_LORA = 384
KV_LORA = 256
ROPE_BASE = 10000.0
CONV_CH = 256
CONV_WIDTH = 31
CONV_PAD = (CONV_WIDTH - 1) // 2
FOURIER_GROUPS = 4
FOURIER_GROUP_DIM = 64
FOURIER_CH = FOURIER_GROUPS * FOURIER_GROUP_DIM
ATTN_OUT = N_HEADS * V_HEAD
OFF_CQ = 0
OFF_CKV = OFF_CQ + Q_LORA
OFF_KR = OFF_CKV + KV_LORA
OFF_CONV = OFF_KR + QK_ROPE
OFF_FOUR = OFF_CONV + 2 * CONV_CH
IN_COLS = OFF_FOUR + FOURIER_CH
D_FF = 2816
N_EXPERTS = 8
N_MOD = 6
EPS = 1e-6

LANES = 128
SUBLANES = 8
HEAD_PAD = LANES
VMEM_CAP = 56 * 1024 * 1024
FF_CHUNK = 256

SUB_ROWS = 256
TM_FRONT = 512
TQ_ATTN = 1024
TQ_SUB = 512
TK_ATTN = 256
TM_OUT = 512
TM_MOE = 512
TM_ROUTE = 512
CONV_ROWS = 128
TR_FOURIER = 512


def _cparams(sem, vmem_bytes):
    return pltpu.CompilerParams(dimension_semantics=sem, vmem_limit_bytes=int(min(VMEM_CAP, vmem_bytes)))


def _rms(v, g):
    return v * lax.rsqrt(jnp.mean(v * v, axis=-1, keepdims=True) + EPS) * g


def _sigmoid(v):
    return 1.0 / (1.0 + jnp.exp(-v))


def _mod_kernel(c_ref, w_ref, b_ref, o_ref):
    c = c_ref[...]
    s = (c * _sigmoid(c)).astype(BF16)
    o_ref[...] = jnp.dot(s, w_ref[...].astype(BF16), preferred_element_type=F32) + b_ref[...]


def _modulation(cc, ada_w, ada_b):
    rows = cc.shape[0]
    tn = 1536
    n_out = N_MOD * D_MODEL
    return pl.pallas_call(
        _mod_kernel,
        grid=(DEPTH, n_out // tn),
        in_specs=[
            pl.BlockSpec((rows, D_MODEL), lambda l, j: (0, 0)),
            pl.BlockSpec((None, D_MODEL, tn), lambda l, j: (l, 0, j)),
            pl.BlockSpec((None, 1, tn), lambda l, j: (l, 0, j)),
        ],
        out_specs=pl.BlockSpec((None, rows, tn), lambda l, j: (l, 0, j)),
        out_shape=jax.ShapeDtypeStruct((DEPTH, rows, n_out), F32),
        compiler_params=_cparams(("arbitrary", "arbitrary"), 4 * D_MODEL * tn * 4),
        name="modulation",
    )(cc, ada_w, ada_b.reshape(DEPTH, 1, n_out))


def _front_kernel(*refs, full, rope):
    it = iter(refs)
    x_ref, mod_ref, g_ref, win_ref = next(it), next(it), next(it), next(it)
    if full:
        qlg_ref, wuq_ref = next(it), next(it)
    kvlg_ref, wukv_ref, vones_ref, onesh_ref, expand_ref = next(it), next(it), next(it), next(it), next(it)
    if rope:
        cq_ref, sq_ref, ck_ref, sk_ref = next(it), next(it), next(it), next(it)
    else:
        qg_ref = next(it) if full else None
        kg_ref = next(it)
    if full:
        bd_ref = next(it)
        q_out = next(it)
    k_out, v_out = next(it), next(it)
    if full:
        y_out, zz_out = next(it), next(it)

    shift = mod_ref[0:1, :]
    scale = mod_ref[1:2, :]
    hw = N_HEADS * HEAD_PAD

    def head_inv_rms(raw):
        ss = jnp.dot((raw * raw).astype(BF16), onesh_ref[...], preferred_element_type=F32)
        rs = lax.rsqrt(ss * (1.0 / QK_HEAD) + EPS)
        rs_hi = rs.astype(BF16)
        rs_lo = (rs - rs_hi.astype(F32)).astype(BF16)
        return jnp.dot(jnp.concatenate([rs_hi, rs_lo], axis=1), expand_ref[...], preferred_element_type=F32)

    def sub_block(rows):
        x = x_ref[rows, :]
        h = _rms(x, g_ref[...]) * (1.0 + scale) + shift
        cols = jnp.dot(h.astype(BF16), win_ref[...], preferred_element_type=F32)

        o = 0
        if full:
            cq = cols[:, 0:Q_LORA]
            o = Q_LORA
            qall = jnp.dot(_rms(cq, qlg_ref[...]).astype(BF16), wuq_ref[...], preferred_element_type=F32)
            rsb = head_inv_rms(qall)
            for hd in range(N_HEADS):
                sl = slice(hd * HEAD_PAD, (hd + 1) * HEAD_PAD)
                if rope:
                    val = qall[:, sl] * cq_ref[rows, :] + pltpu.roll(qall[:, sl], HEAD_PAD - QK_ROPE, 1) * \
                        sq_ref[rows, :]
                else:
                    val = qall[:, sl] * qg_ref[...]
                q_out[rows, sl] = (val * rsb[:, sl]).astype(BF16)

        ckv = cols[:, o:o + KV_LORA]
        o += KV_LORA
        kv = jnp.dot(_rms(ckv, kvlg_ref[...]).astype(BF16), wukv_ref[...], preferred_element_type=F32)
        if full:
            a = cols[:, o:o + CONV_CH]
            gt = cols[:, o + CONV_CH:o + 2 * CONV_CH]
            y_out[rows, :] = a * _sigmoid(gt)
            o += 2 * CONV_CH
            z = cols[:, o:o + FOURIER_CH]
            o += FOURIER_CH
            zz_out[rows, :] = jnp.dot(z.astype(BF16), bd_ref[...], preferred_element_type=F32).astype(BF16)
        krb = cols[:, o:o + HEAD_PAD]
        kraw = [kv[:, hd * HEAD_PAD:(hd + 1) * HEAD_PAD] + krb for hd in range(N_HEADS)]
        rsb = head_inv_rms(jnp.concatenate(kraw, axis=1))
        if rope:
            kpart = pltpu.roll(krb, HEAD_PAD - QK_ROPE, 1) * sk_ref[rows, :]
        for hd in range(N_HEADS):
            sl = slice(hd * HEAD_PAD, (hd + 1) * HEAD_PAD)
            val = kraw[hd] * ck_ref[rows, :] + kpart if rope else kraw[hd] * kg_ref[...]
            k_out[rows, sl] = (val * rsb[:, sl]).astype(BF16)
        v_out[rows, :] = (kv[:, hw:] + vones_ref[...]).astype(BF16)

    tm = x_ref.shape[0]
    sub = min(SUB_ROWS, tm)
    for sb in range(tm // sub):
        sub_block(slice(sb * sub, (sb + 1) * sub))


def _front(xs, mod, norm_g, p, rope_tabs, *, full):
    B, S, D = xs.shape
    tm = min(TM_FRONT, S)
    rope = rope_tabs is not None
    assert full or not rope
    w_in = p["w_in_full"] if full else p["w_in_kv"]
    w_uq = p["w_uq"]
    ncol = w_in.shape[1]
    bm = mod.shape[0]
    mod_map = (lambda b, i: (b, 0, 0)) if bm > 1 else (lambda b, i: (0, 0, 0))
    const2 = lambda b, i: (0, 0)
    hw = N_HEADS * HEAD_PAD

    args = [xs, mod, norm_g, w_in]
    specs = [
        pl.BlockSpec((None, tm, D), lambda b, i: (b, i, 0)),
        pl.BlockSpec((None, N_MOD, D), mod_map),
        pl.BlockSpec((1, D), const2),
        pl.BlockSpec((D, ncol), const2),
    ]
    if full:
        args += [p["q_lat_g"], w_uq]
        specs += [pl.BlockSpec((1, Q_LORA), const2), pl.BlockSpec((Q_LORA, w_uq.shape[1]), const2)]
    args += [p["kv_lat_g"], p["w_ukv"], p["v_ones"], p["ones_h"], p["expand_h"]]
    specs += [pl.BlockSpec((1, KV_LORA), const2),
              pl.BlockSpec((KV_LORA, 2 * hw), const2),
              pl.BlockSpec((1, hw), const2),
              pl.BlockSpec((hw, LANES), const2),
              pl.BlockSpec((2 * LANES, hw), const2)]
    if rope:
        args += list(rope_tabs)
        specs += [pl.BlockSpec((tm, HEAD_PAD), lambda b, i: (i, 0))] * 4
    else:
        if full:
            args += [p["q_norm_g"]]
            specs += [pl.BlockSpec((1, HEAD_PAD), const2)]
        args += [p["k_norm_g"]]
        specs += [pl.BlockSpec((1, HEAD_PAD), const2)]
    if full:
        args += [p["bd"]]
        specs += [pl.BlockSpec((FOURIER_CH, 2 * FOURIER_CH), const2)]

    out_shapes, out_specs = [], []

    def add_out(width, dtype):
        out_shapes.append(jax.ShapeDtypeStruct((B, S, width), dtype))
        out_specs.append(pl.BlockSpec((None, tm, width), lambda b, i: (b, i, 0)))

    if full:
        add_out(N_HEADS * HEAD_PAD, BF16)
    add_out(N_HEADS * HEAD_PAD, BF16)
    add_out(N_HEADS * HEAD_PAD, BF16)
    if full:
        add_out(CONV_CH, F32)
        add_out(2 * FOURIER_CH, BF16)

    vmem = 2 * (D * ncol * 2 + Q_LORA * 1024 * 2 + KV_LORA * 2048 * 2) + 28 * tm * D * 4
    return pl.pallas_call(
        functools.partial(_front_kernel, full=full, rope=rope),
        grid=(B, S // tm),
        in_specs=specs,
        out_specs=out_specs,
        out_shape=out_shapes,
        compiler_params=_cparams(("arbitrary", "arbitrary"), vmem),
        name="front_full" if full else "front_kv",
    )(*args)


def _attn_kernel(*refs, nsrc, ncast):
    q_ref = refs[0]
    k_refs = refs[1:1 + nsrc]
    v_refs = refs[1 + nsrc:1 + 2 * nsrc]
    cast_in = refs[1 + 2 * nsrc:1 + 2 * nsrc + ncast]
    o_ref = refs[1 + 2 * nsrc + ncast]
    cast_out = refs[2 + 2 * nsrc + ncast:]
    chunks = []
    for k_ref, v_ref in zip(k_refs, v_refs):
        for s0 in range(0, k_ref.shape[0], TK_ATTN):
            chunks.append((k_ref, v_ref, s0, min(TK_ATTN, k_ref.shape[0] - s0)))
    tq = q_ref.shape[0]
    sub = min(TQ_SUB, tq)
    for r0 in range(0, tq, sub):
        rows = slice(r0, r0 + sub)
        state = [None, None]
        for k_ref, v_ref, s0, sz in chunks:
            for hh in range(2):
                sl = slice(hh * HEAD_PAD, (hh + 1) * HEAD_PAD)
                s = lax.dot_general(q_ref[rows, sl], k_ref[s0:s0 + sz, sl], (((1,), (1,)), ((), ())),
                                    preferred_element_type=F32)
                m = jnp.max(s, axis=-1, keepdims=True)
                if state[hh] is not None:
                    m_old, acc_old = state[hh]
                    m = jnp.maximum(m_old, m)
                pv = jnp.dot(jnp.exp2((s - m).astype(BF16)), v_ref[s0:s0 + sz, sl], preferred_element_type=F32)
                if state[hh] is not None:
                    pv = pv + jnp.exp2(m_old - m) * acc_old
                state[hh] = (m, pv)
        accs = [state[0][1], state[1][1]]
        lane = lax.broadcasted_iota(jnp.int32, accs[0].shape, 1)
        lo = accs[0] / pltpu.roll(accs[0], V_HEAD, 1)
        hi = pltpu.roll(accs[1], V_HEAD, 1) / accs[1]
        o_ref[rows, :] = jnp.where(lane < V_HEAD, lo, hi).astype(BF16)
    for src, dst in zip(cast_in, cast_out):
        dst[...] = src[...].astype(BF16)


def _attend(q, ks, vs, cast=()):
    B, S, _ = q.shape
    tq = min(TQ_ATTN, S)
    nsrc = len(ks)
    nq = S // tq
    steps = B * (N_HEADS // 2) * nq
    specs = [pl.BlockSpec((None, tq, 2 * HEAD_PAD), lambda b, hp, i: (b, i, hp))]
    for kv in list(ks) + list(vs):
        specs.append(pl.BlockSpec((None, kv.shape[1], 2 * HEAD_PAD), lambda b, hp, i: (b, 0, hp)))
    out_specs = [pl.BlockSpec((None, tq, 2 * V_HEAD), lambda b, hp, i: (b, i, hp))]
    out_shapes = [jax.ShapeDtypeStruct((B, S, ATTN_OUT), BF16)]
    cast_bytes = 0
    for w in cast:
        rows, cols = w.shape
        nblk = max(d for d in range(1, steps + 1)
                   if steps % d == 0 and rows % d == 0 and (rows // d) % (2 * SUBLANES) == 0)
        hold = steps // nblk
        spec = pl.BlockSpec((rows // nblk, cols),
                            lambda b, hp, i, hold=hold: (((b * (N_HEADS // 2) + hp) * nq + i) // hold, 0))
        specs.append(spec)
        out_specs.append(spec)
        out_shapes.append(jax.ShapeDtypeStruct(w.shape, BF16))
        cast_bytes += 2 * (rows // nblk) * cols * 6
    t_all = sum(k.shape[1] for k in ks)
    vmem = 2 * (t_all * 512 * 2) * 2 + 16 * min(tq, TQ_SUB) * TK_ATTN * 4 + cast_bytes + (12 << 20)
    outs = pl.pallas_call(
        functools.partial(_attn_kernel, nsrc=nsrc, ncast=len(cast)),
        grid=(B, N_HEADS // 2, nq),
        in_specs=specs,
        out_specs=out_specs,
        out_shape=out_shapes,
        compiler_params=_cparams(("arbitrary", "arbitrary", "arbitrary"), vmem),
        name="attend%d" % nsrc,
    )(q, *ks, *vs, *cast)
    return outs[0] if not cast else outs


def _conv_kernel(y_ref, w_ref, b_ref, lg_ref, lb_ref, o_ref, pad_ref, *, seq):
    halo = 2 * SUBLANES
    total = seq + 2 * halo
    pad_ref[0, 0:halo, :] = jnp.zeros((halo, CONV_CH), F32)
    pad_ref[0, halo + seq:total, :] = jnp.zeros((halo, CONV_CH), F32)
    pad_ref[0, halo:halo + seq, :] = y_ref[...]
    for ph in range(1, SUBLANES):
        pad_ref[ph, 0:total - SUBLANES, :] = pad_ref[0, ph:ph + total - SUBLANES, :]
    rows = CONV_ROWS
    for base in range(0, seq, rows):
        acc = jnp.zeros((rows, CONV_CH), F32)
        for j in range(CONV_WIDTH):
            off = base + halo - CONV_PAD + j
            ph = off % SUBLANES
            acc = acc + pad_ref[ph, off - ph:off - ph + rows, :] * w_ref[j:j + 1, :]
        acc = acc + b_ref[...]
        mu = jnp.mean(acc, axis=-1, keepdims=True)
        cen = acc - mu
        var = jnp.mean(cen * cen, axis=-1, keepdims=True)
        yn = cen * lax.rsqrt(var + EPS) * lg_ref[...] + lb_ref[...]
        o_ref[base:base + rows, :] = (yn * _sigmoid(yn)).astype(BF16)


def _conv(y, p):
    B, S, _ = y.shape
    c2 = lambda b: (0, 0)
    return pl.pallas_call(
        functools.partial(_conv_kernel, seq=S),
        grid=(B,),
        in_specs=[pl.BlockSpec((None, S, CONV_CH), lambda b: (b, 0, 0)),
                  pl.BlockSpec((CONV_WIDTH, CONV_CH), c2),
                  pl.BlockSpec((1, CONV_CH), c2), pl.BlockSpec((1, CONV_CH), c2), pl.BlockSpec((1, CONV_CH), c2)],
        out_specs=pl.BlockSpec((None, S, CONV_CH), lambda b: (b, 0, 0)),
        out_shape=jax.ShapeDtypeStruct((B, S, CONV_CH), BF16),
        scratch_shapes=[pltpu.VMEM((SUBLANES, S + 4 * SUBLANES, CONV_CH), F32)],
        compiler_params=_cparams(("arbitrary",), (SUBLANES + 6) * (S + 32) * CONV_CH * 4 + (8 << 20)),
        name="conv",
    )(y, p["conv_w"], p["conv_b"], p["conv_ln_g"], p["conv_ln_b"])


def _fourier_kernel(cs_ref, zz_ref, o_ref, *, seq):
    acc = jnp.dot(cs_ref[:, 0:seq], zz_ref[:, 0:FOURIER_CH], preferred_element_type=F32)
    acc = acc + jnp.dot(cs_ref[:, seq:2 * seq], zz_ref[:, FOURIER_CH:2 * FOURIER_CH], preferred_element_type=F32)
    o_ref[...] = acc.astype(BF16)


def _fourier(zz, cs):
    B, S, _ = zz.shape
    tr = min(TR_FOURIER, S)
    return pl.pallas_call(
        functools.partial(_fourier_kernel, seq=S),
        grid=(S // tr, B),
        in_specs=[pl.BlockSpec((tr, 2 * S), lambda r, b: (r, 0)),
                  pl.BlockSpec((None, S, 2 * FOURIER_CH), lambda r, b: (b, 0, 0))],
        out_specs=pl.BlockSpec((None, tr, FOURIER_CH), lambda r, b: (b, r, 0)),
        out_shape=jax.ShapeDtypeStruct((B, S, FOURIER_CH), BF16),
        compiler_params=_cparams(("arbitrary", "arbitrary"), 2 * (tr * 2 * S * 2 + S * 512 * 2) + (8 << 20)),
        name="fourier",
    )(cs, zz)


def _dft_tables(seq):
    norm = seq ** -0.5
    ks = (np.arange(seq)[:, None] * np.arange(seq)[None, :]) % seq
    ang = 2.0 * np.pi * ks / seq
    tab = np.concatenate([np.cos(ang), -np.sin(ang)], axis=1) * norm
    return jnp.asarray(tab, F32).astype(BF16)


def _channel_dft():
    n = FOURIER_GROUP_DIM
    ang = 2.0 * np.pi * ((np.arange(n)[:, None] * np.arange(n)[None, :]) % n) / n
    eye = np.eye(FOURIER_GROUPS)
    bd = np.concatenate([np.kron(eye, np.cos(ang)), np.kron(eye, np.sin(ang))], axis=1) * n ** -0.5
    return jnp.asarray(bd, F32).astype(BF16)


def _mixout_kernel(*refs, moe):
    if moe:
        (attn_ref, conv_ref, four_ref, x_ref, mod_ref, wout_ref, g_ref, rw_ref, tri_ref,
         x1_out, h_out, route_out, cnt_out, routet_out) = refs
    else:
        (attn_ref, conv_ref, four_ref, x_ref, mod_ref, wout_ref, g_ref, wg_ref, wu_ref, wd_ref,
         x1_out, h_out) = refs
    tm = x_ref.shape[0]
    sub = min(SUB_ROWS, tm)
    for sb in range(tm // sub):
        rows = slice(sb * sub, (sb + 1) * sub)
        mix = jnp.dot(attn_ref[rows, :], wout_ref[0:ATTN_OUT, :], preferred_element_type=F32)
        mix = mix + jnp.dot(conv_ref[rows, :], wout_ref[ATTN_OUT:ATTN_OUT + CONV_CH, :],
                            preferred_element_type=F32)
        mix = mix + jnp.dot(four_ref[rows, :], wout_ref[ATTN_OUT + CONV_CH:, :], preferred_element_type=F32)
        x1 = x_ref[rows, :] + mod_ref[2:3, :] * mix
        x1_out[rows, :] = x1
        h = _rms(x1, g_ref[...]) * (1.0 + mod_ref[4:5, :]) + mod_ref[3:4, :]
        if moe:
            _store_token_tiles(h_out, sb * sub, h)
            route, cnt = _route(h, rw_ref, tri_ref)
            route_out[rows, :] = route
            cnt_out[sb] = cnt
            routet_out[:, rows] = route.T[0:SUBLANES, :]
        else:
            h_out[rows, :] = h.astype(BF16)
    if not moe:
        x1_out[...] = x1_out[...] + mod_ref[5:6, :] * _swiglu_tile(h_out[...], wg_ref, wu_ref, wd_ref)


def _store_token_tiles(ref, row0, val):
    rows = val.shape[0]
    for j in range(SUBLANES):
        ref[pl.ds(row0 * SUBLANES + j, rows, stride=SUBLANES), :] = val[:, j * LANES:(j + 1) * LANES]


def _load_token_tiles(ref, row0, rows, lead=()):
    idx = tuple(lead)
    return jnp.concatenate([ref[idx + (pl.ds(row0 * SUBLANES + j, rows, stride=SUBLANES), slice(None))]
                            for j in range(SUBLANES)], axis=1)


def _route(h, rw_ref, tri_ref):
    tm = h.shape[0]
    h_hi = h.astype(BF16)
    h_lo = (h - h_hi.astype(F32)).astype(BF16)
    hw = jnp.dot(h_hi, rw_ref[...], preferred_element_type=F32)
    logits = hw[:, 0:LANES] + hw[:, LANES:] + jnp.dot(h_lo, rw_ref[:, 0:LANES], preferred_element_type=F32)
    lane = lax.broadcasted_iota(jnp.int32, (tm, LANES), 1).astype(F32)
    neg = jnp.float32(-jnp.inf)
    lm = jnp.where(lane < N_EXPERTS, logits, neg)
    m1 = jnp.max(lm, axis=-1, keepdims=True)
    i1 = jnp.min(jnp.where(lm == m1, lane, float(LANES)), axis=-1, keepdims=True)
    lm2 = jnp.where(lane == i1, neg, lm)
    m2 = jnp.max(lm2, axis=-1, keepdims=True)
    i2 = jnp.min(jnp.where(lm2 == m2, lane, float(LANES)), axis=-1, keepdims=True)
    t = jnp.exp(m2 - m1)
    w1 = 1.0 / (1.0 + t)
    w2 = t / (1.0 + t)
    oh1 = lane == i1
    oh2 = lane == i2
    tri = tri_ref[...]
    c1 = jnp.dot(tri, jnp.where(oh1, 1.0, 0.0).astype(BF16), preferred_element_type=F32)
    c2 = jnp.dot(tri, jnp.where(oh2, 1.0, 0.0).astype(BF16), preferred_element_type=F32)
    r1 = jnp.sum(jnp.where(oh1, c1, 0.0), axis=-1, keepdims=True)
    r2 = jnp.sum(jnp.where(oh2, c2, 0.0), axis=-1, keepdims=True)
    vals = [i1, i2, w1, w2, r1, r2]
    route = jnp.zeros((tm, LANES), F32)
    for idx, v in enumerate(vals):
        route = jnp.where(lane == idx, v, route)
    n1 = jnp.sum(jnp.where(oh1, 1.0, 0.0), axis=0, keepdims=True)
    n2 = jnp.sum(jnp.where(oh2, 1.0, 0.0), axis=0, keepdims=True)
    row = lax.broadcasted_iota(jnp.int32, (SUBLANES, LANES), 0)
    return route, jnp.where(row == 0, n1, jnp.where(row == 1, n2, 0.0))


def _mix_out(attn, conv, four, xs, mod, p, *, moe):
    B, S, D = xs.shape
    tm = min(TM_OUT, S)
    sub = min(SUB_ROWS, tm)
    bm = mod.shape[0]
    mod_map = (lambda b, i: (b, 0, 0)) if bm > 1 else (lambda b, i: (0, 0, 0))
    c2 = lambda b, i: (0, 0)
    tok = lambda w: pl.BlockSpec((None, tm, w), lambda b, i: (b, i, 0))
    args = [attn, conv, four, xs, mod, p["w_out"], p["ffn_norm_g"]]
    specs = [tok(ATTN_OUT), tok(CONV_CH), tok(FOURIER_CH), tok(D), pl.BlockSpec((None, N_MOD, D), mod_map),
             pl.BlockSpec((D, D), c2), pl.BlockSpec((1, D), c2)]
    out_shapes = [jax.ShapeDtypeStruct((B, S, D), F32)]
    out_specs = [tok(D)]
    scratch = []
    vmem = 2 * D * D * 2 + 24 * tm * D * 4 + (8 << 20)
    if not moe:
        resident = lambda shape: pl.BlockSpec(shape, c2, pipeline_mode=pl.Buffered(1))
        args += [p["ffn_wg"], p["ffn_wu"], p["ffn_wd"]]
        specs += [resident((D, D_FF)), resident((D, D_FF)), resident((D_FF, D))]
        scratch = [pltpu.VMEM((tm, D), BF16)]
        vmem += 3 * D * D_FF * 2
    else:
        out_shapes.append(None)
        out_specs.append(None)
        per = S // tm
        out_shapes[1] = jax.ShapeDtypeStruct((B * S * SUBLANES, LANES), F32)
        out_specs[1] = pl.BlockSpec((tm * SUBLANES, LANES), lambda b, i: (b * per + i, 0))
        tri = jnp.asarray(np.tril(np.ones((sub, sub), np.float32), -1), BF16)
        args += [p["router_w"], tri]
        specs += [pl.BlockSpec((D, 2 * LANES), c2), pl.BlockSpec((sub, sub), c2)]
        out_shapes += [jax.ShapeDtypeStruct((B, S, LANES), F32),
                       jax.ShapeDtypeStruct((B, S // sub, SUBLANES, LANES), F32)]
        out_specs += [tok(LANES), pl.BlockSpec((None, tm // sub, SUBLANES, LANES), lambda b, i: (b, i, 0, 0))]
        out_shapes.append(jax.ShapeDtypeStruct((SUBLANES, B * S), F32))
        out_specs.append(pl.BlockSpec((SUBLANES, tm), lambda b, i: (0, b * per + i)))
    outs = pl.pallas_call(
        functools.partial(_mixout_kernel, moe=moe),
        grid=(B, S // tm),
        in_specs=specs,
        out_specs=out_specs,
        out_shape=out_shapes,
        scratch_shapes=scratch,
        compiler_params=_cparams(("arbitrary", "arbitrary"), vmem),
        name="mix_out_moe" if moe else "mix_ffn",
    )(*args)
    return outs if moe else outs[0]


def _swiglu_tile(hb, wg_ref, wu_ref, wd_ref):
    ff = wg_ref.shape[1]
    acc = jnp.zeros((hb.shape[0], D_MODEL), F32)
    for c0 in range(0, ff, FF_CHUNK):
        sl = slice(c0, min(c0 + FF_CHUNK, ff))
        g = jnp.dot(hb, wg_ref[:, sl].astype(BF16), preferred_element_type=F32)
        u = jnp.dot(hb, wu_ref[:, sl].astype(BF16), preferred_element_type=F32)
        a = (g * _sigmoid(g) * u).astype(BF16)
        acc = acc + jnp.dot(a, wd_ref[sl, :].astype(BF16), preferred_element_type=F32)
    return acc


def _token_copy(src_ref, src_row, dst_ref, dst_row, sem):
    return pltpu.make_async_copy(src_ref.at[pl.ds(pl.multiple_of(src_row, SUBLANES), SUBLANES), :],
                                 dst_ref.at[pl.ds(pl.multiple_of(dst_row, SUBLANES), SUBLANES), :], sem)


def _dispatch_kernel(slot_ref, pad_ref, h_ref, xs_ref, hbuf, zero_ref, in_sem, out_sem, *, tm, npad, steps):
    i = pl.program_id(0)

    tr = tm * SUBLANES

    def fetch(step, slot):
        return pltpu.make_async_copy(h_ref.at[pl.ds(step * tr, tr), :], hbuf.at[slot], in_sem.at[slot])

    def drain(slot):
        for _ in range(2):
            pltpu.make_async_copy(hbuf.at[slot], xs_ref.at[pl.ds(0, tr), :], out_sem.at[slot]).wait()
        pltpu.make_async_copy(hbuf.at[slot, pl.ds(0, npad * SUBLANES), :], xs_ref.at[pl.ds(0, npad * SUBLANES), :],
                              out_sem.at[slot]).wait()

    @pl.when(i == 0)
    def _():
        zero_ref[...] = jnp.zeros(zero_ref.shape, F32)
        fetch(0, 0).start()

    slot = lax.rem(i, 3)
    fetch(i, slot).wait()

    @pl.when(i + 1 < steps)
    def _():
        fetch(i + 1, lax.rem(i + 1, 3)).start()

    def scatter(r, carry):
        for k in range(2):
            _token_copy(hbuf.at[slot], r * SUBLANES, xs_ref, slot_ref[0, 0, k * tm + r],
                        out_sem.at[slot]).start(priority=k)
        return carry

    def scatter_pad(j, carry):
        _token_copy(zero_ref, 0, xs_ref, pad_ref[0, 0, j], out_sem.at[slot]).start()
        return carry

    lax.fori_loop(0, tm, scatter, 0, unroll=8)
    lax.fori_loop(0, npad, scatter_pad, 0, unroll=8)

    @pl.when(i > 0)
    def _():
        drain(lax.rem(i + 2, 3))

    @pl.when(i == steps - 1)
    def _():
        drain(slot)


def _dispatch(h_tiles, slots, pad_slots, n_tokens_out):
    n = h_tiles.shape[0] // SUBLANES
    tm = TM_ROUTE
    npad = pad_slots.shape[-1]
    steps = n // tm
    return pl.pallas_call(
        functools.partial(_dispatch_kernel, tm=tm, npad=npad, steps=steps),
        grid=(steps,),
        in_specs=[pl.BlockSpec((1, 1, 2 * tm), lambda i: (i, 0, 0), memory_space=pltpu.SMEM),
                  pl.BlockSpec((1, 1, npad), lambda i: (i, 0, 0), memory_space=pltpu.SMEM),
                  pl.BlockSpec(memory_space=pl.ANY)],
        out_specs=pl.BlockSpec(memory_space=pl.ANY),
        out_shape=jax.ShapeDtypeStruct((n_tokens_out * SUBLANES, LANES), F32),
        scratch_shapes=[pltpu.VMEM((3, tm * SUBLANES, LANES), F32), pltpu.VMEM((SUBLANES, LANES), F32),
                        pltpu.SemaphoreType.DMA((3,)), pltpu.SemaphoreType.DMA((3,))],
        compiler_params=_cparams(("arbitrary",), 3 * tm * D_MODEL * 4 + (4 << 20)),
        name="dispatch",
    )(slots, pad_slots, h_tiles)


def _experts_kernel(te_ref, nt_ref, xs_ref, wg_ref, wu_ref, wd_ref, o_ref, *, tm):
    t = pl.program_id(0)

    @pl.when(t < nt_ref[0])
    def _():
        hb = _load_token_tiles(xs_ref, 0, tm).astype(BF16)
        _store_token_tiles(o_ref, 0, _swiglu_tile(hb, wg_ref, wu_ref, wd_ref))

    @pl.when(t >= nt_ref[0])
    def _():
        o_ref[...] = jnp.zeros(o_ref.shape, F32)


def _experts(xs, tile_expert, n_tiles_used, p):
    D = D_MODEL
    tm = TM_MOE
    n_tiles = tile_expert.shape[0]
    wmap = lambda t, te, nt: (te[t], 0, 0)
    tok = pl.BlockSpec((tm * SUBLANES, LANES), lambda t, te, nt: (t, 0))
    tok_in = pl.BlockSpec((tm * SUBLANES, LANES), lambda t, te, nt: (jnp.minimum(t, nt[0] - 1), 0))
    vmem = 2 * 3 * D * D_FF * p["moe_wg"].dtype.itemsize + 12 * tm * D * 4 + (6 << 20)
    return pl.pallas_call(
        functools.partial(_experts_kernel, tm=tm),
        grid_spec=pltpu.PrefetchScalarGridSpec(
            num_scalar_prefetch=2,
            grid=(n_tiles,),
            in_specs=[tok_in,
                      pl.BlockSpec((None, D, D_FF), wmap),
                      pl.BlockSpec((None, D, D_FF), wmap),
                      pl.BlockSpec((None, D_FF, D), wmap)],
            out_specs=tok),
        out_shape=jax.ShapeDtypeStruct((n_tiles * tm * SUBLANES, LANES), F32),
        compiler_params=_cparams(("arbitrary",), vmem),
        name="experts",
    )(tile_expert, n_tiles_used, xs, p["moe_wg"], p["moe_wu"], p["moe_wd"])


def _combine_kernel(slot_ref, next_ref, x1_ref, route_ref, mod_ref, ys_ref, o_ref, buf_ref, sem, *, tm, steps):
    i = pl.program_id(0)

    def gather(idx_ref, s):
        def body(r, carry):
            for k in range(2):
                _token_copy(ys_ref, idx_ref[0, 0, k * tm + r], buf_ref.at[s, k], r * SUBLANES,
                            sem.at[s]).start(priority=k)
            return carry
        lax.fori_loop(0, tm, body, 0, unroll=8)

    @pl.when(i == 0)
    def _():
        gather(slot_ref, 0)

    @pl.when(i + 1 < steps)
    def _():
        gather(next_ref, lax.rem(i + 1, 2))

    cur = lax.rem(i, 2)
    for k in range(2):
        pltpu.make_async_copy(ys_ref.at[pl.ds(0, tm * SUBLANES), :], buf_ref.at[cur, k], sem.at[cur]).wait()
    route = route_ref[...]
    w1 = route[:, 2:3]
    w2 = route[:, 3:4]
    y0 = _load_token_tiles(buf_ref, 0, tm, lead=(cur, 0))
    y1 = _load_token_tiles(buf_ref, 0, tm, lead=(cur, 1))
    o_ref[...] = x1_ref[...] + mod_ref[5:6, :] * (w1 * y0 + w2 * y1)


def _combine(x1, route, mod, ys, slots):
    B, S, D = x1.shape
    tm = TM_ROUTE
    per = S // tm
    steps = B * per
    tok = lambda w: pl.BlockSpec((None, tm, w), lambda i: (i // per, i % per, 0))
    return pl.pallas_call(
        functools.partial(_combine_kernel, tm=tm, steps=steps),
        grid=(steps,),
        in_specs=[pl.BlockSpec((1, 1, 2 * tm), lambda i: (i, 0, 0), memory_space=pltpu.SMEM),
                  pl.BlockSpec((1, 1, 2 * tm), lambda i: (jnp.minimum(i + 1, steps - 1), 0, 0),
                               memory_space=pltpu.SMEM),
                  tok(D), tok(LANES), pl.BlockSpec((None, N_MOD, D), lambda i: (i // per, 0, 0)),
                  pl.BlockSpec(memory_space=pl.ANY)],
        out_specs=tok(D),
        out_shape=jax.ShapeDtypeStruct((B, S, D), F32),
        scratch_shapes=[pltpu.VMEM((2, 2, tm * SUBLANES, LANES), F32), pltpu.SemaphoreType.DMA((2,))],
        compiler_params=_cparams(("arbitrary",), 12 * tm * D * 4 + (4 << 20)),
        name="combine",
    )(slots, slots, x1, route, mod, ys)


def _moe(h, x1, route, cnt, route_t, mod, p):
    B, S, D = x1.shape
    n = B * S
    sub = n // (cnt.shape[0] * cnt.shape[1])
    tmm = TM_MOE
    n_tiles = 2 * n // tmm + N_EXPERTS
    n_slots = n_tiles * tmm
    steps = n // TM_ROUTE
    counts = cnt[:, :, 0:2, 0:N_EXPERTS].astype(jnp.int32).reshape(-1, 2, N_EXPERTS)
    tile_tot = counts.sum(axis=0)
    n_e = tile_tot.sum(axis=0)
    base = jnp.cumsum(counts, axis=0) - counts
    base = base + jnp.array([0, 1], jnp.int32)[None, :, None] * tile_tot[0][None, None, :]
    tiles_e = (n_e + tmm - 1) // tmm
    pstart = (jnp.cumsum(tiles_e) - tiles_e) * tmm
    expert_ids = jnp.arange(N_EXPERTS, dtype=jnp.int32)[:, None]
    slots_k = []
    for k in range(2):
        e_k = route_t[k].astype(jnp.int32)
        table = jnp.repeat((base[:, k, :] + pstart[None, :]).T, sub, axis=1)
        slot = jnp.sum(jnp.where(e_k[None, :] == expert_ids, table, 0), axis=0) + route_t[4 + k].astype(jnp.int32)
        slots_k.append((slot * SUBLANES).reshape(steps, TM_ROUTE))
    slots = jnp.concatenate(slots_k, axis=1).reshape(steps, 1, 2 * TM_ROUTE)
    tile_end = jnp.cumsum(tiles_e)
    n_used = tile_end[-1]
    t_idx = jnp.minimum(jnp.arange(n_tiles, dtype=jnp.int32), n_used - 1)
    tile_expert = jnp.sum(t_idx[:, None] >= tile_end[None, :], axis=-1).astype(jnp.int32)
    tile_expert = jnp.minimum(tile_expert, N_EXPERTS - 1)

    n_cand = N_EXPERTS * tmm
    pq = jnp.arange(tmm, dtype=jnp.int32)[None, :]
    n_pad_e = (tiles_e * tmm - n_e)[:, None]
    spare = n_slots + expert_ids * tmm + pq
    pad_slots = jnp.where(pq < n_pad_e, (pstart + n_e)[:, None] + pq, spare).astype(jnp.int32)
    assert n_cand % steps == 0
    pad_slots = (pad_slots * SUBLANES).reshape(steps, 1, n_cand // steps)

    xs = _dispatch(h, slots, pad_slots, n_slots + n_cand)
    ys = _experts(xs, tile_expert, n_used.reshape(1).astype(jnp.int32), p)
    return _combine(x1, route, mod, ys, slots)


def _pad_heads(w, width):
    lead = w.shape[:-1]
    w = w.reshape(lead + (N_HEADS, width))
    w = jnp.pad(w, [(0, 0)] * len(lead) + [(0, 0), (0, HEAD_PAD - width)])
    return w.reshape(lead + (N_HEADS * HEAD_PAD,))


def _layer_params(l, w_in, q_lat_g, kv_lat_g, w_uq, w_ukv, q_norm_g, k_norm_g, conv_w, conv_b, conv_ln_g,
                  conv_ln_b, w_out, ffn_norm_g):
    wi = w_in[l]
    kr = _with_partner(jnp.pad(wi[:, OFF_KR:OFF_CONV], ((0, 0), (QK_NOPE, HEAD_PAD - QK_HEAD))))
    ckv = wi[:, OFF_CKV:OFF_KR]
    w_full = jnp.concatenate([wi[:, OFF_CQ:OFF_CKV], ckv, wi[:, OFF_CONV:OFF_FOUR], wi[:, OFF_FOUR:IN_COLS], kr],
                             axis=1)
    w_kv = jnp.concatenate([ckv, kr], axis=1)
    ukv = w_ukv[l].reshape(KV_LORA, N_HEADS, QK_NOPE + V_HEAD)
    uk = _pad_heads(ukv[:, :, :QK_NOPE].reshape(KV_LORA, N_HEADS * QK_NOPE), QK_NOPE)
    uv = _pad_heads(ukv[:, :, QK_NOPE:].reshape(KV_LORA, ATTN_OUT), V_HEAD)
    pad_g = lambda g: jnp.pad(g, (0, HEAD_PAD - QK_HEAD)).reshape(1, HEAD_PAD)
    uq = _with_partner(_pad_heads(w_uq[l], QK_HEAD))
    hw = N_HEADS * HEAD_PAD
    col = np.arange(hw)[:, None]
    head_of_col = col // HEAD_PAD == np.arange(LANES)[None, :]
    expand = np.concatenate([head_of_col.T, head_of_col.T], axis=0)
    return {
        "w_in_full": w_full.astype(BF16),
        "w_in_kv": w_kv.astype(BF16),
        "q_lat_g": q_lat_g[l].reshape(1, Q_LORA),
        "kv_lat_g": kv_lat_g[l].reshape(1, KV_LORA),
        "w_uq": uq.astype(BF16),
        "w_ukv": jnp.concatenate([uk, uv], axis=1).astype(BF16),
        "q_norm_g": pad_g(q_norm_g[l]) * (QK_HEAD ** -0.5 * math.log2(math.e)),
        "v_ones": jnp.tile(jnp.concatenate([jnp.zeros((V_HEAD,), F32), jnp.ones((HEAD_PAD - V_HEAD,), F32)]),
                           N_HEADS).reshape(1, hw),
        "ones_h": jnp.asarray(head_of_col & (col % HEAD_PAD < QK_HEAD), F32).astype(BF16),
        "expand_h": jnp.asarray(expand, F32).astype(BF16),
        "k_norm_g": pad_g(k_norm_g[l]),
        "conv_w": conv_w[l],
        "conv_b": conv_b[l].reshape(1, CONV_CH),
        "conv_ln_g": conv_ln_g[l].reshape(1, CONV_CH),
        "conv_ln_b": conv_ln_b[l].reshape(1, CONV_CH),
        "ffn_norm_g": ffn_norm_g[l].reshape(1, D_MODEL),
        "bd": _channel_dft(),
    }


def _with_partner(w):
    lead = w.shape[:-1]
    half = QK_ROPE // 2
    w3 = w.reshape(lead + (-1, HEAD_PAD))
    out = jnp.concatenate([w3[..., :QK_HEAD], w3[..., QK_NOPE + half:QK_HEAD], w3[..., QK_NOPE:QK_NOPE + half]],
                          axis=-1)
    return out.reshape(w.shape)


def _rot_partner(w):
    lead = w.shape[:-1]
    half = QK_ROPE // 2
    w3 = w.reshape(lead + (-1, HEAD_PAD))
    z = jnp.zeros_like(w3)
    out = jnp.concatenate([z[..., :QK_NOPE], w3[..., QK_NOPE + half:QK_HEAD], w3[..., QK_NOPE:QK_NOPE + half],
                           z[..., QK_HEAD:]], axis=-1)
    return out.reshape(w.shape)


def _rope_tables(seq, gq, gk):
    rows = seq // GRID_W
    row = np.repeat(np.arange(rows, dtype=np.float64), GRID_W)
    col = np.tile(np.arange(GRID_W, dtype=np.float64), rows)
    n_freq = QK_ROPE // 4
    inv = ROPE_BASE ** (-np.arange(n_freq, dtype=np.float64) / n_freq)
    ang = np.concatenate([row[:, None] * inv, col[:, None] * inv], axis=-1)
    cos, sin = np.cos(ang), np.sin(ang)
    ones = np.ones((seq, QK_NOPE))
    tail = np.zeros((seq, HEAD_PAD - QK_HEAD))
    cos_t = jnp.asarray(np.concatenate([ones, cos, cos, tail], axis=1), F32)
    sin_t = jnp.asarray(np.concatenate([np.zeros((seq, QK_NOPE)), -sin, sin, tail], axis=1), F32)
    return (cos_t * gq, sin_t * _rot_partner(gq), cos_t * gk, sin_t * _rot_partner(gk))


def kernel(x, c, ctx, c_ctx, ada_w, ada_b, mix_norm_g, ffn_norm_g, w_in, q_lat_g, kv_lat_g, w_uq, w_ukv, q_norm_g,
           k_norm_g, conv_w, conv_b, conv_ln_g, conv_ln_b, w_out, ffn_w_gate, ffn_w_up, ffn_w_down, router_w,
           moe_w_gate, moe_w_up, moe_w_down):
    B, S, D = x.shape
    T = ctx.shape[1]
    assert (D, DEPTH) == (D_MODEL, ada_w.shape[0]) and S % GRID_W == 0

    cc = jnp.concatenate([c, c_ctx[None, :], jnp.zeros((2 * SUBLANES - B - 1, D), F32)], axis=0)
    mods = _modulation(cc, ada_w, ada_b).reshape(DEPTH, 2 * SUBLANES, N_MOD, D)
    cs_x = _dft_tables(S)
    cs_c = _dft_tables(T)

    def moe_f32(i):
        return (moe_w_gate[i].reshape(N_EXPERTS * D, D_FF), moe_w_up[i].reshape(N_EXPERTS * D, D_FF),
                moe_w_down[i].reshape(N_EXPERTS * D_FF, D))

    moe_bf16 = {}
    for l in range(DEPTH):
        last = l == DEPTH - 1
        p = _layer_params(l, w_in, q_lat_g, kv_lat_g, w_uq, w_ukv, q_norm_g, k_norm_g, conv_w, conv_b, conv_ln_g,
                          conv_ln_b, w_out, ffn_norm_g)
        p["mix_norm_g"] = mix_norm_g[l].reshape(1, D)
        rope_tabs = _rope_tables(S, p["q_norm_g"], p["k_norm_g"])
        i = l // 2
        moe = l % 2 == 1
        mod_x = mods[l, :B]
        mod_c = mods[l, B:B + 1]

        nxt = l + 1 if l % 2 == 0 else l + 2
        cast = [w_out[l]]
        if not moe:
            cast += [ffn_w_gate[i], ffn_w_up[i], ffn_w_down[i]]
        if nxt < DEPTH:
            cast += list(moe_f32(nxt // 2))

        def channel_mixer(attn, conv, four, xs, mod):
            if moe:
                x1, h, route, cnt, route_t = _mix_out(attn, conv, four, xs, mod, p, moe=True)
                mod_b = mod if mod.shape[0] > 1 else jnp.broadcast_to(mod, (B,) + mod.shape[1:])
                return _moe(h, x1, route, cnt, route_t, mod_b, p)
            return _mix_out(attn, conv, four, xs, mod, p, moe=False)

        if last:
            k_c, v_c = _front(ctx, mod_c, p["mix_norm_g"], p, None, full=False)
        else:
            q_c, k_c, v_c, y_c, zz_c = _front(ctx, mod_c, p["mix_norm_g"], p, None, full=True)
        q_x, k_x, v_x, y_x, zz_x = _front(x, mod_x, p["mix_norm_g"], p, rope_tabs, full=True)
        attn_x, *casted = _attend(q_x, [k_x, k_c], [v_x, v_c], cast)
        p["w_out"] = casted[0]
        casted = casted[1:]
        if moe:
            rw = jnp.pad(router_w[i], ((0, 0), (0, LANES - N_EXPERTS)))
            rw_hi = rw.astype(BF16)
            p["router_w"] = jnp.concatenate([rw_hi, (rw - rw_hi.astype(F32)).astype(BF16)], axis=1)
            wg, wu, wd = moe_bf16.pop(i) if i in moe_bf16 else [w.astype(BF16) for w in moe_f32(i)]
            p["moe_wg"] = wg.reshape(N_EXPERTS, D, D_FF)
            p["moe_wu"] = wu.reshape(N_EXPERTS, D, D_FF)
            p["moe_wd"] = wd.reshape(N_EXPERTS, D_FF, D)
        else:
            p["ffn_wg"], p["ffn_wu"], p["ffn_wd"] = casted[:3]
            casted = casted[3:]
        if nxt < DEPTH:
            moe_bf16[nxt // 2] = casted

        if not last:
            attn_c = _attend(q_c, [k_c], [v_c])
            ctx_next = channel_mixer(attn_c, _conv(y_c, p), _fourier(zz_c, cs_c), ctx, mod_c)

        x = channel_mixer(attn_x, _conv(y_x, p), _fourier(zz_x, cs_x), x, mod_x)
        if not last:
            ctx = ctx_next
    return x
```

```python
import functools
import math

import numpy as np
import jax
import jax.numpy as jnp
from jax import lax
from jax.experimental import pallas as pl
from jax.experimental.pallas import tpu as pltpu

F32 = jnp.float32
BF16 = jnp.bfloat16

D_MODEL = 1024
DEPTH = 2
GRID_W = 64
N_HEADS = 8
QK_NOPE = 64
QK_ROPE = 32
QK_HEAD = QK_NOPE + QK_ROPE
V_HEAD = 64
Q_LORA = 384
KV_LORA = 256
ROPE_BASE = 10000.0
CONV_CH = 256
CONV_WIDTH = 31
CONV_PAD = (CONV_WIDTH - 1) // 2
FOURIER_GROUPS = 4
FOURIER_GROUP_DIM = 64
FOURIER_CH = FOURIER_GROUPS * FOURIER_GROUP_DIM
ATTN_OUT = N_HEADS * V_HEAD
OFF_CQ = 0
OFF_CKV = OFF_CQ + Q_LORA
OFF_KR = OFF_CKV + KV_LORA
OFF_CONV = OFF_KR + QK_ROPE
OFF_FOUR = OFF_CONV + 2 * CONV_CH
IN_COLS = OFF_FOUR + FOURIER_CH
D_FF = 2816
N_EXPERTS = 8
N_MOD = 6
EPS = 1e-6

LANES = 128
SUBLANES = 8
HEAD_PAD = LANES
VMEM_CAP = 56 * 1024 * 1024
FF_CHUNK = 256

SUB_ROWS = 256
SUB_ROWS_MOE = 512
TM_FRONT = 512
TQ_ATTN = 1024
TQ_SUB = 512
TK_ATTN = 256
TM_OUT = 512
TM_MOE = 512
TM_ROUTE = 512
CONV_ROWS = 128
TR_FOURIER = 512


def _cparams(sem, vmem_bytes):
    return pltpu.CompilerParams(dimension_semantics=sem, vmem_limit_bytes=int(min(VMEM_CAP, vmem_bytes)))


def _rms(v, g):
    return v * lax.rsqrt(jnp.mean(v * v, axis=-1, keepdims=True) + EPS) * g


def _sigmoid(v):
    return 1.0 / (1.0 + jnp.exp(-v))


def _mod_kernel(c_ref, w_ref, b_ref, o_ref):
    c = c_ref[...]
    s = (c * _sigmoid(c)).astype(BF16)
    o_ref[...] = jnp.dot(s, w_ref[...].astype(BF16), preferred_element_type=F32) + b_ref[...]


def _modulation(cc, ada_w, ada_b):
    rows = cc.shape[0]
    tn = 1536
    n_out = N_MOD * D_MODEL
    return pl.pallas_call(
        _mod_kernel,
        grid=(DEPTH, n_out // tn),
        in_specs=[
            pl.BlockSpec((rows, D_MODEL), lambda l, j: (0, 0)),
            pl.BlockSpec((None, D_MODEL, tn), lambda l, j: (l, 0, j)),
            pl.BlockSpec((None, 1, tn), lambda l, j: (l, 0, j)),
        ],
        out_specs=pl.BlockSpec((None, rows, tn), lambda l, j: (l, 0, j)),
        out_shape=jax.ShapeDtypeStruct((DEPTH, rows, n_out), F32),
        compiler_params=_cparams(("arbitrary", "arbitrary"), 4 * D_MODEL * tn * 4),
        name="modulation",
    )(cc, ada_w, ada_b.reshape(DEPTH, 1, n_out))


def _front_kernel(*refs, full, rope):
    it = iter(refs)
    x_ref, mod_ref, g_ref, win_ref = next(it), next(it), next(it), next(it)
    if full:
        qlg_ref, wuq_ref = next(it), next(it)
    kvlg_ref, wukv_ref, vones_ref, onesh_ref = next(it), next(it), next(it), next(it)
    if rope:
        cq_ref, sq_ref, ck_ref, sk_ref = next(it), next(it), next(it), next(it)
    else:
        qg_ref = next(it) if full else None
        kg_ref = next(it)
    if full:
        bd_ref = next(it)
        q_out = next(it)
    k_out, v_out = next(it), next(it)
    if full:
        y_out, zz_out = next(it), next(it)

    shift = mod_ref[0:1, :]
    scale = mod_ref[1:2, :]
    hw = N_HEADS * HEAD_PAD

    def head_inv_rms(raw):
        ss = jnp.dot((raw * raw).astype(BF16), onesh_ref[...], preferred_element_type=F32)
        rs = lax.rsqrt(ss * (1.0 / QK_HEAD) + EPS)
        return jnp.concatenate([jnp.broadcast_to(rs[:, hd:hd + 1], (rs.shape[0], HEAD_PAD)) for hd in range(N_HEADS)],
                               axis=1)

    def sub_block(rows):
        x = x_ref[rows, :]
        h = _rms(x, g_ref[...]) * (1.0 + scale) + shift
        cols = jnp.dot(h.astype(BF16), win_ref[...], preferred_element_type=F32)

        o = 0
        if full:
            cq = cols[:, 0:Q_LORA]
            o = Q_LORA
            qall = jnp.dot(_rms(cq, qlg_ref[...]).astype(BF16), wuq_ref[...], preferred_element_type=F32)
            rsb = head_inv_rms(qall)
            for hd in range(N_HEADS):
                sl = slice(hd * HEAD_PAD, (hd + 1) * HEAD_PAD)
                if rope:
                    val = qall[:, sl] * cq_ref[rows, :] + pltpu.roll(qall[:, sl], HEAD_PAD - QK_ROPE, 1) * \
                        sq_ref[rows, :]
                else:
                    val = qall[:, sl] * qg_ref[...]
                q_out[rows, sl] = (val * rsb[:, sl]).astype(BF16)

        ckv = cols[:, o:o + KV_LORA]
        o += KV_LORA
        kv = jnp.dot(_rms(ckv, kvlg_ref[...]).astype(BF16), wukv_ref[...], preferred_element_type=F32)
        if full:
            a = cols[:, o:o + CONV_CH]
            gt = cols[:, o + CONV_CH:o + 2 * CONV_CH]
            y_out[rows, :] = a * _sigmoid(gt)
            o += 2 * CONV_CH
            z = cols[:, o:o + FOURIER_CH]
            o += FOURIER_CH
            zz_out[rows, :] = jnp.dot(z.astype(BF16), bd_ref[...], preferred_element_type=F32).astype(BF16)
        krb = cols[:, o:o + HEAD_PAD]
        kraw = [kv[:, hd * HEAD_PAD:(hd + 1) * HEAD_PAD] + krb for hd in range(N_HEADS)]
        rsb = head_inv_rms(jnp.concatenate(kraw, axis=1))
        if rope:
            kpart = pltpu.roll(krb, HEAD_PAD - QK_ROPE, 1) * sk_ref[rows, :]
        for hd in range(N_HEADS):
            sl = slice(hd * HEAD_PAD, (hd + 1) * HEAD_PAD)
            val = kraw[hd] * ck_ref[rows, :] + kpart if rope else kraw[hd] * kg_ref[...]
            k_out[rows, sl] = (val * rsb[:, sl]).astype(BF16)
        v_out[rows, :] = (kv[:, hw:] + vones_ref[...]).astype(BF16)

    tm = x_ref.shape[0]
    sub = min(SUB_ROWS, tm)
    for sb in range(tm // sub):
        sub_block(slice(sb * sub, (sb + 1) * sub))


def _front(xs, mod, norm_g, p, rope_tabs, *, full):
    B, S, D = xs.shape
    tm = min(TM_FRONT, S)
    rope = rope_tabs is not None
    assert full or not rope
    w_in = p["w_in_full"] if full else p["w_in_kv"]
    w_uq = p["w_uq"]
    ncol = w_in.shape[1]
    bm = mod.shape[0]
    mod_map = (lambda b, i: (b, 0, 0)) if bm > 1 else (lambda b, i: (0, 0, 0))
    const2 = lambda b, i: (0, 0)
    hw = N_HEADS * HEAD_PAD

    args = [xs, mod, norm_g, w_in]
    specs = [
        pl.BlockSpec((None, tm, D), lambda b, i: (b, i, 0)),
        pl.BlockSpec((None, N_MOD, D), mod_map),
        pl.BlockSpec((1, D), const2),
        pl.BlockSpec((D, ncol), const2),
    ]
    if full:
        args += [p["q_lat_g"], w_uq]
        specs += [pl.BlockSpec((1, Q_LORA), const2), pl.BlockSpec((Q_LORA, w_uq.shape[1]), const2)]
    args += [p["kv_lat_g"], p["w_ukv"], p["v_ones"], p["ones_h"]]
    specs += [pl.BlockSpec((1, KV_LORA), const2),
              pl.BlockSpec((KV_LORA, 2 * hw), const2),
              pl.BlockSpec((1, hw), const2),
              pl.BlockSpec((hw, LANES), const2)]
    if rope:
        args += list(rope_tabs)
        specs += [pl.BlockSpec((tm, HEAD_PAD), lambda b, i: (i, 0))] * 4
    else:
        if full:
            args += [p["q_norm_g"]]
            specs += [pl.BlockSpec((1, HEAD_PAD), const2)]
        args += [p["k_norm_g"]]
        specs += [pl.BlockSpec((1, HEAD_PAD), const2)]
    if full:
        args += [p["bd"]]
        specs += [pl.BlockSpec((FOURIER_CH, 2 * FOURIER_CH), const2)]

    out_shapes, out_specs = [], []

    def add_out(width, dtype):
        out_shapes.append(jax.ShapeDtypeStruct((B, S, width), dtype))
        out_specs.append(pl.BlockSpec((None, tm, width), lambda b, i: (b, i, 0)))

    if full:
        add_out(N_HEADS * HEAD_PAD, BF16)
    add_out(N_HEADS * HEAD_PAD, BF16)
    add_out(N_HEADS * HEAD_PAD, BF16)
    if full:
        add_out(CONV_CH, F32)
        add_out(2 * FOURIER_CH, BF16)

    vmem = 2 * (D * ncol * 2 + Q_LORA * 1024 * 2 + KV_LORA * 2048 * 2) + 28 * tm * D * 4
    return pl.pallas_call(
        functools.partial(_front_kernel, full=full, rope=rope),
        grid=(B, S // tm),
        in_specs=specs,
        out_specs=out_specs,
        out_shape=out_shapes,
        compiler_params=_cparams(("arbitrary", "arbitrary"), vmem),
        name="front_full" if full else "front_kv",
    )(*args)


def _attn_kernel(*refs, nsrc, ncast):
    q_ref = refs[0]
    k_refs = refs[1:1 + nsrc]
    v_refs = refs[1 + nsrc:1 + 2 * nsrc]
    cast_in = refs[1 + 2 * nsrc:1 + 2 * nsrc + ncast]
    o_ref = refs[1 + 2 * nsrc + ncast]
    cast_out = refs[2 + 2 * nsrc + ncast:]
    chunks = []
    for k_ref, v_ref in zip(k_refs, v_refs):
        for s0 in range(0, k_ref.shape[0], TK_ATTN):
            chunks.append((k_ref, v_ref, s0, min(TK_ATTN, k_ref.shape[0] - s0)))
    tq = q_ref.shape[0]
    sub = min(TQ_SUB, tq)
    for r0 in range(0, tq, sub):
        rows = slice(r0, r0 + sub)
        state = [None, None]
        for k_ref, v_ref, s0, sz in chunks:
            for hh in range(2):
                sl = slice(hh * HEAD_PAD, (hh + 1) * HEAD_PAD)
                s = lax.dot_general(q_ref[rows, sl], k_ref[s0:s0 + sz, sl], (((1,), (1,)), ((), ())),
                                    preferred_element_type=F32)
                m = jnp.max(s, axis=-1, keepdims=True)
                if state[hh] is not None:
                    m_old, acc_old = state[hh]
                    m = jnp.maximum(m_old, m)
                pv = jnp.dot(jnp.exp2((s - m).astype(BF16)), v_ref[s0:s0 + sz, sl], preferred_element_type=F32)
                if state[hh] is not None:
                    pv = pv + jnp.exp2(m_old - m) * acc_old
                state[hh] = (m, pv)
        accs = [state[0][1], state[1][1]]
        lane = lax.broadcasted_iota(jnp.int32, accs[0].shape, 1)
        lo = accs[0] / pltpu.roll(accs[0], V_HEAD, 1)
        hi = pltpu.roll(accs[1], V_HEAD, 1) / accs[1]
        o_ref[rows, :] = jnp.where(lane < V_HEAD, lo, hi).astype(BF16)
    for src, dst in zip(cast_in, cast_out):
        dst[...] = src[...].astype(BF16)


def _attend(q, ks, vs, cast=()):
    B, S, _ = q.shape
    tq = min(TQ_ATTN, S)
    nsrc = len(ks)
    nq = S // tq
    steps = B * (N_HEADS // 2) * nq
    specs = [pl.BlockSpec((None, tq, 2 * HEAD_PAD), lambda b, hp, i: (b, i, hp))]
    for kv in list(ks) + list(vs):
        specs.append(pl.BlockSpec((None, kv.shape[1], 2 * HEAD_PAD), lambda b, hp, i: (b, 0, hp)))
    out_specs = [pl.BlockSpec((None, tq, 2 * V_HEAD), lambda b, hp, i: (b, i, hp))]
    out_shapes = [jax.ShapeDtypeStruct((B, S, ATTN_OUT), BF16)]
    cast_bytes = 0
    for w in cast:
        rows, cols = w.shape
        nblk = max(d for d in range(1, steps + 1)
                   if steps % d == 0 and rows % d == 0 and (rows // d) % (2 * SUBLANES) == 0)
        hold = steps // nblk
        spec = pl.BlockSpec((rows // nblk, cols),
                            lambda b, hp, i, hold=hold: (((b * (N_HEADS // 2) + hp) * nq + i) // hold, 0))
        specs.append(spec)
        out_specs.append(spec)
        out_shapes.append(jax.ShapeDtypeStruct(w.shape, BF16))
        cast_bytes += 2 * (rows // nblk) * cols * 6
    t_all = sum(k.shape[1] for k in ks)
    vmem = 2 * (t_all * 512 * 2) * 2 + 16 * min(tq, TQ_SUB) * TK_ATTN * 4 + cast_bytes + (12 << 20)
    outs = pl.pallas_call(
        functools.partial(_attn_kernel, nsrc=nsrc, ncast=len(cast)),
        grid=(B, N_HEADS // 2, nq),
        in_specs=specs,
        out_specs=out_specs,
        out_shape=out_shapes,
        compiler_params=_cparams(("arbitrary", "arbitrary", "arbitrary"), vmem),
        name="attend%d" % nsrc,
    )(q, *ks, *vs, *cast)
    return list(outs)


def _conv_kernel(y_ref, w_ref, b_ref, lg_ref, lb_ref, o_ref, pad_ref, *, seq):
    halo = 2 * SUBLANES
    total = seq + 2 * halo
    pad_ref[0, 0:halo, :] = jnp.zeros((halo, CONV_CH), F32)
    pad_ref[0, halo + seq:total, :] = jnp.zeros((halo, CONV_CH), F32)
    pad_ref[0, halo:halo + seq, :] = y_ref[...]
    for ph in range(1, SUBLANES):
        pad_ref[ph, 0:total - SUBLANES, :] = pad_ref[0, ph:ph + total - SUBLANES, :]
    rows = CONV_ROWS
    for base in range(0, seq, rows):
        acc = jnp.zeros((rows, CONV_CH), F32)
        for j in range(CONV_WIDTH):
            off = base + halo - CONV_PAD + j
            ph = off % SUBLANES
            acc = acc + pad_ref[ph, off - ph:off - ph + rows, :] * w_ref[j:j + 1, :]
        acc = acc + b_ref[...]
        mu = jnp.mean(acc, axis=-1, keepdims=True)
        cen = acc - mu
        var = jnp.mean(cen * cen, axis=-1, keepdims=True)
        yn = cen * lax.rsqrt(var + EPS) * lg_ref[...] + lb_ref[...]
        o_ref[base:base + rows, :] = (yn * _sigmoid(yn)).astype(BF16)


def _conv(y, p):
    B, S, _ = y.shape
    c2 = lambda b: (0, 0)
    return pl.pallas_call(
        functools.partial(_conv_kernel, seq=S),
        grid=(B,),
        in_specs=[pl.BlockSpec((None, S, CONV_CH), lambda b: (b, 0, 0)),
                  pl.BlockSpec((CONV_WIDTH, CONV_CH), c2),
                  pl.BlockSpec((1, CONV_CH), c2), pl.BlockSpec((1, CONV_CH), c2), pl.BlockSpec((1, CONV_CH), c2)],
        out_specs=pl.BlockSpec((None, S, CONV_CH), lambda b: (b, 0, 0)),
        out_shape=jax.ShapeDtypeStruct((B, S, CONV_CH), BF16),
        scratch_shapes=[pltpu.VMEM((SUBLANES, S + 4 * SUBLANES, CONV_CH), F32)],
        compiler_params=_cparams(("arbitrary",), (SUBLANES + 6) * (S + 32) * CONV_CH * 4 + (8 << 20)),
        name="conv",
    )(y, p["conv_w"], p["conv_b"], p["conv_ln_g"], p["conv_ln_b"])


def _fourier_kernel(cs_ref, zz_ref, o_ref, *, seq):
    acc = jnp.dot(cs_ref[:, 0:seq], zz_ref[:, 0:FOURIER_CH], preferred_element_type=F32)
    acc = acc + jnp.dot(cs_ref[:, seq:2 * seq], zz_ref[:, FOURIER_CH:2 * FOURIER_CH], preferred_element_type=F32)
    o_ref[...] = acc.astype(BF16)


def _fourier(zz, cs):
    B, S, _ = zz.shape
    tr = min(TR_FOURIER, S)
    return pl.pallas_call(
        functools.partial(_fourier_kernel, seq=S),
        grid=(S // tr, B),
        in_specs=[pl.BlockSpec((tr, 2 * S), lambda r, b: (r, 0)),
                  pl.BlockSpec((None, S, 2 * FOURIER_CH), lambda r, b: (b, 0, 0))],
        out_specs=pl.BlockSpec((None, tr, FOURIER_CH), lambda r, b: (b, r, 0)),
        out_shape=jax.ShapeDtypeStruct((B, S, FOURIER_CH), BF16),
        compiler_params=_cparams(("arbitrary", "arbitrary"), 2 * (tr * 2 * S * 2 + S * 512 * 2) + (8 << 20)),
        name="fourier",
    )(cs, zz)


def _dft_tables(seq):
    norm = seq ** -0.5
    ks = (np.arange(seq)[:, None] * np.arange(seq)[None, :]) % seq
    ang = 2.0 * np.pi * ks / seq
    tab = np.concatenate([np.cos(ang), -np.sin(ang)], axis=1) * norm
    return jnp.asarray(tab, F32).astype(BF16)


def _channel_dft():
    n = FOURIER_GROUP_DIM
    ang = 2.0 * np.pi * ((np.arange(n)[:, None] * np.arange(n)[None, :]) % n) / n
    eye = np.eye(FOURIER_GROUPS)
    bd = np.concatenate([np.kron(eye, np.cos(ang)), np.kron(eye, np.sin(ang))], axis=1) * n ** -0.5
    return jnp.asarray(bd, F32).astype(BF16)


def _mixout_kernel(*refs, moe):
    if moe:
        (attn_ref, conv_ref, four_ref, x_ref, mod_ref, wout_ref, g_ref, rw_ref, tri_ref,
         x1_out, h_out, route_out, cnt_out, routet_out) = refs
    else:
        (attn_ref, conv_ref, four_ref, x_ref, mod_ref, wout_ref, g_ref, wg_ref, wu_ref, wd_ref,
         x1_out, h_out) = refs
    tm = x_ref.shape[0]
    sub = min(SUB_ROWS_MOE if moe else SUB_ROWS, tm)
    for sb in range(tm // sub):
        rows = slice(sb * sub, (sb + 1) * sub)
        mix = jnp.dot(attn_ref[rows, :], wout_ref[0:ATTN_OUT, :], preferred_element_type=F32)
        mix = mix + jnp.dot(conv_ref[rows, :], wout_ref[ATTN_OUT:ATTN_OUT + CONV_CH, :],
                            preferred_element_type=F32)
        mix = mix + jnp.dot(four_ref[rows, :], wout_ref[ATTN_OUT + CONV_CH:, :], preferred_element_type=F32)
        x1 = x_ref[rows, :] + mod_ref[2:3, :] * mix
        x1_out[rows, :] = x1
        h = _rms(x1, g_ref[...]) * (1.0 + mod_ref[4:5, :]) + mod_ref[3:4, :]
        if moe:
            _store_token_tiles(h_out, sb * sub, h)
            route, cnt = _route(h, rw_ref, tri_ref)
            route_out[rows, :] = route
            cnt_out[sb] = cnt
            routet_out[:, rows] = route.T[0:SUBLANES, :]
        else:
            h_out[rows, :] = h.astype(BF16)
    if not moe:
        x1_out[...] = x1_out[...] + mod_ref[5:6, :] * _swiglu_tile(h_out[...], wg_ref, wu_ref, wd_ref)


def _store_token_tiles(ref, row0, val):
    rows = val.shape[0]
    for j in range(SUBLANES):
        ref[pl.ds(row0 * SUBLANES + j, rows, stride=SUBLANES), :] = val[:, j * LANES:(j + 1) * LANES]


def _load_token_tiles(ref, row0, rows, lead=()):
    idx = tuple(lead)
    return jnp.concatenate([ref[idx + (pl.ds(row0 * SUBLANES + j, rows, stride=SUBLANES), slice(None))]
                            for j in range(SUBLANES)], axis=1)


def _route(h, rw_ref, tri_ref):
    tm = h.shape[0]
    h_hi = h.astype(BF16)
    h_lo = (h - h_hi.astype(F32)).astype(BF16)
    hw = jnp.dot(h_hi, rw_ref[...], preferred_element_type=F32)
    logits = hw[:, 0:LANES] + hw[:, LANES:] + jnp.dot(h_lo, rw_ref[:, 0:LANES], preferred_element_type=F32)
    lane = lax.broadcasted_iota(jnp.int32, (tm, LANES), 1).astype(F32)
    neg = jnp.float32(-jnp.inf)
    lm = jnp.where(lane < N_EXPERTS, logits, neg)
    m1 = jnp.max(lm, axis=-1, keepdims=True)
    i1 = jnp.min(jnp.where(lm == m1, lane, float(LANES)), axis=-1, keepdims=True)
    lm2 = jnp.where(lane == i1, neg, lm)
    m2 = jnp.max(lm2, axis=-1, keepdims=True)
    i2 = jnp.min(jnp.where(lm2 == m2, lane, float(LANES)), axis=-1, keepdims=True)
    t = jnp.exp(m2 - m1)
    w1 = 1.0 / (1.0 + t)
    w2 = t / (1.0 + t)
    oh1 = lane == i1
    oh2 = lane == i2
    tri = tri_ref[...]
    c1 = jnp.dot(tri, jnp.where(oh1, 1.0, 0.0).astype(BF16), preferred_element_type=F32)
    c2 = jnp.dot(tri, jnp.where(oh2, 1.0, 0.0).astype(BF16), preferred_element_type=F32)
    r1 = jnp.sum(jnp.where(oh1, c1, 0.0), axis=-1, keepdims=True)
    r2 = jnp.sum(jnp.where(oh2, c2, 0.0), axis=-1, keepdims=True)
    vals = [i1, i2, w1, w2, r1, r2]
    route = jnp.zeros((tm, LANES), F32)
    for idx, v in enumerate(vals):
        route = jnp.where(lane == idx, v, route)
    n1 = jnp.sum(jnp.where(oh1, 1.0, 0.0), axis=0, keepdims=True)
    n2 = jnp.sum(jnp.where(oh2, 1.0, 0.0), axis=0, keepdims=True)
    row = lax.broadcasted_iota(jnp.int32, (SUBLANES, LANES), 0)
    return route, jnp.where(row == 0, n1, jnp.where(row == 1, n2, 0.0))


def _mix_out(attn, conv, four, xs, mod, p, *, moe):
    B, S, D = xs.shape
    tm = min(TM_OUT, S)
    sub = min(SUB_ROWS_MOE if moe else SUB_ROWS, tm)
    bm = mod.shape[0]
    mod_map = (lambda b, i: (b, 0, 0)) if bm > 1 else (lambda b, i: (0, 0, 0))
    c2 = lambda b, i: (0, 0)
    tok = lambda w: pl.BlockSpec((None, tm, w), lambda b, i: (b, i, 0))
    args = [attn, conv, four, xs, mod, p["w_out"], p["ffn_norm_g"]]
    specs = [tok(ATTN_OUT), tok(CONV_CH), tok(FOURIER_CH), tok(D), pl.BlockSpec((None, N_MOD, D), mod_map),
             pl.BlockSpec((D, D), c2), pl.BlockSpec((1, D), c2)]
    out_shapes = [jax.ShapeDtypeStruct((B, S, D), F32)]
    out_specs = [tok(D)]
    scratch = []
    vmem = 2 * D * D * 2 + 24 * tm * D * 4 + (8 << 20)
    if not moe:
        resident = lambda shape: pl.BlockSpec(shape, c2, pipeline_mode=pl.Buffered(1))
        args += [p["ffn_wg"], p["ffn_wu"], p["ffn_wd"]]
        specs += [resident((D, D_FF)), resident((D, D_FF)), resident((D_FF, D))]
        scratch = [pltpu.VMEM((tm, D), BF16)]
        vmem += 3 * D * D_FF * 2
    else:
        out_shapes.append(None)
        out_specs.append(None)
        per = S // tm
        out_shapes[1] = jax.ShapeDtypeStruct((B * S * SUBLANES, LANES), F32)
        out_specs[1] = pl.BlockSpec((tm * SUBLANES, LANES), lambda b, i: (b * per + i, 0))
        tri = jnp.asarray(np.tril(np.ones((sub, sub), np.float32), -1), BF16)
        args += [p["router_w"], tri]
        specs += [pl.BlockSpec((D, 2 * LANES), c2), pl.BlockSpec((sub, sub), c2)]
        out_shapes += [jax.ShapeDtypeStruct((B, S, LANES), F32),
                       jax.ShapeDtypeStruct((B, S // sub, SUBLANES, LANES), F32)]
        out_specs += [tok(LANES), pl.BlockSpec((None, tm // sub, SUBLANES, LANES), lambda b, i: (b, i, 0, 0))]
        out_shapes.append(jax.ShapeDtypeStruct((SUBLANES, B * S), F32))
        out_specs.append(pl.BlockSpec((SUBLANES, tm), lambda b, i: (0, b * per + i)))
    outs = pl.pallas_call(
        functools.partial(_mixout_kernel, moe=moe),
        grid=(B, S // tm),
        in_specs=specs,
        out_specs=out_specs,
        out_shape=out_shapes,
        scratch_shapes=scratch,
        compiler_params=_cparams(("arbitrary", "arbitrary"), vmem),
        name="mix_out_moe" if moe else "mix_ffn",
    )(*args)
    return outs if moe else outs[0]


def _swiglu_tile(hb, wg_ref, wu_ref, wd_ref):
    ff = wg_ref.shape[1]
    acc = jnp.zeros((hb.shape[0], D_MODEL), F32)
    for c0 in range(0, ff, FF_CHUNK):
        sl = slice(c0, min(c0 + FF_CHUNK, ff))
        g = jnp.dot(hb, wg_ref[:, sl].astype(BF16), preferred_element_type=F32)
        u = jnp.dot(hb, wu_ref[:, sl].astype(BF16), preferred_element_type=F32)
        a = (g * _sigmoid(g) * u).astype(BF16)
        acc = acc + jnp.dot(a, wd_ref[sl, :].astype(BF16), preferred_element_type=F32)
    return acc


def _token_copy(src_ref, src_row, dst_ref, dst_row, sem):
    return pltpu.make_async_copy(src_ref.at[pl.ds(pl.multiple_of(src_row, SUBLANES), SUBLANES), :],
                                 dst_ref.at[pl.ds(pl.multiple_of(dst_row, SUBLANES), SUBLANES), :], sem)


def _dispatch_kernel(slot_ref, pad_ref, h_ref, xs_ref, hbuf, zero_ref, in_sem, out_sem, *, tm, npad, steps):
    i = pl.program_id(0)

    tr = tm * SUBLANES

    def fetch(step, slot):
        return pltpu.make_async_copy(h_ref.at[pl.ds(step * tr, tr), :], hbuf.at[slot], in_sem.at[slot])

    def drain(slot):
        for _ in range(2):
            pltpu.make_async_copy(hbuf.at[slot], xs_ref.at[pl.ds(0, tr), :], out_sem.at[slot]).wait()
        pltpu.make_async_copy(hbuf.at[slot, pl.ds(0, npad * SUBLANES), :], xs_ref.at[pl.ds(0, npad * SUBLANES), :],
                              out_sem.at[slot]).wait()

    @pl.when(i == 0)
    def _():
        zero_ref[...] = jnp.zeros(zero_ref.shape, F32)
        fetch(0, 0).start()

    slot = lax.rem(i, 3)
    fetch(i, slot).wait()

    @pl.when(i + 1 < steps)
    def _():
        fetch(i + 1, lax.rem(i + 1, 3)).start()

    def scatter(r, carry):
        for k in range(2):
            _token_copy(hbuf.at[slot], r * SUBLANES, xs_ref, slot_ref[0, 0, k * tm + r],
                        out_sem.at[slot]).start(priority=k)
        return carry

    def scatter_pad(j, carry):
        _token_copy(zero_ref, 0, xs_ref, pad_ref[0, 0, j], out_sem.at[slot]).start()
        return carry

    lax.fori_loop(0, tm, scatter, 0, unroll=8)
    lax.fori_loop(0, npad, scatter_pad, 0, unroll=8)

    @pl.when(i > 0)
    def _():
        drain(lax.rem(i + 2, 3))

    @pl.when(i == steps - 1)
    def _():
        drain(slot)


def _dispatch(h_tiles, slots, pad_slots, n_tokens_out):
    n = h_tiles.shape[0] // SUBLANES
    tm = TM_ROUTE
    npad = pad_slots.shape[-1]
    steps = n // tm
    return pl.pallas_call(
        functools.partial(_dispatch_kernel, tm=tm, npad=npad, steps=steps),
        grid=(steps,),
        in_specs=[pl.BlockSpec((1, 1, 2 * tm), lambda i: (i, 0, 0), memory_space=pltpu.SMEM),
                  pl.BlockSpec((1, 1, npad), lambda i: (i, 0, 0), memory_space=pltpu.SMEM),
                  pl.BlockSpec(memory_space=pl.ANY)],
        out_specs=pl.BlockSpec(memory_space=pl.ANY),
        out_shape=jax.ShapeDtypeStruct((n_tokens_out * SUBLANES, LANES), F32),
        scratch_shapes=[pltpu.VMEM((3, tm * SUBLANES, LANES), F32), pltpu.VMEM((SUBLANES, LANES), F32),
                        pltpu.SemaphoreType.DMA((3,)), pltpu.SemaphoreType.DMA((3,))],
        compiler_params=_cparams(("arbitrary",), 3 * tm * D_MODEL * 4 + (4 << 20)),
        name="dispatch",
    )(slots, pad_slots, h_tiles)


def _experts_kernel(te_ref, nt_ref, xs_ref, wg_ref, wu_ref, wd_ref, o_ref, *, tm):
    t = pl.program_id(0)

    @pl.when(t < nt_ref[0])
    def _():
        hb = _load_token_tiles(xs_ref, 0, tm).astype(BF16)
        _store_token_tiles(o_ref, 0, _swiglu_tile(hb, wg_ref, wu_ref, wd_ref))

    @pl.when(t >= nt_ref[0])
    def _():
        o_ref[...] = jnp.zeros(o_ref.shape, F32)


def _experts(xs, tile_expert, n_tiles_used, p):
    D = D_MODEL
    tm = TM_MOE
    n_tiles = tile_expert.shape[0]
    wmap = lambda t, te, nt: (te[t], 0, 0)
    tok = pl.BlockSpec((tm * SUBLANES, LANES), lambda t, te, nt: (t, 0))
    tok_in = pl.BlockSpec((tm * SUBLANES, LANES), lambda t, te, nt: (jnp.minimum(t, nt[0] - 1), 0))
    vmem = 2 * 3 * D * D_FF * p["moe_wg"].dtype.itemsize + 12 * tm * D * 4 + (6 << 20)
    return pl.pallas_call(
        functools.partial(_experts_kernel, tm=tm),
        grid_spec=pltpu.PrefetchScalarGridSpec(
            num_scalar_prefetch=2,
            grid=(n_tiles,),
            in_specs=[tok_in,
                      pl.BlockSpec((None, D, D_FF), wmap),
                      pl.BlockSpec((None, D, D_FF), wmap),
                      pl.BlockSpec((None, D_FF, D), wmap)],
            out_specs=tok),
        out_shape=jax.ShapeDtypeStruct((n_tiles * tm * SUBLANES, LANES), F32),
        compiler_params=_cparams(("arbitrary",), vmem),
        name="experts",
    )(tile_expert, n_tiles_used, xs, p["moe_wg"], p["moe_wu"], p["moe_wd"])


def _combine_kernel(slot_ref, next_ref, x1_ref, route_ref, mod_ref, ys_ref, o_ref, buf_ref, sem, *, tm, steps):
    i = pl.program_id(0)

    def gather(idx_ref, s):
        def body(r, carry):
            for k in range(2):
                _token_copy(ys_ref, idx_ref[0, 0, k * tm + r], buf_ref.at[s, k], r * SUBLANES,
                            sem.at[s]).start(priority=k)
            return carry
        lax.fori_loop(0, tm, body, 0, unroll=8)

    @pl.when(i == 0)
    def _():
        gather(slot_ref, 0)

    @pl.when(i + 1 < steps)
    def _():
        gather(next_ref, lax.rem(i + 1, 2))

    cur = lax.rem(i, 2)
    for k in range(2):
        pltpu.make_async_copy(ys_ref.at[pl.ds(0, tm * SUBLANES), :], buf_ref.at[cur, k], sem.at[cur]).wait()
    route = route_ref[...]
    w1 = route[:, 2:3]
    w2 = route[:, 3:4]
    y0 = _load_token_tiles(buf_ref, 0, tm, lead=(cur, 0))
    y1 = _load_token_tiles(buf_ref, 0, tm, lead=(cur, 1))
    o_ref[...] = x1_ref[...] + mod_ref[5:6, :] * (w1 * y0 + w2 * y1)


def _combine(x1, route, mod, ys, slots):
    B, S, D = x1.shape
    tm = TM_ROUTE
    per = S // tm
    steps = B * per
    tok = lambda w: pl.BlockSpec((None, tm, w), lambda i: (i // per, i % per, 0))
    return pl.pallas_call(
        functools.partial(_combine_kernel, tm=tm, steps=steps),
        grid=(steps,),
        in_specs=[pl.BlockSpec((1, 1, 2 * tm), lambda i: (i, 0, 0), memory_space=pltpu.SMEM),
                  pl.BlockSpec((1, 1, 2 * tm), lambda i: (jnp.minimum(i + 1, steps - 1), 0, 0),
                               memory_space=pltpu.SMEM),
                  tok(D), tok(LANES), pl.BlockSpec((None, N_MOD, D), lambda i: (i // per, 0, 0)),
                  pl.BlockSpec(memory_space=pl.ANY)],
        out_specs=tok(D),
        out_shape=jax.ShapeDtypeStruct((B, S, D), F32),
        scratch_shapes=[pltpu.VMEM((2, 2, tm * SUBLANES, LANES), F32), pltpu.SemaphoreType.DMA((2,))],
        compiler_params=_cparams(("arbitrary",), 12 * tm * D * 4 + (4 << 20)),
        name="combine",
    )(slots, slots, x1, route, mod, ys)


def _moe(h, x1, route, cnt, route_t, mod, p):
    B, S, D = x1.shape
    n = B * S
    sub = n // (cnt.shape[0] * cnt.shape[1])
    tmm = TM_MOE
    n_tiles = 2 * n // tmm + N_EXPERTS
    n_slots = n_tiles * tmm
    steps = n // TM_ROUTE
    counts = cnt[:, :, 0:2, 0:N_EXPERTS].astype(jnp.int32).reshape(-1, 2, N_EXPERTS)
    tile_tot = counts.sum(axis=0)
    n_e = tile_tot.sum(axis=0)
    base = jnp.cumsum(counts, axis=0) - counts
    base = base + jnp.array([0, 1], jnp.int32)[None, :, None] * tile_tot[0][None, None, :]
    tiles_e = (n_e + tmm - 1) // tmm
    pstart = (jnp.cumsum(tiles_e) - tiles_e) * tmm
    expert_ids = jnp.arange(N_EXPERTS, dtype=jnp.int32)[:, None]
    slots_k = []
    for k in range(2):
        e_k = route_t[k].astype(jnp.int32)
        table = jnp.repeat((base[:, k, :] + pstart[None, :]).T, sub, axis=1)
        slot = jnp.sum(jnp.where(e_k[None, :] == expert_ids, table, 0), axis=0) + route_t[4 + k].astype(jnp.int32)
        slots_k.append((slot * SUBLANES).reshape(steps, TM_ROUTE))
    slots = jnp.concatenate(slots_k, axis=1).reshape(steps, 1, 2 * TM_ROUTE)
    tile_end = jnp.cumsum(tiles_e)
    n_used = tile_end[-1]
    t_idx = jnp.minimum(jnp.arange(n_tiles, dtype=jnp.int32), n_used - 1)
    tile_expert = jnp.sum(t_idx[:, None] >= tile_end[None, :], axis=-1).astype(jnp.int32)
    tile_expert = jnp.minimum(tile_expert, N_EXPERTS - 1)

    n_cand = N_EXPERTS * tmm
    pq = jnp.arange(tmm, dtype=jnp.int32)[None, :]
    n_pad_e = (tiles_e * tmm - n_e)[:, None]
    spare = n_slots + expert_ids * tmm + pq
    pad_slots = jnp.where(pq < n_pad_e, (pstart + n_e)[:, None] + pq, spare).astype(jnp.int32)
    assert n_cand % steps == 0
    pad_slots = (pad_slots * SUBLANES).reshape(steps, 1, n_cand // steps)

    xs = _dispatch(h, slots, pad_slots, n_slots + n_cand)
    ys = _experts(xs, tile_expert, n_used.reshape(1).astype(jnp.int32), p)
    return _combine(x1, route, mod, ys, slots)


def _pad_heads(w, width):
    lead = w.shape[:-1]
    w = w.reshape(lead + (N_HEADS, width))
    w = jnp.pad(w, [(0, 0)] * len(lead) + [(0, 0), (0, HEAD_PAD - width)])
    return w.reshape(lead + (N_HEADS * HEAD_PAD,))


def _layer_params(l, w_in, q_lat_g, kv_lat_g, w_uq, w_ukv, q_norm_g, k_norm_g, conv_w, conv_b, conv_ln_g,
                  conv_ln_b, w_out, ffn_norm_g):
    wi = w_in[l]
    kr = _with_partner(jnp.pad(wi[:, OFF_KR:OFF_CONV], ((0, 0), (QK_NOPE, HEAD_PAD - QK_HEAD))))
    ckv = wi[:, OFF_CKV:OFF_KR]
    w_full = jnp.concatenate([wi[:, OFF_CQ:OFF_CKV], ckv, wi[:, OFF_CONV:OFF_FOUR], wi[:, OFF_FOUR:IN_COLS], kr],
                             axis=1)
    w_kv = jnp.concatenate([ckv, kr], axis=1)
    ukv = w_ukv[l].reshape(KV_LORA, N_HEADS, QK_NOPE + V_HEAD)
    uk = _pad_heads(ukv[:, :, :QK_NOPE].reshape(KV_LORA, N_HEADS * QK_NOPE), QK_NOPE)
    uv = _pad_heads(ukv[:, :, QK_NOPE:].reshape(KV_LORA, ATTN_OUT), V_HEAD)
    pad_g = lambda g: jnp.pad(g, (0, HEAD_PAD - QK_HEAD)).reshape(1, HEAD_PAD)
    uq = _with_partner(_pad_heads(w_uq[l], QK_HEAD))
    hw = N_HEADS * HEAD_PAD
    col = np.arange(hw)[:, None]
    head_of_col = col // HEAD_PAD == np.arange(LANES)[None, :]
    return {
        "w_in_full": w_full.astype(BF16),
        "w_in_kv": w_kv.astype(BF16),
        "q_lat_g": q_lat_g[l].reshape(1, Q_LORA),
        "kv_lat_g": kv_lat_g[l].reshape(1, KV_LORA),
        "w_uq": uq.astype(BF16),
        "w_ukv": jnp.concatenate([uk, uv], axis=1).astype(BF16),
        "q_norm_g": pad_g(q_norm_g[l]) * (QK_HEAD ** -0.5 * math.log2(math.e)),
        "v_ones": jnp.tile(jnp.concatenate([jnp.zeros((V_HEAD,), F32), jnp.ones((HEAD_PAD - V_HEAD,), F32)]),
                           N_HEADS).reshape(1, hw),
        "ones_h": jnp.asarray(head_of_col & (col % HEAD_PAD < QK_HEAD), F32).astype(BF16),
        "k_norm_g": pad_g(k_norm_g[l]),
        "conv_w": conv_w[l],
        "conv_b": conv_b[l].reshape(1, CONV_CH),
        "conv_ln_g": conv_ln_g[l].reshape(1, CONV_CH),
        "conv_ln_b": conv_ln_b[l].reshape(1, CONV_CH),
        "ffn_norm_g": ffn_norm_g[l].reshape(1, D_MODEL),
        "bd": _channel_dft(),
    }


def _with_partner(w):
    lead = w.shape[:-1]
    half = QK_ROPE // 2
    w3 = w.reshape(lead + (-1, HEAD_PAD))
    out = jnp.concatenate([w3[..., :QK_HEAD], w3[..., QK_NOPE + half:QK_HEAD], w3[..., QK_NOPE:QK_NOPE + half]],
                          axis=-1)
    return out.reshape(w.shape)


def _rot_partner(w):
    lead = w.shape[:-1]
    half = QK_ROPE // 2
    w3 = w.reshape(lead + (-1, HEAD_PAD))
    z = jnp.zeros_like(w3)
    out = jnp.concatenate([z[..., :QK_NOPE], w3[..., QK_NOPE + half:QK_HEAD], w3[..., QK_NOPE:QK_NOPE + half],
                           z[..., QK_HEAD:]], axis=-1)
    return out.reshape(w.shape)


def _rope_tables(seq, gq, gk):
    rows = seq // GRID_W
    row = np.repeat(np.arange(rows, dtype=np.float64), GRID_W)
    col = np.tile(np.arange(GRID_W, dtype=np.float64), rows)
    n_freq = QK_ROPE // 4
    inv = ROPE_BASE ** (-np.arange(n_freq, dtype=np.float64) / n_freq)
    ang = np.concatenate([row[:, None] * inv, col[:, None] * inv], axis=-1)
    cos, sin = np.cos(ang), np.sin(ang)
    ones = np.ones((seq, QK_NOPE))
    tail = np.zeros((seq, HEAD_PAD - QK_HEAD))
    cos_t = jnp.asarray(np.concatenate([ones, cos, cos, tail], axis=1), F32)
    sin_t = jnp.asarray(np.concatenate([np.zeros((seq, QK_NOPE)), -sin, sin, tail], axis=1), F32)
    return (cos_t * gq, sin_t * _rot_partner(gq), cos_t * gk, sin_t * _rot_partner(gk))


def kernel(x, c, ctx, c_ctx, ada_w, ada_b, mix_norm_g, ffn_norm_g, w_in, q_lat_g, kv_lat_g, w_uq, w_ukv, q_norm_g,
           k_norm_g, conv_w, conv_b, conv_ln_g, conv_ln_b, w_out, ffn_w_gate, ffn_w_up, ffn_w_down, router_w,
           moe_w_gate, moe_w_up, moe_w_down):
    B, S, D = x.shape
    T = ctx.shape[1]
    assert (D, DEPTH) == (D_MODEL, ada_w.shape[0]) and S % GRID_W == 0

    cc = jnp.concatenate([c, c_ctx[None, :], jnp.zeros((2 * SUBLANES - B - 1, D), F32)], axis=0)
    mods = _modulation(cc, ada_w, ada_b).reshape(DEPTH, 2 * SUBLANES, N_MOD, D)
    cs_x = _dft_tables(S)
    cs_c = _dft_tables(T)

    def moe_f32(i):
        return (moe_w_gate[i].reshape(N_EXPERTS * D, D_FF), moe_w_up[i].reshape(N_EXPERTS * D, D_FF),
                moe_w_down[i].reshape(N_EXPERTS * D_FF, D))

    moe_bf16 = {}
    for l in range(DEPTH):
        last = l == DEPTH - 1
        p = _layer_params(l, w_in, q_lat_g, kv_lat_g, w_uq, w_ukv, q_norm_g, k_norm_g, conv_w, conv_b, conv_ln_g,
                          conv_ln_b, w_out, ffn_norm_g)
        p["mix_norm_g"] = mix_norm_g[l].reshape(1, D)
        rope_tabs = _rope_tables(S, p["q_norm_g"], p["k_norm_g"])
        i = l // 2
        moe = l % 2 == 1
        mod_x = mods[l, :B]
        mod_c = mods[l, B:B + 1]

        nxt = l + 1 if l % 2 == 0 else l + 2
        cast = []
        if not moe:
            cast += [ffn_w_gate[i], ffn_w_up[i], ffn_w_down[i]]
        if nxt < DEPTH:
            cast += list(moe_f32(nxt // 2))
        p["w_out"] = w_out[l].astype(BF16)

        def channel_mixer(attn, conv, four, xs, mod):
            if moe:
                x1, h, route, cnt, route_t = _mix_out(attn, conv, four, xs, mod, p, moe=True)
                mod_b = mod if mod.shape[0] > 1 else jnp.broadcast_to(mod, (B,) + mod.shape[1:])
                return _moe(h, x1, route, cnt, route_t, mod_b, p)
            return _mix_out(attn, conv, four, xs, mod, p, moe=False)

        if last:
            k_c, v_c = _front(ctx, mod_c, p["mix_norm_g"], p, None, full=False)
        else:
            q_c, k_c, v_c, y_c, zz_c = _front(ctx, mod_c, p["mix_norm_g"], p, None, full=True)
        q_x, k_x, v_x, y_x, zz_x = _front(x, mod_x, p["mix_norm_g"], p, rope_tabs, full=True)
        attn_x, *casted = _attend(q_x, [k_x, k_c], [v_x, v_c], cast)
        if moe:
            rw = jnp.pad(router_w[i], ((0, 0), (0, LANES - N_EXPERTS)))
            rw_hi = rw.astype(BF16)
            p["router_w"] = jnp.concatenate([rw_hi, (rw - rw_hi.astype(F32)).astype(BF16)], axis=1)
            wg, wu, wd = moe_bf16.pop(i) if i in moe_bf16 else [w.astype(BF16) for w in moe_f32(i)]
            p["moe_wg"] = wg.reshape(N_EXPERTS, D, D_FF)
            p["moe_wu"] = wu.reshape(N_EXPERTS, D, D_FF)
            p["moe_wd"] = wd.reshape(N_EXPERTS, D_FF, D)
        else:
            p["ffn_wg"], p["ffn_wu"], p["ffn_wd"] = casted[:3]
            casted = casted[3:]
        if nxt < DEPTH:
            moe_bf16[nxt // 2] = casted

        if not last:
            attn_c, = _attend(q_c, [k_c], [v_c])
            ctx_next = channel_mixer(attn_c, _conv(y_c, p), _fourier(zz_c, cs_c), ctx, mod_c)

        x = channel_mixer(attn_x, _conv(y_x, p), _fourier(zz_x, cs_x), x, mod_x)
        if not last:
            ctx = ctx_next
    return x
```

```python
import functools
import math

import numpy as np
import jax
import jax.numpy as jnp
from jax import lax
from jax.experimental import pallas as pl
from jax.experimental.pallas import tpu as pltpu

F32 = jnp.float32
BF16 = jnp.bfloat16

D_MODEL = 1024
DEPTH = 2
GRID_W = 64
N_HEADS = 8
QK_NOPE = 64
QK_ROPE = 32
QK_HEAD = QK_NOPE + QK_ROPE
V_HEAD = 64
Q_LORA = 384
KV_LORA = 256
ROPE_BASE = 10000.0
CONV_CH = 256
CONV_WIDTH = 31
CONV_PAD = (CONV_WIDTH - 1) // 2
FOURIER_GROUPS = 4
FOURIER_GROUP_DIM = 64
FOURIER_CH = FOURIER_GROUPS * FOURIER_GROUP_DIM
ATTN_OUT = N_HEADS * V_HEAD
OFF_CQ = 0
OFF_CKV = OFF_CQ + Q_LORA
OFF_KR = OFF_CKV + KV_LORA
OFF_CONV = OFF_KR + QK_ROPE
OFF_FOUR = OFF_CONV + 2 * CONV_CH
IN_COLS = OFF_FOUR + FOURIER_CH
D_FF = 2816
N_EXPERTS = 8
N_MOD = 6
EPS = 1e-6

LANES = 128
SUBLANES = 8
HEAD_PAD = LANES
VMEM_CAP = 56 * 1024 * 1024
FF_CHUNK = 256

SUB_ROWS = 256
SUB_ROWS_MOE = 512
TM_FRONT = 512
TQ_ATTN = 1024
TQ_SUB = 512
TK_ATTN = 256
TM_OUT = 512
TM_MOE = 512
TM_ROUTE = 512
CONV_ROWS = 128
TR_FOURIER = 512


def _cparams(sem, vmem_bytes):
    return pltpu.CompilerParams(dimension_semantics=sem, vmem_limit_bytes=int(min(VMEM_CAP, vmem_bytes)))


def _rms(v, g):
    return v * lax.rsqrt(jnp.mean(v * v, axis=-1, keepdims=True) + EPS) * g


def _sigmoid(v):
    return 1.0 / (1.0 + jnp.exp(-v))


def _mod_kernel(c_ref, w_ref, b_ref, o_ref):
    c = c_ref[...]
    s = (c * _sigmoid(c)).astype(BF16)
    o_ref[...] = jnp.dot(s, w_ref[...].astype(BF16), preferred_element_type=F32) + b_ref[...]


def _modulation(cc, ada_w, ada_b):
    rows = cc.shape[0]
    tn = 1536
    n_out = N_MOD * D_MODEL
    return pl.pallas_call(
        _mod_kernel,
        grid=(DEPTH, n_out // tn),
        in_specs=[
            pl.BlockSpec((rows, D_MODEL), lambda l, j: (0, 0)),
            pl.BlockSpec((None, D_MODEL, tn), lambda l, j: (l, 0, j)),
            pl.BlockSpec((None, 1, tn), lambda l, j: (l, 0, j)),
        ],
        out_specs=pl.BlockSpec((None, rows, tn), lambda l, j: (l, 0, j)),
        out_shape=jax.ShapeDtypeStruct((DEPTH, rows, n_out), F32),
        compiler_params=_cparams(("arbitrary", "arbitrary"), 4 * D_MODEL * tn * 4),
        name="modulation",
    )(cc, ada_w, ada_b.reshape(DEPTH, 1, n_out))


def _front_kernel(*refs, full, rope):
    it = iter(refs)
    x_ref, mod_ref, g_ref, win_ref = next(it), next(it), next(it), next(it)
    if full:
        qlg_ref, wuq_ref = next(it), next(it)
    kvlg_ref, wukv_ref, vones_ref, onesh_ref = next(it), next(it), next(it), next(it)
    if rope:
        cq_ref, sq_ref, ck_ref, sk_ref = next(it), next(it), next(it), next(it)
    else:
        qg_ref = next(it) if full else None
        kg_ref = next(it)
    if full:
        bd_ref = next(it)
        q_out = next(it)
    k_out, v_out = next(it), next(it)
    if full:
        y_out, zz_out = next(it), next(it)

    shift = mod_ref[0:1, :]
    scale = mod_ref[1:2, :]
    hw = N_HEADS * HEAD_PAD

    def head_inv_rms(raw):
        ss = jnp.dot((raw * raw).astype(BF16), onesh_ref[...], preferred_element_type=F32)
        rs = lax.rsqrt(ss * (1.0 / QK_HEAD) + EPS)
        return jnp.concatenate([jnp.broadcast_to(rs[:, hd:hd + 1], (rs.shape[0], HEAD_PAD)) for hd in range(N_HEADS)],
                               axis=1)

    def sub_block(rows):
        x = x_ref[rows, :]
        h = _rms(x, g_ref[...]) * (1.0 + scale) + shift
        cols = jnp.dot(h.astype(BF16), win_ref[...], preferred_element_type=F32)

        o = 0
        if full:
            cq = cols[:, 0:Q_LORA]
            o = Q_LORA
            qall = jnp.dot(_rms(cq, qlg_ref[...]).astype(BF16), wuq_ref[...], preferred_element_type=F32)
            rsb = head_inv_rms(qall)
            for hd in range(N_HEADS):
                sl = slice(hd * HEAD_PAD, (hd + 1) * HEAD_PAD)
                if rope:
                    val = qall[:, sl] * cq_ref[rows, :] + pltpu.roll(qall[:, sl], HEAD_PAD - QK_ROPE, 1) * \
                        sq_ref[rows, :]
                else:
                    val = qall[:, sl] * qg_ref[...]
                q_out[rows, sl] = (val * rsb[:, sl]).astype(BF16)

        ckv = cols[:, o:o + KV_LORA]
        o += KV_LORA
        kv = jnp.dot(_rms(ckv, kvlg_ref[...]).astype(BF16), wukv_ref[...], preferred_element_type=F32)
        if full:
            a = cols[:, o:o + CONV_CH]
            gt = cols[:, o + CONV_CH:o + 2 * CONV_CH]
            y_out[rows, :] = a * _sigmoid(gt)
            o += 2 * CONV_CH
            z = cols[:, o:o + FOURIER_CH]
            o += FOURIER_CH
            zz_out[rows, :] = jnp.dot(z.astype(BF16), bd_ref[...], preferred_element_type=F32).astype(BF16)
        krb = cols[:, o:o + HEAD_PAD]
        kraw = [kv[:, hd * HEAD_PAD:(hd + 1) * HEAD_PAD] + krb for hd in range(N_HEADS)]
        rsb = head_inv_rms(jnp.concatenate(kraw, axis=1))
        if rope:
            kpart = pltpu.roll(krb, HEAD_PAD - QK_ROPE, 1) * sk_ref[rows, :]
        for hd in range(N_HEADS):
            sl = slice(hd * HEAD_PAD, (hd + 1) * HEAD_PAD)
            val = kraw[hd] * ck_ref[rows, :] + kpart if rope else kraw[hd] * kg_ref[...]
            k_out[rows, sl] = (val * rsb[:, sl]).astype(BF16)
        v_out[rows, :] = (kv[:, hw:] + vones_ref[...]).astype(BF16)

    tm = x_ref.shape[0]
    sub = min(SUB_ROWS, tm)
    for sb in range(tm // sub):
        sub_block(slice(sb * sub, (sb + 1) * sub))


def _front(xs, mod, norm_g, p, rope_tabs, *, full):
    if mod.shape[0] == 1 and xs.shape[0] > 1 and rope_tabs is None:
        outs = _front(xs.reshape(1, -1, xs.shape[-1]), mod, norm_g, p, None, full=full)
        return [o.reshape(xs.shape[:2] + o.shape[2:]) for o in outs]
    B, S, D = xs.shape
    tm = min(TM_FRONT, S)
    rope = rope_tabs is not None
    assert full or not rope
    w_in = p["w_in_full"] if full else p["w_in_kv"]
    w_uq = p["w_uq"]
    ncol = w_in.shape[1]
    bm = mod.shape[0]
    mod_map = (lambda b, i: (b, 0, 0)) if bm > 1 else (lambda b, i: (0, 0, 0))
    const2 = lambda b, i: (0, 0)
    hw = N_HEADS * HEAD_PAD

    args = [xs, mod, norm_g, w_in]
    specs = [
        pl.BlockSpec((None, tm, D), lambda b, i: (b, i, 0)),
        pl.BlockSpec((None, N_MOD, D), mod_map),
        pl.BlockSpec((1, D), const2),
        pl.BlockSpec((D, ncol), const2),
    ]
    if full:
        args += [p["q_lat_g"], w_uq]
        specs += [pl.BlockSpec((1, Q_LORA), const2), pl.BlockSpec((Q_LORA, w_uq.shape[1]), const2)]
    args += [p["kv_lat_g"], p["w_ukv"], p["v_ones"], p["ones_h"]]
    specs += [pl.BlockSpec((1, KV_LORA), const2),
              pl.BlockSpec((KV_LORA, 2 * hw), const2),
              pl.BlockSpec((1, hw), const2),
              pl.BlockSpec((hw, LANES), const2)]
    if rope:
        args += list(rope_tabs)
        specs += [pl.BlockSpec((tm, HEAD_PAD), lambda b, i: (i, 0))] * 4
    else:
        if full:
            args += [p["q_norm_g"]]
            specs += [pl.BlockSpec((1, HEAD_PAD), const2)]
        args += [p["k_norm_g"]]
        specs += [pl.BlockSpec((1, HEAD_PAD), const2)]
    if full:
        args += [p["bd"]]
        specs += [pl.BlockSpec((FOURIER_CH, 2 * FOURIER_CH), const2)]

    out_shapes, out_specs = [], []

    def add_out(width, dtype):
        out_shapes.append(jax.ShapeDtypeStruct((B, S, width), dtype))
        out_specs.append(pl.BlockSpec((None, tm, width), lambda b, i: (b, i, 0)))

    if full:
        add_out(N_HEADS * HEAD_PAD, BF16)
    add_out(N_HEADS * HEAD_PAD, BF16)
    add_out(N_HEADS * HEAD_PAD, BF16)
    if full:
        add_out(CONV_CH, F32)
        add_out(2 * FOURIER_CH, BF16)

    vmem = 2 * (D * ncol * 2 + Q_LORA * 1024 * 2 + KV_LORA * 2048 * 2) + 28 * tm * D * 4
    return pl.pallas_call(
        functools.partial(_front_kernel, full=full, rope=rope),
        grid=(B, S // tm),
        in_specs=specs,
        out_specs=out_specs,
        out_shape=out_shapes,
        compiler_params=_cparams(("arbitrary", "arbitrary"), vmem),
        name="front_full" if full else "front_kv",
    )(*args)


def _attn_kernel(*refs, nsrc, ncast):
    q_ref = refs[0]
    k_refs = refs[1:1 + nsrc]
    v_refs = refs[1 + nsrc:1 + 2 * nsrc]
    cast_in = refs[1 + 2 * nsrc:1 + 2 * nsrc + ncast]
    o_ref = refs[1 + 2 * nsrc + ncast]
    cast_out = refs[2 + 2 * nsrc + ncast:]
    chunks = []
    for k_ref, v_ref in zip(k_refs, v_refs):
        for s0 in range(0, k_ref.shape[0], TK_ATTN):
            chunks.append((k_ref, v_ref, s0, min(TK_ATTN, k_ref.shape[0] - s0)))
    tq = q_ref.shape[0]
    sub = min(TQ_SUB, tq)
    for r0 in range(0, tq, sub):
        rows = slice(r0, r0 + sub)
        state = [None, None]
        for k_ref, v_ref, s0, sz in chunks:
            for hh in range(2):
                sl = slice(hh * HEAD_PAD, (hh + 1) * HEAD_PAD)
                s = lax.dot_general(q_ref[rows, sl], k_ref[s0:s0 + sz, sl], (((1,), (1,)), ((), ())),
                                    preferred_element_type=F32)
                m = jnp.max(s, axis=-1, keepdims=True)
                if state[hh] is not None:
                    m_old, acc_old = state[hh]
                    m = jnp.maximum(m_old, m)
                pv = jnp.dot(jnp.exp2((s - m).astype(BF16)), v_ref[s0:s0 + sz, sl], preferred_element_type=F32)
                if state[hh] is not None:
                    pv = pv + jnp.exp2(m_old - m) * acc_old
                state[hh] = (m, pv)
        accs = [state[0][1], state[1][1]]
        lane = lax.broadcasted_iota(jnp.int32, accs[0].shape, 1)
        lo = accs[0] / pltpu.roll(accs[0], V_HEAD, 1)
        hi = pltpu.roll(accs[1], V_HEAD, 1) / accs[1]
        o_ref[rows, :] = jnp.where(lane < V_HEAD, lo, hi).astype(BF16)
    for src, dst in zip(cast_in, cast_out):
        dst[...] = src[...].astype(BF16)


def _attend(q, ks, vs, cast=()):
    B, S, _ = q.shape
    tq = min(TQ_ATTN, S)
    nsrc = len(ks)
    nq = S // tq
    steps = B * (N_HEADS // 2) * nq
    specs = [pl.BlockSpec((None, tq, 2 * HEAD_PAD), lambda b, hp, i: (b, i, hp))]
    for kv in list(ks) + list(vs):
        specs.append(pl.BlockSpec((None, kv.shape[1], 2 * HEAD_PAD), lambda b, hp, i: (b, 0, hp)))
    out_specs = [pl.BlockSpec((None, tq, 2 * V_HEAD), lambda b, hp, i: (b, i, hp))]
    out_shapes = [jax.ShapeDtypeStruct((B, S, ATTN_OUT), BF16)]
    cast_bytes = 0
    for w in cast:
        rows, cols = w.shape
        nblk = max(d for d in range(1, steps + 1)
                   if steps % d == 0 and rows % d == 0 and (rows // d) % (2 * SUBLANES) == 0)
        hold = steps // nblk
        spec = pl.BlockSpec((rows // nblk, cols),
                            lambda b, hp, i, hold=hold: (((b * (N_HEADS // 2) + hp) * nq + i) // hold, 0))
        specs.append(spec)
        out_specs.append(spec)
        out_shapes.append(jax.ShapeDtypeStruct(w.shape, BF16))
        cast_bytes += 2 * (rows // nblk) * cols * 6
    t_all = sum(k.shape[1] for k in ks)
    vmem = 2 * (t_all * 512 * 2) * 2 + 16 * min(tq, TQ_SUB) * TK_ATTN * 4 + cast_bytes + (12 << 20)
    outs = pl.pallas_call(
        functools.partial(_attn_kernel, nsrc=nsrc, ncast=len(cast)),
        grid=(B, N_HEADS // 2, nq),
        in_specs=specs,
        out_specs=out_specs,
        out_shape=out_shapes,
        compiler_params=_cparams(("arbitrary", "arbitrary", "arbitrary"), vmem),
        name="attend%d" % nsrc,
    )(q, *ks, *vs, *cast)
    return list(outs)


def _conv_fourier_kernel(y_ref, w_ref, b_ref, lg_ref, lb_ref, cs_ref, zz_ref, conv_out, four_out, pad_ref, *, seq):
    halo = 2 * SUBLANES
    total = seq + 2 * halo
    pad_ref[0, 0:halo, :] = jnp.zeros((halo, CONV_CH), F32)
    pad_ref[0, halo + seq:total, :] = jnp.zeros((halo, CONV_CH), F32)
    pad_ref[0, halo:halo + seq, :] = y_ref[...]
    for ph in range(1, SUBLANES):
        pad_ref[ph, 0:total - SUBLANES, :] = pad_ref[0, ph:ph + total - SUBLANES, :]

    def conv_rows(base, rows):
        acc = jnp.zeros((rows, CONV_CH), F32)
        for j in range(CONV_WIDTH):
            off = base + halo - CONV_PAD + j
            ph = off % SUBLANES
            acc = acc + pad_ref[ph, off - ph:off - ph + rows, :] * w_ref[j:j + 1, :]
        acc = acc + b_ref[...]
        mu = jnp.mean(acc, axis=-1, keepdims=True)
        cen = acc - mu
        var = jnp.mean(cen * cen, axis=-1, keepdims=True)
        yn = cen * lax.rsqrt(var + EPS) * lg_ref[...] + lb_ref[...]
        conv_out[base:base + rows, :] = (yn * _sigmoid(yn)).astype(BF16)

    def dft_rows(base, rows):
        r = slice(base, base + rows)
        acc = jnp.dot(cs_ref[r, 0:seq], zz_ref[:, 0:FOURIER_CH], preferred_element_type=F32)
        acc = acc + jnp.dot(cs_ref[r, seq:2 * seq], zz_ref[:, FOURIER_CH:2 * FOURIER_CH],
                            preferred_element_type=F32)
        four_out[r, :] = acc.astype(BF16)

    tr = min(TR_FOURIER, seq)
    for base in range(0, seq, tr):
        for cb in range(base, base + tr, CONV_ROWS):
            conv_rows(cb, CONV_ROWS)
        dft_rows(base, tr)


def _conv_fourier(y, zz, cs, p):
    B, S, _ = y.shape
    c2 = lambda b: (0, 0)
    seq_block = lambda w: pl.BlockSpec((None, S, w), lambda b: (b, 0, 0))
    vmem = (SUBLANES + 6) * (S + 32) * CONV_CH * 4 + S * 2 * S * 2 + 2 * S * 512 * 2 + (10 << 20)
    return pl.pallas_call(
        functools.partial(_conv_fourier_kernel, seq=S),
        grid=(B,),
        in_specs=[seq_block(CONV_CH),
                  pl.BlockSpec((CONV_WIDTH, CONV_CH), c2),
                  pl.BlockSpec((1, CONV_CH), c2), pl.BlockSpec((1, CONV_CH), c2), pl.BlockSpec((1, CONV_CH), c2),
                  pl.BlockSpec((S, 2 * S), c2, pipeline_mode=pl.Buffered(1)),
                  seq_block(2 * FOURIER_CH)],
        out_specs=[seq_block(CONV_CH), seq_block(FOURIER_CH)],
        out_shape=[jax.ShapeDtypeStruct((B, S, CONV_CH), BF16), jax.ShapeDtypeStruct((B, S, FOURIER_CH), BF16)],
        scratch_shapes=[pltpu.VMEM((SUBLANES, S + 4 * SUBLANES, CONV_CH), F32)],
        compiler_params=_cparams(("arbitrary",), vmem),
        name="conv_fourier",
    )(y, p["conv_w"], p["conv_b"], p["conv_ln_g"], p["conv_ln_b"], cs, zz)


def _dft_tables(seq):
    norm = seq ** -0.5
    ks = (np.arange(seq)[:, None] * np.arange(seq)[None, :]) % seq
    ang = 2.0 * np.pi * ks / seq
    tab = np.concatenate([np.cos(ang), -np.sin(ang)], axis=1) * norm
    return jnp.asarray(tab, F32).astype(BF16)


def _channel_dft():
    n = FOURIER_GROUP_DIM
    ang = 2.0 * np.pi * ((np.arange(n)[:, None] * np.arange(n)[None, :]) % n) / n
    eye = np.eye(FOURIER_GROUPS)
    bd = np.concatenate([np.kron(eye, np.cos(ang)), np.kron(eye, np.sin(ang))], axis=1) * n ** -0.5
    return jnp.asarray(bd, F32).astype(BF16)


def _mixout_kernel(*refs, moe):
    if moe:
        (attn_ref, conv_ref, four_ref, x_ref, mod_ref, wout_ref, g_ref, rw_ref, tri_ref,
         x1_out, h_out, route_out, cnt_out, routet_out) = refs
    else:
        (attn_ref, conv_ref, four_ref, x_ref, mod_ref, wout_ref, g_ref, wg_ref, wu_ref, wd_ref,
         x1_out, h_out) = refs
    tm = x_ref.shape[0]
    sub = min(SUB_ROWS_MOE if moe else SUB_ROWS, tm)
    for sb in range(tm // sub):
        rows = slice(sb * sub, (sb + 1) * sub)
        mix = jnp.dot(attn_ref[rows, :], wout_ref[0:ATTN_OUT, :], preferred_element_type=F32)
        mix = mix + jnp.dot(conv_ref[rows, :], wout_ref[ATTN_OUT:ATTN_OUT + CONV_CH, :],
                            preferred_element_type=F32)
        mix = mix + jnp.dot(four_ref[rows, :], wout_ref[ATTN_OUT + CONV_CH:, :], preferred_element_type=F32)
        x1 = x_ref[rows, :] + mod_ref[2:3, :] * mix
        x1_out[rows, :] = x1
        h = _rms(x1, g_ref[...]) * (1.0 + mod_ref[4:5, :]) + mod_ref[3:4, :]
        if moe:
            _store_token_tiles(h_out, sb * sub, h)
            route, cnt = _route(h, rw_ref, tri_ref)
            route_out[rows, :] = route
            cnt_out[sb] = cnt
            routet_out[:, rows] = route.T[0:SUBLANES, :]
        else:
            h_out[rows, :] = h.astype(BF16)
    if not moe:
        x1_out[...] = x1_out[...] + mod_ref[5:6, :] * _swiglu_tile(h_out[...], wg_ref, wu_ref, wd_ref)


def _store_token_tiles(ref, row0, val):
    rows = val.shape[0]
    for j in range(SUBLANES):
        ref[pl.ds(row0 * SUBLANES + j, rows, stride=SUBLANES), :] = val[:, j * LANES:(j + 1) * LANES]


def _load_token_tiles(ref, row0, rows, lead=()):
    idx = tuple(lead)
    return jnp.concatenate([ref[idx + (pl.ds(row0 * SUBLANES + j, rows, stride=SUBLANES), slice(None))]
                            for j in range(SUBLANES)], axis=1)


def _route(h, rw_ref, tri_ref):
    tm = h.shape[0]
    h_hi = h.astype(BF16)
    h_lo = (h - h_hi.astype(F32)).astype(BF16)
    hw = jnp.dot(h_hi, rw_ref[...], preferred_element_type=F32)
    logits = hw[:, 0:LANES] + hw[:, LANES:] + jnp.dot(h_lo, rw_ref[:, 0:LANES], preferred_element_type=F32)
    lane = lax.broadcasted_iota(jnp.int32, (tm, LANES), 1).astype(F32)
    neg = jnp.float32(-jnp.inf)
    lm = jnp.where(lane < N_EXPERTS, logits, neg)
    m1 = jnp.max(lm, axis=-1, keepdims=True)
    i1 = jnp.min(jnp.where(lm == m1, lane, float(LANES)), axis=-1, keepdims=True)
    lm2 = jnp.where(lane == i1, neg, lm)
    m2 = jnp.max(lm2, axis=-1, keepdims=True)
    i2 = jnp.min(jnp.where(lm2 == m2, lane, float(LANES)), axis=-1, keepdims=True)
    t = jnp.exp(m2 - m1)
    w1 = 1.0 / (1.0 + t)
    w2 = t / (1.0 + t)
    oh1 = lane == i1
    oh2 = lane == i2
    tri = tri_ref[...]
    c1 = jnp.dot(tri, jnp.where(oh1, 1.0, 0.0).astype(BF16), preferred_element_type=F32)
    c2 = jnp.dot(tri, jnp.where(oh2, 1.0, 0.0).astype(BF16), preferred_element_type=F32)
    r1 = jnp.sum(jnp.where(oh1, c1, 0.0), axis=-1, keepdims=True)
    r2 = jnp.sum(jnp.where(oh2, c2, 0.0), axis=-1, keepdims=True)
    vals = [i1, i2, w1, w2, r1, r2]
    route = jnp.zeros((tm, LANES), F32)
    for idx, v in enumerate(vals):
        route = jnp.where(lane == idx, v, route)
    n1 = jnp.sum(jnp.where(oh1, 1.0, 0.0), axis=0, keepdims=True)
    n2 = jnp.sum(jnp.where(oh2, 1.0, 0.0), axis=0, keepdims=True)
    row = lax.broadcasted_iota(jnp.int32, (SUBLANES, LANES), 0)
    return route, jnp.where(row == 0, n1, jnp.where(row == 1, n2, 0.0))


def _mix_out(attn, conv, four, xs, mod, p, *, moe):
    if mod.shape[0] == 1 and xs.shape[0] > 1 and not moe:
        flat = lambda a: a.reshape(1, -1, a.shape[-1])
        return _mix_out(flat(attn), flat(conv), flat(four), flat(xs), mod, p, moe=moe).reshape(xs.shape)
    B, S, D = xs.shape
    tm = min(TM_OUT, S)
    sub = min(SUB_ROWS_MOE if moe else SUB_ROWS, tm)
    bm = mod.shape[0]
    mod_map = (lambda b, i: (b, 0, 0)) if bm > 1 else (lambda b, i: (0, 0, 0))
    c2 = lambda b, i: (0, 0)
    tok = lambda w: pl.BlockSpec((None, tm, w), lambda b, i: (b, i, 0))
    args = [attn, conv, four, xs, mod, p["w_out"], p["ffn_norm_g"]]
    specs = [tok(ATTN_OUT), tok(CONV_CH), tok(FOURIER_CH), tok(D), pl.BlockSpec((None, N_MOD, D), mod_map),
             pl.BlockSpec((D, D), c2), pl.BlockSpec((1, D), c2)]
    out_shapes = [jax.ShapeDtypeStruct((B, S, D), F32)]
    out_specs = [tok(D)]
    scratch = []
    vmem = 2 * D * D * 2 + 24 * tm * D * 4 + (8 << 20)
    if not moe:
        resident = lambda shape: pl.BlockSpec(shape, c2, pipeline_mode=pl.Buffered(1))
        args += [p["ffn_wg"], p["ffn_wu"], p["ffn_wd"]]
        specs += [resident((D, D_FF)), resident((D, D_FF)), resident((D_FF, D))]
        scratch = [pltpu.VMEM((tm, D), BF16)]
        vmem += 3 * D * D_FF * 2
    else:
        out_shapes.append(None)
        out_specs.append(None)
        per = S // tm
        out_shapes[1] = jax.ShapeDtypeStruct((B * S * SUBLANES, LANES), F32)
        out_specs[1] = pl.BlockSpec((tm * SUBLANES, LANES), lambda b, i: (b * per + i, 0))
        tri = jnp.asarray(np.tril(np.ones((sub, sub), np.float32), -1), BF16)
        args += [p["router_w"], tri]
        specs += [pl.BlockSpec((D, 2 * LANES), c2), pl.BlockSpec((sub, sub), c2)]
        out_shapes += [jax.ShapeDtypeStruct((B, S, LANES), F32),
                       jax.ShapeDtypeStruct((B, S // sub, SUBLANES, LANES), F32)]
        out_specs += [tok(LANES), pl.BlockSpec((None, tm // sub, SUBLANES, LANES), lambda b, i: (b, i, 0, 0))]
        out_shapes.append(jax.ShapeDtypeStruct((SUBLANES, B * S), F32))
        out_specs.append(pl.BlockSpec((SUBLANES, tm), lambda b, i: (0, b * per + i)))
    outs = pl.pallas_call(
        functools.partial(_mixout_kernel, moe=moe),
        grid=(B, S // tm),
        in_specs=specs,
        out_specs=out_specs,
        out_shape=out_shapes,
        scratch_shapes=scratch,
        compiler_params=_cparams(("arbitrary", "arbitrary"), vmem),
        name="mix_out_moe" if moe else "mix_ffn",
    )(*args)
    return outs if moe else outs[0]


def _swiglu_tile(hb, wg_ref, wu_ref, wd_ref):
    ff = wg_ref.shape[1]
    acc = jnp.zeros((hb.shape[0], D_MODEL), F32)
    for c0 in range(0, ff, FF_CHUNK):
        sl = slice(c0, min(c0 + FF_CHUNK, ff))
        g = jnp.dot(hb, wg_ref[:, sl].astype(BF16), preferred_element_type=F32)
        u = jnp.dot(hb, wu_ref[:, sl].astype(BF16), preferred_element_type=F32)
        a = (g * _sigmoid(g) * u).astype(BF16)
        acc = acc + jnp.dot(a, wd_ref[sl, :].astype(BF16), preferred_element_type=F32)
    return acc


def _token_copy(src_ref, src_row, dst_ref, dst_row, sem):
    return pltpu.make_async_copy(src_ref.at[pl.ds(pl.multiple_of(src_row, SUBLANES), SUBLANES), :],
                                 dst_ref.at[pl.ds(pl.multiple_of(dst_row, SUBLANES), SUBLANES), :], sem)


def _dispatch_kernel(slot_ref, pad_ref, h_ref, xs_ref, hbuf, zero_ref, in_sem, out_sem, *, tm, npad, steps):
    i = pl.program_id(0)

    tr = tm * SUBLANES

    def fetch(step, slot):
        return pltpu.make_async_copy(h_ref.at[pl.ds(step * tr, tr), :], hbuf.at[slot], in_sem.at[slot])

    def drain(slot):
        for _ in range(2):
            pltpu.make_async_copy(hbuf.at[slot], xs_ref.at[pl.ds(0, tr), :], out_sem.at[slot]).wait()
        pltpu.make_async_copy(hbuf.at[slot, pl.ds(0, npad * SUBLANES), :], xs_ref.at[pl.ds(0, npad * SUBLANES), :],
                              out_sem.at[slot]).wait()

    @pl.when(i == 0)
    def _():
        zero_ref[...] = jnp.zeros(zero_ref.shape, F32)
        fetch(0, 0).start()

    slot = lax.rem(i, 3)
    fetch(i, slot).wait()

    @pl.when(i + 1 < steps)
    def _():
        fetch(i + 1, lax.rem(i + 1, 3)).start()

    def scatter(r, carry):
        for k in range(2):
            _token_copy(hbuf.at[slot], r * SUBLANES, xs_ref, slot_ref[0, 0, k * tm + r],
                        out_sem.at[slot]).start(priority=k)
        return carry

    def scatter_pad(j, carry):
        _token_copy(zero_ref, 0, xs_ref, pad_ref[0, 0, j], out_sem.at[slot]).start()
        return carry

    lax.fori_loop(0, tm, scatter, 0, unroll=8)
    lax.fori_loop(0, npad, scatter_pad, 0, unroll=8)

    @pl.when(i > 0)
    def _():
        drain(lax.rem(i + 2, 3))

    @pl.when(i == steps - 1)
    def _():
        drain(slot)


def _dispatch(h_tiles, slots, pad_slots, n_tokens_out):
    n = h_tiles.shape[0] // SUBLANES
    tm = TM_ROUTE
    npad = pad_slots.shape[-1]
    steps = n // tm
    return pl.pallas_call(
        functools.partial(_dispatch_kernel, tm=tm, npad=npad, steps=steps),
        grid=(steps,),
        in_specs=[pl.BlockSpec((1, 1, 2 * tm), lambda i: (i, 0, 0), memory_space=pltpu.SMEM),
                  pl.BlockSpec((1, 1, npad), lambda i: (i, 0, 0), memory_space=pltpu.SMEM),
                  pl.BlockSpec(memory_space=pl.ANY)],
        out_specs=pl.BlockSpec(memory_space=pl.ANY),
        out_shape=jax.ShapeDtypeStruct((n_tokens_out * SUBLANES, LANES), F32),
        scratch_shapes=[pltpu.VMEM((3, tm * SUBLANES, LANES), F32), pltpu.VMEM((SUBLANES, LANES), F32),
                        pltpu.SemaphoreType.DMA((3,)), pltpu.SemaphoreType.DMA((3,))],
        compiler_params=_cparams(("arbitrary",), 3 * tm * D_MODEL * 4 + (4 << 20)),
        name="dispatch",
    )(slots, pad_slots, h_tiles)


def _experts_kernel(te_ref, nt_ref, xs_ref, wg_ref, wu_ref, wd_ref, o_ref, *, tm):
    t = pl.program_id(0)

    @pl.when(t < nt_ref[0])
    def _():
        hb = _load_token_tiles(xs_ref, 0, tm).astype(BF16)
        _store_token_tiles(o_ref, 0, _swiglu_tile(hb, wg_ref, wu_ref, wd_ref))

    @pl.when(t >= nt_ref[0])
    def _():
        o_ref[...] = jnp.zeros(o_ref.shape, F32)


def _experts(xs, tile_expert, n_tiles_used, p):
    D = D_MODEL
    tm = TM_MOE
    n_tiles = tile_expert.shape[0]
    wmap = lambda t, te, nt: (te[t], 0, 0)
    tok = pl.BlockSpec((tm * SUBLANES, LANES), lambda t, te, nt: (t, 0))
    tok_in = pl.BlockSpec((tm * SUBLANES, LANES), lambda t, te, nt: (jnp.minimum(t, nt[0] - 1), 0))
    vmem = 2 * 3 * D * D_FF * p["moe_wg"].dtype.itemsize + 12 * tm * D * 4 + (6 << 20)
    return pl.pallas_call(
        functools.partial(_experts_kernel, tm=tm),
        grid_spec=pltpu.PrefetchScalarGridSpec(
            num_scalar_prefetch=2,
            grid=(n_tiles,),
            in_specs=[tok_in,
                      pl.BlockSpec((None, D, D_FF), wmap),
                      pl.BlockSpec((None, D, D_FF), wmap),
                      pl.BlockSpec((None, D_FF, D), wmap)],
            out_specs=tok),
        out_shape=jax.ShapeDtypeStruct((n_tiles * tm * SUBLANES, LANES), F32),
        compiler_params=_cparams(("arbitrary",), vmem),
        name="experts",
    )(tile_expert, n_tiles_used, xs, p["moe_wg"], p["moe_wu"], p["moe_wd"])


def _combine_kernel(slot_ref, next_ref, x1_ref, route_ref, mod_ref, ys_ref, o_ref, buf_ref, sem, *, tm, steps):
    i = pl.program_id(0)

    def gather(idx_ref, s):
        def body(r, carry):
            for k in range(2):
                _token_copy(ys_ref, idx_ref[0, 0, k * tm + r], buf_ref.at[s, k], r * SUBLANES,
                            sem.at[s]).start(priority=k)
            return carry
        lax.fori_loop(0, tm, body, 0, unroll=8)

    @pl.when(i == 0)
    def _():
        gather(slot_ref, 0)

    @pl.when(i + 1 < steps)
    def _():
        gather(next_ref, lax.rem(i + 1, 2))

    cur = lax.rem(i, 2)
    for k in range(2):
        pltpu.make_async_copy(ys_ref.at[pl.ds(0, tm * SUBLANES), :], buf_ref.at[cur, k], sem.at[cur]).wait()
    route = route_ref[...]
    w1 = route[:, 2:3]
    w2 = route[:, 3:4]
    y0 = _load_token_tiles(buf_ref, 0, tm, lead=(cur, 0))
    y1 = _load_token_tiles(buf_ref, 0, tm, lead=(cur, 1))
    o_ref[...] = x1_ref[...] + mod_ref[5:6, :] * (w1 * y0 + w2 * y1)


def _combine(x1, route, mod, ys, slots):
    B, S, D = x1.shape
    tm = TM_ROUTE
    per = S // tm
    steps = B * per
    tok = lambda w: pl.BlockSpec((None, tm, w), lambda i: (i // per, i % per, 0))
    return pl.pallas_call(
        functools.partial(_combine_kernel, tm=tm, steps=steps),
        grid=(steps,),
        in_specs=[pl.BlockSpec((1, 1, 2 * tm), lambda i: (i, 0, 0), memory_space=pltpu.SMEM),
                  pl.BlockSpec((1, 1, 2 * tm), lambda i: (jnp.minimum(i + 1, steps - 1), 0, 0),
                               memory_space=pltpu.SMEM),
                  tok(D), tok(LANES), pl.BlockSpec((None, N_MOD, D), lambda i: (i // per, 0, 0)),
                  pl.BlockSpec(memory_space=pl.ANY)],
        out_specs=tok(D),
        out_shape=jax.ShapeDtypeStruct((B, S, D), F32),
        scratch_shapes=[pltpu.VMEM((2, 2, tm * SUBLANES, LANES), F32), pltpu.SemaphoreType.DMA((2,))],
        compiler_params=_cparams(("arbitrary",), 12 * tm * D * 4 + (4 << 20)),
        name="combine",
    )(slots, slots, x1, route, mod, ys)


def _moe(h, x1, route, cnt, route_t, mod, p):
    B, S, D = x1.shape
    n = B * S
    sub = n // (cnt.shape[0] * cnt.shape[1])
    tmm = TM_MOE
    n_tiles = 2 * n // tmm + N_EXPERTS
    n_slots = n_tiles * tmm
    steps = n // TM_ROUTE
    counts = cnt[:, :, 0:2, 0:N_EXPERTS].astype(jnp.int32).reshape(-1, 2, N_EXPERTS)
    tile_tot = counts.sum(axis=0)
    n_e = tile_tot.sum(axis=0)
    base = jnp.cumsum(counts, axis=0) - counts
    base = base + jnp.array([0, 1], jnp.int32)[None, :, None] * tile_tot[0][None, None, :]
    tiles_e = (n_e + tmm - 1) // tmm
    pstart = (jnp.cumsum(tiles_e) - tiles_e) * tmm
    expert_ids = jnp.arange(N_EXPERTS, dtype=jnp.int32)[:, None]
    slots_k = []
    for k in range(2):
        e_k = route_t[k].astype(jnp.int32)
        table = jnp.repeat((base[:, k, :] + pstart[None, :]).T, sub, axis=1)
        slot = jnp.sum(jnp.where(e_k[None, :] == expert_ids, table, 0), axis=0) + route_t[4 + k].astype(jnp.int32)
        slots_k.append((slot * SUBLANES).reshape(steps, TM_ROUTE))
    slots = jnp.concatenate(slots_k, axis=1).reshape(steps, 1, 2 * TM_ROUTE)
    tile_end = jnp.cumsum(tiles_e)
    n_used = tile_end[-1]
    t_idx = jnp.minimum(jnp.arange(n_tiles, dtype=jnp.int32), n_used - 1)
    tile_expert = jnp.sum(t_idx[:, None] >= tile_end[None, :], axis=-1).astype(jnp.int32)
    tile_expert = jnp.minimum(tile_expert, N_EXPERTS - 1)

    n_cand = N_EXPERTS * tmm
    pq = jnp.arange(tmm, dtype=jnp.int32)[None, :]
    n_pad_e = (tiles_e * tmm - n_e)[:, None]
    spare = n_slots + expert_ids * tmm + pq
    pad_slots = jnp.where(pq < n_pad_e, (pstart + n_e)[:, None] + pq, spare).astype(jnp.int32)
    assert n_cand % steps == 0
    pad_slots = (pad_slots * SUBLANES).reshape(steps, 1, n_cand // steps)

    xs = _dispatch(h, slots, pad_slots, n_slots + n_cand)
    ys = _experts(xs, tile_expert, n_used.reshape(1).astype(jnp.int32), p)
    return _combine(x1, route, mod, ys, slots)


def _pad_heads(w, width):
    lead = w.shape[:-1]
    w = w.reshape(lead + (N_HEADS, width))
    w = jnp.pad(w, [(0, 0)] * len(lead) + [(0, 0), (0, HEAD_PAD - width)])
    return w.reshape(lead + (N_HEADS * HEAD_PAD,))


def _layer_params(l, w_in, q_lat_g, kv_lat_g, w_uq, w_ukv, q_norm_g, k_norm_g, conv_w, conv_b, conv_ln_g,
                  conv_ln_b, w_out, ffn_norm_g):
    wi = w_in[l]
    kr = _with_partner(jnp.pad(wi[:, OFF_KR:OFF_CONV], ((0, 0), (QK_NOPE, HEAD_PAD - QK_HEAD))))
    ckv = wi[:, OFF_CKV:OFF_KR]
    w_full = jnp.concatenate([wi[:, OFF_CQ:OFF_CKV], ckv, wi[:, OFF_CONV:OFF_FOUR], wi[:, OFF_FOUR:IN_COLS], kr],
                             axis=1)
    w_kv = jnp.concatenate([ckv, kr], axis=1)
    ukv = w_ukv[l].reshape(KV_LORA, N_HEADS, QK_NOPE + V_HEAD)
    uk = _pad_heads(ukv[:, :, :QK_NOPE].reshape(KV_LORA, N_HEADS * QK_NOPE), QK_NOPE)
    uv = _pad_heads(ukv[:, :, QK_NOPE:].reshape(KV_LORA, ATTN_OUT), V_HEAD)
    pad_g = lambda g: jnp.pad(g, (0, HEAD_PAD - QK_HEAD)).reshape(1, HEAD_PAD)
    uq = _with_partner(_pad_heads(w_uq[l], QK_HEAD))
    hw = N_HEADS * HEAD_PAD
    col = np.arange(hw)[:, None]
    head_of_col = col // HEAD_PAD == np.arange(LANES)[None, :]
    return {
        "w_in_full": w_full.astype(BF16),
        "w_in_kv": w_kv.astype(BF16),
        "q_lat_g": q_lat_g[l].reshape(1, Q_LORA),
        "kv_lat_g": kv_lat_g[l].reshape(1, KV_LORA),
        "w_uq": uq.astype(BF16),
        "w_ukv": jnp.concatenate([uk, uv], axis=1).astype(BF16),
        "q_norm_g": pad_g(q_norm_g[l]) * (QK_HEAD ** -0.5 * math.log2(math.e)),
        "v_ones": jnp.tile(jnp.concatenate([jnp.zeros((V_HEAD,), F32), jnp.ones((HEAD_PAD - V_HEAD,), F32)]),
                           N_HEADS).reshape(1, hw),
        "ones_h": jnp.asarray(head_of_col & (col % HEAD_PAD < QK_HEAD), F32).astype(BF16),
        "k_norm_g": pad_g(k_norm_g[l]),
        "conv_w": conv_w[l],
        "conv_b": conv_b[l].reshape(1, CONV_CH),
        "conv_ln_g": conv_ln_g[l].reshape(1, CONV_CH),
        "conv_ln_b": conv_ln_b[l].reshape(1, CONV_CH),
        "ffn_norm_g": ffn_norm_g[l].reshape(1, D_MODEL),
        "bd": _channel_dft(),
    }


def _with_partner(w):
    lead = w.shape[:-1]
    half = QK_ROPE // 2
    w3 = w.reshape(lead + (-1, HEAD_PAD))
    out = jnp.concatenate([w3[..., :QK_HEAD], w3[..., QK_NOPE + half:QK_HEAD], w3[..., QK_NOPE:QK_NOPE + half]],
                          axis=-1)
    return out.reshape(w.shape)


def _rot_partner(w):
    lead = w.shape[:-1]
    half = QK_ROPE // 2
    w3 = w.reshape(lead + (-1, HEAD_PAD))
    z = jnp.zeros_like(w3)
    out = jnp.concatenate([z[..., :QK_NOPE], w3[..., QK_NOPE + half:QK_HEAD], w3[..., QK_NOPE:QK_NOPE + half],
                           z[..., QK_HEAD:]], axis=-1)
    return out.reshape(w.shape)


def _rope_tables(seq, gq, gk):
    rows = seq // GRID_W
    row = np.repeat(np.arange(rows, dtype=np.float64), GRID_W)
    col = np.tile(np.arange(GRID_W, dtype=np.float64), rows)
    n_freq = QK_ROPE // 4
    inv = ROPE_BASE ** (-np.arange(n_freq, dtype=np.float64) / n_freq)
    ang = np.concatenate([row[:, None] * inv, col[:, None] * inv], axis=-1)
    cos, sin = np.cos(ang), np.sin(ang)
    ones = np.ones((seq, QK_NOPE))
    tail = np.zeros((seq, HEAD_PAD - QK_HEAD))
    cos_t = jnp.asarray(np.concatenate([ones, cos, cos, tail], axis=1), F32)
    sin_t = jnp.asarray(np.concatenate([np.zeros((seq, QK_NOPE)), -sin, sin, tail], axis=1), F32)
    return (cos_t * gq, sin_t * _rot_partner(gq), cos_t * gk, sin_t * _rot_partner(gk))


def kernel(x, c, ctx, c_ctx, ada_w, ada_b, mix_norm_g, ffn_norm_g, w_in, q_lat_g, kv_lat_g, w_uq, w_ukv, q_norm_g,
           k_norm_g, conv_w, conv_b, conv_ln_g, conv_ln_b, w_out, ffn_w_gate, ffn_w_up, ffn_w_down, router_w,
           moe_w_gate, moe_w_up, moe_w_down):
    B, S, D = x.shape
    T = ctx.shape[1]
    assert (D, DEPTH) == (D_MODEL, ada_w.shape[0]) and S % GRID_W == 0

    cc = jnp.concatenate([c, c_ctx[None, :], jnp.zeros((2 * SUBLANES - B - 1, D), F32)], axis=0)
    mods = _modulation(cc, ada_w, ada_b).reshape(DEPTH, 2 * SUBLANES, N_MOD, D)
    cs_x = _dft_tables(S)
    cs_c = _dft_tables(T)

    def moe_f32(i):
        return (moe_w_gate[i].reshape(N_EXPERTS * D, D_FF), moe_w_up[i].reshape(N_EXPERTS * D, D_FF),
                moe_w_down[i].reshape(N_EXPERTS * D_FF, D))

    moe_bf16 = {}
    for l in range(DEPTH):
        last = l == DEPTH - 1
        p = _layer_params(l, w_in, q_lat_g, kv_lat_g, w_uq, w_ukv, q_norm_g, k_norm_g, conv_w, conv_b, conv_ln_g,
                          conv_ln_b, w_out, ffn_norm_g)
        p["mix_norm_g"] = mix_norm_g[l].reshape(1, D)
        rope_tabs = _rope_tables(S, p["q_norm_g"], p["k_norm_g"])
        i = l // 2
        moe = l % 2 == 1
        mod_x = mods[l, :B]
        mod_c = mods[l, B:B + 1]

        nxt = l + 1 if l % 2 == 0 else l + 2
        cast = []
        if not moe:
            cast += [ffn_w_gate[i], ffn_w_up[i], ffn_w_down[i]]
        if nxt < DEPTH:
            cast += list(moe_f32(nxt // 2))
        p["w_out"] = w_out[l].astype(BF16)

        def channel_mixer(attn, conv, four, xs, mod):
            if moe:
                x1, h, route, cnt, route_t = _mix_out(attn, conv, four, xs, mod, p, moe=True)
                mod_b = mod if mod.shape[0] > 1 else jnp.broadcast_to(mod, (B,) + mod.shape[1:])
                return _moe(h, x1, route, cnt, route_t, mod_b, p)
            return _mix_out(attn, conv, four, xs, mod, p, moe=False)

        if last:
            k_c, v_c = _front(ctx, mod_c, p["mix_norm_g"], p, None, full=False)
        else:
            q_c, k_c, v_c, y_c, zz_c = _front(ctx, mod_c, p["mix_norm_g"], p, None, full=True)
        q_x, k_x, v_x, y_x, zz_x = _front(x, mod_x, p["mix_norm_g"], p, rope_tabs, full=True)
        attn_x, *casted = _attend(q_x, [k_x, k_c], [v_x, v_c], cast)
        if moe:
            rw = jnp.pad(router_w[i], ((0, 0), (0, LANES - N_EXPERTS)))
            rw_hi = rw.astype(BF16)
            p["router_w"] = jnp.concatenate([rw_hi, (rw - rw_hi.astype(F32)).astype(BF16)], axis=1)
            wg, wu, wd = moe_bf16.pop(i) if i in moe_bf16 else [w.astype(BF16) for w in moe_f32(i)]
            p["moe_wg"] = wg.reshape(N_EXPERTS, D, D_FF)
            p["moe_wu"] = wu.reshape(N_EXPERTS, D, D_FF)
            p["moe_wd"] = wd.reshape(N_EXPERTS, D_FF, D)
        else:
            p["ffn_wg"], p["ffn_wu"], p["ffn_wd"] = casted[:3]
            casted = casted[3:]
        if nxt < DEPTH:
            moe_bf16[nxt // 2] = casted

        if not last:
            attn_c, = _attend(q_c, [k_c], [v_c])
            ctx_next = channel_mixer(attn_c, *_conv_fourier(y_c, zz_c, cs_c, p), ctx, mod_c)

        x = channel_mixer(attn_x, *_conv_fourier(y_x, zz_x, cs_x, p), x, mod_x)
        if not last:
            ctx = ctx_next
    return x
```

```python
import functools
import math

import numpy as np
import jax
import jax.numpy as jnp
from jax import lax
from jax.experimental import pallas as pl
from jax.experimental.pallas import tpu as pltpu

F32 = jnp.float32
BF16 = jnp.bfloat16

D_MODEL = 1024
DEPTH = 2
GRID_W = 64
N_HEADS = 8
QK_NOPE = 64
QK_ROPE = 32
QK_HEAD = QK_NOPE + QK_ROPE
V_HEAD = 64
Q_LORA = 384
KV_LORA = 256
ROPE_BASE = 10000.0
CONV_CH = 256
CONV_WIDTH = 31
CONV_PAD = (CONV_WIDTH - 1) // 2
FOURIER_GROUPS = 4
FOURIER_GROUP_DIM = 64
FOURIER_CH = FOURIER_GROUPS * FOURIER_GROUP_DIM
ATTN_OUT = N_HEADS * V_HEAD
OFF_CQ = 0
OFF_CKV = OFF_CQ + Q_LORA
OFF_KR = OFF_CKV + KV_LORA
OFF_CONV = OFF_KR + QK_ROPE
OFF_FOUR = OFF_CONV + 2 * CONV_CH
IN_COLS = OFF_FOUR + FOURIER_CH
D_FF = 2816
N_EXPERTS = 8
N_MOD = 6
EPS = 1e-6

LANES = 128
SUBLANES = 8
HEAD_PAD = LANES
VMEM_CAP = 56 * 1024 * 1024
FF_CHUNK = 256

SUB_ROWS = 256
SUB_ROWS_MOE = 512
TM_FRONT = 1024
TQ_ATTN = 1024
TQ_SUB = 512
TK_ATTN = 256
TM_OUT = 512
TM_MOE = 512
TM_ROUTE = 512
CONV_ROWS = 128
TR_FOURIER = 512


def _cparams(sem, vmem_bytes):
    return pltpu.CompilerParams(dimension_semantics=sem, vmem_limit_bytes=int(min(VMEM_CAP, vmem_bytes)))


def _rms(v, g):
    return v * lax.rsqrt(jnp.mean(v * v, axis=-1, keepdims=True) + EPS) * g


def _sigmoid(v):
    return 1.0 / (1.0 + jnp.exp(-v))


def _mod_kernel(c_ref, w_ref, b_ref, o_ref):
    c = c_ref[...]
    s = (c * _sigmoid(c)).astype(BF16)
    o_ref[...] = jnp.dot(s, w_ref[...].astype(BF16), preferred_element_type=F32) + b_ref[...]


def _modulation(cc, ada_w, ada_b):
    rows = cc.shape[0]
    tn = 1536
    n_out = N_MOD * D_MODEL
    return pl.pallas_call(
        _mod_kernel,
        grid=(DEPTH, n_out // tn),
        in_specs=[
            pl.BlockSpec((rows, D_MODEL), lambda l, j: (0, 0)),
            pl.BlockSpec((None, D_MODEL, tn), lambda l, j: (l, 0, j)),
            pl.BlockSpec((None, 1, tn), lambda l, j: (l, 0, j)),
        ],
        out_specs=pl.BlockSpec((None, rows, tn), lambda l, j: (l, 0, j)),
        out_shape=jax.ShapeDtypeStruct((DEPTH, rows, n_out), F32),
        compiler_params=_cparams(("arbitrary", "arbitrary"), 4 * D_MODEL * tn * 4),
        name="modulation",
    )(cc, ada_w, ada_b.reshape(DEPTH, 1, n_out))


def _front_kernel(*refs, full, rope):
    it = iter(refs)
    x_ref, mod_ref, g_ref, win_ref = next(it), next(it), next(it), next(it)
    if full:
        qlg_ref, wuq_ref = next(it), next(it)
    kvlg_ref, wukv_ref, vones_ref, onesh_ref = next(it), next(it), next(it), next(it)
    if rope:
        cq_ref, sq_ref, ck_ref, sk_ref = next(it), next(it), next(it), next(it)
    else:
        qg_ref = next(it) if full else None
        kg_ref = next(it)
    if full:
        bd_ref = next(it)
        q_out = next(it)
    k_out, v_out = next(it), next(it)
    if full:
        y_out, zz_out = next(it), next(it)

    shift = mod_ref[0:1, :]
    gain = g_ref[...] * (1.0 + mod_ref[1:2, :])
    hw = N_HEADS * HEAD_PAD

    def head_inv_rms(raw):
        ss = jnp.dot((raw * raw).astype(BF16), onesh_ref[...], preferred_element_type=F32)
        rs = lax.rsqrt(ss * (1.0 / QK_HEAD) + EPS)
        return jnp.concatenate([jnp.broadcast_to(rs[:, hd:hd + 1], (rs.shape[0], HEAD_PAD)) for hd in range(N_HEADS)],
                               axis=1)

    def sub_block(rows):
        x = x_ref[rows, :]
        h = _rms(x, gain) + shift
        cols = jnp.dot(h.astype(BF16), win_ref[...], preferred_element_type=F32)

        o = 0
        if full:
            cq = cols[:, 0:Q_LORA]
            o = Q_LORA
            qall = jnp.dot(_rms(cq, qlg_ref[...]).astype(BF16), wuq_ref[...], preferred_element_type=F32)
            rsb = head_inv_rms(qall)
            for hd in range(N_HEADS):
                sl = slice(hd * HEAD_PAD, (hd + 1) * HEAD_PAD)
                if rope:
                    val = qall[:, sl] * cq_ref[rows, :] + pltpu.roll(qall[:, sl], HEAD_PAD - QK_ROPE, 1) * \
                        sq_ref[rows, :]
                else:
                    val = qall[:, sl] * qg_ref[...]
                q_out[rows, sl] = (val * rsb[:, sl]).astype(BF16)

        ckv = cols[:, o:o + KV_LORA]
        o += KV_LORA
        kv = jnp.dot(_rms(ckv, kvlg_ref[...]).astype(BF16), wukv_ref[...], preferred_element_type=F32)
        if full:
            a = cols[:, o:o + CONV_CH]
            gt = cols[:, o + CONV_CH:o + 2 * CONV_CH]
            y_out[rows, :] = a * _sigmoid(gt)
            o += 2 * CONV_CH
            z = cols[:, o:o + FOURIER_CH]
            o += FOURIER_CH
            zz_out[rows, :] = jnp.dot(z.astype(BF16), bd_ref[...], preferred_element_type=F32).astype(BF16)
        krb = cols[:, o:o + HEAD_PAD]
        kraw = [kv[:, hd * HEAD_PAD:(hd + 1) * HEAD_PAD] + krb for hd in range(N_HEADS)]
        rsb = head_inv_rms(jnp.concatenate(kraw, axis=1))
        if rope:
            kpart = pltpu.roll(krb, HEAD_PAD - QK_ROPE, 1) * sk_ref[rows, :]
        for hd in range(N_HEADS):
            sl = slice(hd * HEAD_PAD, (hd + 1) * HEAD_PAD)
            val = kraw[hd] * ck_ref[rows, :] + kpart if rope else kraw[hd] * kg_ref[...]
            k_out[rows, sl] = (val * rsb[:, sl]).astype(BF16)
        v_out[rows, :] = (kv[:, hw:] + vones_ref[...]).astype(BF16)

    tm = x_ref.shape[0]
    sub = min(SUB_ROWS, tm)
    for sb in range(tm // sub):
        sub_block(slice(sb * sub, (sb + 1) * sub))


def _front(xs, mod, norm_g, p, rope_tabs, *, full):
    if mod.shape[0] == 1 and xs.shape[0] > 1 and rope_tabs is None:
        outs = _front(xs.reshape(1, -1, xs.shape[-1]), mod, norm_g, p, None, full=full)
        return [o.reshape(xs.shape[:2] + o.shape[2:]) for o in outs]
    B, S, D = xs.shape
    tm = min(TM_FRONT, S)
    rope = rope_tabs is not None
    assert full or not rope
    w_in = p["w_in_full"] if full else p["w_in_kv"]
    w_uq = p["w_uq"]
    ncol = w_in.shape[1]
    bm = mod.shape[0]
    mod_map = (lambda b, i: (b, 0, 0)) if bm > 1 else (lambda b, i: (0, 0, 0))
    const2 = lambda b, i: (0, 0)
    hw = N_HEADS * HEAD_PAD

    args = [xs, mod, norm_g, w_in]
    specs = [
        pl.BlockSpec((None, tm, D), lambda b, i: (b, i, 0)),
        pl.BlockSpec((None, N_MOD, D), mod_map),
        pl.BlockSpec((1, D), const2),
        pl.BlockSpec((D, ncol), const2),
    ]
    if full:
        args += [p["q_lat_g"], w_uq]
        specs += [pl.BlockSpec((1, Q_LORA), const2), pl.BlockSpec((Q_LORA, w_uq.shape[1]), const2)]
    args += [p["kv_lat_g"], p["w_ukv"], p["v_ones"], p["ones_h"]]
    specs += [pl.BlockSpec((1, KV_LORA), const2),
              pl.BlockSpec((KV_LORA, 2 * hw), const2),
              pl.BlockSpec((1, hw), const2),
              pl.BlockSpec((hw, LANES), const2)]
    if rope:
        args += list(rope_tabs)
        specs += [pl.BlockSpec((tm, HEAD_PAD), lambda b, i: (i, 0))] * 4
    else:
        if full:
            args += [p["q_norm_g"]]
            specs += [pl.BlockSpec((1, HEAD_PAD), const2)]
        args += [p["k_norm_g"]]
        specs += [pl.BlockSpec((1, HEAD_PAD), const2)]
    if full:
        args += [p["bd"]]
        specs += [pl.BlockSpec((FOURIER_CH, 2 * FOURIER_CH), const2)]

    out_shapes, out_specs = [], []

    def add_out(width, dtype):
        out_shapes.append(jax.ShapeDtypeStruct((B, S, width), dtype))
        out_specs.append(pl.BlockSpec((None, tm, width), lambda b, i: (b, i, 0)))

    if full:
        add_out(N_HEADS * HEAD_PAD, BF16)
    add_out(N_HEADS * HEAD_PAD, BF16)
    add_out(N_HEADS * HEAD_PAD, BF16)
    if full:
        add_out(CONV_CH, F32)
        add_out(2 * FOURIER_CH, BF16)

    vmem = 2 * (D * ncol * 2 + Q_LORA * 1024 * 2 + KV_LORA * 2048 * 2) + 28 * tm * D * 4
    return pl.pallas_call(
        functools.partial(_front_kernel, full=full, rope=rope),
        grid=(B, S // tm),
        in_specs=specs,
        out_specs=out_specs,
        out_shape=out_shapes,
        compiler_params=_cparams(("arbitrary", "arbitrary"), vmem),
        name="front_full" if full else "front_kv",
    )(*args)


def _attn_kernel(*refs, nsrc, ncast):
    q_ref = refs[0]
    k_refs = refs[1:1 + nsrc]
    v_refs = refs[1 + nsrc:1 + 2 * nsrc]
    cast_in = refs[1 + 2 * nsrc:1 + 2 * nsrc + ncast]
    o_ref = refs[1 + 2 * nsrc + ncast]
    cast_out = refs[2 + 2 * nsrc + ncast:]
    chunks = []
    for k_ref, v_ref in zip(k_refs, v_refs):
        for s0 in range(0, k_ref.shape[0], TK_ATTN):
            chunks.append((k_ref, v_ref, s0, min(TK_ATTN, k_ref.shape[0] - s0)))
    tq = q_ref.shape[0]
    sub = min(TQ_SUB, tq)
    for r0 in range(0, tq, sub):
        rows = slice(r0, r0 + sub)
        state = [None, None]
        for k_ref, v_ref, s0, sz in chunks:
            for hh in range(2):
                sl = slice(hh * HEAD_PAD, (hh + 1) * HEAD_PAD)
                s = lax.dot_general(q_ref[rows, sl], k_ref[s0:s0 + sz, sl], (((1,), (1,)), ((), ())),
                                    preferred_element_type=F32)
                m = jnp.max(s, axis=-1, keepdims=True)
                if state[hh] is not None:
                    m_old, acc_old = state[hh]
                    m = jnp.maximum(m_old, m)
                pv = jnp.dot(jnp.exp2((s - m).astype(BF16)), v_ref[s0:s0 + sz, sl], preferred_element_type=F32)
                if state[hh] is not None:
                    pv = pv + jnp.exp2(m_old - m) * acc_old
                state[hh] = (m, pv)
        accs = [state[0][1], state[1][1]]
        lane = lax.broadcasted_iota(jnp.int32, accs[0].shape, 1)
        lo = accs[0] / pltpu.roll(accs[0], V_HEAD, 1)
        hi = pltpu.roll(accs[1], V_HEAD, 1) / accs[1]
        o_ref[rows, :] = jnp.where(lane < V_HEAD, lo, hi).astype(BF16)
    for src, dst in zip(cast_in, cast_out):
        dst[...] = src[...].astype(BF16)


def _attend(q, ks, vs, cast=()):
    B, S, _ = q.shape
    tq = min(TQ_ATTN, S)
    nsrc = len(ks)
    nq = S // tq
    steps = B * (N_HEADS // 2) * nq
    specs = [pl.BlockSpec((None, tq, 2 * HEAD_PAD), lambda b, hp, i: (b, i, hp))]
    for kv in list(ks) + list(vs):
        specs.append(pl.BlockSpec((None, kv.shape[1], 2 * HEAD_PAD), lambda b, hp, i: (b, 0, hp)))
    out_specs = [pl.BlockSpec((None, tq, 2 * V_HEAD), lambda b, hp, i: (b, i, hp))]
    out_shapes = [jax.ShapeDtypeStruct((B, S, ATTN_OUT), BF16)]
    cast_bytes = 0
    for w in cast:
        rows, cols = w.shape
        nblk = max(d for d in range(1, steps + 1)
                   if steps % d == 0 and rows % d == 0 and (rows // d) % (2 * SUBLANES) == 0)
        hold = steps // nblk
        spec = pl.BlockSpec((rows // nblk, cols),
                            lambda b, hp, i, hold=hold: (((b * (N_HEADS // 2) + hp) * nq + i) // hold, 0))
        specs.append(spec)
        out_specs.append(spec)
        out_shapes.append(jax.ShapeDtypeStruct(w.shape, BF16))
        cast_bytes += 2 * (rows // nblk) * cols * 6
    t_all = sum(k.shape[1] for k in ks)
    vmem = 2 * (t_all * 512 * 2) * 2 + 16 * min(tq, TQ_SUB) * TK_ATTN * 4 + cast_bytes + (12 << 20)
    outs = pl.pallas_call(
        functools.partial(_attn_kernel, nsrc=nsrc, ncast=len(cast)),
        grid=(B, N_HEADS // 2, nq),
        in_specs=specs,
        out_specs=out_specs,
        out_shape=out_shapes,
        compiler_params=_cparams(("arbitrary", "arbitrary", "arbitrary"), vmem),
        name="attend%d" % nsrc,
    )(q, *ks, *vs, *cast)
    return list(outs)


def _conv_fourier_kernel(y_ref, w_ref, b_ref, lg_ref, lb_ref, cs_ref, zz_ref, conv_out, four_out, pad_ref, *, seq):
    halo = 2 * SUBLANES
    total = seq + 2 * halo
    pad_ref[0, 0:halo, :] = jnp.zeros((halo, CONV_CH), F32)
    pad_ref[0, halo + seq:total, :] = jnp.zeros((halo, CONV_CH), F32)
    pad_ref[0, halo:halo + seq, :] = y_ref[...]
    slab = pad_ref[0]
    for ph in range(1, SUBLANES):
        pad_ref[ph] = pltpu.roll(slab, total - ph, 0)

    def conv_rows(base, rows):
        acc = jnp.zeros((rows, CONV_CH), F32)
        for j in range(CONV_WIDTH):
            off = base + halo - CONV_PAD + j
            ph = off % SUBLANES
            acc = acc + pad_ref[ph, off - ph:off - ph + rows, :] * w_ref[j:j + 1, :]
        acc = acc + b_ref[...]
        mu = jnp.mean(acc, axis=-1, keepdims=True)
        cen = acc - mu
        var = jnp.mean(cen * cen, axis=-1, keepdims=True)
        yn = cen * lax.rsqrt(var + EPS) * lg_ref[...] + lb_ref[...]
        conv_out[base:base + rows, :] = (yn * _sigmoid(yn)).astype(BF16)

    def dft_rows(base, rows):
        r = slice(base, base + rows)
        acc = jnp.dot(cs_ref[r, 0:seq], zz_ref[:, 0:FOURIER_CH], preferred_element_type=F32)
        acc = acc + jnp.dot(cs_ref[r, seq:2 * seq], zz_ref[:, FOURIER_CH:2 * FOURIER_CH],
                            preferred_element_type=F32)
        four_out[r, :] = acc.astype(BF16)

    tr = min(TR_FOURIER, seq)
    for base in range(0, seq, tr):
        for cb in range(base, base + tr, CONV_ROWS):
            conv_rows(cb, CONV_ROWS)
        dft_rows(base, tr)


def _conv_fourier(y, zz, cs, p):
    B, S, _ = y.shape
    c2 = lambda b: (0, 0)
    seq_block = lambda w: pl.BlockSpec((None, S, w), lambda b: (b, 0, 0))
    vmem = (SUBLANES + 6) * (S + 32) * CONV_CH * 4 + S * 2 * S * 2 + 2 * S * 512 * 2 + (10 << 20)
    return pl.pallas_call(
        functools.partial(_conv_fourier_kernel, seq=S),
        grid=(B,),
        in_specs=[seq_block(CONV_CH),
                  pl.BlockSpec((CONV_WIDTH, CONV_CH), c2),
                  pl.BlockSpec((1, CONV_CH), c2), pl.BlockSpec((1, CONV_CH), c2), pl.BlockSpec((1, CONV_CH), c2),
                  pl.BlockSpec((S, 2 * S), c2, pipeline_mode=pl.Buffered(1)),
                  seq_block(2 * FOURIER_CH)],
        out_specs=[seq_block(CONV_CH), seq_block(FOURIER_CH)],
        out_shape=[jax.ShapeDtypeStruct((B, S, CONV_CH), BF16), jax.ShapeDtypeStruct((B, S, FOURIER_CH), BF16)],
        scratch_shapes=[pltpu.VMEM((SUBLANES, S + 4 * SUBLANES, CONV_CH), F32)],
        compiler_params=_cparams(("arbitrary",), vmem),
        name="conv_fourier",
    )(y, p["conv_w"], p["conv_b"], p["conv_ln_g"], p["conv_ln_b"], cs, zz)


def _dft_tables(seq):
    norm = seq ** -0.5
    ks = (np.arange(seq)[:, None] * np.arange(seq)[None, :]) % seq
    ang = 2.0 * np.pi * ks / seq
    tab = np.concatenate([np.cos(ang), -np.sin(ang)], axis=1) * norm
    return jnp.asarray(tab, F32).astype(BF16)


def _channel_dft():
    n = FOURIER_GROUP_DIM
    ang = 2.0 * np.pi * ((np.arange(n)[:, None] * np.arange(n)[None, :]) % n) / n
    eye = np.eye(FOURIER_GROUPS)
    bd = np.concatenate([np.kron(eye, np.cos(ang)), np.kron(eye, np.sin(ang))], axis=1) * n ** -0.5
    return jnp.asarray(bd, F32).astype(BF16)


def _mixout_kernel(*refs, moe):
    if moe:
        (attn_ref, conv_ref, four_ref, x_ref, mod_ref, wout_ref, g_ref, rw_ref, tri_ref,
         x1_out, h_out, route_out, cnt_out, routet_out) = refs
    else:
        (attn_ref, conv_ref, four_ref, x_ref, mod_ref, wout_ref, g_ref, wg_ref, wu_ref, wd_ref,
         x1_out, h_out) = refs
    tm = x_ref.shape[0]
    sub = min(SUB_ROWS_MOE if moe else SUB_ROWS, tm)
    gain = g_ref[...] * (1.0 + mod_ref[4:5, :])
    for sb in range(tm // sub):
        rows = slice(sb * sub, (sb + 1) * sub)
        mix = jnp.dot(attn_ref[rows, :], wout_ref[0:ATTN_OUT, :], preferred_element_type=F32)
        mix = mix + jnp.dot(conv_ref[rows, :], wout_ref[ATTN_OUT:ATTN_OUT + CONV_CH, :],
                            preferred_element_type=F32)
        mix = mix + jnp.dot(four_ref[rows, :], wout_ref[ATTN_OUT + CONV_CH:, :], preferred_element_type=F32)
        x1 = x_ref[rows, :] + mod_ref[2:3, :] * mix
        x1_out[rows, :] = x1
        h = _rms(x1, gain) + mod_ref[3:4, :]
        if moe:
            _store_token_tiles(h_out, sb * sub, h)
            route, cnt = _route(h, rw_ref, tri_ref)
            route_out[rows, :] = route
            cnt_out[sb] = cnt
            routet_out[:, rows] = route.T[0:SUBLANES, :]
        else:
            h_out[rows, :] = h.astype(BF16)
    if not moe:
        x1_out[...] = x1_out[...] + mod_ref[5:6, :] * _swiglu_tile(h_out[...], wg_ref, wu_ref, wd_ref)


def _store_token_tiles(ref, row0, val):
    rows = val.shape[0]
    for j in range(SUBLANES):
        ref[pl.ds(row0 * SUBLANES + j, rows, stride=SUBLANES), :] = val[:, j * LANES:(j + 1) * LANES]


def _load_token_tiles(ref, row0, rows, lead=()):
    idx = tuple(lead)
    return jnp.concatenate([ref[idx + (pl.ds(row0 * SUBLANES + j, rows, stride=SUBLANES), slice(None))]
                            for j in range(SUBLANES)], axis=1)


def _route(h, rw_ref, tri_ref):
    tm = h.shape[0]
    h_hi = h.astype(BF16)
    h_lo = (h - h_hi.astype(F32)).astype(BF16)
    hw = jnp.dot(h_hi, rw_ref[...], preferred_element_type=F32)
    logits = hw[:, 0:LANES] + hw[:, LANES:] + jnp.dot(h_lo, rw_ref[:, 0:LANES], preferred_element_type=F32)
    lane = lax.broadcasted_iota(jnp.int32, (tm, LANES), 1).astype(F32)
    neg = jnp.float32(-jnp.inf)
    lm = jnp.where(lane < N_EXPERTS, logits, neg)
    m1 = jnp.max(lm, axis=-1, keepdims=True)
    i1 = jnp.min(jnp.where(lm == m1, lane, float(LANES)), axis=-1, keepdims=True)
    lm2 = jnp.where(lane == i1, neg, lm)
    m2 = jnp.max(lm2, axis=-1, keepdims=True)
    i2 = jnp.min(jnp.where(lm2 == m2, lane, float(LANES)), axis=-1, keepdims=True)
    t = jnp.exp(m2 - m1)
    w1 = 1.0 / (1.0 + t)
    w2 = t / (1.0 + t)
    oh1 = lane == i1
    oh2 = lane == i2
    tri = tri_ref[...]
    c1 = jnp.dot(tri, jnp.where(oh1, 1.0, 0.0).astype(BF16), preferred_element_type=F32)
    c2 = jnp.dot(tri, jnp.where(oh2, 1.0, 0.0).astype(BF16), preferred_element_type=F32)
    r1 = jnp.sum(jnp.where(oh1, c1, 0.0), axis=-1, keepdims=True)
    r2 = jnp.sum(jnp.where(oh2, c2, 0.0), axis=-1, keepdims=True)
    vals = [i1, i2, w1, w2, r1, r2]
    route = jnp.zeros((tm, LANES), F32)
    for idx, v in enumerate(vals):
        route = jnp.where(lane == idx, v, route)
    n1 = jnp.sum(jnp.where(oh1, 1.0, 0.0), axis=0, keepdims=True)
    n2 = jnp.sum(jnp.where(oh2, 1.0, 0.0), axis=0, keepdims=True)
    row = lax.broadcasted_iota(jnp.int32, (SUBLANES, LANES), 0)
    return route, jnp.where(row == 0, n1, jnp.where(row == 1, n2, 0.0))


def _mix_out(attn, conv, four, xs, mod, p, *, moe):
    if mod.shape[0] == 1 and xs.shape[0] > 1 and not moe:
        flat = lambda a: a.reshape(1, -1, a.shape[-1])
        return _mix_out(flat(attn), flat(conv), flat(four), flat(xs), mod, p, moe=moe).reshape(xs.shape)
    B, S, D = xs.shape
    tm = min(TM_OUT, S)
    sub = min(SUB_ROWS_MOE if moe else SUB_ROWS, tm)
    bm = mod.shape[0]
    mod_map = (lambda b, i: (b, 0, 0)) if bm > 1 else (lambda b, i: (0, 0, 0))
    c2 = lambda b, i: (0, 0)
    tok = lambda w: pl.BlockSpec((None, tm, w), lambda b, i: (b, i, 0))
    args = [attn, conv, four, xs, mod, p["w_out"], p["ffn_norm_g"]]
    specs = [tok(ATTN_OUT), tok(CONV_CH), tok(FOURIER_CH), tok(D), pl.BlockSpec((None, N_MOD, D), mod_map),
             pl.BlockSpec((D, D), c2), pl.BlockSpec((1, D), c2)]
    out_shapes = [jax.ShapeDtypeStruct((B, S, D), F32)]
    out_specs = [tok(D)]
    scratch = []
    vmem = 2 * D * D * 2 + 24 * tm * D * 4 + (8 << 20)
    if not moe:
        resident = lambda shape: pl.BlockSpec(shape, c2, pipeline_mode=pl.Buffered(1))
        args += [p["ffn_wg"], p["ffn_wu"], p["ffn_wd"]]
        specs += [resident((D, D_FF)), resident((D, D_FF)), resident((D_FF, D))]
        scratch = [pltpu.VMEM((tm, D), BF16)]
        vmem += 3 * D * D_FF * 2
    else:
        out_shapes.append(None)
        out_specs.append(None)
        per = S // tm
        out_shapes[1] = jax.ShapeDtypeStruct((B * S * SUBLANES, LANES), F32)
        out_specs[1] = pl.BlockSpec((tm * SUBLANES, LANES), lambda b, i: (b * per + i, 0))
        tri = jnp.asarray(np.tril(np.ones((sub, sub), np.float32), -1), BF16)
        args += [p["router_w"], tri]
        specs += [pl.BlockSpec((D, 2 * LANES), c2), pl.BlockSpec((sub, sub), c2)]
        out_shapes += [jax.ShapeDtypeStruct((B, S, LANES), F32),
                       jax.ShapeDtypeStruct((B, S // sub, SUBLANES, LANES), F32)]
        out_specs += [tok(LANES), pl.BlockSpec((None, tm // sub, SUBLANES, LANES), lambda b, i: (b, i, 0, 0))]
        out_shapes.append(jax.ShapeDtypeStruct((SUBLANES, B * S), F32))
        out_specs.append(pl.BlockSpec((SUBLANES, tm), lambda b, i: (0, b * per + i)))
    outs = pl.pallas_call(
        functools.partial(_mixout_kernel, moe=moe),
        grid=(B, S // tm),
        in_specs=specs,
        out_specs=out_specs,
        out_shape=out_shapes,
        scratch_shapes=scratch,
        compiler_params=_cparams(("arbitrary", "arbitrary"), vmem),
        name="mix_out_moe" if moe else "mix_ffn",
    )(*args)
    return outs if moe else outs[0]


def _swiglu_tile(hb, wg_ref, wu_ref, wd_ref):
    ff = wg_ref.shape[1]
    acc = jnp.zeros((hb.shape[0], D_MODEL), F32)
    for c0 in range(0, ff, FF_CHUNK):
        sl = slice(c0, min(c0 + FF_CHUNK, ff))
        g = jnp.dot(hb, wg_ref[:, sl].astype(BF16), preferred_element_type=F32)
        u = jnp.dot(hb, wu_ref[:, sl].astype(BF16), preferred_element_type=F32)
        a = (g * _sigmoid(g) * u).astype(BF16)
        acc = acc + jnp.dot(a, wd_ref[sl, :].astype(BF16), preferred_element_type=F32)
    return acc


def _token_copy(src_ref, src_row, dst_ref, dst_row, sem):
    return pltpu.make_async_copy(src_ref.at[pl.ds(pl.multiple_of(src_row, SUBLANES), SUBLANES), :],
                                 dst_ref.at[pl.ds(pl.multiple_of(dst_row, SUBLANES), SUBLANES), :], sem)


def _dispatch_kernel(slot_ref, pad_ref, h_ref, xs_ref, hbuf, zero_ref, in_sem, out_sem, *, tm, npad, steps):
    i = pl.program_id(0)

    tr = tm * SUBLANES

    def fetch(step, slot):
        return pltpu.make_async_copy(h_ref.at[pl.ds(step * tr, tr), :], hbuf.at[slot], in_sem.at[slot])

    def drain(slot):
        for _ in range(2):
            pltpu.make_async_copy(hbuf.at[slot], xs_ref.at[pl.ds(0, tr), :], out_sem.at[slot]).wait()
        pltpu.make_async_copy(hbuf.at[slot, pl.ds(0, npad * SUBLANES), :], xs_ref.at[pl.ds(0, npad * SUBLANES), :],
                              out_sem.at[slot]).wait()

    @pl.when(i == 0)
    def _():
        zero_ref[...] = jnp.zeros(zero_ref.shape, F32)
        fetch(0, 0).start()

    slot = lax.rem(i, 3)
    fetch(i, slot).wait()

    @pl.when(i + 1 < steps)
    def _():
        fetch(i + 1, lax.rem(i + 1, 3)).start()

    def scatter(r, carry):
        for k in range(2):
            _token_copy(hbuf.at[slot], r * SUBLANES, xs_ref, slot_ref[0, 0, k * tm + r],
                        out_sem.at[slot]).start(priority=k)
        return carry

    def scatter_pad(j, carry):
        _token_copy(zero_ref, 0, xs_ref, pad_ref[0, 0, j], out_sem.at[slot]).start()
        return carry

    lax.fori_loop(0, tm, scatter, 0, unroll=8)
    lax.fori_loop(0, npad, scatter_pad, 0, unroll=8)

    @pl.when(i > 0)
    def _():
        drain(lax.rem(i + 2, 3))

    @pl.when(i == steps - 1)
    def _():
        drain(slot)


def _dispatch(h_tiles, slots, pad_slots, n_tokens_out):
    n = h_tiles.shape[0] // SUBLANES
    tm = TM_ROUTE
    npad = pad_slots.shape[-1]
    steps = n // tm
    return pl.pallas_call(
        functools.partial(_dispatch_kernel, tm=tm, npad=npad, steps=steps),
        grid=(steps,),
        in_specs=[pl.BlockSpec((1, 1, 2 * tm), lambda i: (i, 0, 0), memory_space=pltpu.SMEM),
                  pl.BlockSpec((1, 1, npad), lambda i: (i, 0, 0), memory_space=pltpu.SMEM),
                  pl.BlockSpec(memory_space=pl.ANY)],
        out_specs=pl.BlockSpec(memory_space=pl.ANY),
        out_shape=jax.ShapeDtypeStruct((n_tokens_out * SUBLANES, LANES), F32),
        scratch_shapes=[pltpu.VMEM((3, tm * SUBLANES, LANES), F32), pltpu.VMEM((SUBLANES, LANES), F32),
                        pltpu.SemaphoreType.DMA((3,)), pltpu.SemaphoreType.DMA((3,))],
        compiler_params=_cparams(("arbitrary",), 3 * tm * D_MODEL * 4 + (4 << 20)),
        name="dispatch",
    )(slots, pad_slots, h_tiles)


def _experts_kernel(te_ref, nt_ref, xs_ref, wg_ref, wu_ref, wd_ref, o_ref, *, tm):
    t = pl.program_id(0)

    @pl.when(t < nt_ref[0])
    def _():
        hb = _load_token_tiles(xs_ref, 0, tm).astype(BF16)
        _store_token_tiles(o_ref, 0, _swiglu_tile(hb, wg_ref, wu_ref, wd_ref))

    @pl.when(t >= nt_ref[0])
    def _():
        o_ref[...] = jnp.zeros(o_ref.shape, F32)


def _experts(xs, tile_expert, n_tiles_used, p):
    D = D_MODEL
    tm = TM_MOE
    n_tiles = tile_expert.shape[0]
    wmap = lambda t, te, nt: (te[t], 0, 0)
    tok = pl.BlockSpec((tm * SUBLANES, LANES), lambda t, te, nt: (t, 0))
    tok_in = pl.BlockSpec((tm * SUBLANES, LANES), lambda t, te, nt: (jnp.minimum(t, nt[0] - 1), 0))
    vmem = 2 * 3 * D * D_FF * p["moe_wg"].dtype.itemsize + 12 * tm * D * 4 + (6 << 20)
    return pl.pallas_call(
        functools.partial(_experts_kernel, tm=tm),
        grid_spec=pltpu.PrefetchScalarGridSpec(
            num_scalar_prefetch=2,
            grid=(n_tiles,),
            in_specs=[tok_in,
                      pl.BlockSpec((None, D, D_FF), wmap),
                      pl.BlockSpec((None, D, D_FF), wmap),
                      pl.BlockSpec((None, D_FF, D), wmap)],
            out_specs=tok),
        out_shape=jax.ShapeDtypeStruct((n_tiles * tm * SUBLANES, LANES), F32),
        compiler_params=_cparams(("arbitrary",), vmem),
        name="experts",
    )(tile_expert, n_tiles_used, xs, p["moe_wg"], p["moe_wu"], p["moe_wd"])


def _combine_kernel(slot_ref, next_ref, x1_ref, route_ref, mod_ref, ys_ref, o_ref, buf_ref, sem, *, tm, steps):
    i = pl.program_id(0)

    def gather(idx_ref, s):
        def body(r, carry):
            for k in range(2):
                _token_copy(ys_ref, idx_ref[0, 0, k * tm + r], buf_ref.at[s, k], r * SUBLANES,
                            sem.at[s]).start(priority=k)
            return carry
        lax.fori_loop(0, tm, body, 0, unroll=8)

    @pl.when(i == 0)
    def _():
        gather(slot_ref, 0)

    @pl.when(i + 1 < steps)
    def _():
        gather(next_ref, lax.rem(i + 1, 2))

    cur = lax.rem(i, 2)
    for k in range(2):
        pltpu.make_async_copy(ys_ref.at[pl.ds(0, tm * SUBLANES), :], buf_ref.at[cur, k], sem.at[cur]).wait()
    route = route_ref[...]
    w1 = route[:, 2:3]
    w2 = route[:, 3:4]
    y0 = _load_token_tiles(buf_ref, 0, tm, lead=(cur, 0))
    y1 = _load_token_tiles(buf_ref, 0, tm, lead=(cur, 1))
    o_ref[...] = x1_ref[...] + mod_ref[5:6, :] * (w1 * y0 + w2 * y1)


def _combine(x1, route, mod, ys, slots):
    B, S, D = x1.shape
    tm = TM_ROUTE
    per = S // tm
    steps = B * per
    tok = lambda w: pl.BlockSpec((None, tm, w), lambda i: (i // per, i % per, 0))
    return pl.pallas_call(
        functools.partial(_combine_kernel, tm=tm, steps=steps),
        grid=(steps,),
        in_specs=[pl.BlockSpec((1, 1, 2 * tm), lambda i: (i, 0, 0), memory_space=pltpu.SMEM),
                  pl.BlockSpec((1, 1, 2 * tm), lambda i: (jnp.minimum(i + 1, steps - 1), 0, 0),
                               memory_space=pltpu.SMEM),
                  tok(D), tok(LANES), pl.BlockSpec((None, N_MOD, D), lambda i: (i // per, 0, 0)),
                  pl.BlockSpec(memory_space=pl.ANY)],
        out_specs=tok(D),
        out_shape=jax.ShapeDtypeStruct((B, S, D), F32),
        scratch_shapes=[pltpu.VMEM((2, 2, tm * SUBLANES, LANES), F32), pltpu.SemaphoreType.DMA((2,))],
        compiler_params=_cparams(("arbitrary",), 12 * tm * D * 4 + (4 << 20)),
        name="combine",
    )(slots, slots, x1, route, mod, ys)


def _moe(h, x1, route, cnt, route_t, mod, p):
    B, S, D = x1.shape
    n = B * S
    sub = n // (cnt.shape[0] * cnt.shape[1])
    tmm = TM_MOE
    n_tiles = 2 * n // tmm + N_EXPERTS
    n_slots = n_tiles * tmm
    steps = n // TM_ROUTE
    counts = cnt[:, :, 0:2, 0:N_EXPERTS].astype(jnp.int32).reshape(-1, 2, N_EXPERTS)
    tile_tot = counts.sum(axis=0)
    n_e = tile_tot.sum(axis=0)
    base = jnp.cumsum(counts, axis=0) - counts
    base = base + jnp.array([0, 1], jnp.int32)[None, :, None] * tile_tot[0][None, None, :]
    tiles_e = (n_e + tmm - 1) // tmm
    pstart = (jnp.cumsum(tiles_e) - tiles_e) * tmm
    expert_ids = jnp.arange(N_EXPERTS, dtype=jnp.int32)[:, None]
    slots_k = []
    for k in range(2):
        e_k = route_t[k].astype(jnp.int32)
        table = jnp.repeat((base[:, k, :] + pstart[None, :]).T, sub, axis=1)
        slot = jnp.sum(jnp.where(e_k[None, :] == expert_ids, table, 0), axis=0) + route_t[4 + k].astype(jnp.int32)
        slots_k.append((slot * SUBLANES).reshape(steps, TM_ROUTE))
    slots = jnp.concatenate(slots_k, axis=1).reshape(steps, 1, 2 * TM_ROUTE)
    tile_end = jnp.cumsum(tiles_e)
    n_used = tile_end[-1]
    t_idx = jnp.minimum(jnp.arange(n_tiles, dtype=jnp.int32), n_used - 1)
    tile_expert = jnp.sum(t_idx[:, None] >= tile_end[None, :], axis=-1).astype(jnp.int32)
    tile_expert = jnp.minimum(tile_expert, N_EXPERTS - 1)

    n_cand = N_EXPERTS * tmm
    pq = jnp.arange(tmm, dtype=jnp.int32)[None, :]
    n_pad_e = (tiles_e * tmm - n_e)[:, None]
    spare = n_slots + expert_ids * tmm + pq
    pad_slots = jnp.where(pq < n_pad_e, (pstart + n_e)[:, None] + pq, spare).astype(jnp.int32)
    assert n_cand % steps == 0
    pad_slots = (pad_slots * SUBLANES).reshape(steps, 1, n_cand // steps)

    xs = _dispatch(h, slots, pad_slots, n_slots + n_cand)
    ys = _experts(xs, tile_expert, n_used.reshape(1).astype(jnp.int32), p)
    return _combine(x1, route, mod, ys, slots)


def _pad_heads(w, width):
    lead = w.shape[:-1]
    w = w.reshape(lead + (N_HEADS, width))
    w = jnp.pad(w, [(0, 0)] * len(lead) + [(0, 0), (0, HEAD_PAD - width)])
    return w.reshape(lead + (N_HEADS * HEAD_PAD,))


def _layer_params(l, w_in, q_lat_g, kv_lat_g, w_uq, w_ukv, q_norm_g, k_norm_g, conv_w, conv_b, conv_ln_g,
                  conv_ln_b, w_out, ffn_norm_g):
    wi = w_in[l]
    kr = _with_partner(jnp.pad(wi[:, OFF_KR:OFF_CONV], ((0, 0), (QK_NOPE, HEAD_PAD - QK_HEAD))))
    ckv = wi[:, OFF_CKV:OFF_KR]
    w_full = jnp.concatenate([wi[:, OFF_CQ:OFF_CKV], ckv, wi[:, OFF_CONV:OFF_FOUR], wi[:, OFF_FOUR:IN_COLS], kr],
                             axis=1)
    w_kv = jnp.concatenate([ckv, kr], axis=1)
    ukv = w_ukv[l].reshape(KV_LORA, N_HEADS, QK_NOPE + V_HEAD)
    uk = _pad_heads(ukv[:, :, :QK_NOPE].reshape(KV_LORA, N_HEADS * QK_NOPE), QK_NOPE)
    uv = _pad_heads(ukv[:, :, QK_NOPE:].reshape(KV_LORA, ATTN_OUT), V_HEAD)
    pad_g = lambda g: jnp.pad(g, (0, HEAD_PAD - QK_HEAD)).reshape(1, HEAD_PAD)
    uq = _with_partner(_pad_heads(w_uq[l], QK_HEAD))
    hw = N_HEADS * HEAD_PAD
    col = np.arange(hw)[:, None]
    head_of_col = col // HEAD_PAD == np.arange(LANES)[None, :]
    return {
        "w_in_full": w_full.astype(BF16),
        "w_in_kv": w_kv.astype(BF16),
        "q_lat_g": q_lat_g[l].reshape(1, Q_LORA),
        "kv_lat_g": kv_lat_g[l].reshape(1, KV_LORA),
        "w_uq": uq.astype(BF16),
        "w_ukv": jnp.concatenate([uk, uv], axis=1).astype(BF16),
        "q_norm_g": pad_g(q_norm_g[l]) * (QK_HEAD ** -0.5 * math.log2(math.e)),
        "v_ones": jnp.tile(jnp.concatenate([jnp.zeros((V_HEAD,), F32), jnp.ones((HEAD_PAD - V_HEAD,), F32)]),
                           N_HEADS).reshape(1, hw),
        "ones_h": jnp.asarray(head_of_col & (col % HEAD_PAD < QK_HEAD), F32).astype(BF16),
        "k_norm_g": pad_g(k_norm_g[l]),
        "conv_w": conv_w[l],
        "conv_b": conv_b[l].reshape(1, CONV_CH),
        "conv_ln_g": conv_ln_g[l].reshape(1, CONV_CH),
        "conv_ln_b": conv_ln_b[l].reshape(1, CONV_CH),
        "ffn_norm_g": ffn_norm_g[l].reshape(1, D_MODEL),
        "bd": _channel_dft(),
    }


def _with_partner(w):
    lead = w.shape[:-1]
    half = QK_ROPE // 2
    w3 = w.reshape(lead + (-1, HEAD_PAD))
    out = jnp.concatenate([w3[..., :QK_HEAD], w3[..., QK_NOPE + half:QK_HEAD], w3[..., QK_NOPE:QK_NOPE + half]],
                          axis=-1)
    return out.reshape(w.shape)


def _rot_partner(w):
    lead = w.shape[:-1]
    half = QK_ROPE // 2
    w3 = w.reshape(lead + (-1, HEAD_PAD))
    z = jnp.zeros_like(w3)
    out = jnp.concatenate([z[..., :QK_NOPE], w3[..., QK_NOPE + half:QK_HEAD], w3[..., QK_NOPE:QK_NOPE + half],
                           z[..., QK_HEAD:]], axis=-1)
    return out.reshape(w.shape)


def _rope_tables(seq, gq, gk):
    rows = seq // GRID_W
    row = np.repeat(np.arange(rows, dtype=np.float64), GRID_W)
    col = np.tile(np.arange(GRID_W, dtype=np.float64), rows)
    n_freq = QK_ROPE // 4
    inv = ROPE_BASE ** (-np.arange(n_freq, dtype=np.float64) / n_freq)
    ang = np.concatenate([row[:, None] * inv, col[:, None] * inv], axis=-1)
    cos, sin = np.cos(ang), np.sin(ang)
    ones = np.ones((seq, QK_NOPE))
    tail = np.zeros((seq, HEAD_PAD - QK_HEAD))
    cos_t = jnp.asarray(np.concatenate([ones, cos, cos, tail], axis=1), F32)
    sin_t = jnp.asarray(np.concatenate([np.zeros((seq, QK_NOPE)), -sin, sin, tail], axis=1), F32)
    return (cos_t * gq, sin_t * _rot_partner(gq), cos_t * gk, sin_t * _rot_partner(gk))


def kernel(x, c, ctx, c_ctx, ada_w, ada_b, mix_norm_g, ffn_norm_g, w_in, q_lat_g, kv_lat_g, w_uq, w_ukv, q_norm_g,
           k_norm_g, conv_w, conv_b, conv_ln_g, conv_ln_b, w_out, ffn_w_gate, ffn_w_up, ffn_w_down, router_w,
           moe_w_gate, moe_w_up, moe_w_down):
    B, S, D = x.shape
    T = ctx.shape[1]
    assert (D, DEPTH) == (D_MODEL, ada_w.shape[0]) and S % GRID_W == 0

    cc = jnp.concatenate([c, c_ctx[None, :], jnp.zeros((2 * SUBLANES - B - 1, D), F32)], axis=0)
    mods = _modulation(cc, ada_w, ada_b).reshape(DEPTH, 2 * SUBLANES, N_MOD, D)
    cs_x = _dft_tables(S)
    cs_c = _dft_tables(T)

    def moe_f32(i):
        return (moe_w_gate[i].reshape(N_EXPERTS * D, D_FF), moe_w_up[i].reshape(N_EXPERTS * D, D_FF),
                moe_w_down[i].reshape(N_EXPERTS * D_FF, D))

    moe_bf16 = {}
    for l in range(DEPTH):
        last = l == DEPTH - 1
        p = _layer_params(l, w_in, q_lat_g, kv_lat_g, w_uq, w_ukv, q_norm_g, k_norm_g, conv_w, conv_b, conv_ln_g,
                          conv_ln_b, w_out, ffn_norm_g)
        p["mix_norm_g"] = mix_norm_g[l].reshape(1, D)
        rope_tabs = _rope_tables(S, p["q_norm_g"], p["k_norm_g"])
        i = l // 2
        moe = l % 2 == 1
        mod_x = mods[l, :B]
        mod_c = mods[l, B:B + 1]

        nxt = l + 1 if l % 2 == 0 else l + 2
        cast = []
        if not moe:
            cast += [ffn_w_gate[i], ffn_w_up[i], ffn_w_down[i]]
        if nxt < DEPTH:
            cast += list(moe_f32(nxt // 2))
        p["w_out"] = w_out[l].astype(BF16)

        def channel_mixer(attn, conv, four, xs, mod):
            if moe:
                x1, h, route, cnt, route_t = _mix_out(attn, conv, four, xs, mod, p, moe=True)
                mod_b = mod if mod.shape[0] > 1 else jnp.broadcast_to(mod, (B,) + mod.shape[1:])
                return _moe(h, x1, route, cnt, route_t, mod_b, p)
            return _mix_out(attn, conv, four, xs, mod, p, moe=False)

        if last:
            k_c, v_c = _front(ctx, mod_c, p["mix_norm_g"], p, None, full=False)
        else:
            q_c, k_c, v_c, y_c, zz_c = _front(ctx, mod_c, p["mix_norm_g"], p, None, full=True)
        q_x, k_x, v_x, y_x, zz_x = _front(x, mod_x, p["mix_norm_g"], p, rope_tabs, full=True)
        attn_x, *casted = _attend(q_x, [k_x, k_c], [v_x, v_c], cast)
        if moe:
            rw = jnp.pad(router_w[i], ((0, 0), (0, LANES - N_EXPERTS)))
            rw_hi = rw.astype(BF16)
            p["router_w"] = jnp.concatenate([rw_hi, (rw - rw_hi.astype(F32)).astype(BF16)], axis=1)
            wg, wu, wd = moe_bf16.pop(i) if i in moe_bf16 else [w.astype(BF16) for w in moe_f32(i)]
            p["moe_wg"] = wg.reshape(N_EXPERTS, D, D_FF)
            p["moe_wu"] = wu.reshape(N_EXPERTS, D, D_FF)
            p["moe_wd"] = wd.reshape(N_EXPERTS, D_FF, D)
        else:
            p["ffn_wg"], p["ffn_wu"], p["ffn_wd"] = casted[:3]
            casted = casted[3:]
        if nxt < DEPTH:
            moe_bf16[nxt // 2] = casted

        if not last:
            attn_c, = _attend(q_c, [k_c], [v_c])
            ctx_next = channel_mixer(attn_c, *_conv_fourier(y_c, zz_c, cs_c, p), ctx, mod_c)

        x = channel_mixer(attn_x, *_conv_fourier(y_x, zz_x, cs_x, p), x, mod_x)
        if not last:
            ctx = ctx_next
    return x
```

```python
import functools
import math

import numpy as np
import jax
import jax.numpy as jnp
from jax import lax
from jax.experimental import pallas as pl
from jax.experimental.pallas import tpu as pltpu

F32 = jnp.float32
BF16 = jnp.bfloat16

D_MODEL = 1024
DEPTH = 2
GRID_W = 64
N_HEADS = 8
QK_NOPE = 64
QK_ROPE = 32
QK_HEAD = QK_NOPE + QK_ROPE
V_HEAD = 64
Q_LORA = 384
KV_LORA = 256
ROPE_BASE = 10000.0
CONV_CH = 256
CONV_WIDTH = 31
CONV_PAD = (CONV_WIDTH - 1) // 2
FOURIER_GROUPS = 4
FOURIER_GROUP_DIM = 64
FOURIER_CH = FOURIER_GROUPS * FOURIER_GROUP_DIM
ATTN_OUT = N_HEADS * V_HEAD
OFF_CQ = 0
OFF_CKV = OFF_CQ + Q_LORA
OFF_KR = OFF_CKV + KV_LORA
OFF_CONV = OFF_KR + QK_ROPE
OFF_FOUR = OFF_CONV + 2 * CONV_CH
IN_COLS = OFF_FOUR + FOURIER_CH
D_FF = 2816
N_EXPERTS = 8
N_MOD = 6
EPS = 1e-6

LANES = 128
SUBLANES = 8
HEAD_PAD = LANES
VMEM_CAP = 56 * 1024 * 1024
FF_CHUNK = 256

SUB_ROWS = 256
SUB_ROWS_MOE = 512
TM_FRONT = 1024
TQ_ATTN = 1024
TQ_SUB = 512
TK_ATTN = 256
TM_OUT = 512
TM_MOE = 512
TM_ROUTE = 1024
CONV_ROWS = 128
TR_FOURIER = 512


def _cparams(sem, vmem_bytes):
    return pltpu.CompilerParams(dimension_semantics=sem, vmem_limit_bytes=int(min(VMEM_CAP, vmem_bytes)))


def _rms(v, g):
    return v * lax.rsqrt(jnp.mean(v * v, axis=-1, keepdims=True) + EPS) * g


def _sigmoid(v):
    return 1.0 / (1.0 + jnp.exp(-v))


def _mod_kernel(c_ref, w_ref, b_ref, o_ref):
    c = c_ref[...]
    s = (c * _sigmoid(c)).astype(BF16)
    o_ref[...] = jnp.dot(s, w_ref[...].astype(BF16), preferred_element_type=F32) + b_ref[...]


def _modulation(cc, ada_w, ada_b):
    rows = cc.shape[0]
    tn = 1536
    n_out = N_MOD * D_MODEL
    return pl.pallas_call(
        _mod_kernel,
        grid=(DEPTH, n_out // tn),
        in_specs=[
            pl.BlockSpec((rows, D_MODEL), lambda l, j: (0, 0)),
            pl.BlockSpec((None, D_MODEL, tn), lambda l, j: (l, 0, j)),
            pl.BlockSpec((None, 1, tn), lambda l, j: (l, 0, j)),
        ],
        out_specs=pl.BlockSpec((None, rows, tn), lambda l, j: (l, 0, j)),
        out_shape=jax.ShapeDtypeStruct((DEPTH, rows, n_out), F32),
        compiler_params=_cparams(("arbitrary", "arbitrary"), 4 * D_MODEL * tn * 4),
        name="modulation",
    )(cc, ada_w, ada_b.reshape(DEPTH, 1, n_out))


def _front_kernel(*refs, full, rope):
    it = iter(refs)
    x_ref, mod_ref, g_ref, win_ref = next(it), next(it), next(it), next(it)
    if full:
        qlg_ref, wuq_ref = next(it), next(it)
    kvlg_ref, wukv_ref, vones_ref, onesh_ref = next(it), next(it), next(it), next(it)
    if rope:
        cq_ref, sq_ref, ck_ref, sk_ref = next(it), next(it), next(it), next(it)
    else:
        qg_ref = next(it) if full else None
        kg_ref = next(it)
    if full:
        bd_ref = next(it)
        q_out = next(it)
    k_out, v_out = next(it), next(it)
    if full:
        y_out, zz_out = next(it), next(it)

    shift = mod_ref[0:1, :]
    gain = g_ref[...] * (1.0 + mod_ref[1:2, :])
    hw = N_HEADS * HEAD_PAD

    def head_inv_rms(raw):
        ss = jnp.dot((raw * raw).astype(BF16), onesh_ref[...], preferred_element_type=F32)
        rs = lax.rsqrt(ss * (1.0 / QK_HEAD) + EPS)
        return jnp.concatenate([jnp.broadcast_to(rs[:, hd:hd + 1], (rs.shape[0], HEAD_PAD)) for hd in range(N_HEADS)],
                               axis=1)

    def sub_block(rows):
        x = x_ref[rows, :]
        h = _rms(x, gain) + shift
        cols = jnp.dot(h.astype(BF16), win_ref[...], preferred_element_type=F32)

        o = 0
        if full:
            cq = cols[:, 0:Q_LORA]
            o = Q_LORA
            qall = jnp.dot(_rms(cq, qlg_ref[...]).astype(BF16), wuq_ref[...], preferred_element_type=F32)
            rsb = head_inv_rms(qall)
            for hd in range(N_HEADS):
                sl = slice(hd * HEAD_PAD, (hd + 1) * HEAD_PAD)
                if rope:
                    val = qall[:, sl] * cq_ref[rows, :] + pltpu.roll(qall[:, sl], HEAD_PAD - QK_ROPE, 1) * \
                        sq_ref[rows, :]
                else:
                    val = qall[:, sl] * qg_ref[...]
                q_out[rows, sl] = (val * rsb[:, sl]).astype(BF16)

        ckv = cols[:, o:o + KV_LORA]
        o += KV_LORA
        kv = jnp.dot(_rms(ckv, kvlg_ref[...]).astype(BF16), wukv_ref[...], preferred_element_type=F32)
        if full:
            a = cols[:, o:o + CONV_CH]
            gt = cols[:, o + CONV_CH:o + 2 * CONV_CH]
            y_out[rows, :] = a * _sigmoid(gt)
            o += 2 * CONV_CH
            z = cols[:, o:o + FOURIER_CH]
            o += FOURIER_CH
            zz_out[rows, :] = jnp.dot(z.astype(BF16), bd_ref[...], preferred_element_type=F32).astype(BF16)
        krb = cols[:, o:o + HEAD_PAD]
        kraw = [kv[:, hd * HEAD_PAD:(hd + 1) * HEAD_PAD] + krb for hd in range(N_HEADS)]
        rsb = head_inv_rms(jnp.concatenate(kraw, axis=1))
        if rope:
            kpart = pltpu.roll(krb, HEAD_PAD - QK_ROPE, 1) * sk_ref[rows, :]
        for hd in range(N_HEADS):
            sl = slice(hd * HEAD_PAD, (hd + 1) * HEAD_PAD)
            val = kraw[hd] * ck_ref[rows, :] + kpart if rope else kraw[hd] * kg_ref[...]
            k_out[rows, sl] = (val * rsb[:, sl]).astype(BF16)
        v_out[rows, :] = (kv[:, hw:] + vones_ref[...]).astype(BF16)

    tm = x_ref.shape[0]
    sub = min(SUB_ROWS, tm)
    for sb in range(tm // sub):
        sub_block(slice(sb * sub, (sb + 1) * sub))


def _front(xs, mod, norm_g, p, rope_tabs, *, full):
    if mod.shape[0] == 1 and xs.shape[0] > 1 and rope_tabs is None:
        outs = _front(xs.reshape(1, -1, xs.shape[-1]), mod, norm_g, p, None, full=full)
        return [o.reshape(xs.shape[:2] + o.shape[2:]) for o in outs]
    B, S, D = xs.shape
    tm = min(TM_FRONT, S)
    rope = rope_tabs is not None
    assert full or not rope
    w_in = p["w_in_full"] if full else p["w_in_kv"]
    w_uq = p["w_uq"]
    ncol = w_in.shape[1]
    bm = mod.shape[0]
    mod_map = (lambda b, i: (b, 0, 0)) if bm > 1 else (lambda b, i: (0, 0, 0))
    const2 = lambda b, i: (0, 0)
    hw = N_HEADS * HEAD_PAD

    args = [xs, mod, norm_g, w_in]
    specs = [
        pl.BlockSpec((None, tm, D), lambda b, i: (b, i, 0)),
        pl.BlockSpec((None, N_MOD, D), mod_map),
        pl.BlockSpec((1, D), const2),
        pl.BlockSpec((D, ncol), const2),
    ]
    if full:
        args += [p["q_lat_g"], w_uq]
        specs += [pl.BlockSpec((1, Q_LORA), const2), pl.BlockSpec((Q_LORA, w_uq.shape[1]), const2)]
    args += [p["kv_lat_g"], p["w_ukv"], p["v_ones"], p["ones_h"]]
    specs += [pl.BlockSpec((1, KV_LORA), const2),
              pl.BlockSpec((KV_LORA, 2 * hw), const2),
              pl.BlockSpec((1, hw), const2),
              pl.BlockSpec((hw, LANES), const2)]
    if rope:
        args += list(rope_tabs)
        specs += [pl.BlockSpec((tm, HEAD_PAD), lambda b, i: (i, 0))] * 4
    else:
        if full:
            args += [p["q_norm_g"]]
            specs += [pl.BlockSpec((1, HEAD_PAD), const2)]
        args += [p["k_norm_g"]]
        specs += [pl.BlockSpec((1, HEAD_PAD), const2)]
    if full:
        args += [p["bd"]]
        specs += [pl.BlockSpec((FOURIER_CH, 2 * FOURIER_CH), const2)]

    out_shapes, out_specs = [], []

    def add_out(width, dtype):
        out_shapes.append(jax.ShapeDtypeStruct((B, S, width), dtype))
        out_specs.append(pl.BlockSpec((None, tm, width), lambda b, i: (b, i, 0)))

    if full:
        add_out(N_HEADS * HEAD_PAD, BF16)
    add_out(N_HEADS * HEAD_PAD, BF16)
    add_out(N_HEADS * HEAD_PAD, BF16)
    if full:
        add_out(CONV_CH, F32)
        add_out(2 * FOURIER_CH, BF16)

    vmem = 2 * (D * ncol * 2 + Q_LORA * 1024 * 2 + KV_LORA * 2048 * 2) + 28 * tm * D * 4
    return pl.pallas_call(
        functools.partial(_front_kernel, full=full, rope=rope),
        grid=(B, S // tm),
        in_specs=specs,
        out_specs=out_specs,
        out_shape=out_shapes,
        compiler_params=_cparams(("arbitrary", "arbitrary"), vmem),
        name="front_full" if full else "front_kv",
    )(*args)


def _attn_kernel(*refs, nsrc, ncast):
    q_ref = refs[0]
    k_refs = refs[1:1 + nsrc]
    v_refs = refs[1 + nsrc:1 + 2 * nsrc]
    cast_in = refs[1 + 2 * nsrc:1 + 2 * nsrc + ncast]
    o_ref = refs[1 + 2 * nsrc + ncast]
    cast_out = refs[2 + 2 * nsrc + ncast:]
    chunks = []
    for k_ref, v_ref in zip(k_refs, v_refs):
        for s0 in range(0, k_ref.shape[0], TK_ATTN):
            chunks.append((k_ref, v_ref, s0, min(TK_ATTN, k_ref.shape[0] - s0)))
    tq = q_ref.shape[0]
    sub = min(TQ_SUB, tq)
    for pair, r0 in [(pr, rr) for pr in range(q_ref.shape[1] // (2 * HEAD_PAD)) for rr in range(0, tq, sub)]:
        rows = slice(r0, r0 + sub)
        state = [None, None]
        for k_ref, v_ref, s0, sz in chunks:
            for hh in range(2):
                sl = slice((2 * pair + hh) * HEAD_PAD, (2 * pair + hh + 1) * HEAD_PAD)
                s = lax.dot_general(q_ref[rows, sl], k_ref[s0:s0 + sz, sl], (((1,), (1,)), ((), ())),
                                    preferred_element_type=F32)
                m = jnp.max(s, axis=-1, keepdims=True)
                if state[hh] is not None:
                    m_old, acc_old = state[hh]
                    m = jnp.maximum(m_old, m)
                pv = jnp.dot(jnp.exp2((s - m).astype(BF16)), v_ref[s0:s0 + sz, sl], preferred_element_type=F32)
                if state[hh] is not None:
                    pv = pv + jnp.exp2(m_old - m) * acc_old
                state[hh] = (m, pv)
        accs = [state[0][1], state[1][1]]
        lane = lax.broadcasted_iota(jnp.int32, accs[0].shape, 1)
        lo = accs[0] / pltpu.roll(accs[0], V_HEAD, 1)
        hi = pltpu.roll(accs[1], V_HEAD, 1) / accs[1]
        o_ref[rows, pair * 2 * V_HEAD:(pair + 1) * 2 * V_HEAD] = jnp.where(lane < V_HEAD, lo, hi).astype(BF16)
    for src, dst in zip(cast_in, cast_out):
        dst[...] = src[...].astype(BF16)


def _attend(q, ks, vs, cast=()):
    B, S, _ = q.shape
    tq = min(TQ_ATTN, S)
    nsrc = len(ks)
    nq = S // tq
    pp = N_HEADS // 2 if nq == 1 and S <= TQ_SUB else 1
    ngrp = N_HEADS // 2 // pp
    steps = B * ngrp * nq
    specs = [pl.BlockSpec((None, tq, pp * 2 * HEAD_PAD), lambda b, hp, i: (b, i, hp))]
    for kv in list(ks) + list(vs):
        specs.append(pl.BlockSpec((None, kv.shape[1], pp * 2 * HEAD_PAD), lambda b, hp, i: (b, 0, hp)))
    out_specs = [pl.BlockSpec((None, tq, pp * 2 * V_HEAD), lambda b, hp, i: (b, i, hp))]
    out_shapes = [jax.ShapeDtypeStruct((B, S, ATTN_OUT), BF16)]
    cast_bytes = 0
    for w in cast:
        rows, cols = w.shape
        nblk = max(d for d in range(1, steps + 1)
                   if steps % d == 0 and rows % d == 0 and (rows // d) % (2 * SUBLANES) == 0)
        hold = steps // nblk
        spec = pl.BlockSpec((rows // nblk, cols),
                            lambda b, hp, i, hold=hold: (((b * ngrp + hp) * nq + i) // hold, 0))
        specs.append(spec)
        out_specs.append(spec)
        out_shapes.append(jax.ShapeDtypeStruct(w.shape, BF16))
        cast_bytes += 2 * (rows // nblk) * cols * 6
    t_all = sum(k.shape[1] for k in ks)
    vmem = 2 * (t_all * pp * 512 * 2) * 2 + 16 * min(tq, TQ_SUB) * TK_ATTN * 4 + cast_bytes + (12 << 20)
    outs = pl.pallas_call(
        functools.partial(_attn_kernel, nsrc=nsrc, ncast=len(cast)),
        grid=(B, ngrp, nq),
        in_specs=specs,
        out_specs=out_specs,
        out_shape=out_shapes,
        compiler_params=_cparams(("arbitrary", "arbitrary", "arbitrary"), vmem),
        name="attend%d" % nsrc,
    )(q, *ks, *vs, *cast)
    return list(outs)


def _conv_fourier_kernel(y_ref, w_ref, b_ref, lg_ref, lb_ref, cs_ref, zz_ref, conv_out, four_out, pad_ref, *, seq):
    halo = 2 * SUBLANES
    total = seq + 2 * halo
    pad_ref[0, 0:halo, :] = jnp.zeros((halo, CONV_CH), F32)
    pad_ref[0, halo + seq:total, :] = jnp.zeros((halo, CONV_CH), F32)
    pad_ref[0, halo:halo + seq, :] = y_ref[...]
    slab = pad_ref[0]
    for ph in range(1, SUBLANES):
        pad_ref[ph] = pltpu.roll(slab, total - ph, 0)

    def conv_rows(base, rows):
        acc = jnp.zeros((rows, CONV_CH), F32)
        for j in range(CONV_WIDTH):
            off = base + halo - CONV_PAD + j
            ph = off % SUBLANES
            acc = acc + pad_ref[ph, off - ph:off - ph + rows, :] * w_ref[j:j + 1, :]
        acc = acc + b_ref[...]
        mu = jnp.mean(acc, axis=-1, keepdims=True)
        cen = acc - mu
        var = jnp.mean(cen * cen, axis=-1, keepdims=True)
        yn = cen * lax.rsqrt(var + EPS) * lg_ref[...] + lb_ref[...]
        conv_out[base:base + rows, :] = (yn * _sigmoid(yn)).astype(BF16)

    def dft_rows(base, rows):
        r = slice(base, base + rows)
        acc = jnp.dot(cs_ref[r, 0:seq], zz_ref[:, 0:FOURIER_CH], preferred_element_type=F32)
        acc = acc + jnp.dot(cs_ref[r, seq:2 * seq], zz_ref[:, FOURIER_CH:2 * FOURIER_CH],
                            preferred_element_type=F32)
        four_out[r, :] = acc.astype(BF16)

    tr = min(TR_FOURIER, seq)
    for base in range(0, seq, tr):
        for cb in range(base, base + tr, CONV_ROWS):
            conv_rows(cb, CONV_ROWS)
        dft_rows(base, tr)


def _conv_fourier(y, zz, cs, p):
    B, S, _ = y.shape
    c2 = lambda b: (0, 0)
    seq_block = lambda w: pl.BlockSpec((None, S, w), lambda b: (b, 0, 0))
    vmem = (SUBLANES + 6) * (S + 32) * CONV_CH * 4 + S * 2 * S * 2 + 2 * S * 512 * 2 + (10 << 20)
    return pl.pallas_call(
        functools.partial(_conv_fourier_kernel, seq=S),
        grid=(B,),
        in_specs=[seq_block(CONV_CH),
                  pl.BlockSpec((CONV_WIDTH, CONV_CH), c2),
                  pl.BlockSpec((1, CONV_CH), c2), pl.BlockSpec((1, CONV_CH), c2), pl.BlockSpec((1, CONV_CH), c2),
                  pl.BlockSpec((S, 2 * S), c2, pipeline_mode=pl.Buffered(1)),
                  seq_block(2 * FOURIER_CH)],
        out_specs=[seq_block(CONV_CH), seq_block(FOURIER_CH)],
        out_shape=[jax.ShapeDtypeStruct((B, S, CONV_CH), BF16), jax.ShapeDtypeStruct((B, S, FOURIER_CH), BF16)],
        scratch_shapes=[pltpu.VMEM((SUBLANES, S + 4 * SUBLANES, CONV_CH), F32)],
        compiler_params=_cparams(("arbitrary",), vmem),
        name="conv_fourier",
    )(y, p["conv_w"], p["conv_b"], p["conv_ln_g"], p["conv_ln_b"], cs, zz)


def _dft_tables(seq):
    norm = seq ** -0.5
    ks = (np.arange(seq)[:, None] * np.arange(seq)[None, :]) % seq
    ang = 2.0 * np.pi * ks / seq
    tab = np.concatenate([np.cos(ang), -np.sin(ang)], axis=1) * norm
    return jnp.asarray(tab, F32).astype(BF16)


def _channel_dft():
    n = FOURIER_GROUP_DIM
    ang = 2.0 * np.pi * ((np.arange(n)[:, None] * np.arange(n)[None, :]) % n) / n
    eye = np.eye(FOURIER_GROUPS)
    bd = np.concatenate([np.kron(eye, np.cos(ang)), np.kron(eye, np.sin(ang))], axis=1) * n ** -0.5
    return jnp.asarray(bd, F32).astype(BF16)


def _mixout_kernel(*refs, moe):
    if moe:
        (attn_ref, conv_ref, four_ref, x_ref, mod_ref, wout_ref, g_ref, rw_ref, tri_ref,
         x1_out, h_out, route_out, cnt_out, routet_out) = refs
    else:
        (attn_ref, conv_ref, four_ref, x_ref, mod_ref, wout_ref, g_ref, wg_ref, wu_ref, wd_ref,
         x1_out, h_out) = refs
    tm = x_ref.shape[0]
    sub = min(SUB_ROWS_MOE if moe else SUB_ROWS, tm)
    gain = g_ref[...] * (1.0 + mod_ref[4:5, :])
    for sb in range(tm // sub):
        rows = slice(sb * sub, (sb + 1) * sub)
        mix = jnp.dot(attn_ref[rows, :], wout_ref[0:ATTN_OUT, :], preferred_element_type=F32)
        mix = mix + jnp.dot(conv_ref[rows, :], wout_ref[ATTN_OUT:ATTN_OUT + CONV_CH, :],
                            preferred_element_type=F32)
        mix = mix + jnp.dot(four_ref[rows, :], wout_ref[ATTN_OUT + CONV_CH:, :], preferred_element_type=F32)
        x1 = x_ref[rows, :] + mod_ref[2:3, :] * mix
        x1_out[rows, :] = x1
        h = _rms(x1, gain) + mod_ref[3:4, :]
        if moe:
            _store_token_tiles(h_out, sb * sub, h)
            route, cnt = _route(h, rw_ref, tri_ref)
            route_out[rows, :] = route
            cnt_out[sb] = cnt
            routet_out[:, rows] = route.T[0:SUBLANES, :]
        else:
            h_out[rows, :] = h.astype(BF16)
    if not moe:
        x1_out[...] = x1_out[...] + mod_ref[5:6, :] * _swiglu_tile(h_out[...], wg_ref, wu_ref, wd_ref)


def _store_token_tiles(ref, row0, val):
    rows = val.shape[0]
    for j in range(SUBLANES):
        ref[pl.ds(row0 * SUBLANES + j, rows, stride=SUBLANES), :] = val[:, j * LANES:(j + 1) * LANES]


def _load_token_tiles(ref, row0, rows, lead=()):
    idx = tuple(lead)
    return jnp.concatenate([ref[idx + (pl.ds(row0 * SUBLANES + j, rows, stride=SUBLANES), slice(None))]
                            for j in range(SUBLANES)], axis=1)


def _route(h, rw_ref, tri_ref):
    tm = h.shape[0]
    h_hi = h.astype(BF16)
    h_lo = (h - h_hi.astype(F32)).astype(BF16)
    hw = jnp.dot(h_hi, rw_ref[...], preferred_element_type=F32)
    logits = hw[:, 0:LANES] + hw[:, LANES:] + jnp.dot(h_lo, rw_ref[:, 0:LANES], preferred_element_type=F32)
    lane = lax.broadcasted_iota(jnp.int32, (tm, LANES), 1).astype(F32)
    neg = jnp.float32(-jnp.inf)
    lm = jnp.where(lane < N_EXPERTS, logits, neg)
    m1 = jnp.max(lm, axis=-1, keepdims=True)
    i1 = jnp.min(jnp.where(lm == m1, lane, float(LANES)), axis=-1, keepdims=True)
    lm2 = jnp.where(lane == i1, neg, lm)
    m2 = jnp.max(lm2, axis=-1, keepdims=True)
    i2 = jnp.min(jnp.where(lm2 == m2, lane, float(LANES)), axis=-1, keepdims=True)
    t = jnp.exp(m2 - m1)
    w1 = 1.0 / (1.0 + t)
    w2 = t / (1.0 + t)
    oh1 = lane == i1
    oh2 = lane == i2
    tri = tri_ref[...]
    c1 = jnp.dot(tri, jnp.where(oh1, 1.0, 0.0).astype(BF16), preferred_element_type=F32)
    c2 = jnp.dot(tri, jnp.where(oh2, 1.0, 0.0).astype(BF16), preferred_element_type=F32)
    r1 = jnp.sum(jnp.where(oh1, c1, 0.0), axis=-1, keepdims=True)
    r2 = jnp.sum(jnp.where(oh2, c2, 0.0), axis=-1, keepdims=True)
    vals = [i1, i2, w1, w2, r1, r2]
    route = jnp.zeros((tm, LANES), F32)
    for idx, v in enumerate(vals):
        route = jnp.where(lane == idx, v, route)
    n1 = jnp.sum(jnp.where(oh1, 1.0, 0.0), axis=0, keepdims=True)
    n2 = jnp.sum(jnp.where(oh2, 1.0, 0.0), axis=0, keepdims=True)
    row = lax.broadcasted_iota(jnp.int32, (SUBLANES, LANES), 0)
    return route, jnp.where(row == 0, n1, jnp.where(row == 1, n2, 0.0))


def _mix_out(attn, conv, four, xs, mod, p, *, moe):
    if mod.shape[0] == 1 and xs.shape[0] > 1 and not moe:
        flat = lambda a: a.reshape(1, -1, a.shape[-1])
        return _mix_out(flat(attn), flat(conv), flat(four), flat(xs), mod, p, moe=moe).reshape(xs.shape)
    B, S, D = xs.shape
    tm = min(TM_OUT, S)
    sub = min(SUB_ROWS_MOE if moe else SUB_ROWS, tm)
    bm = mod.shape[0]
    mod_map = (lambda b, i: (b, 0, 0)) if bm > 1 else (lambda b, i: (0, 0, 0))
    c2 = lambda b, i: (0, 0)
    tok = lambda w: pl.BlockSpec((None, tm, w), lambda b, i: (b, i, 0))
    args = [attn, conv, four, xs, mod, p["w_out"], p["ffn_norm_g"]]
    specs = [tok(ATTN_OUT), tok(CONV_CH), tok(FOURIER_CH), tok(D), pl.BlockSpec((None, N_MOD, D), mod_map),
             pl.BlockSpec((D, D), c2), pl.BlockSpec((1, D), c2)]
    out_shapes = [jax.ShapeDtypeStruct((B, S, D), F32)]
    out_specs = [tok(D)]
    scratch = []
    vmem = 2 * D * D * 2 + 24 * tm * D * 4 + (8 << 20)
    if not moe:
        resident = lambda shape: pl.BlockSpec(shape, c2, pipeline_mode=pl.Buffered(1))
        args += [p["ffn_wg"], p["ffn_wu"], p["ffn_wd"]]
        specs += [resident((D, D_FF)), resident((D, D_FF)), resident((D_FF, D))]
        scratch = [pltpu.VMEM((tm, D), BF16)]
        vmem += 3 * D * D_FF * 2
    else:
        out_shapes.append(None)
        out_specs.append(None)
        per = S // tm
        out_shapes[1] = jax.ShapeDtypeStruct((B * S * SUBLANES, LANES), F32)
        out_specs[1] = pl.BlockSpec((tm * SUBLANES, LANES), lambda b, i: (b * per + i, 0))
        tri = jnp.asarray(np.tril(np.ones((sub, sub), np.float32), -1), BF16)
        args += [p["router_w"], tri]
        specs += [pl.BlockSpec((D, 2 * LANES), c2), pl.BlockSpec((sub, sub), c2)]
        out_shapes += [jax.ShapeDtypeStruct((B, S, LANES), F32),
                       jax.ShapeDtypeStruct((B, S // sub, SUBLANES, LANES), F32)]
        out_specs += [tok(LANES), pl.BlockSpec((None, tm // sub, SUBLANES, LANES), lambda b, i: (b, i, 0, 0))]
        out_shapes.append(jax.ShapeDtypeStruct((SUBLANES, B * S), F32))
        out_specs.append(pl.BlockSpec((SUBLANES, tm), lambda b, i: (0, b * per + i)))
    outs = pl.pallas_call(
        functools.partial(_mixout_kernel, moe=moe),
        grid=(B, S // tm),
        in_specs=specs,
        out_specs=out_specs,
        out_shape=out_shapes,
        scratch_shapes=scratch,
        compiler_params=_cparams(("arbitrary", "arbitrary"), vmem),
        name="mix_out_moe" if moe else "mix_ffn",
    )(*args)
    return outs if moe else outs[0]


def _swiglu_tile(hb, wg_ref, wu_ref, wd_ref):
    ff = wg_ref.shape[1]
    acc = jnp.zeros((hb.shape[0], D_MODEL), F32)
    for c0 in range(0, ff, FF_CHUNK):
        sl = slice(c0, min(c0 + FF_CHUNK, ff))
        g = jnp.dot(hb, wg_ref[:, sl].astype(BF16), preferred_element_type=F32)
        u = jnp.dot(hb, wu_ref[:, sl].astype(BF16), preferred_element_type=F32)
        a = (g * _sigmoid(g) * u).astype(BF16)
        acc = acc + jnp.dot(a, wd_ref[sl, :].astype(BF16), preferred_element_type=F32)
    return acc


def _token_copy(src_ref, src_row, dst_ref, dst_row, sem):
    return pltpu.make_async_copy(src_ref.at[pl.ds(pl.multiple_of(src_row, SUBLANES), SUBLANES), :],
                                 dst_ref.at[pl.ds(pl.multiple_of(dst_row, SUBLANES), SUBLANES), :], sem)


def _dispatch_kernel(slot_ref, pad_ref, h_ref, xs_ref, hbuf, zero_ref, in_sem, out_sem, *, tm, npad, steps):
    i = pl.program_id(0)

    tr = tm * SUBLANES

    def fetch(step, slot):
        return pltpu.make_async_copy(h_ref.at[pl.ds(step * tr, tr), :], hbuf.at[slot], in_sem.at[slot])

    def drain(slot):
        for _ in range(2):
            pltpu.make_async_copy(hbuf.at[slot], xs_ref.at[pl.ds(0, tr), :], out_sem.at[slot]).wait()
        pltpu.make_async_copy(hbuf.at[slot, pl.ds(0, npad * SUBLANES), :], xs_ref.at[pl.ds(0, npad * SUBLANES), :],
                              out_sem.at[slot]).wait()

    @pl.when(i == 0)
    def _():
        zero_ref[...] = jnp.zeros(zero_ref.shape, F32)
        fetch(0, 0).start()

    slot = lax.rem(i, 3)
    fetch(i, slot).wait()

    @pl.when(i + 1 < steps)
    def _():
        fetch(i + 1, lax.rem(i + 1, 3)).start()

    def scatter(r, carry):
        for k in range(2):
            _token_copy(hbuf.at[slot], r * SUBLANES, xs_ref, slot_ref[0, 0, k * tm + r],
                        out_sem.at[slot]).start(priority=k)
        return carry

    def scatter_pad(j, carry):
        _token_copy(zero_ref, 0, xs_ref, pad_ref[0, 0, j], out_sem.at[slot]).start()
        return carry

    lax.fori_loop(0, tm, scatter, 0, unroll=8)
    lax.fori_loop(0, npad, scatter_pad, 0, unroll=8)

    @pl.when(i > 0)
    def _():
        drain(lax.rem(i + 2, 3))

    @pl.when(i == steps - 1)
    def _():
        drain(slot)


def _dispatch(h_tiles, slots, pad_slots, n_tokens_out):
    n = h_tiles.shape[0] // SUBLANES
    tm = TM_ROUTE
    npad = pad_slots.shape[-1]
    steps = n // tm
    return pl.pallas_call(
        functools.partial(_dispatch_kernel, tm=tm, npad=npad, steps=steps),
        grid=(steps,),
        in_specs=[pl.BlockSpec((1, 1, 2 * tm), lambda i: (i, 0, 0), memory_space=pltpu.SMEM),
                  pl.BlockSpec((1, 1, npad), lambda i: (i, 0, 0), memory_space=pltpu.SMEM),
                  pl.BlockSpec(memory_space=pl.ANY)],
        out_specs=pl.BlockSpec(memory_space=pl.ANY),
        out_shape=jax.ShapeDtypeStruct((n_tokens_out * SUBLANES, LANES), F32),
        scratch_shapes=[pltpu.VMEM((3, tm * SUBLANES, LANES), F32), pltpu.VMEM((SUBLANES, LANES), F32),
                        pltpu.SemaphoreType.DMA((3,)), pltpu.SemaphoreType.DMA((3,))],
        compiler_params=_cparams(("arbitrary",), 3 * tm * D_MODEL * 4 + (4 << 20)),
        name="dispatch",
    )(slots, pad_slots, h_tiles)


def _experts_kernel(te_ref, nt_ref, xs_ref, wg_ref, wu_ref, wd_ref, o_ref, *, tm):
    t = pl.program_id(0)

    @pl.when(t < nt_ref[0])
    def _():
        hb = _load_token_tiles(xs_ref, 0, tm).astype(BF16)
        _store_token_tiles(o_ref, 0, _swiglu_tile(hb, wg_ref, wu_ref, wd_ref))

    @pl.when(t >= nt_ref[0])
    def _():
        o_ref[...] = jnp.zeros(o_ref.shape, F32)


def _experts(xs, tile_expert, n_tiles_used, p):
    D = D_MODEL
    tm = TM_MOE
    n_tiles = tile_expert.shape[0]
    wmap = lambda t, te, nt: (te[t], 0, 0)
    tok = pl.BlockSpec((tm * SUBLANES, LANES), lambda t, te, nt: (t, 0))
    tok_in = pl.BlockSpec((tm * SUBLANES, LANES), lambda t, te, nt: (jnp.minimum(t, nt[0] - 1), 0))
    vmem = 2 * 3 * D * D_FF * p["moe_wg"].dtype.itemsize + 12 * tm * D * 4 + (6 << 20)
    return pl.pallas_call(
        functools.partial(_experts_kernel, tm=tm),
        grid_spec=pltpu.PrefetchScalarGridSpec(
            num_scalar_prefetch=2,
            grid=(n_tiles,),
            in_specs=[tok_in,
                      pl.BlockSpec((None, D, D_FF), wmap),
                      pl.BlockSpec((None, D, D_FF), wmap),
                      pl.BlockSpec((None, D_FF, D), wmap)],
            out_specs=tok),
        out_shape=jax.ShapeDtypeStruct((n_tiles * tm * SUBLANES, LANES), F32),
        compiler_params=_cparams(("arbitrary",), vmem),
        name="experts",
    )(tile_expert, n_tiles_used, xs, p["moe_wg"], p["moe_wu"], p["moe_wd"])


def _combine_kernel(slot_ref, next_ref, x1_ref, route_ref, mod_ref, ys_ref, o_ref, buf_ref, sem, *, tm, steps):
    i = pl.program_id(0)

    def gather(idx_ref, s):
        def body(r, carry):
            for k in range(2):
                _token_copy(ys_ref, idx_ref[0, 0, k * tm + r], buf_ref.at[s, k], r * SUBLANES,
                            sem.at[s]).start(priority=k)
            return carry
        lax.fori_loop(0, tm, body, 0, unroll=8)

    @pl.when(i == 0)
    def _():
        gather(slot_ref, 0)

    @pl.when(i + 1 < steps)
    def _():
        gather(next_ref, lax.rem(i + 1, 2))

    cur = lax.rem(i, 2)
    for k in range(2):
        pltpu.make_async_copy(ys_ref.at[pl.ds(0, tm * SUBLANES), :], buf_ref.at[cur, k], sem.at[cur]).wait()
    route = route_ref[...]
    w1 = route[:, 2:3]
    w2 = route[:, 3:4]
    y0 = _load_token_tiles(buf_ref, 0, tm, lead=(cur, 0))
    y1 = _load_token_tiles(buf_ref, 0, tm, lead=(cur, 1))
    o_ref[...] = x1_ref[...] + mod_ref[5:6, :] * (w1 * y0 + w2 * y1)


def _combine(x1, route, mod, ys, slots):
    B, S, D = x1.shape
    tm = TM_ROUTE
    per = S // tm
    steps = B * per
    tok = lambda w: pl.BlockSpec((None, tm, w), lambda i: (i // per, i % per, 0))
    return pl.pallas_call(
        functools.partial(_combine_kernel, tm=tm, steps=steps),
        grid=(steps,),
        in_specs=[pl.BlockSpec((1, 1, 2 * tm), lambda i: (i, 0, 0), memory_space=pltpu.SMEM),
                  pl.BlockSpec((1, 1, 2 * tm), lambda i: (jnp.minimum(i + 1, steps - 1), 0, 0),
                               memory_space=pltpu.SMEM),
                  tok(D), tok(LANES), pl.BlockSpec((None, N_MOD, D), lambda i: (i // per, 0, 0)),
                  pl.BlockSpec(memory_space=pl.ANY)],
        out_specs=tok(D),
        out_shape=jax.ShapeDtypeStruct((B, S, D), F32),
        scratch_shapes=[pltpu.VMEM((2, 2, tm * SUBLANES, LANES), F32), pltpu.SemaphoreType.DMA((2,))],
        compiler_params=_cparams(("arbitrary",), 12 * tm * D * 4 + (4 << 20)),
        name="combine",
    )(slots, slots, x1, route, mod, ys)


def _moe(h, x1, route, cnt, route_t, mod, p):
    B, S, D = x1.shape
    n = B * S
    sub = n // (cnt.shape[0] * cnt.shape[1])
    tmm = TM_MOE
    n_tiles = 2 * n // tmm + N_EXPERTS
    n_slots = n_tiles * tmm
    steps = n // TM_ROUTE
    counts = cnt[:, :, 0:2, 0:N_EXPERTS].astype(jnp.int32).reshape(-1, 2, N_EXPERTS)
    tile_tot = counts.sum(axis=0)
    n_e = tile_tot.sum(axis=0)
    base = jnp.cumsum(counts, axis=0) - counts
    base = base + jnp.array([0, 1], jnp.int32)[None, :, None] * tile_tot[0][None, None, :]
    tiles_e = (n_e + tmm - 1) // tmm
    pstart = (jnp.cumsum(tiles_e) - tiles_e) * tmm
    expert_ids = jnp.arange(N_EXPERTS, dtype=jnp.int32)[:, None]
    slots_k = []
    for k in range(2):
        e_k = route_t[k].astype(jnp.int32)
        table = jnp.repeat((base[:, k, :] + pstart[None, :]).T, sub, axis=1)
        slot = jnp.sum(jnp.where(e_k[None, :] == expert_ids, table, 0), axis=0) + route_t[4 + k].astype(jnp.int32)
        slots_k.append((slot * SUBLANES).reshape(steps, TM_ROUTE))
    slots = jnp.concatenate(slots_k, axis=1).reshape(steps, 1, 2 * TM_ROUTE)
    tile_end = jnp.cumsum(tiles_e)
    n_used = tile_end[-1]
    t_idx = jnp.minimum(jnp.arange(n_tiles, dtype=jnp.int32), n_used - 1)
    tile_expert = jnp.sum(t_idx[:, None] >= tile_end[None, :], axis=-1).astype(jnp.int32)
    tile_expert = jnp.minimum(tile_expert, N_EXPERTS - 1)

    n_cand = N_EXPERTS * tmm
    pq = jnp.arange(tmm, dtype=jnp.int32)[None, :]
    n_pad_e = (tiles_e * tmm - n_e)[:, None]
    spare = n_slots + expert_ids * tmm + pq
    pad_slots = jnp.where(pq < n_pad_e, (pstart + n_e)[:, None] + pq, spare).astype(jnp.int32)
    assert n_cand % steps == 0
    pad_slots = (pad_slots * SUBLANES).reshape(steps, 1, n_cand // steps)

    xs = _dispatch(h, slots, pad_slots, n_slots + n_cand)
    ys = _experts(xs, tile_expert, n_used.reshape(1).astype(jnp.int32), p)
    return _combine(x1, route, mod, ys, slots)


def _pad_heads(w, width):
    lead = w.shape[:-1]
    w = w.reshape(lead + (N_HEADS, width))
    w = jnp.pad(w, [(0, 0)] * len(lead) + [(0, 0), (0, HEAD_PAD - width)])
    return w.reshape(lead + (N_HEADS * HEAD_PAD,))


def _layer_params(l, w_in, q_lat_g, kv_lat_g, w_uq, w_ukv, q_norm_g, k_norm_g, conv_w, conv_b, conv_ln_g,
                  conv_ln_b, w_out, ffn_norm_g):
    wi = w_in[l]
    kr = _with_partner(jnp.pad(wi[:, OFF_KR:OFF_CONV], ((0, 0), (QK_NOPE, HEAD_PAD - QK_HEAD))))
    ckv = wi[:, OFF_CKV:OFF_KR]
    w_full = jnp.concatenate([wi[:, OFF_CQ:OFF_CKV], ckv, wi[:, OFF_CONV:OFF_FOUR], wi[:, OFF_FOUR:IN_COLS], kr],
                             axis=1)
    w_kv = jnp.concatenate([ckv, kr], axis=1)
    ukv = w_ukv[l].reshape(KV_LORA, N_HEADS, QK_NOPE + V_HEAD)
    uk = _pad_heads(ukv[:, :, :QK_NOPE].reshape(KV_LORA, N_HEADS * QK_NOPE), QK_NOPE)
    uv = _pad_heads(ukv[:, :, QK_NOPE:].reshape(KV_LORA, ATTN_OUT), V_HEAD)
    pad_g = lambda g: jnp.pad(g, (0, HEAD_PAD - QK_HEAD)).reshape(1, HEAD_PAD)
    uq = _with_partner(_pad_heads(w_uq[l], QK_HEAD))
    hw = N_HEADS * HEAD_PAD
    col = np.arange(hw)[:, None]
    head_of_col = col // HEAD_PAD == np.arange(LANES)[None, :]
    return {
        "w_in_full": w_full.astype(BF16),
        "w_in_kv": w_kv.astype(BF16),
        "q_lat_g": q_lat_g[l].reshape(1, Q_LORA),
        "kv_lat_g": kv_lat_g[l].reshape(1, KV_LORA),
        "w_uq": uq.astype(BF16),
        "w_ukv": jnp.concatenate([uk, uv], axis=1).astype(BF16),
        "q_norm_g": pad_g(q_norm_g[l]) * (QK_HEAD ** -0.5 * math.log2(math.e)),
        "v_ones": jnp.tile(jnp.concatenate([jnp.zeros((V_HEAD,), F32), jnp.ones((HEAD_PAD - V_HEAD,), F32)]),
                           N_HEADS).reshape(1, hw),
        "ones_h": jnp.asarray(head_of_col & (col % HEAD_PAD < QK_HEAD), F32).astype(BF16),
        "k_norm_g": pad_g(k_norm_g[l]),
        "conv_w": conv_w[l],
        "conv_b": conv_b[l].reshape(1, CONV_CH),
        "conv_ln_g": conv_ln_g[l].reshape(1, CONV_CH),
        "conv_ln_b": conv_ln_b[l].reshape(1, CONV_CH),
        "ffn_norm_g": ffn_norm_g[l].reshape(1, D_MODEL),
        "bd": _channel_dft(),
    }


def _with_partner(w):
    lead = w.shape[:-1]
    half = QK_ROPE // 2
    w3 = w.reshape(lead + (-1, HEAD_PAD))
    out = jnp.concatenate([w3[..., :QK_HEAD], w3[..., QK_NOPE + half:QK_HEAD], w3[..., QK_NOPE:QK_NOPE + half]],
                          axis=-1)
    return out.reshape(w.shape)


def _rot_partner(w):
    lead = w.shape[:-1]
    half = QK_ROPE // 2
    w3 = w.reshape(lead + (-1, HEAD_PAD))
    z = jnp.zeros_like(w3)
    out = jnp.concatenate([z[..., :QK_NOPE], w3[..., QK_NOPE + half:QK_HEAD], w3[..., QK_NOPE:QK_NOPE + half],
                           z[..., QK_HEAD:]], axis=-1)
    return out.reshape(w.shape)


def _rope_tables(seq, gq, gk):
    rows = seq // GRID_W
    row = np.repeat(np.arange(rows, dtype=np.float64), GRID_W)
    col = np.tile(np.arange(GRID_W, dtype=np.float64), rows)
    n_freq = QK_ROPE // 4
    inv = ROPE_BASE ** (-np.arange(n_freq, dtype=np.float64) / n_freq)
    ang = np.concatenate([row[:, None] * inv, col[:, None] * inv], axis=-1)
    cos, sin = np.cos(ang), np.sin(ang)
    ones = np.ones((seq, QK_NOPE))
    tail = np.zeros((seq, HEAD_PAD - QK_HEAD))
    cos_t = jnp.asarray(np.concatenate([ones, cos, cos, tail], axis=1), F32)
    sin_t = jnp.asarray(np.concatenate([np.zeros((seq, QK_NOPE)), -sin, sin, tail], axis=1), F32)
    return (cos_t * gq, sin_t * _rot_partner(gq), cos_t * gk, sin_t * _rot_partner(gk))


def kernel(x, c, ctx, c_ctx, ada_w, ada_b, mix_norm_g, ffn_norm_g, w_in, q_lat_g, kv_lat_g, w_uq, w_ukv, q_norm_g,
           k_norm_g, conv_w, conv_b, conv_ln_g, conv_ln_b, w_out, ffn_w_gate, ffn_w_up, ffn_w_down, router_w,
           moe_w_gate, moe_w_up, moe_w_down):
    B, S, D = x.shape
    T = ctx.shape[1]
    assert (D, DEPTH) == (D_MODEL, ada_w.shape[0]) and S % GRID_W == 0

    cc = jnp.concatenate([c, c_ctx[None, :], jnp.zeros((2 * SUBLANES - B - 1, D), F32)], axis=0)
    mods = _modulation(cc, ada_w, ada_b).reshape(DEPTH, 2 * SUBLANES, N_MOD, D)
    cs_x = _dft_tables(S)
    cs_c = _dft_tables(T)

    def moe_f32(i):
        return (moe_w_gate[i].reshape(N_EXPERTS * D, D_FF), moe_w_up[i].reshape(N_EXPERTS * D, D_FF),
                moe_w_down[i].reshape(N_EXPERTS * D_FF, D))

    moe_bf16 = {}
    for l in range(DEPTH):
        last = l == DEPTH - 1
        p = _layer_params(l, w_in, q_lat_g, kv_lat_g, w_uq, w_ukv, q_norm_g, k_norm_g, conv_w, conv_b, conv_ln_g,
                          conv_ln_b, w_out, ffn_norm_g)
        p["mix_norm_g"] = mix_norm_g[l].reshape(1, D)
        rope_tabs = _rope_tables(S, p["q_norm_g"], p["k_norm_g"])
        i = l // 2
        moe = l % 2 == 1
        mod_x = mods[l, :B]
        mod_c = mods[l, B:B + 1]

        nxt = l + 1 if l % 2 == 0 else l + 2
        cast = []
        if not moe:
            cast += [ffn_w_gate[i], ffn_w_up[i], ffn_w_down[i]]
        if nxt < DEPTH:
            cast += list(moe_f32(nxt // 2))
        p["w_out"] = w_out[l].astype(BF16)

        def channel_mixer(attn, conv, four, xs, mod):
            if moe:
                x1, h, route, cnt, route_t = _mix_out(attn, conv, four, xs, mod, p, moe=True)
                mod_b = mod if mod.shape[0] > 1 else jnp.broadcast_to(mod, (B,) + mod.shape[1:])
                return _moe(h, x1, route, cnt, route_t, mod_b, p)
            return _mix_out(attn, conv, four, xs, mod, p, moe=False)

        if last:
            k_c, v_c = _front(ctx, mod_c, p["mix_norm_g"], p, None, full=False)
        else:
            q_c, k_c, v_c, y_c, zz_c = _front(ctx, mod_c, p["mix_norm_g"], p, None, full=True)
        q_x, k_x, v_x, y_x, zz_x = _front(x, mod_x, p["mix_norm_g"], p, rope_tabs, full=True)
        attn_x, *casted = _attend(q_x, [k_x, k_c], [v_x, v_c], cast)
        if moe:
            rw = jnp.pad(router_w[i], ((0, 0), (0, LANES - N_EXPERTS)))
            rw_hi = rw.astype(BF16)
            p["router_w"] = jnp.concatenate([rw_hi, (rw - rw_hi.astype(F32)).astype(BF16)], axis=1)
            wg, wu, wd = moe_bf16.pop(i) if i in moe_bf16 else [w.astype(BF16) for w in moe_f32(i)]
            p["moe_wg"] = wg.reshape(N_EXPERTS, D, D_FF)
            p["moe_wu"] = wu.reshape(N_EXPERTS, D, D_FF)
            p["moe_wd"] = wd.reshape(N_EXPERTS, D_FF, D)
        else:
            p["ffn_wg"], p["ffn_wu"], p["ffn_wd"] = casted[:3]
            casted = casted[3:]
        if nxt < DEPTH:
            moe_bf16[nxt // 2] = casted

        if not last:
            attn_c, = _attend(q_c, [k_c], [v_c])
            ctx_next = channel_mixer(attn_c, *_conv_fourier(y_c, zz_c, cs_c, p), ctx, mod_c)

        x = channel_mixer(attn_x, *_conv_fourier(y_x, zz_x, cs_x, p), x, mod_x)
        if not last:
            ctx = ctx_next
    return x
```

```python
import functools
import math

import numpy as np
import jax
import jax.numpy as jnp
from jax import lax
from jax.experimental import pallas as pl
from jax.experimental.pallas import tpu as pltpu

F32 = jnp.float32
BF16 = jnp.bfloat16

D_MODEL = 1024
DEPTH = 2
GRID_W = 64
N_HEADS = 8
QK_NOPE = 64
QK_ROPE = 32
QK_HEAD = QK_NOPE + QK_ROPE
V_HEAD = 64
Q_LORA = 384
KV_LORA = 256
ROPE_BASE = 10000.0
CONV_CH = 256
CONV_WIDTH = 31
CONV_PAD = (CONV_WIDTH - 1) // 2
FOURIER_GROUPS = 4
FOURIER_GROUP_DIM = 64
FOURIER_CH = FOURIER_GROUPS * FOURIER_GROUP_DIM
ATTN_OUT = N_HEADS * V_HEAD
OFF_CQ = 0
OFF_CKV = OFF_CQ + Q_LORA
OFF_KR = OFF_CKV + KV_LORA
OFF_CONV = OFF_KR + QK_ROPE
OFF_FOUR = OFF_CONV + 2 * CONV_CH
IN_COLS = OFF_FOUR + FOURIER_CH
D_FF = 2816
N_EXPERTS = 8
N_MOD = 6
EPS = 1e-6

LANES = 128
SUBLANES = 8
HEAD_PAD = LANES
VMEM_CAP = 56 * 1024 * 1024
FF_CHUNK = 256

SUB_ROWS = 256
SUB_ROWS_MOE = 512
TM_FRONT = 1024
TQ_ATTN = 1024
TQ_SUB = 512
TK_ATTN = 256
TM_OUT = 512
TM_MOE = 512
TM_ROUTE = 512
CONV_ROWS = 128
TR_FOURIER = 512


def _cparams(sem, vmem_bytes):
    return pltpu.CompilerParams(dimension_semantics=sem, vmem_limit_bytes=int(min(VMEM_CAP, vmem_bytes)))


def _rms(v, g):
    return v * lax.rsqrt(jnp.mean(v * v, axis=-1, keepdims=True) + EPS) * g


def _sigmoid(v):
    return 1.0 / (1.0 + jnp.exp(-v))


def _mod_kernel(c_ref, w_ref, b_ref, o_ref):
    c = c_ref[...]
    s = (c * _sigmoid(c)).astype(BF16)
    o_ref[...] = jnp.dot(s, w_ref[...].astype(BF16), preferred_element_type=F32) + b_ref[...]


def _modulation(cc, ada_w, ada_b):
    rows = cc.shape[0]
    tn = 1536
    n_out = N_MOD * D_MODEL
    return pl.pallas_call(
        _mod_kernel,
        grid=(DEPTH, n_out // tn),
        in_specs=[
            pl.BlockSpec((rows, D_MODEL), lambda l, j: (0, 0)),
            pl.BlockSpec((None, D_MODEL, tn), lambda l, j: (l, 0, j)),
            pl.BlockSpec((None, 1, tn), lambda l, j: (l, 0, j)),
        ],
        out_specs=pl.BlockSpec((None, rows, tn), lambda l, j: (l, 0, j)),
        out_shape=jax.ShapeDtypeStruct((DEPTH, rows, n_out), F32),
        compiler_params=_cparams(("arbitrary", "arbitrary"), 4 * D_MODEL * tn * 4),
        name="modulation",
    )(cc, ada_w, ada_b.reshape(DEPTH, 1, n_out))


def _front_kernel(*refs, full, rope):
    it = iter(refs)
    x_ref, mod_ref, g_ref, win_ref = next(it), next(it), next(it), next(it)
    if full:
        qlg_ref, wuq_ref = next(it), next(it)
    kvlg_ref, wukv_ref, vones_ref, onesh_ref = next(it), next(it), next(it), next(it)
    if rope:
        cq_ref, sq_ref, ck_ref, sk_ref = next(it), next(it), next(it), next(it)
    else:
        qg_ref = next(it) if full else None
        kg_ref = next(it)
    if full:
        bd_ref = next(it)
        q_out = next(it)
    k_out, v_out = next(it), next(it)
    if full:
        y_out, zz_out = next(it), next(it)

    shift = mod_ref[0:1, :]
    gain = g_ref[...] * (1.0 + mod_ref[1:2, :])
    hw = N_HEADS * HEAD_PAD

    def head_inv_rms(raw):
        ss = jnp.dot((raw * raw).astype(BF16), onesh_ref[...], preferred_element_type=F32)
        rs = lax.rsqrt(ss * (1.0 / QK_HEAD) + EPS)
        return jnp.concatenate([jnp.broadcast_to(rs[:, hd:hd + 1], (rs.shape[0], HEAD_PAD)) for hd in range(N_HEADS)],
                               axis=1)

    def sub_block(rows):
        x = x_ref[rows, :]
        h = _rms(x, gain) + shift
        cols = jnp.dot(h.astype(BF16), win_ref[...], preferred_element_type=F32)

        o = 0
        if full:
            cq = cols[:, 0:Q_LORA]
            o = Q_LORA
            qall = jnp.dot(_rms(cq, qlg_ref[...]).astype(BF16), wuq_ref[...], preferred_element_type=F32)
            rsb = head_inv_rms(qall)
            for hd in range(N_HEADS):
                sl = slice(hd * HEAD_PAD, (hd + 1) * HEAD_PAD)
                if rope:
                    val = qall[:, sl] * cq_ref[rows, :] + pltpu.roll(qall[:, sl], HEAD_PAD - QK_ROPE, 1) * \
                        sq_ref[rows, :]
                else:
                    val = qall[:, sl] * qg_ref[...]
                q_out[rows, sl] = (val * rsb[:, sl]).astype(BF16)

        ckv = cols[:, o:o + KV_LORA]
        o += KV_LORA
        kv = jnp.dot(_rms(ckv, kvlg_ref[...]).astype(BF16), wukv_ref[...], preferred_element_type=F32)
        if full:
            a = cols[:, o:o + CONV_CH]
            gt = cols[:, o + CONV_CH:o + 2 * CONV_CH]
            y_out[rows, :] = a * _sigmoid(gt)
            o += 2 * CONV_CH
            z = cols[:, o:o + FOURIER_CH]
            o += FOURIER_CH
            zz_out[rows, :] = jnp.dot(z.astype(BF16), bd_ref[...], preferred_element_type=F32).astype(BF16)
        krb = cols[:, o:o + HEAD_PAD]
        kraw = [kv[:, hd * HEAD_PAD:(hd + 1) * HEAD_PAD] + krb for hd in range(N_HEADS)]
        rsb = head_inv_rms(jnp.concatenate(kraw, axis=1))
        if rope:
            kpart = pltpu.roll(krb, HEAD_PAD - QK_ROPE, 1) * sk_ref[rows, :]
        for hd in range(N_HEADS):
            sl = slice(hd * HEAD_PAD, (hd + 1) * HEAD_PAD)
            val = kraw[hd] * ck_ref[rows, :] + kpart if rope else kraw[hd] * kg_ref[...]
            k_out[rows, sl] = (val * rsb[:, sl]).astype(BF16)
        v_out[rows, :] = (kv[:, hw:] + vones_ref[...]).astype(BF16)

    tm = x_ref.shape[0]
    sub = min(SUB_ROWS, tm)
    for sb in range(tm // sub):
        sub_block(slice(sb * sub, (sb + 1) * sub))


def _front(xs, mod, norm_g, p, rope_tabs, *, full):
    if mod.shape[0] == 1 and xs.shape[0] > 1 and rope_tabs is None:
        outs = _front(xs.reshape(1, -1, xs.shape[-1]), mod, norm_g, p, None, full=full)
        return [o.reshape(xs.shape[:2] + o.shape[2:]) for o in outs]
    B, S, D = xs.shape
    tm = min(TM_FRONT, S)
    rope = rope_tabs is not None
    assert full or not rope
    w_in = p["w_in_full"] if full else p["w_in_kv"]
    w_uq = p["w_uq"]
    ncol = w_in.shape[1]
    bm = mod.shape[0]
    mod_map = (lambda b, i: (b, 0, 0)) if bm > 1 else (lambda b, i: (0, 0, 0))
    const2 = lambda b, i: (0, 0)
    hw = N_HEADS * HEAD_PAD

    args = [xs, mod, norm_g, w_in]
    specs = [
        pl.BlockSpec((None, tm, D), lambda b, i: (b, i, 0)),
        pl.BlockSpec((None, N_MOD, D), mod_map),
        pl.BlockSpec((1, D), const2),
        pl.BlockSpec((D, ncol), const2),
    ]
    if full:
        args += [p["q_lat_g"], w_uq]
        specs += [pl.BlockSpec((1, Q_LORA), const2), pl.BlockSpec((Q_LORA, w_uq.shape[1]), const2)]
    args += [p["kv_lat_g"], p["w_ukv"], p["v_ones"], p["ones_h"]]
    specs += [pl.BlockSpec((1, KV_LORA), const2),
              pl.BlockSpec((KV_LORA, 2 * hw), const2),
              pl.BlockSpec((1, hw), const2),
              pl.BlockSpec((hw, LANES), const2)]
    if rope:
        args += list(rope_tabs)
        specs += [pl.BlockSpec((tm, HEAD_PAD), lambda b, i: (i, 0))] * 4
    else:
        if full:
            args += [p["q_norm_g"]]
            specs += [pl.BlockSpec((1, HEAD_PAD), const2)]
        args += [p["k_norm_g"]]
        specs += [pl.BlockSpec((1, HEAD_PAD), const2)]
    if full:
        args += [p["bd"]]
        specs += [pl.BlockSpec((FOURIER_CH, 2 * FOURIER_CH), const2)]

    out_shapes, out_specs = [], []

    def add_out(width, dtype):
        out_shapes.append(jax.ShapeDtypeStruct((B, S, width), dtype))
        out_specs.append(pl.BlockSpec((None, tm, width), lambda b, i: (b, i, 0)))

    if full:
        add_out(N_HEADS * HEAD_PAD, BF16)
    add_out(N_HEADS * HEAD_PAD, BF16)
    add_out(N_HEADS * HEAD_PAD, BF16)
    if full:
        add_out(CONV_CH, F32)
        add_out(2 * FOURIER_CH, BF16)

    vmem = 2 * (D * ncol * 2 + Q_LORA * 1024 * 2 + KV_LORA * 2048 * 2) + 28 * tm * D * 4
    return pl.pallas_call(
        functools.partial(_front_kernel, full=full, rope=rope),
        grid=(B, S // tm),
        in_specs=specs,
        out_specs=out_specs,
        out_shape=out_shapes,
        compiler_params=_cparams(("arbitrary", "arbitrary"), vmem),
        name="front_full" if full else "front_kv",
    )(*args)


def _attn_kernel(*refs, nsrc, ncast):
    q_ref = refs[0]
    k_refs = refs[1:1 + nsrc]
    v_refs = refs[1 + nsrc:1 + 2 * nsrc]
    cast_in = refs[1 + 2 * nsrc:1 + 2 * nsrc + ncast]
    o_ref = refs[1 + 2 * nsrc + ncast]
    cast_out = refs[2 + 2 * nsrc + ncast:]
    chunks = []
    for k_ref, v_ref in zip(k_refs, v_refs):
        for s0 in range(0, k_ref.shape[0], TK_ATTN):
            chunks.append((k_ref, v_ref, s0, min(TK_ATTN, k_ref.shape[0] - s0)))
    tq = q_ref.shape[0]
    sub = min(TQ_SUB, tq)
    for pair, r0 in [(pr, rr) for pr in range(q_ref.shape[1] // (2 * HEAD_PAD)) for rr in range(0, tq, sub)]:
        rows = slice(r0, r0 + sub)
        state = [None, None]
        for k_ref, v_ref, s0, sz in chunks:
            for hh in range(2):
                sl = slice((2 * pair + hh) * HEAD_PAD, (2 * pair + hh + 1) * HEAD_PAD)
                s = lax.dot_general(q_ref[rows, sl], k_ref[s0:s0 + sz, sl], (((1,), (1,)), ((), ())),
                                    preferred_element_type=F32)
                m = jnp.max(s, axis=-1, keepdims=True)
                if state[hh] is not None:
                    m_old, acc_old = state[hh]
                    m = jnp.maximum(m_old, m)
                pv = jnp.dot(jnp.exp2((s - m).astype(BF16)), v_ref[s0:s0 + sz, sl], preferred_element_type=F32)
                if state[hh] is not None:
                    pv = pv + jnp.exp2(m_old - m) * acc_old
                state[hh] = (m, pv)
        accs = [state[0][1], state[1][1]]
        lane = lax.broadcasted_iota(jnp.int32, accs[0].shape, 1)
        lo = accs[0] / pltpu.roll(accs[0], V_HEAD, 1)
        hi = pltpu.roll(accs[1], V_HEAD, 1) / accs[1]
        o_ref[rows, pair * 2 * V_HEAD:(pair + 1) * 2 * V_HEAD] = jnp.where(lane < V_HEAD, lo, hi).astype(BF16)
    for src, dst in zip(cast_in, cast_out):
        dst[...] = src[...].astype(BF16)


def _attend(q, ks, vs, cast=()):
    B, S, _ = q.shape
    tq = min(TQ_ATTN, S)
    nsrc = len(ks)
    nq = S // tq
    pp = N_HEADS // 2 if nq == 1 and S <= TQ_SUB else 1
    ngrp = N_HEADS // 2 // pp
    steps = B * ngrp * nq
    specs = [pl.BlockSpec((None, tq, pp * 2 * HEAD_PAD), lambda b, hp, i: (b, i, hp))]
    for kv in list(ks) + list(vs):
        specs.append(pl.BlockSpec((None, kv.shape[1], pp * 2 * HEAD_PAD), lambda b, hp, i: (b, 0, hp)))
    out_specs = [pl.BlockSpec((None, tq, pp * 2 * V_HEAD), lambda b, hp, i: (b, i, hp))]
    out_shapes = [jax.ShapeDtypeStruct((B, S, ATTN_OUT), BF16)]
    cast_bytes = 0
    for w in cast:
        rows, cols = w.shape
        nblk = max(d for d in range(1, steps + 1)
                   if steps % d == 0 and rows % d == 0 and (rows // d) % (2 * SUBLANES) == 0)
        hold = steps // nblk
        spec = pl.BlockSpec((rows // nblk, cols),
                            lambda b, hp, i, hold=hold: (((b * ngrp + hp) * nq + i) // hold, 0))
        specs.append(spec)
        out_specs.append(spec)
        out_shapes.append(jax.ShapeDtypeStruct(w.shape, BF16))
        cast_bytes += 2 * (rows // nblk) * cols * 6
    t_all = sum(k.shape[1] for k in ks)
    vmem = 2 * (t_all * pp * 512 * 2) * 2 + 16 * min(tq, TQ_SUB) * TK_ATTN * 4 + cast_bytes + (12 << 20)
    outs = pl.pallas_call(
        functools.partial(_attn_kernel, nsrc=nsrc, ncast=len(cast)),
        grid=(B, ngrp, nq),
        in_specs=specs,
        out_specs=out_specs,
        out_shape=out_shapes,
        compiler_params=_cparams(("arbitrary", "arbitrary", "arbitrary"), vmem),
        name="attend%d" % nsrc,
    )(q, *ks, *vs, *cast)
    return list(outs)


def _conv_fourier_kernel(y_ref, w_ref, b_ref, lg_ref, lb_ref, cs_ref, zz_ref, conv_out, four_out, pad_ref, *, seq):
    halo = 2 * SUBLANES
    total = seq + 2 * halo
    pad_ref[0, 0:halo, :] = jnp.zeros((halo, CONV_CH), F32)
    pad_ref[0, halo + seq:total, :] = jnp.zeros((halo, CONV_CH), F32)
    pad_ref[0, halo:halo + seq, :] = y_ref[...]
    slab = pad_ref[0]
    for ph in range(1, SUBLANES):
        pad_ref[ph] = pltpu.roll(slab, total - ph, 0)

    def conv_rows(base, rows):
        acc = jnp.zeros((rows, CONV_CH), F32)
        for j in range(CONV_WIDTH):
            off = base + halo - CONV_PAD + j
            ph = off % SUBLANES
            acc = acc + pad_ref[ph, off - ph:off - ph + rows, :] * w_ref[j:j + 1, :]
        acc = acc + b_ref[...]
        mu = jnp.mean(acc, axis=-1, keepdims=True)
        cen = acc - mu
        var = jnp.mean(cen * cen, axis=-1, keepdims=True)
        yn = cen * lax.rsqrt(var + EPS) * lg_ref[...] + lb_ref[...]
        conv_out[base:base + rows, :] = (yn * _sigmoid(yn)).astype(BF16)

    def dft_rows(base, rows):
        r = slice(base, base + rows)
        acc = jnp.dot(cs_ref[r, 0:seq], zz_ref[:, 0:FOURIER_CH], preferred_element_type=F32)
        acc = acc + jnp.dot(cs_ref[r, seq:2 * seq], zz_ref[:, FOURIER_CH:2 * FOURIER_CH],
                            preferred_element_type=F32)
        four_out[r, :] = acc.astype(BF16)

    tr = min(TR_FOURIER, seq)
    for base in range(0, seq, tr):
        for cb in range(base, base + tr, CONV_ROWS):
            conv_rows(cb, CONV_ROWS)
        dft_rows(base, tr)


def _conv_fourier(y, zz, cs, p):
    B, S, _ = y.shape
    c2 = lambda b: (0, 0)
    seq_block = lambda w: pl.BlockSpec((None, S, w), lambda b: (b, 0, 0))
    vmem = (SUBLANES + 6) * (S + 32) * CONV_CH * 4 + S * 2 * S * 2 + 2 * S * 512 * 2 + (10 << 20)
    return pl.pallas_call(
        functools.partial(_conv_fourier_kernel, seq=S),
        grid=(B,),
        in_specs=[seq_block(CONV_CH),
                  pl.BlockSpec((CONV_WIDTH, CONV_CH), c2),
                  pl.BlockSpec((1, CONV_CH), c2), pl.BlockSpec((1, CONV_CH), c2), pl.BlockSpec((1, CONV_CH), c2),
                  pl.BlockSpec((S, 2 * S), c2, pipeline_mode=pl.Buffered(1)),
                  seq_block(2 * FOURIER_CH)],
        out_specs=[seq_block(CONV_CH), seq_block(FOURIER_CH)],
        out_shape=[jax.ShapeDtypeStruct((B, S, CONV_CH), BF16), jax.ShapeDtypeStruct((B, S, FOURIER_CH), BF16)],
        scratch_shapes=[pltpu.VMEM((SUBLANES, S + 4 * SUBLANES, CONV_CH), F32)],
        compiler_params=_cparams(("arbitrary",), vmem),
        name="conv_fourier",
    )(y, p["conv_w"], p["conv_b"], p["conv_ln_g"], p["conv_ln_b"], cs, zz)


def _dft_tables(seq):
    norm = seq ** -0.5
    ks = (np.arange(seq)[:, None] * np.arange(seq)[None, :]) % seq
    ang = 2.0 * np.pi * ks / seq
    tab = np.concatenate([np.cos(ang), -np.sin(ang)], axis=1) * norm
    return jnp.asarray(tab, F32).astype(BF16)


def _channel_dft():
    n = FOURIER_GROUP_DIM
    ang = 2.0 * np.pi * ((np.arange(n)[:, None] * np.arange(n)[None, :]) % n) / n
    eye = np.eye(FOURIER_GROUPS)
    bd = np.concatenate([np.kron(eye, np.cos(ang)), np.kron(eye, np.sin(ang))], axis=1) * n ** -0.5
    return jnp.asarray(bd, F32).astype(BF16)


def _mixout_kernel(*refs, moe):
    if moe:
        (attn_ref, conv_ref, four_ref, x_ref, mod_ref, wout_ref, g_ref, rw_ref, tri_ref,
         x1_out, h_out, route_out, cnt_out, routet_out) = refs
    else:
        (attn_ref, conv_ref, four_ref, x_ref, mod_ref, wout_ref, g_ref, wg_ref, wu_ref, wd_ref,
         x1_out, h_out) = refs
    tm = x_ref.shape[0]
    sub = min(SUB_ROWS_MOE if moe else SUB_ROWS, tm)
    gain = g_ref[...] * (1.0 + mod_ref[4:5, :])
    for sb in range(tm // sub):
        rows = slice(sb * sub, (sb + 1) * sub)
        mix = jnp.dot(attn_ref[rows, :], wout_ref[0:ATTN_OUT, :], preferred_element_type=F32)
        mix = mix + jnp.dot(conv_ref[rows, :], wout_ref[ATTN_OUT:ATTN_OUT + CONV_CH, :],
                            preferred_element_type=F32)
        mix = mix + jnp.dot(four_ref[rows, :], wout_ref[ATTN_OUT + CONV_CH:, :], preferred_element_type=F32)
        x1 = x_ref[rows, :] + mod_ref[2:3, :] * mix
        x1_out[rows, :] = x1
        h = _rms(x1, gain) + mod_ref[3:4, :]
        if moe:
            _store_token_tiles(h_out, sb * sub, h)
            route, cnt = _route(h, rw_ref, tri_ref)
            route_out[rows, :] = route
            cnt_out[sb] = cnt
            routet_out[:, rows] = route.T[0:SUBLANES, :]
        else:
            h_out[rows, :] = h.astype(BF16)
    if not moe:
        x1_out[...] = x1_out[...] + mod_ref[5:6, :] * _swiglu_tile(h_out[...], wg_ref, wu_ref, wd_ref)


def _store_token_tiles(ref, row0, val):
    rows = val.shape[0]
    for j in range(SUBLANES):
        ref[pl.ds(row0 * SUBLANES + j, rows, stride=SUBLANES), :] = val[:, j * LANES:(j + 1) * LANES]


def _load_token_tiles(ref, row0, rows, lead=()):
    idx = tuple(lead)
    return jnp.concatenate([ref[idx + (pl.ds(row0 * SUBLANES + j, rows, stride=SUBLANES), slice(None))]
                            for j in range(SUBLANES)], axis=1)


def _route(h, rw_ref, tri_ref):
    tm = h.shape[0]
    h_hi = h.astype(BF16)
    h_lo = (h - h_hi.astype(F32)).astype(BF16)
    hw = jnp.dot(h_hi, rw_ref[...], preferred_element_type=F32)
    logits = hw[:, 0:LANES] + hw[:, LANES:] + jnp.dot(h_lo, rw_ref[:, 0:LANES], preferred_element_type=F32)
    lane = lax.broadcasted_iota(jnp.int32, (tm, LANES), 1).astype(F32)
    neg = jnp.float32(-jnp.inf)
    lm = jnp.where(lane < N_EXPERTS, logits, neg)
    m1 = jnp.max(lm, axis=-1, keepdims=True)
    i1 = jnp.min(jnp.where(lm == m1, lane, float(LANES)), axis=-1, keepdims=True)
    lm2 = jnp.where(lane == i1, neg, lm)
    m2 = jnp.max(lm2, axis=-1, keepdims=True)
    i2 = jnp.min(jnp.where(lm2 == m2, lane, float(LANES)), axis=-1, keepdims=True)
    t = jnp.exp(m2 - m1)
    w1 = 1.0 / (1.0 + t)
    w2 = t / (1.0 + t)
    oh1 = lane == i1
    oh2 = lane == i2
    tri = tri_ref[...]
    c1 = jnp.dot(tri, jnp.where(oh1, 1.0, 0.0).astype(BF16), preferred_element_type=F32)
    c2 = jnp.dot(tri, jnp.where(oh2, 1.0, 0.0).astype(BF16), preferred_element_type=F32)
    r1 = jnp.sum(jnp.where(oh1, c1, 0.0), axis=-1, keepdims=True)
    r2 = jnp.sum(jnp.where(oh2, c2, 0.0), axis=-1, keepdims=True)
    vals = [i1, i2, w1, w2, r1, r2]
    route = jnp.zeros((tm, LANES), F32)
    for idx, v in enumerate(vals):
        route = jnp.where(lane == idx, v, route)
    n1 = jnp.sum(jnp.where(oh1, 1.0, 0.0), axis=0, keepdims=True)
    n2 = jnp.sum(jnp.where(oh2, 1.0, 0.0), axis=0, keepdims=True)
    row = lax.broadcasted_iota(jnp.int32, (SUBLANES, LANES), 0)
    return route, jnp.where(row == 0, n1, jnp.where(row == 1, n2, 0.0))


def _mix_out(attn, conv, four, xs, mod, p, *, moe):
    if mod.shape[0] == 1 and xs.shape[0] > 1 and not moe:
        flat = lambda a: a.reshape(1, -1, a.shape[-1])
        return _mix_out(flat(attn), flat(conv), flat(four), flat(xs), mod, p, moe=moe).reshape(xs.shape)
    B, S, D = xs.shape
    tm = min(TM_OUT, S)
    sub = min(SUB_ROWS_MOE if moe else SUB_ROWS, tm)
    bm = mod.shape[0]
    mod_map = (lambda b, i: (b, 0, 0)) if bm > 1 else (lambda b, i: (0, 0, 0))
    c2 = lambda b, i: (0, 0)
    tok = lambda w: pl.BlockSpec((None, tm, w), lambda b, i: (b, i, 0))
    args = [attn, conv, four, xs, mod, p["w_out"], p["ffn_norm_g"]]
    specs = [tok(ATTN_OUT), tok(CONV_CH), tok(FOURIER_CH), tok(D), pl.BlockSpec((None, N_MOD, D), mod_map),
             pl.BlockSpec((D, D), c2), pl.BlockSpec((1, D), c2)]
    out_shapes = [jax.ShapeDtypeStruct((B, S, D), F32)]
    out_specs = [tok(D)]
    scratch = []
    vmem = 2 * D * D * 2 + 24 * tm * D * 4 + (8 << 20)
    if not moe:
        resident = lambda shape: pl.BlockSpec(shape, c2, pipeline_mode=pl.Buffered(1))
        args += [p["ffn_wg"], p["ffn_wu"], p["ffn_wd"]]
        specs += [resident((D, D_FF)), resident((D, D_FF)), resident((D_FF, D))]
        scratch = [pltpu.VMEM((tm, D), BF16)]
        vmem += 3 * D * D_FF * 2
    else:
        out_shapes.append(None)
        out_specs.append(None)
        per = S // tm
        out_shapes[1] = jax.ShapeDtypeStruct((B * S * SUBLANES, LANES), F32)
        out_specs[1] = pl.BlockSpec((tm * SUBLANES, LANES), lambda b, i: (b * per + i, 0))
        tri = jnp.asarray(np.tril(np.ones((sub, sub), np.float32), -1), BF16)
        args += [p["router_w"], tri]
        specs += [pl.BlockSpec((D, 2 * LANES), c2), pl.BlockSpec((sub, sub), c2)]
        out_shapes += [jax.ShapeDtypeStruct((B, S, LANES), F32),
                       jax.ShapeDtypeStruct((B, S // sub, SUBLANES, LANES), F32)]
        out_specs += [tok(LANES), pl.BlockSpec((None, tm // sub, SUBLANES, LANES), lambda b, i: (b, i, 0, 0))]
        out_shapes.append(jax.ShapeDtypeStruct((SUBLANES, B * S), F32))
        out_specs.append(pl.BlockSpec((SUBLANES, tm), lambda b, i: (0, b * per + i)))
    outs = pl.pallas_call(
        functools.partial(_mixout_kernel, moe=moe),
        grid=(B, S // tm),
        in_specs=specs,
        out_specs=out_specs,
        out_shape=out_shapes,
        scratch_shapes=scratch,
        compiler_params=_cparams(("arbitrary", "arbitrary"), vmem),
        name="mix_out_moe" if moe else "mix_ffn",
    )(*args)
    return outs if moe else outs[0]


def _swiglu_tile(hb, wg_ref, wu_ref, wd_ref):
    ff = wg_ref.shape[1]
    acc = jnp.zeros((hb.shape[0], D_MODEL), F32)
    for c0 in range(0, ff, FF_CHUNK):
        sl = slice(c0, min(c0 + FF_CHUNK, ff))
        g = jnp.dot(hb, wg_ref[:, sl].astype(BF16), preferred_element_type=F32)
        u = jnp.dot(hb, wu_ref[:, sl].astype(BF16), preferred_element_type=F32)
        a = (g * _sigmoid(g) * u).astype(BF16)
        acc = acc + jnp.dot(a, wd_ref[sl, :].astype(BF16), preferred_element_type=F32)
    return acc


def _token_copy(src_ref, src_row, dst_ref, dst_row, sem):
    return pltpu.make_async_copy(src_ref.at[pl.ds(pl.multiple_of(src_row, SUBLANES), SUBLANES), :],
                                 dst_ref.at[pl.ds(pl.multiple_of(dst_row, SUBLANES), SUBLANES), :], sem)


def _dispatch_kernel(slot_ref, pad_ref, h_ref, xs_ref, hbuf, zero_ref, in_sem, out_sem, *, tm, npad, steps):
    i = pl.program_id(0)

    tr = tm * SUBLANES

    def fetch(step, slot):
        return pltpu.make_async_copy(h_ref.at[pl.ds(step * tr, tr), :], hbuf.at[slot], in_sem.at[slot])

    def drain(slot):
        for _ in range(2):
            pltpu.make_async_copy(hbuf.at[slot], xs_ref.at[pl.ds(0, tr), :], out_sem.at[slot]).wait()
        pltpu.make_async_copy(hbuf.at[slot, pl.ds(0, npad * SUBLANES), :], xs_ref.at[pl.ds(0, npad * SUBLANES), :],
                              out_sem.at[slot]).wait()

    @pl.when(i == 0)
    def _():
        zero_ref[...] = jnp.zeros(zero_ref.shape, F32)
        fetch(0, 0).start()

    slot = lax.rem(i, 3)
    fetch(i, slot).wait()

    @pl.when(i + 1 < steps)
    def _():
        fetch(i + 1, lax.rem(i + 1, 3)).start()

    def scatter(r, carry):
        for k in range(2):
            _token_copy(hbuf.at[slot], r * SUBLANES, xs_ref, slot_ref[0, 0, k * tm + r],
                        out_sem.at[slot]).start(priority=k)
        return carry

    def scatter_pad(j, carry):
        _token_copy(zero_ref, 0, xs_ref, pad_ref[0, 0, j], out_sem.at[slot]).start()
        return carry

    lax.fori_loop(0, tm, scatter, 0, unroll=8)
    lax.fori_loop(0, npad, scatter_pad, 0, unroll=8)

    @pl.when(i > 0)
    def _():
        drain(lax.rem(i + 2, 3))

    @pl.when(i == steps - 1)
    def _():
        drain(slot)


def _dispatch(h_tiles, slots, pad_slots, n_tokens_out):
    n = h_tiles.shape[0] // SUBLANES
    tm = TM_ROUTE
    npad = pad_slots.shape[-1]
    steps = n // tm
    return pl.pallas_call(
        functools.partial(_dispatch_kernel, tm=tm, npad=npad, steps=steps),
        grid=(steps,),
        in_specs=[pl.BlockSpec((1, 1, 2 * tm), lambda i: (i, 0, 0), memory_space=pltpu.SMEM),
                  pl.BlockSpec((1, 1, npad), lambda i: (i, 0, 0), memory_space=pltpu.SMEM),
                  pl.BlockSpec(memory_space=pl.ANY)],
        out_specs=pl.BlockSpec(memory_space=pl.ANY),
        out_shape=jax.ShapeDtypeStruct((n_tokens_out * SUBLANES, LANES), F32),
        scratch_shapes=[pltpu.VMEM((3, tm * SUBLANES, LANES), F32), pltpu.VMEM((SUBLANES, LANES), F32),
                        pltpu.SemaphoreType.DMA((3,)), pltpu.SemaphoreType.DMA((3,))],
        compiler_params=_cparams(("arbitrary",), 3 * tm * D_MODEL * 4 + (4 << 20)),
        name="dispatch",
    )(slots, pad_slots, h_tiles)


def _experts_kernel(te_ref, nt_ref, xs_ref, wg_ref, wu_ref, wd_ref, o_ref, *, tm):
    t = pl.program_id(0)

    @pl.when(t < nt_ref[0])
    def _():
        hb = _load_token_tiles(xs_ref, 0, tm).astype(BF16)
        _store_token_tiles(o_ref, 0, _swiglu_tile(hb, wg_ref, wu_ref, wd_ref))

    @pl.when(t >= nt_ref[0])
    def _():
        o_ref[...] = jnp.zeros(o_ref.shape, F32)


def _experts(xs, tile_expert, n_tiles_used, p):
    D = D_MODEL
    tm = TM_MOE
    n_tiles = tile_expert.shape[0]
    wmap = lambda t, te, nt: (te[t], 0, 0)
    tok = pl.BlockSpec((tm * SUBLANES, LANES), lambda t, te, nt: (t, 0))
    tok_in = pl.BlockSpec((tm * SUBLANES, LANES), lambda t, te, nt: (jnp.minimum(t, nt[0] - 1), 0))
    vmem = 2 * 3 * D * D_FF * p["moe_wg"].dtype.itemsize + 12 * tm * D * 4 + (6 << 20)
    return pl.pallas_call(
        functools.partial(_experts_kernel, tm=tm),
        grid_spec=pltpu.PrefetchScalarGridSpec(
            num_scalar_prefetch=2,
            grid=(n_tiles,),
            in_specs=[tok_in,
                      pl.BlockSpec((None, D, D_FF), wmap),
                      pl.BlockSpec((None, D, D_FF), wmap),
                      pl.BlockSpec((None, D_FF, D), wmap)],
            out_specs=tok),
        out_shape=jax.ShapeDtypeStruct((n_tiles * tm * SUBLANES, LANES), F32),
        compiler_params=_cparams(("arbitrary",), vmem),
        name="experts",
    )(tile_expert, n_tiles_used, xs, p["moe_wg"], p["moe_wu"], p["moe_wd"])


def _combine_kernel(slot_ref, next_ref, x1_ref, route_ref, mod_ref, ys_ref, o_ref, buf_ref, sem, *, tm, steps):
    i = pl.program_id(0)

    def gather(idx_ref, s):
        def body(r, carry):
            for k in range(2):
                _token_copy(ys_ref, idx_ref[0, 0, k * tm + r], buf_ref.at[s, k], r * SUBLANES,
                            sem.at[s]).start(priority=k)
            return carry
        lax.fori_loop(0, tm, body, 0, unroll=8)

    @pl.when(i == 0)
    def _():
        gather(slot_ref, 0)

    @pl.when(i + 1 < steps)
    def _():
        gather(next_ref, lax.rem(i + 1, 2))

    cur = lax.rem(i, 2)
    for k in range(2):
        pltpu.make_async_copy(ys_ref.at[pl.ds(0, tm * SUBLANES), :], buf_ref.at[cur, k], sem.at[cur]).wait()
    route = route_ref[...]
    w1 = route[:, 2:3]
    w2 = route[:, 3:4]
    y0 = _load_token_tiles(buf_ref, 0, tm, lead=(cur, 0))
    y1 = _load_token_tiles(buf_ref, 0, tm, lead=(cur, 1))
    o_ref[...] = x1_ref[...] + mod_ref[5:6, :] * (w1 * y0 + w2 * y1)


def _combine(x1, route, mod, ys, slots):
    B, S, D = x1.shape
    tm = TM_ROUTE
    per = S // tm
    steps = B * per
    tok = lambda w: pl.BlockSpec((None, tm, w), lambda i: (i // per, i % per, 0))
    return pl.pallas_call(
        functools.partial(_combine_kernel, tm=tm, steps=steps),
        grid=(steps,),
        in_specs=[pl.BlockSpec((1, 1, 2 * tm), lambda i: (i, 0, 0), memory_space=pltpu.SMEM),
                  pl.BlockSpec((1, 1, 2 * tm), lambda i: (jnp.minimum(i + 1, steps - 1), 0, 0),
                               memory_space=pltpu.SMEM),
                  tok(D), tok(LANES), pl.BlockSpec((None, N_MOD, D), lambda i: (i // per, 0, 0)),
                  pl.BlockSpec(memory_space=pl.ANY)],
        out_specs=tok(D),
        out_shape=jax.ShapeDtypeStruct((B, S, D), F32),
        scratch_shapes=[pltpu.VMEM((2, 2, tm * SUBLANES, LANES), F32), pltpu.SemaphoreType.DMA((2,))],
        compiler_params=_cparams(("arbitrary",), 12 * tm * D * 4 + (4 << 20)),
        name="combine",
    )(slots, slots, x1, route, mod, ys)


def _moe(h, x1, route, cnt, route_t, mod, p):
    B, S, D = x1.shape
    n = B * S
    sub = n // (cnt.shape[0] * cnt.shape[1])
    tmm = TM_MOE
    n_tiles = 2 * n // tmm + N_EXPERTS
    n_slots = n_tiles * tmm
    steps = n // TM_ROUTE
    counts = cnt[:, :, 0:2, 0:N_EXPERTS].astype(jnp.int32).reshape(-1, 2, N_EXPERTS)
    tile_tot = counts.sum(axis=0)
    n_e = tile_tot.sum(axis=0)
    base = jnp.cumsum(counts, axis=0) - counts
    base = base + jnp.array([0, 1], jnp.int32)[None, :, None] * tile_tot[0][None, None, :]
    tiles_e = (n_e + tmm - 1) // tmm
    pstart = (jnp.cumsum(tiles_e) - tiles_e) * tmm
    expert_ids = jnp.arange(N_EXPERTS, dtype=jnp.int32)[:, None]
    slots_k = []
    for k in range(2):
        e_k = route_t[k].astype(jnp.int32)
        table = jnp.repeat((base[:, k, :] + pstart[None, :]).T, sub, axis=1)
        slot = jnp.sum(jnp.where(e_k[None, :] == expert_ids, table, 0), axis=0) + route_t[4 + k].astype(jnp.int32)
        slots_k.append((slot * SUBLANES).reshape(steps, TM_ROUTE))
    slots = jnp.concatenate(slots_k, axis=1).reshape(steps, 1, 2 * TM_ROUTE)
    tile_end = jnp.cumsum(tiles_e)
    n_used = tile_end[-1]
    t_idx = jnp.minimum(jnp.arange(n_tiles, dtype=jnp.int32), n_used - 1)
    tile_expert = jnp.sum(t_idx[:, None] >= tile_end[None, :], axis=-1).astype(jnp.int32)
    tile_expert = jnp.minimum(tile_expert, N_EXPERTS - 1)

    n_cand = N_EXPERTS * tmm
    pq = jnp.arange(tmm, dtype=jnp.int32)[None, :]
    n_pad_e = (tiles_e * tmm - n_e)[:, None]
    spare = n_slots + expert_ids * tmm + pq
    pad_slots = jnp.where(pq < n_pad_e, (pstart + n_e)[:, None] + pq, spare).astype(jnp.int32)
    assert n_cand % steps == 0
    pad_slots = (pad_slots * SUBLANES).reshape(steps, 1, n_cand // steps)

    xs = _dispatch(h, slots, pad_slots, n_slots + n_cand)
    ys = _experts(xs, tile_expert, n_used.reshape(1).astype(jnp.int32), p)
    return _combine(x1, route, mod, ys, slots)


def _pad_heads(w, width):
    lead = w.shape[:-1]
    w = w.reshape(lead + (N_HEADS, width))
    w = jnp.pad(w, [(0, 0)] * len(lead) + [(0, 0), (0, HEAD_PAD - width)])
    return w.reshape(lead + (N_HEADS * HEAD_PAD,))


def _layer_params(l, w_in, q_lat_g, kv_lat_g, w_uq, w_ukv, q_norm_g, k_norm_g, conv_w, conv_b, conv_ln_g,
                  conv_ln_b, w_out, ffn_norm_g):
    wi = w_in[l]
    kr = _with_partner(jnp.pad(wi[:, OFF_KR:OFF_CONV], ((0, 0), (QK_NOPE, HEAD_PAD - QK_HEAD))))
    ckv = wi[:, OFF_CKV:OFF_KR]
    w_full = jnp.concatenate([wi[:, OFF_CQ:OFF_CKV], ckv, wi[:, OFF_CONV:OFF_FOUR], wi[:, OFF_FOUR:IN_COLS], kr],
                             axis=1)
    w_kv = jnp.concatenate([ckv, kr], axis=1)
    ukv = w_ukv[l].reshape(KV_LORA, N_HEADS, QK_NOPE + V_HEAD)
    uk = _pad_heads(ukv[:, :, :QK_NOPE].reshape(KV_LORA, N_HEADS * QK_NOPE), QK_NOPE)
    uv = _pad_heads(ukv[:, :, QK_NOPE:].reshape(KV_LORA, ATTN_OUT), V_HEAD)
    pad_g = lambda g: jnp.pad(g, (0, HEAD_PAD - QK_HEAD)).reshape(1, HEAD_PAD)
    uq = _with_partner(_pad_heads(w_uq[l], QK_HEAD))
    hw = N_HEADS * HEAD_PAD
    col = np.arange(hw)[:, None]
    head_of_col = col // HEAD_PAD == np.arange(LANES)[None, :]
    return {
        "w_in_full": w_full.astype(BF16),
        "w_in_kv": w_kv.astype(BF16),
        "q_lat_g": q_lat_g[l].reshape(1, Q_LORA),
        "kv_lat_g": kv_lat_g[l].reshape(1, KV_LORA),
        "w_uq": uq.astype(BF16),
        "w_ukv": jnp.concatenate([uk, uv], axis=1).astype(BF16),
        "q_norm_g": pad_g(q_norm_g[l]) * (QK_HEAD ** -0.5 * math.log2(math.e)),
        "v_ones": jnp.tile(jnp.concatenate([jnp.zeros((V_HEAD,), F32), jnp.ones((HEAD_PAD - V_HEAD,), F32)]),
                           N_HEADS).reshape(1, hw),
        "ones_h": jnp.asarray(head_of_col & (col % HEAD_PAD < QK_HEAD), F32).astype(BF16),
        "k_norm_g": pad_g(k_norm_g[l]),
        "conv_w": conv_w[l],
        "conv_b": conv_b[l].reshape(1, CONV_CH),
        "conv_ln_g": conv_ln_g[l].reshape(1, CONV_CH),
        "conv_ln_b": conv_ln_b[l].reshape(1, CONV_CH),
        "ffn_norm_g": ffn_norm_g[l].reshape(1, D_MODEL),
        "bd": _channel_dft(),
    }


def _with_partner(w):
    lead = w.shape[:-1]
    half = QK_ROPE // 2
    w3 = w.reshape(lead + (-1, HEAD_PAD))
    out = jnp.concatenate([w3[..., :QK_HEAD], w3[..., QK_NOPE + half:QK_HEAD], w3[..., QK_NOPE:QK_NOPE + half]],
                          axis=-1)
    return out.reshape(w.shape)


def _rot_partner(w):
    lead = w.shape[:-1]
    half = QK_ROPE // 2
    w3 = w.reshape(lead + (-1, HEAD_PAD))
    z = jnp.zeros_like(w3)
    out = jnp.concatenate([z[..., :QK_NOPE], w3[..., QK_NOPE + half:QK_HEAD], w3[..., QK_NOPE:QK_NOPE + half],
                           z[..., QK_HEAD:]], axis=-1)
    return out.reshape(w.shape)


def _rope_tables(seq, gq, gk):
    rows = seq // GRID_W
    row = np.repeat(np.arange(rows, dtype=np.float64), GRID_W)
    col = np.tile(np.arange(GRID_W, dtype=np.float64), rows)
    n_freq = QK_ROPE // 4
    inv = ROPE_BASE ** (-np.arange(n_freq, dtype=np.float64) / n_freq)
    ang = np.concatenate([row[:, None] * inv, col[:, None] * inv], axis=-1)
    cos, sin = np.cos(ang), np.sin(ang)
    ones = np.ones((seq, QK_NOPE))
    tail = np.zeros((seq, HEAD_PAD - QK_HEAD))
    cos_t = jnp.asarray(np.concatenate([ones, cos, cos, tail], axis=1), F32)
    sin_t = jnp.asarray(np.concatenate([np.zeros((seq, QK_NOPE)), -sin, sin, tail], axis=1), F32)
    return (cos_t * gq, sin_t * _rot_partner(gq), cos_t * gk, sin_t * _rot_partner(gk))


def kernel(x, c, ctx, c_ctx, ada_w, ada_b, mix_norm_g, ffn_norm_g, w_in, q_lat_g, kv_lat_g, w_uq, w_ukv, q_norm_g,
           k_norm_g, conv_w, conv_b, conv_ln_g, conv_ln_b, w_out, ffn_w_gate, ffn_w_up, ffn_w_down, router_w,
           moe_w_gate, moe_w_up, moe_w_down):
    B, S, D = x.shape
    T = ctx.shape[1]
    assert (D, DEPTH) == (D_MODEL, ada_w.shape[0]) and S % GRID_W == 0

    cc = jnp.concatenate([c, c_ctx[None, :], jnp.zeros((2 * SUBLANES - B - 1, D), F32)], axis=0)
    mods = _modulation(cc, ada_w, ada_b).reshape(DEPTH, 2 * SUBLANES, N_MOD, D)
    cs_x = _dft_tables(S)
    cs_c = _dft_tables(T)

    def moe_f32(i):
        return (moe_w_gate[i].reshape(N_EXPERTS * D, D_FF), moe_w_up[i].reshape(N_EXPERTS * D, D_FF),
                moe_w_down[i].reshape(N_EXPERTS * D_FF, D))

    moe_bf16 = {}
    for l in range(DEPTH):
        last = l == DEPTH - 1
        p = _layer_params(l, w_in, q_lat_g, kv_lat_g, w_uq, w_ukv, q_norm_g, k_norm_g, conv_w, conv_b, conv_ln_g,
                          conv_ln_b, w_out, ffn_norm_g)
        p["mix_norm_g"] = mix_norm_g[l].reshape(1, D)
        rope_tabs = _rope_tables(S, p["q_norm_g"], p["k_norm_g"])
        i = l // 2
        moe = l % 2 == 1
        mod_x = mods[l, :B]
        mod_c = mods[l, B:B + 1]

        nxt = l + 1 if l % 2 == 0 else l + 2
        cast = []
        if not moe:
            cast += [ffn_w_gate[i], ffn_w_up[i], ffn_w_down[i]]
        if nxt < DEPTH:
            cast += list(moe_f32(nxt // 2))
        p["w_out"] = w_out[l].astype(BF16)

        def channel_mixer(attn, conv, four, xs, mod):
            if moe:
                x1, h, route, cnt, route_t = _mix_out(attn, conv, four, xs, mod, p, moe=True)
                mod_b = mod if mod.shape[0] > 1 else jnp.broadcast_to(mod, (B,) + mod.shape[1:])
                return _moe(h, x1, route, cnt, route_t, mod_b, p)
            return _mix_out(attn, conv, four, xs, mod, p, moe=False)

        if last:
            k_c, v_c = _front(ctx, mod_c, p["mix_norm_g"], p, None, full=False)
        else:
            q_c, k_c, v_c, y_c, zz_c = _front(ctx, mod_c, p["mix_norm_g"], p, None, full=True)
        q_x, k_x, v_x, y_x, zz_x = _front(x, mod_x, p["mix_norm_g"], p, rope_tabs, full=True)
        attn_x, *casted = _attend(q_x, [k_x, k_c], [v_x, v_c], cast)
        if moe:
            rw = jnp.pad(router_w[i], ((0, 0), (0, LANES - N_EXPERTS)))
            rw_hi = rw.astype(BF16)
            p["router_w"] = jnp.concatenate([rw_hi, (rw - rw_hi.astype(F32)).astype(BF16)], axis=1)
            wg, wu, wd = moe_bf16.pop(i) if i in moe_bf16 else [w.astype(BF16) for w in moe_f32(i)]
            p["moe_wg"] = wg.reshape(N_EXPERTS, D, D_FF)
            p["moe_wu"] = wu.reshape(N_EXPERTS, D, D_FF)
            p["moe_wd"] = wd.reshape(N_EXPERTS, D_FF, D)
        else:
            p["ffn_wg"], p["ffn_wu"], p["ffn_wd"] = casted[:3]
            casted = casted[3:]
        if nxt < DEPTH:
            moe_bf16[nxt // 2] = casted

        if not last:
            attn_c, = _attend(q_c, [k_c], [v_c])
            ctx_next = channel_mixer(attn_c, *_conv_fourier(y_c, zz_c, cs_c, p), ctx, mod_c)

        x = channel_mixer(attn_x, *_conv_fourier(y_x, zz_x, cs_x, p), x, mod_x)
        if not last:
            ctx = ctx_next
    return x
```

```python
import functools
import math

import numpy as np
import jax
import jax.numpy as jnp
from jax import lax
from jax.experimental import pallas as pl
from jax.experimental.pallas import tpu as pltpu

F32 = jnp.float32
BF16 = jnp.bfloat16

D_MODEL = 1024
DEPTH = 2
GRID_W = 64
N_HEADS = 8
QK_NOPE = 64
QK_ROPE = 32
QK_HEAD = QK_NOPE + QK_ROPE
V_HEAD = 64
Q_LORA = 384
KV_LORA = 256
ROPE_BASE = 10000.0
CONV_CH = 256
CONV_WIDTH = 31
CONV_PAD = (CONV_WIDTH - 1) // 2
FOURIER_GROUPS = 4
FOURIER_GROUP_DIM = 64
FOURIER_CH = FOURIER_GROUPS * FOURIER_GROUP_DIM
ATTN_OUT = N_HEADS * V_HEAD
OFF_CQ = 0
OFF_CKV = OFF_CQ + Q_LORA
OFF_KR = OFF_CKV + KV_LORA
OFF_CONV = OFF_KR + QK_ROPE
OFF_FOUR = OFF_CONV + 2 * CONV_CH
IN_COLS = OFF_FOUR + FOURIER_CH
D_FF = 2816
N_EXPERTS = 8
N_MOD = 6
EPS = 1e-6

LANES = 128
SUBLANES = 8
HEAD_PAD = LANES
VMEM_CAP = 56 * 1024 * 1024
FF_CHUNK = 256

SUB_ROWS = 256
SUB_ROWS_MOE = 512
TM_FRONT = 1024
TQ_ATTN = 1024
TQ_SUB = 512
TK_ATTN = 256
TM_OUT = 512
TM_MOE = 512
TM_ROUTE = 512
CONV_ROWS = 128
TR_FOURIER = 512


def _cparams(sem, vmem_bytes):
    return pltpu.CompilerParams(dimension_semantics=sem, vmem_limit_bytes=int(min(VMEM_CAP, vmem_bytes)))


def _rms(v, g):
    return v * lax.rsqrt(jnp.mean(v * v, axis=-1, keepdims=True) + EPS) * g


def _sigmoid(v):
    return 1.0 / (1.0 + jnp.exp(-v))


def _mod_kernel(c_ref, w_ref, b_ref, o_ref):
    c = c_ref[...]
    s = (c * _sigmoid(c)).astype(BF16)
    o_ref[...] = jnp.dot(s, w_ref[...].astype(BF16), preferred_element_type=F32) + b_ref[...]


def _modulation(cc, ada_w, ada_b):
    rows = cc.shape[0]
    tn = 1536
    n_out = N_MOD * D_MODEL
    return pl.pallas_call(
        _mod_kernel,
        grid=(DEPTH, n_out // tn),
        in_specs=[
            pl.BlockSpec((rows, D_MODEL), lambda l, j: (0, 0)),
            pl.BlockSpec((None, D_MODEL, tn), lambda l, j: (l, 0, j)),
            pl.BlockSpec((None, 1, tn), lambda l, j: (l, 0, j)),
        ],
        out_specs=pl.BlockSpec((None, rows, tn), lambda l, j: (l, 0, j)),
        out_shape=jax.ShapeDtypeStruct((DEPTH, rows, n_out), F32),
        compiler_params=_cparams(("arbitrary", "arbitrary"), 4 * D_MODEL * tn * 4),
        name="modulation",
    )(cc, ada_w, ada_b.reshape(DEPTH, 1, n_out))


def _front_kernel(*refs, full, rope):
    it = iter(refs)
    x_ref, mod_ref, g_ref, win_ref = next(it), next(it), next(it), next(it)
    if full:
        qlg_ref, wuq_ref = next(it), next(it)
    kvlg_ref, wukv_ref, vones_ref, onesh_ref = next(it), next(it), next(it), next(it)
    if rope:
        cq_ref, sq_ref, ck_ref, sk_ref = next(it), next(it), next(it), next(it)
    else:
        qg_ref = next(it) if full else None
        kg_ref = next(it)
    if full:
        bd_ref = next(it)
        q_out = next(it)
    k_out, v_out = next(it), next(it)
    if full:
        y_out, zz_out = next(it), next(it)

    shift = mod_ref[0:1, :]
    gain = g_ref[...] * (1.0 + mod_ref[1:2, :])
    hw = N_HEADS * HEAD_PAD

    def head_inv_rms(raw):
        ss = jnp.dot((raw * raw).astype(BF16), onesh_ref[...], preferred_element_type=F32)
        rs = lax.rsqrt(ss * (1.0 / QK_HEAD) + EPS)
        return jnp.concatenate([jnp.broadcast_to(rs[:, hd:hd + 1], (rs.shape[0], HEAD_PAD)) for hd in range(N_HEADS)],
                               axis=1)

    def sub_block(rows):
        x = x_ref[rows, :]
        h = _rms(x, gain) + shift
        cols = jnp.dot(h.astype(BF16), win_ref[...], preferred_element_type=F32)

        o = 0
        if full:
            cq = cols[:, 0:Q_LORA]
            o = Q_LORA
            qall = jnp.dot(_rms(cq, qlg_ref[...]).astype(BF16), wuq_ref[...], preferred_element_type=F32)
            rsb = head_inv_rms(qall)
            for hd in range(N_HEADS):
                sl = slice(hd * HEAD_PAD, (hd + 1) * HEAD_PAD)
                if rope:
                    val = qall[:, sl] * cq_ref[rows, :] + pltpu.roll(qall[:, sl], HEAD_PAD - QK_ROPE, 1) * \
                        sq_ref[rows, :]
                else:
                    val = qall[:, sl] * qg_ref[...]
                q_out[rows, sl] = (val * rsb[:, sl]).astype(BF16)

        ckv = cols[:, o:o + KV_LORA]
        o += KV_LORA
        kv = jnp.dot(_rms(ckv, kvlg_ref[...]).astype(BF16), wukv_ref[...], preferred_element_type=F32)
        if full:
            a = cols[:, o:o + CONV_CH]
            gt = cols[:, o + CONV_CH:o + 2 * CONV_CH]
            y_out[rows, :] = a * _sigmoid(gt)
            o += 2 * CONV_CH
            z = cols[:, o:o + FOURIER_CH]
            o += FOURIER_CH
            zz_out[rows, :] = jnp.dot(z.astype(BF16), bd_ref[...], preferred_element_type=F32).astype(BF16)
        krb = cols[:, o:o + HEAD_PAD]
        kraw = [kv[:, hd * HEAD_PAD:(hd + 1) * HEAD_PAD] + krb for hd in range(N_HEADS)]
        rsb = head_inv_rms(jnp.concatenate(kraw, axis=1))
        if rope:
            kpart = pltpu.roll(krb, HEAD_PAD - QK_ROPE, 1) * sk_ref[rows, :]
        for hd in range(N_HEADS):
            sl = slice(hd * HEAD_PAD, (hd + 1) * HEAD_PAD)
            val = kraw[hd] * ck_ref[rows, :] + kpart if rope else kraw[hd] * kg_ref[...]
            k_out[rows, sl] = (val * rsb[:, sl]).astype(BF16)
        v_out[rows, :] = (kv[:, hw:] + vones_ref[...]).astype(BF16)

    tm = x_ref.shape[0]
    sub = min(SUB_ROWS, tm)
    for sb in range(tm // sub):
        sub_block(slice(sb * sub, (sb + 1) * sub))


def _front(xs, mod, norm_g, p, rope_tabs, *, full):
    if mod.shape[0] == 1 and xs.shape[0] > 1 and rope_tabs is None:
        outs = _front(xs.reshape(1, -1, xs.shape[-1]), mod, norm_g, p, None, full=full)
        return [o.reshape(xs.shape[:2] + o.shape[2:]) for o in outs]
    B, S, D = xs.shape
    tm = min(TM_FRONT, S)
    rope = rope_tabs is not None
    assert full or not rope
    w_in = p["w_in_full"] if full else p["w_in_kv"]
    w_uq = p["w_uq"]
    ncol = w_in.shape[1]
    bm = mod.shape[0]
    mod_map = (lambda b, i: (b, 0, 0)) if bm > 1 else (lambda b, i: (0, 0, 0))
    const2 = lambda b, i: (0, 0)
    hw = N_HEADS * HEAD_PAD

    args = [xs, mod, norm_g, w_in]
    specs = [
        pl.BlockSpec((None, tm, D), lambda b, i: (b, i, 0)),
        pl.BlockSpec((None, N_MOD, D), mod_map),
        pl.BlockSpec((1, D), const2),
        pl.BlockSpec((D, ncol), const2),
    ]
    if full:
        args += [p["q_lat_g"], w_uq]
        specs += [pl.BlockSpec((1, Q_LORA), const2), pl.BlockSpec((Q_LORA, w_uq.shape[1]), const2)]
    args += [p["kv_lat_g"], p["w_ukv"], p["v_ones"], p["ones_h"]]
    specs += [pl.BlockSpec((1, KV_LORA), const2),
              pl.BlockSpec((KV_LORA, 2 * hw), const2),
              pl.BlockSpec((1, hw), const2),
              pl.BlockSpec((hw, LANES), const2)]
    if rope:
        args += list(rope_tabs)
        specs += [pl.BlockSpec((tm, HEAD_PAD), lambda b, i: (i, 0))] * 4
    else:
        if full:
            args += [p["q_norm_g"]]
            specs += [pl.BlockSpec((1, HEAD_PAD), const2)]
        args += [p["k_norm_g"]]
        specs += [pl.BlockSpec((1, HEAD_PAD), const2)]
    if full:
        args += [p["bd"]]
        specs += [pl.BlockSpec((FOURIER_CH, 2 * FOURIER_CH), const2)]

    out_shapes, out_specs = [], []

    def add_out(width, dtype):
        out_shapes.append(jax.ShapeDtypeStruct((B, S, width), dtype))
        out_specs.append(pl.BlockSpec((None, tm, width), lambda b, i: (b, i, 0)))

    if full:
        add_out(N_HEADS * HEAD_PAD, BF16)
    add_out(N_HEADS * HEAD_PAD, BF16)
    add_out(N_HEADS * HEAD_PAD, BF16)
    if full:
        add_out(CONV_CH, F32)
        add_out(2 * FOURIER_CH, BF16)

    vmem = 2 * (D * ncol * 2 + Q_LORA * 1024 * 2 + KV_LORA * 2048 * 2) + 28 * tm * D * 4
    return pl.pallas_call(
        functools.partial(_front_kernel, full=full, rope=rope),
        grid=(B, S // tm),
        in_specs=specs,
        out_specs=out_specs,
        out_shape=out_shapes,
        compiler_params=_cparams(("arbitrary", "arbitrary"), vmem),
        name="front_full" if full else "front_kv",
    )(*args)


def _attn_kernel(*refs, nsrc, ncast):
    q_ref = refs[0]
    k_refs = refs[1:1 + nsrc]
    v_refs = refs[1 + nsrc:1 + 2 * nsrc]
    cast_in = refs[1 + 2 * nsrc:1 + 2 * nsrc + ncast]
    o_ref = refs[1 + 2 * nsrc + ncast]
    cast_out = refs[2 + 2 * nsrc + ncast:]
    chunks = []
    for k_ref, v_ref in zip(k_refs, v_refs):
        for s0 in range(0, k_ref.shape[0], TK_ATTN):
            chunks.append((k_ref, v_ref, s0, min(TK_ATTN, k_ref.shape[0] - s0)))
    tq = q_ref.shape[0]
    sub = min(TQ_SUB, tq)
    for pair, r0 in [(pr, rr) for pr in range(q_ref.shape[1] // (2 * HEAD_PAD)) for rr in range(0, tq, sub)]:
        rows = slice(r0, r0 + sub)
        state = [None, None]
        for k_ref, v_ref, s0, sz in chunks:
            for hh in range(2):
                sl = slice((2 * pair + hh) * HEAD_PAD, (2 * pair + hh + 1) * HEAD_PAD)
                s = lax.dot_general(q_ref[rows, sl], k_ref[s0:s0 + sz, sl], (((1,), (1,)), ((), ())),
                                    preferred_element_type=F32)
                m = jnp.max(s, axis=-1, keepdims=True)
                if state[hh] is not None:
                    m_old, acc_old = state[hh]
                    m = jnp.maximum(m_old, m)
                pv = jnp.dot(jnp.exp2((s - m).astype(BF16)), v_ref[s0:s0 + sz, sl], preferred_element_type=F32)
                if state[hh] is not None:
                    pv = pv + jnp.exp2(m_old - m) * acc_old
                state[hh] = (m, pv)
        accs = [state[0][1], state[1][1]]
        lane = lax.broadcasted_iota(jnp.int32, accs[0].shape, 1)
        lo = accs[0] / pltpu.roll(accs[0], V_HEAD, 1)
        hi = pltpu.roll(accs[1], V_HEAD, 1) / accs[1]
        o_ref[rows, pair * 2 * V_HEAD:(pair + 1) * 2 * V_HEAD] = jnp.where(lane < V_HEAD, lo, hi).astype(BF16)
    for src, dst in zip(cast_in, cast_out):
        dst[...] = src[...].astype(BF16)


def _attend(q, ks, vs, cast=()):
    B, S, _ = q.shape
    tq = min(TQ_ATTN, S)
    nsrc = len(ks)
    nq = S // tq
    pp = N_HEADS // 2 if nq == 1 and S <= TQ_SUB else 1
    ngrp = N_HEADS // 2 // pp
    steps = B * ngrp * nq
    specs = [pl.BlockSpec((None, tq, pp * 2 * HEAD_PAD), lambda b, hp, i: (b, i, hp))]
    for kv in list(ks) + list(vs):
        specs.append(pl.BlockSpec((None, kv.shape[1], pp * 2 * HEAD_PAD), lambda b, hp, i: (b, 0, hp)))
    out_specs = [pl.BlockSpec((None, tq, pp * 2 * V_HEAD), lambda b, hp, i: (b, i, hp))]
    out_shapes = [jax.ShapeDtypeStruct((B, S, ATTN_OUT), BF16)]
    cast_bytes = 0
    for w in cast:
        rows, cols = w.shape
        nblk = max(d for d in range(1, steps + 1)
                   if steps % d == 0 and rows % d == 0 and (rows // d) % (2 * SUBLANES) == 0)
        hold = steps // nblk
        spec = pl.BlockSpec((rows // nblk, cols),
                            lambda b, hp, i, hold=hold: (((b * ngrp + hp) * nq + i) // hold, 0))
        specs.append(spec)
        out_specs.append(spec)
        out_shapes.append(jax.ShapeDtypeStruct(w.shape, BF16))
        cast_bytes += 2 * (rows // nblk) * cols * 6
    t_all = sum(k.shape[1] for k in ks)
    vmem = 2 * (t_all * pp * 512 * 2) * 2 + 16 * min(tq, TQ_SUB) * TK_ATTN * 4 + cast_bytes + (12 << 20)
    outs = pl.pallas_call(
        functools.partial(_attn_kernel, nsrc=nsrc, ncast=len(cast)),
        grid=(B, ngrp, nq),
        in_specs=specs,
        out_specs=out_specs,
        out_shape=out_shapes,
        compiler_params=_cparams(("arbitrary", "arbitrary", "arbitrary"), vmem),
        name="attend%d" % nsrc,
    )(q, *ks, *vs, *cast)
    return list(outs)


def _conv_fourier_kernel(y_ref, w_ref, b_ref, lg_ref, lb_ref, cs_ref, zz_ref, conv_out, four_out, pad_ref, *, seq):
    halo = 2 * SUBLANES
    total = seq + 2 * halo
    pad_ref[0, 0:halo, :] = jnp.zeros((halo, CONV_CH), F32)
    pad_ref[0, halo + seq:total, :] = jnp.zeros((halo, CONV_CH), F32)
    pad_ref[0, halo:halo + seq, :] = y_ref[...]
    slab = pad_ref[0]
    for ph in range(1, SUBLANES):
        pad_ref[ph] = pltpu.roll(slab, total - ph, 0)

    def conv_rows(base, rows):
        acc = jnp.zeros((rows, CONV_CH), F32)
        for j in range(CONV_WIDTH):
            off = base + halo - CONV_PAD + j
            ph = off % SUBLANES
            acc = acc + pad_ref[ph, off - ph:off - ph + rows, :] * w_ref[j:j + 1, :]
        acc = acc + b_ref[...]
        mu = jnp.mean(acc, axis=-1, keepdims=True)
        cen = acc - mu
        var = jnp.mean(cen * cen, axis=-1, keepdims=True)
        yn = cen * lax.rsqrt(var + EPS) * lg_ref[...] + lb_ref[...]
        conv_out[base:base + rows, :] = (yn * _sigmoid(yn)).astype(BF16)

    def dft_rows(base, rows):
        r = slice(base, base + rows)
        acc = jnp.dot(cs_ref[r, 0:seq], zz_ref[:, 0:FOURIER_CH], preferred_element_type=F32)
        acc = acc + jnp.dot(cs_ref[r, seq:2 * seq], zz_ref[:, FOURIER_CH:2 * FOURIER_CH],
                            preferred_element_type=F32)
        four_out[r, :] = acc.astype(BF16)

    tr = min(TR_FOURIER, seq)
    for base in range(0, seq, tr):
        for cb in range(base, base + tr, CONV_ROWS):
            conv_rows(cb, CONV_ROWS)
        dft_rows(base, tr)


def _conv_fourier(y, zz, cs, p):
    B, S, _ = y.shape
    c2 = lambda b: (0, 0)
    seq_block = lambda w: pl.BlockSpec((None, S, w), lambda b: (b, 0, 0))
    vmem = (SUBLANES + 6) * (S + 32) * CONV_CH * 4 + S * 2 * S * 2 + 2 * S * 512 * 2 + (10 << 20)
    return pl.pallas_call(
        functools.partial(_conv_fourier_kernel, seq=S),
        grid=(B,),
        in_specs=[seq_block(CONV_CH),
                  pl.BlockSpec((CONV_WIDTH, CONV_CH), c2),
                  pl.BlockSpec((1, CONV_CH), c2), pl.BlockSpec((1, CONV_CH), c2), pl.BlockSpec((1, CONV_CH), c2),
                  pl.BlockSpec((S, 2 * S), c2, pipeline_mode=pl.Buffered(1)),
                  seq_block(2 * FOURIER_CH)],
        out_specs=[seq_block(CONV_CH), seq_block(FOURIER_CH)],
        out_shape=[jax.ShapeDtypeStruct((B, S, CONV_CH), BF16), jax.ShapeDtypeStruct((B, S, FOURIER_CH), BF16)],
        scratch_shapes=[pltpu.VMEM((SUBLANES, S + 4 * SUBLANES, CONV_CH), F32)],
        compiler_params=_cparams(("arbitrary",), vmem),
        name="conv_fourier",
    )(y, p["conv_w"], p["conv_b"], p["conv_ln_g"], p["conv_ln_b"], cs, zz)


def _dft_tables(seq):
    norm = seq ** -0.5
    ks = (np.arange(seq)[:, None] * np.arange(seq)[None, :]) % seq
    ang = 2.0 * np.pi * ks / seq
    tab = np.concatenate([np.cos(ang), -np.sin(ang)], axis=1) * norm
    return jnp.asarray(tab, F32).astype(BF16)


def _channel_dft():
    n = FOURIER_GROUP_DIM
    ang = 2.0 * np.pi * ((np.arange(n)[:, None] * np.arange(n)[None, :]) % n) / n
    eye = np.eye(FOURIER_GROUPS)
    bd = np.concatenate([np.kron(eye, np.cos(ang)), np.kron(eye, np.sin(ang))], axis=1) * n ** -0.5
    return jnp.asarray(bd, F32).astype(BF16)


def _mixout_kernel(*refs, moe, ncast=0):
    cast_in = cast_out = ()
    if moe:
        (attn_ref, conv_ref, four_ref, x_ref, mod_ref, wout_ref, g_ref, rw_ref, tri_ref,
         x1_out, h_out, route_out, cnt_out, routet_out) = refs
    else:
        (attn_ref, conv_ref, four_ref, x_ref, mod_ref, wout_ref, g_ref, wg_ref, wu_ref, wd_ref) = refs[:10]
        cast_in = refs[10:10 + ncast]
        x1_out = refs[10 + ncast]
        cast_out = refs[11 + ncast:11 + 2 * ncast]
        h_out = refs[11 + 2 * ncast]
    tm = x_ref.shape[0]
    sub = min(SUB_ROWS_MOE if moe else SUB_ROWS, tm)
    gain = g_ref[...] * (1.0 + mod_ref[4:5, :])
    for sb in range(tm // sub):
        rows = slice(sb * sub, (sb + 1) * sub)
        mix = jnp.dot(attn_ref[rows, :], wout_ref[0:ATTN_OUT, :], preferred_element_type=F32)
        mix = mix + jnp.dot(conv_ref[rows, :], wout_ref[ATTN_OUT:ATTN_OUT + CONV_CH, :],
                            preferred_element_type=F32)
        mix = mix + jnp.dot(four_ref[rows, :], wout_ref[ATTN_OUT + CONV_CH:, :], preferred_element_type=F32)
        x1 = x_ref[rows, :] + mod_ref[2:3, :] * mix
        x1_out[rows, :] = x1
        h = _rms(x1, gain) + mod_ref[3:4, :]
        if moe:
            _store_token_tiles(h_out, sb * sub, h)
            route, cnt = _route(h, rw_ref, tri_ref)
            route_out[rows, :] = route
            cnt_out[sb] = cnt
            routet_out[:, rows] = route.T[0:SUBLANES, :]
        else:
            h_out[rows, :] = h.astype(BF16)
    if not moe:
        x1_out[...] = x1_out[...] + mod_ref[5:6, :] * _swiglu_tile(h_out[...], wg_ref, wu_ref, wd_ref)
    for src, dst in zip(cast_in, cast_out):
        dst[...] = src[...].astype(BF16)


def _store_token_tiles(ref, row0, val):
    rows = val.shape[0]
    for j in range(SUBLANES):
        ref[pl.ds(row0 * SUBLANES + j, rows, stride=SUBLANES), :] = val[:, j * LANES:(j + 1) * LANES]


def _load_token_tiles(ref, row0, rows, lead=()):
    idx = tuple(lead)
    return jnp.concatenate([ref[idx + (pl.ds(row0 * SUBLANES + j, rows, stride=SUBLANES), slice(None))]
                            for j in range(SUBLANES)], axis=1)


def _route(h, rw_ref, tri_ref):
    tm = h.shape[0]
    h_hi = h.astype(BF16)
    h_lo = (h - h_hi.astype(F32)).astype(BF16)
    hw = jnp.dot(h_hi, rw_ref[...], preferred_element_type=F32)
    logits = hw[:, 0:LANES] + hw[:, LANES:] + jnp.dot(h_lo, rw_ref[:, 0:LANES], preferred_element_type=F32)
    lane = lax.broadcasted_iota(jnp.int32, (tm, LANES), 1).astype(F32)
    neg = jnp.float32(-jnp.inf)
    lm = jnp.where(lane < N_EXPERTS, logits, neg)
    m1 = jnp.max(lm, axis=-1, keepdims=True)
    i1 = jnp.min(jnp.where(lm == m1, lane, float(LANES)), axis=-1, keepdims=True)
    lm2 = jnp.where(lane == i1, neg, lm)
    m2 = jnp.max(lm2, axis=-1, keepdims=True)
    i2 = jnp.min(jnp.where(lm2 == m2, lane, float(LANES)), axis=-1, keepdims=True)
    t = jnp.exp(m2 - m1)
    w1 = 1.0 / (1.0 + t)
    w2 = t / (1.0 + t)
    oh1 = lane == i1
    oh2 = lane == i2
    tri = tri_ref[...]
    c1 = jnp.dot(tri, jnp.where(oh1, 1.0, 0.0).astype(BF16), preferred_element_type=F32)
    c2 = jnp.dot(tri, jnp.where(oh2, 1.0, 0.0).astype(BF16), preferred_element_type=F32)
    r1 = jnp.sum(jnp.where(oh1, c1, 0.0), axis=-1, keepdims=True)
    r2 = jnp.sum(jnp.where(oh2, c2, 0.0), axis=-1, keepdims=True)
    vals = [i1, i2, w1, w2, r1, r2]
    route = jnp.zeros((tm, LANES), F32)
    for idx, v in enumerate(vals):
        route = jnp.where(lane == idx, v, route)
    n1 = jnp.sum(jnp.where(oh1, 1.0, 0.0), axis=0, keepdims=True)
    n2 = jnp.sum(jnp.where(oh2, 1.0, 0.0), axis=0, keepdims=True)
    row = lax.broadcasted_iota(jnp.int32, (SUBLANES, LANES), 0)
    return route, jnp.where(row == 0, n1, jnp.where(row == 1, n2, 0.0))


def _mix_out(attn, conv, four, xs, mod, p, *, moe, cast=()):
    if mod.shape[0] == 1 and xs.shape[0] > 1 and not moe:
        assert not cast
        flat = lambda a: a.reshape(1, -1, a.shape[-1])
        return _mix_out(flat(attn), flat(conv), flat(four), flat(xs), mod, p, moe=moe).reshape(xs.shape)
    B, S, D = xs.shape
    tm = min(TM_OUT, S)
    sub = min(SUB_ROWS_MOE if moe else SUB_ROWS, tm)
    bm = mod.shape[0]
    mod_map = (lambda b, i: (b, 0, 0)) if bm > 1 else (lambda b, i: (0, 0, 0))
    c2 = lambda b, i: (0, 0)
    tok = lambda w: pl.BlockSpec((None, tm, w), lambda b, i: (b, i, 0))
    args = [attn, conv, four, xs, mod, p["w_out"], p["ffn_norm_g"]]
    specs = [tok(ATTN_OUT), tok(CONV_CH), tok(FOURIER_CH), tok(D), pl.BlockSpec((None, N_MOD, D), mod_map),
             pl.BlockSpec((D, D), c2), pl.BlockSpec((1, D), c2)]
    out_shapes = [jax.ShapeDtypeStruct((B, S, D), F32)]
    out_specs = [tok(D)]
    scratch = []
    vmem = 2 * D * D * 2 + 24 * tm * D * 4 + (8 << 20)
    if not moe:
        resident = lambda shape: pl.BlockSpec(shape, c2, pipeline_mode=pl.Buffered(1))
        args += [p["ffn_wg"], p["ffn_wu"], p["ffn_wd"]]
        specs += [resident((D, D_FF)), resident((D, D_FF)), resident((D_FF, D))]
        scratch = [pltpu.VMEM((tm, D), BF16)]
        vmem += 3 * D * D_FF * 2
        steps, per = B * (S // tm), S // tm
        for w in cast:
            rows, cols = w.shape
            assert rows % steps == 0 and (rows // steps) % (2 * SUBLANES) == 0
            spec = pl.BlockSpec((rows // steps, cols), lambda b, i: (b * per + i, 0))
            args.append(w)
            specs.append(spec)
            out_specs.append(spec)
            out_shapes.append(jax.ShapeDtypeStruct(w.shape, BF16))
            vmem += 2 * (rows // steps) * cols * 6
    else:
        out_shapes.append(None)
        out_specs.append(None)
        per = S // tm
        out_shapes[1] = jax.ShapeDtypeStruct((B * S * SUBLANES, LANES), F32)
        out_specs[1] = pl.BlockSpec((tm * SUBLANES, LANES), lambda b, i: (b * per + i, 0))
        tri = jnp.asarray(np.tril(np.ones((sub, sub), np.float32), -1), BF16)
        args += [p["router_w"], tri]
        specs += [pl.BlockSpec((D, 2 * LANES), c2), pl.BlockSpec((sub, sub), c2)]
        out_shapes += [jax.ShapeDtypeStruct((B, S, LANES), F32),
                       jax.ShapeDtypeStruct((B, S // sub, SUBLANES, LANES), F32)]
        out_specs += [tok(LANES), pl.BlockSpec((None, tm // sub, SUBLANES, LANES), lambda b, i: (b, i, 0, 0))]
        out_shapes.append(jax.ShapeDtypeStruct((SUBLANES, B * S), F32))
        out_specs.append(pl.BlockSpec((SUBLANES, tm), lambda b, i: (0, b * per + i)))
    outs = pl.pallas_call(
        functools.partial(_mixout_kernel, moe=moe, ncast=len(cast)),
        grid=(B, S // tm),
        in_specs=specs,
        out_specs=out_specs,
        out_shape=out_shapes,
        scratch_shapes=scratch,
        compiler_params=_cparams(("arbitrary", "arbitrary"), vmem),
        name="mix_out_moe" if moe else "mix_ffn",
    )(*args)
    return outs if moe or cast else outs[0]


def _swiglu_tile(hb, wg_ref, wu_ref, wd_ref):
    ff = wg_ref.shape[1]
    acc = jnp.zeros((hb.shape[0], D_MODEL), F32)
    for c0 in range(0, ff, FF_CHUNK):
        sl = slice(c0, min(c0 + FF_CHUNK, ff))
        g = jnp.dot(hb, wg_ref[:, sl].astype(BF16), preferred_element_type=F32)
        u = jnp.dot(hb, wu_ref[:, sl].astype(BF16), preferred_element_type=F32)
        a = (g * _sigmoid(g) * u).astype(BF16)
        acc = acc + jnp.dot(a, wd_ref[sl, :].astype(BF16), preferred_element_type=F32)
    return acc


def _token_copy(src_ref, src_row, dst_ref, dst_row, sem):
    return pltpu.make_async_copy(src_ref.at[pl.ds(pl.multiple_of(src_row, SUBLANES), SUBLANES), :],
                                 dst_ref.at[pl.ds(pl.multiple_of(dst_row, SUBLANES), SUBLANES), :], sem)


def _dispatch_kernel(slot_ref, pad_ref, h_ref, xs_ref, hbuf, zero_ref, in_sem, out_sem, *, tm, npad, steps):
    i = pl.program_id(0)

    tr = tm * SUBLANES

    def fetch(step, slot):
        return pltpu.make_async_copy(h_ref.at[pl.ds(step * tr, tr), :], hbuf.at[slot], in_sem.at[slot])

    def drain(slot):
        for _ in range(2):
            pltpu.make_async_copy(hbuf.at[slot], xs_ref.at[pl.ds(0, tr), :], out_sem.at[slot]).wait()
        pltpu.make_async_copy(hbuf.at[slot, pl.ds(0, npad * SUBLANES), :], xs_ref.at[pl.ds(0, npad * SUBLANES), :],
                              out_sem.at[slot]).wait()

    @pl.when(i == 0)
    def _():
        zero_ref[...] = jnp.zeros(zero_ref.shape, F32)
        fetch(0, 0).start()

    slot = lax.rem(i, 3)
    fetch(i, slot).wait()

    @pl.when(i + 1 < steps)
    def _():
        fetch(i + 1, lax.rem(i + 1, 3)).start()

    def scatter(r, carry):
        for k in range(2):
            _token_copy(hbuf.at[slot], r * SUBLANES, xs_ref, slot_ref[0, 0, k * tm + r],
                        out_sem.at[slot]).start(priority=k)
        return carry

    def scatter_pad(j, carry):
        _token_copy(zero_ref, 0, xs_ref, pad_ref[0, 0, j], out_sem.at[slot]).start()
        return carry

    lax.fori_loop(0, tm, scatter, 0, unroll=8)
    lax.fori_loop(0, npad, scatter_pad, 0, unroll=8)

    @pl.when(i > 0)
    def _():
        drain(lax.rem(i + 2, 3))

    @pl.when(i == steps - 1)
    def _():
        drain(slot)


def _dispatch(h_tiles, slots, pad_slots, n_tokens_out):
    n = h_tiles.shape[0] // SUBLANES
    tm = TM_ROUTE
    npad = pad_slots.shape[-1]
    steps = n // tm
    return pl.pallas_call(
        functools.partial(_dispatch_kernel, tm=tm, npad=npad, steps=steps),
        grid=(steps,),
        in_specs=[pl.BlockSpec((1, 1, 2 * tm), lambda i: (i, 0, 0), memory_space=pltpu.SMEM),
                  pl.BlockSpec((1, 1, npad), lambda i: (i, 0, 0), memory_space=pltpu.SMEM),
                  pl.BlockSpec(memory_space=pl.ANY)],
        out_specs=pl.BlockSpec(memory_space=pl.ANY),
        out_shape=jax.ShapeDtypeStruct((n_tokens_out * SUBLANES, LANES), F32),
        scratch_shapes=[pltpu.VMEM((3, tm * SUBLANES, LANES), F32), pltpu.VMEM((SUBLANES, LANES), F32),
                        pltpu.SemaphoreType.DMA((3,)), pltpu.SemaphoreType.DMA((3,))],
        compiler_params=_cparams(("arbitrary",), 3 * tm * D_MODEL * 4 + (4 << 20)),
        name="dispatch",
    )(slots, pad_slots, h_tiles)


def _experts_kernel(te_ref, nt_ref, xs_ref, wg_ref, wu_ref, wd_ref, o_ref, *, tm):
    t = pl.program_id(0)

    @pl.when(t < nt_ref[0])
    def _():
        hb = _load_token_tiles(xs_ref, 0, tm).astype(BF16)
        _store_token_tiles(o_ref, 0, _swiglu_tile(hb, wg_ref, wu_ref, wd_ref))

    @pl.when(t >= nt_ref[0])
    def _():
        o_ref[...] = jnp.zeros(o_ref.shape, F32)


def _experts(xs, tile_expert, n_tiles_used, p):
    D = D_MODEL
    tm = TM_MOE
    n_tiles = tile_expert.shape[0]
    wmap = lambda t, te, nt: (te[t], 0, 0)
    tok = pl.BlockSpec((tm * SUBLANES, LANES), lambda t, te, nt: (t, 0))
    tok_in = pl.BlockSpec((tm * SUBLANES, LANES), lambda t, te, nt: (jnp.minimum(t, nt[0] - 1), 0))
    vmem = 2 * 3 * D * D_FF * p["moe_wg"].dtype.itemsize + 12 * tm * D * 4 + (6 << 20)
    return pl.pallas_call(
        functools.partial(_experts_kernel, tm=tm),
        grid_spec=pltpu.PrefetchScalarGridSpec(
            num_scalar_prefetch=2,
            grid=(n_tiles,),
            in_specs=[tok_in,
                      pl.BlockSpec((None, D, D_FF), wmap),
                      pl.BlockSpec((None, D, D_FF), wmap),
                      pl.BlockSpec((None, D_FF, D), wmap)],
            out_specs=tok),
        out_shape=jax.ShapeDtypeStruct((n_tiles * tm * SUBLANES, LANES), F32),
        compiler_params=_cparams(("arbitrary",), vmem),
        name="experts",
    )(tile_expert, n_tiles_used, xs, p["moe_wg"], p["moe_wu"], p["moe_wd"])


def _combine_kernel(slot_ref, next_ref, x1_ref, route_ref, mod_ref, ys_ref, o_ref, buf_ref, sem, *, tm, steps):
    i = pl.program_id(0)

    def gather(idx_ref, s):
        def body(r, carry):
            for k in range(2):
                _token_copy(ys_ref, idx_ref[0, 0, k * tm + r], buf_ref.at[s, k], r * SUBLANES,
                            sem.at[s]).start(priority=k)
            return carry
        lax.fori_loop(0, tm, body, 0, unroll=8)

    @pl.when(i == 0)
    def _():
        gather(slot_ref, 0)

    @pl.when(i + 1 < steps)
    def _():
        gather(next_ref, lax.rem(i + 1, 2))

    cur = lax.rem(i, 2)
    for k in range(2):
        pltpu.make_async_copy(ys_ref.at[pl.ds(0, tm * SUBLANES), :], buf_ref.at[cur, k], sem.at[cur]).wait()
    route = route_ref[...]
    w1 = route[:, 2:3]
    w2 = route[:, 3:4]
    y0 = _load_token_tiles(buf_ref, 0, tm, lead=(cur, 0))
    y1 = _load_token_tiles(buf_ref, 0, tm, lead=(cur, 1))
    o_ref[...] = x1_ref[...] + mod_ref[5:6, :] * (w1 * y0 + w2 * y1)


def _combine(x1, route, mod, ys, slots):
    B, S, D = x1.shape
    tm = TM_ROUTE
    per = S // tm
    steps = B * per
    tok = lambda w: pl.BlockSpec((None, tm, w), lambda i: (i // per, i % per, 0))
    return pl.pallas_call(
        functools.partial(_combine_kernel, tm=tm, steps=steps),
        grid=(steps,),
        in_specs=[pl.BlockSpec((1, 1, 2 * tm), lambda i: (i, 0, 0), memory_space=pltpu.SMEM),
                  pl.BlockSpec((1, 1, 2 * tm), lambda i: (jnp.minimum(i + 1, steps - 1), 0, 0),
                               memory_space=pltpu.SMEM),
                  tok(D), tok(LANES), pl.BlockSpec((None, N_MOD, D), lambda i: (i // per, 0, 0)),
                  pl.BlockSpec(memory_space=pl.ANY)],
        out_specs=tok(D),
        out_shape=jax.ShapeDtypeStruct((B, S, D), F32),
        scratch_shapes=[pltpu.VMEM((2, 2, tm * SUBLANES, LANES), F32), pltpu.SemaphoreType.DMA((2,))],
        compiler_params=_cparams(("arbitrary",), 12 * tm * D * 4 + (4 << 20)),
        name="combine",
    )(slots, slots, x1, route, mod, ys)


def _moe(h, x1, route, cnt, route_t, mod, p):
    B, S, D = x1.shape
    n = B * S
    sub = n // (cnt.shape[0] * cnt.shape[1])
    tmm = TM_MOE
    n_tiles = 2 * n // tmm + N_EXPERTS
    n_slots = n_tiles * tmm
    steps = n // TM_ROUTE
    counts = cnt[:, :, 0:2, 0:N_EXPERTS].astype(jnp.int32).reshape(-1, 2, N_EXPERTS)
    tile_tot = counts.sum(axis=0)
    n_e = tile_tot.sum(axis=0)
    base = jnp.cumsum(counts, axis=0) - counts
    base = base + jnp.array([0, 1], jnp.int32)[None, :, None] * tile_tot[0][None, None, :]
    tiles_e = (n_e + tmm - 1) // tmm
    pstart = (jnp.cumsum(tiles_e) - tiles_e) * tmm
    expert_ids = jnp.arange(N_EXPERTS, dtype=jnp.int32)[:, None]
    slots_k = []
    for k in range(2):
        e_k = route_t[k].astype(jnp.int32)
        table = jnp.repeat((base[:, k, :] + pstart[None, :]).T, sub, axis=1)
        slot = jnp.sum(jnp.where(e_k[None, :] == expert_ids, table, 0), axis=0) + route_t[4 + k].astype(jnp.int32)
        slots_k.append((slot * SUBLANES).reshape(steps, TM_ROUTE))
    slots = jnp.concatenate(slots_k, axis=1).reshape(steps, 1, 2 * TM_ROUTE)
    tile_end = jnp.cumsum(tiles_e)
    n_used = tile_end[-1]
    t_idx = jnp.minimum(jnp.arange(n_tiles, dtype=jnp.int32), n_used - 1)
    tile_expert = jnp.sum(t_idx[:, None] >= tile_end[None, :], axis=-1).astype(jnp.int32)
    tile_expert = jnp.minimum(tile_expert, N_EXPERTS - 1)

    n_cand = N_EXPERTS * tmm
    pq = jnp.arange(tmm, dtype=jnp.int32)[None, :]
    n_pad_e = (tiles_e * tmm - n_e)[:, None]
    spare = n_slots + expert_ids * tmm + pq
    pad_slots = jnp.where(pq < n_pad_e, (pstart + n_e)[:, None] + pq, spare).astype(jnp.int32)
    assert n_cand % steps == 0
    pad_slots = (pad_slots * SUBLANES).reshape(steps, 1, n_cand // steps)

    xs = _dispatch(h, slots, pad_slots, n_slots + n_cand)
    ys = _experts(xs, tile_expert, n_used.reshape(1).astype(jnp.int32), p)
    return _combine(x1, route, mod, ys, slots)


def _pad_heads(w, width):
    lead = w.shape[:-1]
    w = w.reshape(lead + (N_HEADS, width))
    w = jnp.pad(w, [(0, 0)] * len(lead) + [(0, 0), (0, HEAD_PAD - width)])
    return w.reshape(lead + (N_HEADS * HEAD_PAD,))


def _layer_params(l, w_in, q_lat_g, kv_lat_g, w_uq, w_ukv, q_norm_g, k_norm_g, conv_w, conv_b, conv_ln_g,
                  conv_ln_b, w_out, ffn_norm_g):
    wi = w_in[l]
    kr = _with_partner(jnp.pad(wi[:, OFF_KR:OFF_CONV], ((0, 0), (QK_NOPE, HEAD_PAD - QK_HEAD))))
    ckv = wi[:, OFF_CKV:OFF_KR]
    w_full = jnp.concatenate([wi[:, OFF_CQ:OFF_CKV], ckv, wi[:, OFF_CONV:OFF_FOUR], wi[:, OFF_FOUR:IN_COLS], kr],
                             axis=1)
    w_kv = jnp.concatenate([ckv, kr], axis=1)
    ukv = w_ukv[l].reshape(KV_LORA, N_HEADS, QK_NOPE + V_HEAD)
    uk = _pad_heads(ukv[:, :, :QK_NOPE].reshape(KV_LORA, N_HEADS * QK_NOPE), QK_NOPE)
    uv = _pad_heads(ukv[:, :, QK_NOPE:].reshape(KV_LORA, ATTN_OUT), V_HEAD)
    pad_g = lambda g: jnp.pad(g, (0, HEAD_PAD - QK_HEAD)).reshape(1, HEAD_PAD)
    uq = _with_partner(_pad_heads(w_uq[l], QK_HEAD))
    hw = N_HEADS * HEAD_PAD
    col = np.arange(hw)[:, None]
    head_of_col = col // HEAD_PAD == np.arange(LANES)[None, :]
    return {
        "w_in_full": w_full.astype(BF16),
        "w_in_kv": w_kv.astype(BF16),
        "q_lat_g": q_lat_g[l].reshape(1, Q_LORA),
        "kv_lat_g": kv_lat_g[l].reshape(1, KV_LORA),
        "w_uq": uq.astype(BF16),
        "w_ukv": jnp.concatenate([uk, uv], axis=1).astype(BF16),
        "q_norm_g": pad_g(q_norm_g[l]) * (QK_HEAD ** -0.5 * math.log2(math.e)),
        "v_ones": jnp.tile(jnp.concatenate([jnp.zeros((V_HEAD,), F32), jnp.ones((HEAD_PAD - V_HEAD,), F32)]),
                           N_HEADS).reshape(1, hw),
        "ones_h": jnp.asarray(head_of_col & (col % HEAD_PAD < QK_HEAD), F32).astype(BF16),
        "k_norm_g": pad_g(k_norm_g[l]),
        "conv_w": conv_w[l],
        "conv_b": conv_b[l].reshape(1, CONV_CH),
        "conv_ln_g": conv_ln_g[l].reshape(1, CONV_CH),
        "conv_ln_b": conv_ln_b[l].reshape(1, CONV_CH),
        "ffn_norm_g": ffn_norm_g[l].reshape(1, D_MODEL),
        "bd": _channel_dft(),
    }


def _with_partner(w):
    lead = w.shape[:-1]
    half = QK_ROPE // 2
    w3 = w.reshape(lead + (-1, HEAD_PAD))
    out = jnp.concatenate([w3[..., :QK_HEAD], w3[..., QK_NOPE + half:QK_HEAD], w3[..., QK_NOPE:QK_NOPE + half]],
                          axis=-1)
    return out.reshape(w.shape)


def _rot_partner(w):
    lead = w.shape[:-1]
    half = QK_ROPE // 2
    w3 = w.reshape(lead + (-1, HEAD_PAD))
    z = jnp.zeros_like(w3)
    out = jnp.concatenate([z[..., :QK_NOPE], w3[..., QK_NOPE + half:QK_HEAD], w3[..., QK_NOPE:QK_NOPE + half],
                           z[..., QK_HEAD:]], axis=-1)
    return out.reshape(w.shape)


def _rope_tables(seq, gq, gk):
    rows = seq // GRID_W
    row = np.repeat(np.arange(rows, dtype=np.float64), GRID_W)
    col = np.tile(np.arange(GRID_W, dtype=np.float64), rows)
    n_freq = QK_ROPE // 4
    inv = ROPE_BASE ** (-np.arange(n_freq, dtype=np.float64) / n_freq)
    ang = np.concatenate([row[:, None] * inv, col[:, None] * inv], axis=-1)
    cos, sin = np.cos(ang), np.sin(ang)
    ones = np.ones((seq, QK_NOPE))
    tail = np.zeros((seq, HEAD_PAD - QK_HEAD))
    cos_t = jnp.asarray(np.concatenate([ones, cos, cos, tail], axis=1), F32)
    sin_t = jnp.asarray(np.concatenate([np.zeros((seq, QK_NOPE)), -sin, sin, tail], axis=1), F32)
    return (cos_t * gq, sin_t * _rot_partner(gq), cos_t * gk, sin_t * _rot_partner(gk))


def kernel(x, c, ctx, c_ctx, ada_w, ada_b, mix_norm_g, ffn_norm_g, w_in, q_lat_g, kv_lat_g, w_uq, w_ukv, q_norm_g,
           k_norm_g, conv_w, conv_b, conv_ln_g, conv_ln_b, w_out, ffn_w_gate, ffn_w_up, ffn_w_down, router_w,
           moe_w_gate, moe_w_up, moe_w_down):
    B, S, D = x.shape
    T = ctx.shape[1]
    assert (D, DEPTH) == (D_MODEL, ada_w.shape[0]) and S % GRID_W == 0

    cc = jnp.concatenate([c, c_ctx[None, :], jnp.zeros((2 * SUBLANES - B - 1, D), F32)], axis=0)
    mods = _modulation(cc, ada_w, ada_b).reshape(DEPTH, 2 * SUBLANES, N_MOD, D)
    cs_x = _dft_tables(S)
    cs_c = _dft_tables(T)

    def moe_f32(i):
        return (moe_w_gate[i].reshape(N_EXPERTS * D, D_FF), moe_w_up[i].reshape(N_EXPERTS * D, D_FF),
                moe_w_down[i].reshape(N_EXPERTS * D_FF, D))

    moe_bf16 = {}
    for l in range(DEPTH):
        last = l == DEPTH - 1
        p = _layer_params(l, w_in, q_lat_g, kv_lat_g, w_uq, w_ukv, q_norm_g, k_norm_g, conv_w, conv_b, conv_ln_g,
                          conv_ln_b, w_out, ffn_norm_g)
        p["mix_norm_g"] = mix_norm_g[l].reshape(1, D)
        rope_tabs = _rope_tables(S, p["q_norm_g"], p["k_norm_g"])
        i = l // 2
        moe = l % 2 == 1
        mod_x = mods[l, :B]
        mod_c = mods[l, B:B + 1]

        nxt = l + 1 if l % 2 == 0 else l + 2
        cast, cast_late = [], []
        if not moe:
            cast += [ffn_w_gate[i], ffn_w_up[i], ffn_w_down[i]]
        if nxt < DEPTH:
            nxt_w = list(moe_f32(nxt // 2))
            cast, cast_late = (cast + nxt_w[:2], nxt_w[2:]) if not moe else (cast + nxt_w, [])
        p["w_out"] = w_out[l].astype(BF16)

        def channel_mixer(attn, conv, four, xs, mod, late=()):
            if moe:
                x1, h, route, cnt, route_t = _mix_out(attn, conv, four, xs, mod, p, moe=True)
                mod_b = mod if mod.shape[0] > 1 else jnp.broadcast_to(mod, (B,) + mod.shape[1:])
                return _moe(h, x1, route, cnt, route_t, mod_b, p), []
            if not late:
                return _mix_out(attn, conv, four, xs, mod, p, moe=False), []
            out, *late_bf16 = _mix_out(attn, conv, four, xs, mod, p, moe=False, cast=late)
            return out, late_bf16

        if last:
            k_c, v_c = _front(ctx, mod_c, p["mix_norm_g"], p, None, full=False)
        else:
            q_c, k_c, v_c, y_c, zz_c = _front(ctx, mod_c, p["mix_norm_g"], p, None, full=True)
        q_x, k_x, v_x, y_x, zz_x = _front(x, mod_x, p["mix_norm_g"], p, rope_tabs, full=True)
        attn_x, *casted = _attend(q_x, [k_x, k_c], [v_x, v_c], cast)
        if moe:
            rw = jnp.pad(router_w[i], ((0, 0), (0, LANES - N_EXPERTS)))
            rw_hi = rw.astype(BF16)
            p["router_w"] = jnp.concatenate([rw_hi, (rw - rw_hi.astype(F32)).astype(BF16)], axis=1)
            wg, wu, wd = moe_bf16.pop(i) if i in moe_bf16 else [w.astype(BF16) for w in moe_f32(i)]
            p["moe_wg"] = wg.reshape(N_EXPERTS, D, D_FF)
            p["moe_wu"] = wu.reshape(N_EXPERTS, D, D_FF)
            p["moe_wd"] = wd.reshape(N_EXPERTS, D_FF, D)
        else:
            p["ffn_wg"], p["ffn_wu"], p["ffn_wd"] = casted[:3]
            casted = casted[3:]

        if not last:
            attn_c, = _attend(q_c, [k_c], [v_c])
            ctx_next, _ = channel_mixer(attn_c, *_conv_fourier(y_c, zz_c, cs_c, p), ctx, mod_c)

        x, late_bf16 = channel_mixer(attn_x, *_conv_fourier(y_x, zz_x, cs_x, p), x, mod_x, cast_late)
        if nxt < DEPTH:
            moe_bf16[nxt // 2] = list(casted) + late_bf16
        if not last:
            ctx = ctx_next
    return x
```

```python
import functools
import math

import numpy as np
import jax
import jax.numpy as jnp
from jax import lax
from jax.experimental import pallas as pl
from jax.experimental.pallas import tpu as pltpu

F32 = jnp.float32
BF16 = jnp.bfloat16

D_MODEL = 1024
DEPTH = 2
GRID_W = 64
N_HEADS = 8
QK_NOPE = 64
QK_ROPE = 32
QK_HEAD = QK_NOPE + QK_ROPE
V_HEAD = 64
Q_LORA = 384
KV_LORA = 256
ROPE_BASE = 10000.0
CONV_CH = 256
CONV_WIDTH = 31
CONV_PAD = (CONV_WIDTH - 1) // 2
FOURIER_GROUPS = 4
FOURIER_GROUP_DIM = 64
FOURIER_CH = FOURIER_GROUPS * FOURIER_GROUP_DIM
ATTN_OUT = N_HEADS * V_HEAD
OFF_CQ = 0
OFF_CKV = OFF_CQ + Q_LORA
OFF_KR = OFF_CKV + KV_LORA
OFF_CONV = OFF_KR + QK_ROPE
OFF_FOUR = OFF_CONV + 2 * CONV_CH
IN_COLS = OFF_FOUR + FOURIER_CH
D_FF = 2816
N_EXPERTS = 8
N_MOD = 6
EPS = 1e-6

LANES = 128
SUBLANES = 8
HEAD_PAD = LANES
VMEM_CAP = 56 * 1024 * 1024
FF_CHUNK = 256

SUB_ROWS = 256
SUB_ROWS_MOE = 512
TM_FRONT = 1024
TQ_ATTN = 1024
TQ_SUB = 512
TK_ATTN = 256
TM_OUT = 512
TM_MOE = 512
TM_ROUTE = 512
CONV_ROWS = 128
TR_FOURIER = 512


def _cparams(sem, vmem_bytes):
    return pltpu.CompilerParams(dimension_semantics=sem, vmem_limit_bytes=int(min(VMEM_CAP, vmem_bytes)))


def _rms(v, g):
    return v * lax.rsqrt(jnp.mean(v * v, axis=-1, keepdims=True) + EPS) * g


def _sigmoid(v):
    return 1.0 / (1.0 + jnp.exp(-v))


def _mod_kernel(c_ref, w_ref, b_ref, o_ref):
    c = c_ref[...]
    s = (c * _sigmoid(c)).astype(BF16)
    o_ref[...] = jnp.dot(s, w_ref[...].astype(BF16), preferred_element_type=F32) + b_ref[...]


def _modulation(cc, ada_w, ada_b):
    rows = cc.shape[0]
    tn = 1536
    n_out = N_MOD * D_MODEL
    return pl.pallas_call(
        _mod_kernel,
        grid=(DEPTH, n_out // tn),
        in_specs=[
            pl.BlockSpec((rows, D_MODEL), lambda l, j: (0, 0)),
            pl.BlockSpec((None, D_MODEL, tn), lambda l, j: (l, 0, j)),
            pl.BlockSpec((None, 1, tn), lambda l, j: (l, 0, j)),
        ],
        out_specs=pl.BlockSpec((None, rows, tn), lambda l, j: (l, 0, j)),
        out_shape=jax.ShapeDtypeStruct((DEPTH, rows, n_out), F32),
        compiler_params=_cparams(("arbitrary", "arbitrary"), 4 * D_MODEL * tn * 4),
        name="modulation",
    )(cc, ada_w, ada_b.reshape(DEPTH, 1, n_out))


def _front_kernel(*refs, full, rope):
    it = iter(refs)
    x_ref, mod_ref, g_ref, win_ref = next(it), next(it), next(it), next(it)
    if full:
        qlg_ref, wuq_ref = next(it), next(it)
    kvlg_ref, wukv_ref, vones_ref, onesh_ref = next(it), next(it), next(it), next(it)
    if rope:
        cq_ref, sq_ref, ck_ref, sk_ref = next(it), next(it), next(it), next(it)
    else:
        qg_ref = next(it) if full else None
        kg_ref = next(it)
    if full:
        bd_ref = next(it)
        q_out = next(it)
    k_out, v_out = next(it), next(it)
    if full:
        y_out, zz_out = next(it), next(it)

    shift = mod_ref[0:1, :]
    gain = g_ref[...] * (1.0 + mod_ref[1:2, :])
    hw = N_HEADS * HEAD_PAD

    def head_inv_rms(raw):
        ss = jnp.dot((raw * raw).astype(BF16), onesh_ref[...], preferred_element_type=F32)
        rs = lax.rsqrt(ss * (1.0 / QK_HEAD) + EPS)
        return jnp.concatenate([jnp.broadcast_to(rs[:, hd:hd + 1], (rs.shape[0], HEAD_PAD)) for hd in range(N_HEADS)],
                               axis=1)

    def sub_block(rows):
        x = x_ref[rows, :]
        h = _rms(x, gain) + shift
        cols = jnp.dot(h.astype(BF16), win_ref[...], preferred_element_type=F32)

        o = 0
        if full:
            cq = cols[:, 0:Q_LORA]
            o = Q_LORA
            qall = jnp.dot(_rms(cq, qlg_ref[...]).astype(BF16), wuq_ref[...], preferred_element_type=F32)
            rsb = head_inv_rms(qall)
            for hd in range(N_HEADS):
                sl = slice(hd * HEAD_PAD, (hd + 1) * HEAD_PAD)
                if rope:
                    val = qall[:, sl] * cq_ref[rows, :] + pltpu.roll(qall[:, sl], HEAD_PAD - QK_ROPE, 1) * \
                        sq_ref[rows, :]
                else:
                    val = qall[:, sl] * qg_ref[...]
                q_out[rows, sl] = (val * rsb[:, sl]).astype(BF16)

        ckv = cols[:, o:o + KV_LORA]
        o += KV_LORA
        kv = jnp.dot(_rms(ckv, kvlg_ref[...]).astype(BF16), wukv_ref[...], preferred_element_type=F32)
        if full:
            a = cols[:, o:o + CONV_CH]
            gt = cols[:, o + CONV_CH:o + 2 * CONV_CH]
            y_out[rows, :] = a * _sigmoid(gt)
            o += 2 * CONV_CH
            z = cols[:, o:o + FOURIER_CH]
            o += FOURIER_CH
            zz_out[rows, :] = jnp.dot(z.astype(BF16), bd_ref[...], preferred_element_type=F32).astype(BF16)
        krb = cols[:, o:o + HEAD_PAD]
        kraw = [kv[:, hd * HEAD_PAD:(hd + 1) * HEAD_PAD] + krb for hd in range(N_HEADS)]
        rsb = head_inv_rms(jnp.concatenate(kraw, axis=1))
        if rope:
            kpart = pltpu.roll(krb, HEAD_PAD - QK_ROPE, 1) * sk_ref[rows, :]
        for hd in range(N_HEADS):
            sl = slice(hd * HEAD_PAD, (hd + 1) * HEAD_PAD)
            val = kraw[hd] * ck_ref[rows, :] + kpart if rope else kraw[hd] * kg_ref[...]
            k_out[rows, sl] = (val * rsb[:, sl]).astype(BF16)
        v_out[rows, :] = (kv[:, hw:] + vones_ref[...]).astype(BF16)

    tm = x_ref.shape[0]
    sub = min(SUB_ROWS, tm)
    for sb in range(tm // sub):
        sub_block(slice(sb * sub, (sb + 1) * sub))


def _front(xs, mod, norm_g, p, rope_tabs, *, full):
    if mod.shape[0] == 1 and xs.shape[0] > 1 and rope_tabs is None:
        outs = _front(xs.reshape(1, -1, xs.shape[-1]), mod, norm_g, p, None, full=full)
        return [o.reshape(xs.shape[:2] + o.shape[2:]) for o in outs]
    B, S, D = xs.shape
    tm = min(TM_FRONT, S)
    rope = rope_tabs is not None
    assert full or not rope
    w_in = p["w_in_full"] if full else p["w_in_kv"]
    w_uq = p["w_uq"]
    ncol = w_in.shape[1]
    bm = mod.shape[0]
    mod_map = (lambda b, i: (b, 0, 0)) if bm > 1 else (lambda b, i: (0, 0, 0))
    const2 = lambda b, i: (0, 0)
    hw = N_HEADS * HEAD_PAD

    args = [xs, mod, norm_g, w_in]
    specs = [
        pl.BlockSpec((None, tm, D), lambda b, i: (b, i, 0)),
        pl.BlockSpec((None, N_MOD, D), mod_map),
        pl.BlockSpec((1, D), const2),
        pl.BlockSpec((D, ncol), const2),
    ]
    if full:
        args += [p["q_lat_g"], w_uq]
        specs += [pl.BlockSpec((1, Q_LORA), const2), pl.BlockSpec((Q_LORA, w_uq.shape[1]), const2)]
    args += [p["kv_lat_g"], p["w_ukv"], p["v_ones"], p["ones_h"]]
    specs += [pl.BlockSpec((1, KV_LORA), const2),
              pl.BlockSpec((KV_LORA, 2 * hw), const2),
              pl.BlockSpec((1, hw), const2),
              pl.BlockSpec((hw, LANES), const2)]
    if rope:
        args += list(rope_tabs)
        specs += [pl.BlockSpec((tm, HEAD_PAD), lambda b, i: (i, 0))] * 4
    else:
        if full:
            args += [p["q_norm_g"]]
            specs += [pl.BlockSpec((1, HEAD_PAD), const2)]
        args += [p["k_norm_g"]]
        specs += [pl.BlockSpec((1, HEAD_PAD), const2)]
    if full:
        args += [p["bd"]]
        specs += [pl.BlockSpec((FOURIER_CH, 2 * FOURIER_CH), const2)]

    out_shapes, out_specs = [], []

    def add_out(width, dtype):
        out_shapes.append(jax.ShapeDtypeStruct((B, S, width), dtype))
        out_specs.append(pl.BlockSpec((None, tm, width), lambda b, i: (b, i, 0)))

    if full:
        add_out(N_HEADS * HEAD_PAD, BF16)
    add_out(N_HEADS * HEAD_PAD, BF16)
    add_out(N_HEADS * HEAD_PAD, BF16)
    if full:
        add_out(CONV_CH, F32)
        add_out(2 * FOURIER_CH, BF16)

    vmem = 2 * (D * ncol * 2 + Q_LORA * 1024 * 2 + KV_LORA * 2048 * 2) + 28 * tm * D * 4
    return pl.pallas_call(
        functools.partial(_front_kernel, full=full, rope=rope),
        grid=(B, S // tm),
        in_specs=specs,
        out_specs=out_specs,
        out_shape=out_shapes,
        compiler_params=_cparams(("arbitrary", "arbitrary"), vmem),
        name="front_full" if full else "front_kv",
    )(*args)


def _attn_kernel(*refs, nsrc, ncast):
    q_ref = refs[0]
    k_refs = refs[1:1 + nsrc]
    v_refs = refs[1 + nsrc:1 + 2 * nsrc]
    cast_in = refs[1 + 2 * nsrc:1 + 2 * nsrc + ncast]
    o_ref = refs[1 + 2 * nsrc + ncast]
    cast_out = refs[2 + 2 * nsrc + ncast:]
    chunks = []
    for k_ref, v_ref in zip(k_refs, v_refs):
        for s0 in range(0, k_ref.shape[0], TK_ATTN):
            chunks.append((k_ref, v_ref, s0, min(TK_ATTN, k_ref.shape[0] - s0)))
    tq = q_ref.shape[0]
    sub = min(TQ_SUB, tq)
    for pair, r0 in [(pr, rr) for pr in range(q_ref.shape[1] // (2 * HEAD_PAD)) for rr in range(0, tq, sub)]:
        rows = slice(r0, r0 + sub)
        state = [None, None]
        for k_ref, v_ref, s0, sz in chunks:
            for hh in range(2):
                sl = slice((2 * pair + hh) * HEAD_PAD, (2 * pair + hh + 1) * HEAD_PAD)
                s = lax.dot_general(q_ref[rows, sl], k_ref[s0:s0 + sz, sl], (((1,), (1,)), ((), ())),
                                    preferred_element_type=F32)
                m = jnp.max(s, axis=-1, keepdims=True)
                if state[hh] is not None:
                    m_old, acc_old = state[hh]
                    m = jnp.maximum(m_old, m)
                pv = jnp.dot(jnp.exp2((s - m).astype(BF16)), v_ref[s0:s0 + sz, sl], preferred_element_type=F32)
                if state[hh] is not None:
                    pv = pv + jnp.exp2(m_old - m) * acc_old
                state[hh] = (m, pv)
        accs = [state[0][1], state[1][1]]
        lane = lax.broadcasted_iota(jnp.int32, accs[0].shape, 1)
        lo = accs[0] / pltpu.roll(accs[0], V_HEAD, 1)
        hi = pltpu.roll(accs[1], V_HEAD, 1) / accs[1]
        o_ref[rows, pair * 2 * V_HEAD:(pair + 1) * 2 * V_HEAD] = jnp.where(lane < V_HEAD, lo, hi).astype(BF16)
    for src, dst in zip(cast_in, cast_out):
        dst[...] = src[...].astype(BF16)


def _attend(q, ks, vs, cast=()):
    B, S, _ = q.shape
    tq = min(TQ_ATTN, S)
    nsrc = len(ks)
    nq = S // tq
    pp = N_HEADS // 2 if nq == 1 and S <= TQ_SUB else 1
    ngrp = N_HEADS // 2 // pp
    steps = B * ngrp * nq
    specs = [pl.BlockSpec((None, tq, pp * 2 * HEAD_PAD), lambda b, hp, i: (b, i, hp))]
    for kv in list(ks) + list(vs):
        specs.append(pl.BlockSpec((None, kv.shape[1], pp * 2 * HEAD_PAD), lambda b, hp, i: (b, 0, hp)))
    out_specs = [pl.BlockSpec((None, tq, pp * 2 * V_HEAD), lambda b, hp, i: (b, i, hp))]
    out_shapes = [jax.ShapeDtypeStruct((B, S, ATTN_OUT), BF16)]
    cast_bytes = 0
    for w in cast:
        rows, cols = w.shape
        nblk = max(d for d in range(1, steps + 1)
                   if steps % d == 0 and rows % d == 0 and (rows // d) % (2 * SUBLANES) == 0)
        hold = steps // nblk
        spec = pl.BlockSpec((rows // nblk, cols),
                            lambda b, hp, i, hold=hold: (((b * ngrp + hp) * nq + i) // hold, 0))
        specs.append(spec)
        out_specs.append(spec)
        out_shapes.append(jax.ShapeDtypeStruct(w.shape, BF16))
        cast_bytes += 2 * (rows // nblk) * cols * 6
    t_all = sum(k.shape[1] for k in ks)
    vmem = 2 * (t_all * pp * 512 * 2) * 2 + 16 * min(tq, TQ_SUB) * TK_ATTN * 4 + cast_bytes + (12 << 20)
    outs = pl.pallas_call(
        functools.partial(_attn_kernel, nsrc=nsrc, ncast=len(cast)),
        grid=(B, ngrp, nq),
        in_specs=specs,
        out_specs=out_specs,
        out_shape=out_shapes,
        compiler_params=_cparams(("arbitrary", "arbitrary", "arbitrary"), vmem),
        name="attend%d" % nsrc,
    )(q, *ks, *vs, *cast)
    return list(outs)


def _conv_fourier_kernel(y_ref, w_ref, b_ref, lg_ref, lb_ref, cs_ref, zz_ref, conv_out, four_out, pad_ref, *, seq):
    halo = 2 * SUBLANES
    total = seq + 2 * halo
    pad_ref[0, 0:halo, :] = jnp.zeros((halo, CONV_CH), F32)
    pad_ref[0, halo + seq:total, :] = jnp.zeros((halo, CONV_CH), F32)
    pad_ref[0, halo:halo + seq, :] = y_ref[...]
    slab = pad_ref[0]
    for ph in range(1, SUBLANES):
        pad_ref[ph] = pltpu.roll(slab, total - ph, 0)

    def conv_rows(base, rows):
        acc = jnp.zeros((rows, CONV_CH), F32)
        for j in range(CONV_WIDTH):
            off = base + halo - CONV_PAD + j
            ph = off % SUBLANES
            acc = acc + pad_ref[ph, off - ph:off - ph + rows, :] * w_ref[j:j + 1, :]
        acc = acc + b_ref[...]
        mu = jnp.mean(acc, axis=-1, keepdims=True)
        cen = acc - mu
        var = jnp.mean(cen * cen, axis=-1, keepdims=True)
        yn = cen * lax.rsqrt(var + EPS) * lg_ref[...] + lb_ref[...]
        conv_out[base:base + rows, :] = (yn * _sigmoid(yn)).astype(BF16)

    def dft_rows(base, rows):
        r = slice(base, base + rows)
        acc = jnp.dot(cs_ref[r, 0:seq], zz_ref[:, 0:FOURIER_CH], preferred_element_type=F32)
        acc = acc + jnp.dot(cs_ref[r, seq:2 * seq], zz_ref[:, FOURIER_CH:2 * FOURIER_CH],
                            preferred_element_type=F32)
        four_out[r, :] = acc.astype(BF16)

    tr = min(TR_FOURIER, seq)
    for base in range(0, seq, tr):
        for cb in range(base, base + tr, CONV_ROWS):
            conv_rows(cb, CONV_ROWS)
        dft_rows(base, tr)


def _conv_fourier(y, zz, cs, p):
    B, S, _ = y.shape
    c2 = lambda b: (0, 0)
    seq_block = lambda w: pl.BlockSpec((None, S, w), lambda b: (b, 0, 0))
    vmem = (SUBLANES + 6) * (S + 32) * CONV_CH * 4 + S * 2 * S * 2 + 2 * S * 512 * 2 + (10 << 20)
    return pl.pallas_call(
        functools.partial(_conv_fourier_kernel, seq=S),
        grid=(B,),
        in_specs=[seq_block(CONV_CH),
                  pl.BlockSpec((CONV_WIDTH, CONV_CH), c2),
                  pl.BlockSpec((1, CONV_CH), c2), pl.BlockSpec((1, CONV_CH), c2), pl.BlockSpec((1, CONV_CH), c2),
                  pl.BlockSpec((S, 2 * S), c2, pipeline_mode=pl.Buffered(1)),
                  seq_block(2 * FOURIER_CH)],
        out_specs=[seq_block(CONV_CH), seq_block(FOURIER_CH)],
        out_shape=[jax.ShapeDtypeStruct((B, S, CONV_CH), BF16), jax.ShapeDtypeStruct((B, S, FOURIER_CH), BF16)],
        scratch_shapes=[pltpu.VMEM((SUBLANES, S + 4 * SUBLANES, CONV_CH), F32)],
        compiler_params=_cparams(("arbitrary",), vmem),
        name="conv_fourier",
    )(y, p["conv_w"], p["conv_b"], p["conv_ln_g"], p["conv_ln_b"], cs, zz)


def _dft_tables(seq):
    norm = seq ** -0.5
    ks = (np.arange(seq)[:, None] * np.arange(seq)[None, :]) % seq
    ang = 2.0 * np.pi * ks / seq
    tab = np.concatenate([np.cos(ang), -np.sin(ang)], axis=1) * norm
    return jnp.asarray(tab, F32).astype(BF16)


def _channel_dft():
    n = FOURIER_GROUP_DIM
    ang = 2.0 * np.pi * ((np.arange(n)[:, None] * np.arange(n)[None, :]) % n) / n
    eye = np.eye(FOURIER_GROUPS)
    bd = np.concatenate([np.kron(eye, np.cos(ang)), np.kron(eye, np.sin(ang))], axis=1) * n ** -0.5
    return jnp.asarray(bd, F32).astype(BF16)


def _mixout_kernel(*refs, moe, ncast=0):
    cast_in = cast_out = ()
    if moe:
        (attn_ref, conv_ref, four_ref, x_ref, mod_ref, wout_ref, g_ref, rw_ref, tri_ref,
         x1_out, h_out, route_out, cnt_out, routet_out) = refs
    else:
        (attn_ref, conv_ref, four_ref, x_ref, mod_ref, wout_ref, g_ref, wg_ref, wu_ref, wd_ref) = refs[:10]
        cast_in = refs[10:10 + ncast]
        x1_out = refs[10 + ncast]
        cast_out = refs[11 + ncast:11 + 2 * ncast]
        h_out = refs[11 + 2 * ncast]
    tm = x_ref.shape[0]
    sub = min(SUB_ROWS_MOE if moe else SUB_ROWS, tm)
    gain = g_ref[...] * (1.0 + mod_ref[4:5, :])
    for sb in range(tm // sub):
        rows = slice(sb * sub, (sb + 1) * sub)
        mix = jnp.dot(attn_ref[rows, :], wout_ref[0:ATTN_OUT, :], preferred_element_type=F32)
        mix = mix + jnp.dot(conv_ref[rows, :], wout_ref[ATTN_OUT:ATTN_OUT + CONV_CH, :],
                            preferred_element_type=F32)
        mix = mix + jnp.dot(four_ref[rows, :], wout_ref[ATTN_OUT + CONV_CH:, :], preferred_element_type=F32)
        x1 = x_ref[rows, :] + mod_ref[2:3, :] * mix
        x1_out[rows, :] = x1
        h = _rms(x1, gain) + mod_ref[3:4, :]
        if moe:
            _store_token_tiles(h_out, sb * sub, h)
            route, cnt = _route(h, rw_ref, tri_ref)
            route_out[rows, :] = route
            cnt_out[sb] = cnt
            routet_out[:, rows] = route.T[0:SUBLANES, :]
        else:
            h_out[rows, :] = h.astype(BF16)
    if not moe:
        x1_out[...] = x1_out[...] + mod_ref[5:6, :] * _swiglu_tile(h_out[...], wg_ref, wu_ref, wd_ref)
    for src, dst in zip(cast_in, cast_out):
        dst[...] = src[...].astype(BF16)


def _store_token_tiles(ref, row0, val):
    rows = val.shape[0]
    for j in range(SUBLANES):
        ref[pl.ds(row0 * SUBLANES + j, rows, stride=SUBLANES), :] = val[:, j * LANES:(j + 1) * LANES]


def _load_token_tiles(ref, row0, rows, lead=()):
    idx = tuple(lead)
    return jnp.concatenate([ref[idx + (pl.ds(row0 * SUBLANES + j, rows, stride=SUBLANES), slice(None))]
                            for j in range(SUBLANES)], axis=1)


def _route(h, rw_ref, tri_ref):
    tm = h.shape[0]
    h_hi = h.astype(BF16)
    h_lo = (h - h_hi.astype(F32)).astype(BF16)
    hw = jnp.dot(h_hi, rw_ref[...], preferred_element_type=F32)
    logits = hw[:, 0:LANES] + hw[:, LANES:] + jnp.dot(h_lo, rw_ref[:, 0:LANES], preferred_element_type=F32)
    lane = lax.broadcasted_iota(jnp.int32, (tm, LANES), 1).astype(F32)
    neg = jnp.float32(-jnp.inf)
    lm = jnp.where(lane < N_EXPERTS, logits, neg)
    m1 = jnp.max(lm, axis=-1, keepdims=True)
    i1 = jnp.min(jnp.where(lm == m1, lane, float(LANES)), axis=-1, keepdims=True)
    lm2 = jnp.where(lane == i1, neg, lm)
    m2 = jnp.max(lm2, axis=-1, keepdims=True)
    i2 = jnp.min(jnp.where(lm2 == m2, lane, float(LANES)), axis=-1, keepdims=True)
    t = jnp.exp(m2 - m1)
    w1 = 1.0 / (1.0 + t)
    w2 = t / (1.0 + t)
    oh1 = lane == i1
    oh2 = lane == i2
    tri = tri_ref[...]
    c1 = jnp.dot(tri, jnp.where(oh1, 1.0, 0.0).astype(BF16), preferred_element_type=F32)
    c2 = jnp.dot(tri, jnp.where(oh2, 1.0, 0.0).astype(BF16), preferred_element_type=F32)
    r1 = jnp.sum(jnp.where(oh1, c1, 0.0), axis=-1, keepdims=True)
    r2 = jnp.sum(jnp.where(oh2, c2, 0.0), axis=-1, keepdims=True)
    vals = [i1, i2, w1, w2, r1, r2]
    route = jnp.zeros((tm, LANES), F32)
    for idx, v in enumerate(vals):
        route = jnp.where(lane == idx, v, route)
    n1 = jnp.sum(jnp.where(oh1, 1.0, 0.0), axis=0, keepdims=True)
    n2 = jnp.sum(jnp.where(oh2, 1.0, 0.0), axis=0, keepdims=True)
    row = lax.broadcasted_iota(jnp.int32, (SUBLANES, LANES), 0)
    return route, jnp.where(row == 0, n1, jnp.where(row == 1, n2, 0.0))


def _mix_out(attn, conv, four, xs, mod, p, *, moe, cast=()):
    if mod.shape[0] == 1 and xs.shape[0] > 1 and not moe:
        assert not cast
        flat = lambda a: a.reshape(1, -1, a.shape[-1])
        return _mix_out(flat(attn), flat(conv), flat(four), flat(xs), mod, p, moe=moe).reshape(xs.shape)
    B, S, D = xs.shape
    tm = min(2 * TM_OUT if moe else TM_OUT, S)
    sub = min(SUB_ROWS_MOE if moe else SUB_ROWS, tm)
    bm = mod.shape[0]
    mod_map = (lambda b, i: (b, 0, 0)) if bm > 1 else (lambda b, i: (0, 0, 0))
    c2 = lambda b, i: (0, 0)
    tok = lambda w: pl.BlockSpec((None, tm, w), lambda b, i: (b, i, 0))
    args = [attn, conv, four, xs, mod, p["w_out"], p["ffn_norm_g"]]
    specs = [tok(ATTN_OUT), tok(CONV_CH), tok(FOURIER_CH), tok(D), pl.BlockSpec((None, N_MOD, D), mod_map),
             pl.BlockSpec((D, D), c2), pl.BlockSpec((1, D), c2)]
    out_shapes = [jax.ShapeDtypeStruct((B, S, D), F32)]
    out_specs = [tok(D)]
    scratch = []
    vmem = 2 * D * D * 2 + 24 * tm * D * 4 + (8 << 20)
    if not moe:
        resident = lambda shape: pl.BlockSpec(shape, c2, pipeline_mode=pl.Buffered(1))
        args += [p["ffn_wg"], p["ffn_wu"], p["ffn_wd"]]
        specs += [resident((D, D_FF)), resident((D, D_FF)), resident((D_FF, D))]
        scratch = [pltpu.VMEM((tm, D), BF16)]
        vmem += 3 * D * D_FF * 2
        steps, per = B * (S // tm), S // tm
        for w in cast:
            rows, cols = w.shape
            assert rows % steps == 0 and (rows // steps) % (2 * SUBLANES) == 0
            spec = pl.BlockSpec((rows // steps, cols), lambda b, i: (b * per + i, 0))
            args.append(w)
            specs.append(spec)
            out_specs.append(spec)
            out_shapes.append(jax.ShapeDtypeStruct(w.shape, BF16))
            vmem += 2 * (rows // steps) * cols * 6
    else:
        out_shapes.append(None)
        out_specs.append(None)
        per = S // tm
        out_shapes[1] = jax.ShapeDtypeStruct((B * S * SUBLANES, LANES), F32)
        out_specs[1] = pl.BlockSpec((tm * SUBLANES, LANES), lambda b, i: (b * per + i, 0))
        tri = jnp.asarray(np.tril(np.ones((sub, sub), np.float32), -1), BF16)
        args += [p["router_w"], tri]
        specs += [pl.BlockSpec((D, 2 * LANES), c2), pl.BlockSpec((sub, sub), c2)]
        out_shapes += [jax.ShapeDtypeStruct((B, S, LANES), F32),
                       jax.ShapeDtypeStruct((B, S // sub, SUBLANES, LANES), F32)]
        out_specs += [tok(LANES), pl.BlockSpec((None, tm // sub, SUBLANES, LANES), lambda b, i: (b, i, 0, 0))]
        out_shapes.append(jax.ShapeDtypeStruct((SUBLANES, B * S), F32))
        out_specs.append(pl.BlockSpec((SUBLANES, tm), lambda b, i: (0, b * per + i)))
    outs = pl.pallas_call(
        functools.partial(_mixout_kernel, moe=moe, ncast=len(cast)),
        grid=(B, S // tm),
        in_specs=specs,
        out_specs=out_specs,
        out_shape=out_shapes,
        scratch_shapes=scratch,
        compiler_params=_cparams(("arbitrary", "arbitrary"), vmem),
        name="mix_out_moe" if moe else "mix_ffn",
    )(*args)
    return outs if moe or cast else outs[0]


def _swiglu_tile(hb, wg_ref, wu_ref, wd_ref):
    ff = wg_ref.shape[1]
    acc = jnp.zeros((hb.shape[0], D_MODEL), F32)
    for c0 in range(0, ff, FF_CHUNK):
        sl = slice(c0, min(c0 + FF_CHUNK, ff))
        g = jnp.dot(hb, wg_ref[:, sl].astype(BF16), preferred_element_type=F32)
        u = jnp.dot(hb, wu_ref[:, sl].astype(BF16), preferred_element_type=F32)
        a = (g * _sigmoid(g) * u).astype(BF16)
        acc = acc + jnp.dot(a, wd_ref[sl, :].astype(BF16), preferred_element_type=F32)
    return acc


def _token_copy(src_ref, src_row, dst_ref, dst_row, sem):
    return pltpu.make_async_copy(src_ref.at[pl.ds(pl.multiple_of(src_row, SUBLANES), SUBLANES), :],
                                 dst_ref.at[pl.ds(pl.multiple_of(dst_row, SUBLANES), SUBLANES), :], sem)


def _dispatch_kernel(slot_ref, pad_ref, h_ref, xs_ref, hbuf, zero_ref, in_sem, out_sem, *, tm, npad, steps):
    i = pl.program_id(0)

    tr = tm * SUBLANES

    def fetch(step, slot):
        return pltpu.make_async_copy(h_ref.at[pl.ds(step * tr, tr), :], hbuf.at[slot], in_sem.at[slot])

    def drain(slot):
        for _ in range(2):
            pltpu.make_async_copy(hbuf.at[slot], xs_ref.at[pl.ds(0, tr), :], out_sem.at[slot]).wait()
        pltpu.make_async_copy(hbuf.at[slot, pl.ds(0, npad * SUBLANES), :], xs_ref.at[pl.ds(0, npad * SUBLANES), :],
                              out_sem.at[slot]).wait()

    @pl.when(i == 0)
    def _():
        zero_ref[...] = jnp.zeros(zero_ref.shape, F32)
        fetch(0, 0).start()

    slot = lax.rem(i, 3)
    fetch(i, slot).wait()

    @pl.when(i + 1 < steps)
    def _():
        fetch(i + 1, lax.rem(i + 1, 3)).start()

    def scatter(r, carry):
        for k in range(2):
            _token_copy(hbuf.at[slot], r * SUBLANES, xs_ref, slot_ref[0, 0, k * tm + r],
                        out_sem.at[slot]).start(priority=k)
        return carry

    def scatter_pad(j, carry):
        _token_copy(zero_ref, 0, xs_ref, pad_ref[0, 0, j], out_sem.at[slot]).start()
        return carry

    lax.fori_loop(0, tm, scatter, 0, unroll=8)
    lax.fori_loop(0, npad, scatter_pad, 0, unroll=8)

    @pl.when(i > 0)
    def _():
        drain(lax.rem(i + 2, 3))

    @pl.when(i == steps - 1)
    def _():
        drain(slot)


def _dispatch(h_tiles, slots, pad_slots, n_tokens_out):
    n = h_tiles.shape[0] // SUBLANES
    tm = TM_ROUTE
    npad = pad_slots.shape[-1]
    steps = n // tm
    return pl.pallas_call(
        functools.partial(_dispatch_kernel, tm=tm, npad=npad, steps=steps),
        grid=(steps,),
        in_specs=[pl.BlockSpec((1, 1, 2 * tm), lambda i: (i, 0, 0), memory_space=pltpu.SMEM),
                  pl.BlockSpec((1, 1, npad), lambda i: (i, 0, 0), memory_space=pltpu.SMEM),
                  pl.BlockSpec(memory_space=pl.ANY)],
        out_specs=pl.BlockSpec(memory_space=pl.ANY),
        out_shape=jax.ShapeDtypeStruct((n_tokens_out * SUBLANES, LANES), F32),
        scratch_shapes=[pltpu.VMEM((3, tm * SUBLANES, LANES), F32), pltpu.VMEM((SUBLANES, LANES), F32),
                        pltpu.SemaphoreType.DMA((3,)), pltpu.SemaphoreType.DMA((3,))],
        compiler_params=_cparams(("arbitrary",), 3 * tm * D_MODEL * 4 + (4 << 20)),
        name="dispatch",
    )(slots, pad_slots, h_tiles)


def _experts_kernel(te_ref, nt_ref, xs_ref, wg_ref, wu_ref, wd_ref, o_ref, *, tm):
    t = pl.program_id(0)

    @pl.when(t < nt_ref[0])
    def _():
        hb = _load_token_tiles(xs_ref, 0, tm).astype(BF16)
        _store_token_tiles(o_ref, 0, _swiglu_tile(hb, wg_ref, wu_ref, wd_ref))

    @pl.when(t >= nt_ref[0])
    def _():
        o_ref[...] = jnp.zeros(o_ref.shape, F32)


def _experts(xs, tile_expert, n_tiles_used, p):
    D = D_MODEL
    tm = TM_MOE
    n_tiles = tile_expert.shape[0]
    wmap = lambda t, te, nt: (te[t], 0, 0)
    tok = pl.BlockSpec((tm * SUBLANES, LANES), lambda t, te, nt: (t, 0))
    tok_in = pl.BlockSpec((tm * SUBLANES, LANES), lambda t, te, nt: (jnp.minimum(t, nt[0] - 1), 0))
    vmem = 2 * 3 * D * D_FF * p["moe_wg"].dtype.itemsize + 12 * tm * D * 4 + (6 << 20)
    return pl.pallas_call(
        functools.partial(_experts_kernel, tm=tm),
        grid_spec=pltpu.PrefetchScalarGridSpec(
            num_scalar_prefetch=2,
            grid=(n_tiles,),
            in_specs=[tok_in,
                      pl.BlockSpec((None, D, D_FF), wmap),
                      pl.BlockSpec((None, D, D_FF), wmap),
                      pl.BlockSpec((None, D_FF, D), wmap)],
            out_specs=tok),
        out_shape=jax.ShapeDtypeStruct((n_tiles * tm * SUBLANES, LANES), F32),
        compiler_params=_cparams(("arbitrary",), vmem),
        name="experts",
    )(tile_expert, n_tiles_used, xs, p["moe_wg"], p["moe_wu"], p["moe_wd"])


def _combine_kernel(slot_ref, next_ref, x1_ref, route_ref, mod_ref, ys_ref, o_ref, buf_ref, sem, *, tm, steps):
    i = pl.program_id(0)

    def gather(idx_ref, s):
        def body(r, carry):
            for k in range(2):
                _token_copy(ys_ref, idx_ref[0, 0, k * tm + r], buf_ref.at[s, k], r * SUBLANES,
                            sem.at[s]).start(priority=k)
            return carry
        lax.fori_loop(0, tm, body, 0, unroll=8)

    @pl.when(i == 0)
    def _():
        gather(slot_ref, 0)

    @pl.when(i + 1 < steps)
    def _():
        gather(next_ref, lax.rem(i + 1, 2))

    cur = lax.rem(i, 2)
    for k in range(2):
        pltpu.make_async_copy(ys_ref.at[pl.ds(0, tm * SUBLANES), :], buf_ref.at[cur, k], sem.at[cur]).wait()
    route = route_ref[...]
    w1 = route[:, 2:3]
    w2 = route[:, 3:4]
    y0 = _load_token_tiles(buf_ref, 0, tm, lead=(cur, 0))
    y1 = _load_token_tiles(buf_ref, 0, tm, lead=(cur, 1))
    o_ref[...] = x1_ref[...] + mod_ref[5:6, :] * (w1 * y0 + w2 * y1)


def _combine(x1, route, mod, ys, slots):
    B, S, D = x1.shape
    tm = TM_ROUTE
    per = S // tm
    steps = B * per
    tok = lambda w: pl.BlockSpec((None, tm, w), lambda i: (i // per, i % per, 0))
    return pl.pallas_call(
        functools.partial(_combine_kernel, tm=tm, steps=steps),
        grid=(steps,),
        in_specs=[pl.BlockSpec((1, 1, 2 * tm), lambda i: (i, 0, 0), memory_space=pltpu.SMEM),
                  pl.BlockSpec((1, 1, 2 * tm), lambda i: (jnp.minimum(i + 1, steps - 1), 0, 0),
                               memory_space=pltpu.SMEM),
                  tok(D), tok(LANES), pl.BlockSpec((None, N_MOD, D), lambda i: (i // per, 0, 0)),
                  pl.BlockSpec(memory_space=pl.ANY)],
        out_specs=tok(D),
        out_shape=jax.ShapeDtypeStruct((B, S, D), F32),
        scratch_shapes=[pltpu.VMEM((2, 2, tm * SUBLANES, LANES), F32), pltpu.SemaphoreType.DMA((2,))],
        compiler_params=_cparams(("arbitrary",), 12 * tm * D * 4 + (4 << 20)),
        name="combine",
    )(slots, slots, x1, route, mod, ys)


def _moe(h, x1, route, cnt, route_t, mod, p):
    B, S, D = x1.shape
    n = B * S
    sub = n // (cnt.shape[0] * cnt.shape[1])
    tmm = TM_MOE
    n_tiles = 2 * n // tmm + N_EXPERTS
    n_slots = n_tiles * tmm
    steps = n // TM_ROUTE
    counts = cnt[:, :, 0:2, 0:N_EXPERTS].astype(jnp.int32).reshape(-1, 2, N_EXPERTS)
    tile_tot = counts.sum(axis=0)
    n_e = tile_tot.sum(axis=0)
    base = jnp.cumsum(counts, axis=0) - counts
    base = base + jnp.array([0, 1], jnp.int32)[None, :, None] * tile_tot[0][None, None, :]
    tiles_e = (n_e + tmm - 1) // tmm
    pstart = (jnp.cumsum(tiles_e) - tiles_e) * tmm
    expert_ids = jnp.arange(N_EXPERTS, dtype=jnp.int32)[:, None]
    slots_k = []
    for k in range(2):
        e_k = route_t[k].astype(jnp.int32)
        table = jnp.repeat((base[:, k, :] + pstart[None, :]).T, sub, axis=1)
        slot = jnp.sum(jnp.where(e_k[None, :] == expert_ids, table, 0), axis=0) + route_t[4 + k].astype(jnp.int32)
        slots_k.append((slot * SUBLANES).reshape(steps, TM_ROUTE))
    slots = jnp.concatenate(slots_k, axis=1).reshape(steps, 1, 2 * TM_ROUTE)
    tile_end = jnp.cumsum(tiles_e)
    n_used = tile_end[-1]
    t_idx = jnp.minimum(jnp.arange(n_tiles, dtype=jnp.int32), n_used - 1)
    tile_expert = jnp.sum(t_idx[:, None] >= tile_end[None, :], axis=-1).astype(jnp.int32)
    tile_expert = jnp.minimum(tile_expert, N_EXPERTS - 1)

    n_cand = N_EXPERTS * tmm
    pq = jnp.arange(tmm, dtype=jnp.int32)[None, :]
    n_pad_e = (tiles_e * tmm - n_e)[:, None]
    spare = n_slots + expert_ids * tmm + pq
    pad_slots = jnp.where(pq < n_pad_e, (pstart + n_e)[:, None] + pq, spare).astype(jnp.int32)
    assert n_cand % steps == 0
    pad_slots = (pad_slots * SUBLANES).reshape(steps, 1, n_cand // steps)

    xs = _dispatch(h, slots, pad_slots, n_slots + n_cand)
    ys = _experts(xs, tile_expert, n_used.reshape(1).astype(jnp.int32), p)
    return _combine(x1, route, mod, ys, slots)


def _pad_heads(w, width):
    lead = w.shape[:-1]
    w = w.reshape(lead + (N_HEADS, width))
    w = jnp.pad(w, [(0, 0)] * len(lead) + [(0, 0), (0, HEAD_PAD - width)])
    return w.reshape(lead + (N_HEADS * HEAD_PAD,))


def _layer_params(l, w_in, q_lat_g, kv_lat_g, w_uq, w_ukv, q_norm_g, k_norm_g, conv_w, conv_b, conv_ln_g,
                  conv_ln_b, w_out, ffn_norm_g):
    wi = w_in[l]
    kr = _with_partner(jnp.pad(wi[:, OFF_KR:OFF_CONV], ((0, 0), (QK_NOPE, HEAD_PAD - QK_HEAD))))
    ckv = wi[:, OFF_CKV:OFF_KR]
    w_full = jnp.concatenate([wi[:, OFF_CQ:OFF_CKV], ckv, wi[:, OFF_CONV:OFF_FOUR], wi[:, OFF_FOUR:IN_COLS], kr],
                             axis=1)
    w_kv = jnp.concatenate([ckv, kr], axis=1)
    ukv = w_ukv[l].reshape(KV_LORA, N_HEADS, QK_NOPE + V_HEAD)
    uk = _pad_heads(ukv[:, :, :QK_NOPE].reshape(KV_LORA, N_HEADS * QK_NOPE), QK_NOPE)
    uv = _pad_heads(ukv[:, :, QK_NOPE:].reshape(KV_LORA, ATTN_OUT), V_HEAD)
    pad_g = lambda g: jnp.pad(g, (0, HEAD_PAD - QK_HEAD)).reshape(1, HEAD_PAD)
    uq = _with_partner(_pad_heads(w_uq[l], QK_HEAD))
    hw = N_HEADS * HEAD_PAD
    col = np.arange(hw)[:, None]
    head_of_col = col // HEAD_PAD == np.arange(LANES)[None, :]
    return {
        "w_in_full": w_full.astype(BF16),
        "w_in_kv": w_kv.astype(BF16),
        "q_lat_g": q_lat_g[l].reshape(1, Q_LORA),
        "kv_lat_g": kv_lat_g[l].reshape(1, KV_LORA),
        "w_uq": uq.astype(BF16),
        "w_ukv": jnp.concatenate([uk, uv], axis=1).astype(BF16),
        "q_norm_g": pad_g(q_norm_g[l]) * (QK_HEAD ** -0.5 * math.log2(math.e)),
        "v_ones": jnp.tile(jnp.concatenate([jnp.zeros((V_HEAD,), F32), jnp.ones((HEAD_PAD - V_HEAD,), F32)]),
                           N_HEADS).reshape(1, hw),
        "ones_h": jnp.asarray(head_of_col & (col % HEAD_PAD < QK_HEAD), F32).astype(BF16),
        "k_norm_g": pad_g(k_norm_g[l]),
        "conv_w": conv_w[l],
        "conv_b": conv_b[l].reshape(1, CONV_CH),
        "conv_ln_g": conv_ln_g[l].reshape(1, CONV_CH),
        "conv_ln_b": conv_ln_b[l].reshape(1, CONV_CH),
        "ffn_norm_g": ffn_norm_g[l].reshape(1, D_MODEL),
        "bd": _channel_dft(),
    }


def _with_partner(w):
    lead = w.shape[:-1]
    half = QK_ROPE // 2
    w3 = w.reshape(lead + (-1, HEAD_PAD))
    out = jnp.concatenate([w3[..., :QK_HEAD], w3[..., QK_NOPE + half:QK_HEAD], w3[..., QK_NOPE:QK_NOPE + half]],
                          axis=-1)
    return out.reshape(w.shape)


def _rot_partner(w):
    lead = w.shape[:-1]
    half = QK_ROPE // 2
    w3 = w.reshape(lead + (-1, HEAD_PAD))
    z = jnp.zeros_like(w3)
    out = jnp.concatenate([z[..., :QK_NOPE], w3[..., QK_NOPE + half:QK_HEAD], w3[..., QK_NOPE:QK_NOPE + half],
                           z[..., QK_HEAD:]], axis=-1)
    return out.reshape(w.shape)


def _rope_tables(seq, gq, gk):
    rows = seq // GRID_W
    row = np.repeat(np.arange(rows, dtype=np.float64), GRID_W)
    col = np.tile(np.arange(GRID_W, dtype=np.float64), rows)
    n_freq = QK_ROPE // 4
    inv = ROPE_BASE ** (-np.arange(n_freq, dtype=np.float64) / n_freq)
    ang = np.concatenate([row[:, None] * inv, col[:, None] * inv], axis=-1)
    cos, sin = np.cos(ang), np.sin(ang)
    ones = np.ones((seq, QK_NOPE))
    tail = np.zeros((seq, HEAD_PAD - QK_HEAD))
    cos_t = jnp.asarray(np.concatenate([ones, cos, cos, tail], axis=1), F32)
    sin_t = jnp.asarray(np.concatenate([np.zeros((seq, QK_NOPE)), -sin, sin, tail], axis=1), F32)
    return (cos_t * gq, sin_t * _rot_partner(gq), cos_t * gk, sin_t * _rot_partner(gk))


def kernel(x, c, ctx, c_ctx, ada_w, ada_b, mix_norm_g, ffn_norm_g, w_in, q_lat_g, kv_lat_g, w_uq, w_ukv, q_norm_g,
           k_norm_g, conv_w, conv_b, conv_ln_g, conv_ln_b, w_out, ffn_w_gate, ffn_w_up, ffn_w_down, router_w,
           moe_w_gate, moe_w_up, moe_w_down):
    B, S, D = x.shape
    T = ctx.shape[1]
    assert (D, DEPTH) == (D_MODEL, ada_w.shape[0]) and S % GRID_W == 0

    cc = jnp.concatenate([c, c_ctx[None, :], jnp.zeros((2 * SUBLANES - B - 1, D), F32)], axis=0)
    mods = _modulation(cc, ada_w, ada_b).reshape(DEPTH, 2 * SUBLANES, N_MOD, D)
    cs_x = _dft_tables(S)
    cs_c = _dft_tables(T)

    def moe_f32(i):
        return (moe_w_gate[i].reshape(N_EXPERTS * D, D_FF), moe_w_up[i].reshape(N_EXPERTS * D, D_FF),
                moe_w_down[i].reshape(N_EXPERTS * D_FF, D))

    moe_bf16 = {}
    for l in range(DEPTH):
        last = l == DEPTH - 1
        p = _layer_params(l, w_in, q_lat_g, kv_lat_g, w_uq, w_ukv, q_norm_g, k_norm_g, conv_w, conv_b, conv_ln_g,
                          conv_ln_b, w_out, ffn_norm_g)
        p["mix_norm_g"] = mix_norm_g[l].reshape(1, D)
        rope_tabs = _rope_tables(S, p["q_norm_g"], p["k_norm_g"])
        i = l // 2
        moe = l % 2 == 1
        mod_x = mods[l, :B]
        mod_c = mods[l, B:B + 1]

        nxt = l + 1 if l % 2 == 0 else l + 2
        cast, cast_late = [], []
        if not moe:
            cast += [ffn_w_gate[i], ffn_w_up[i], ffn_w_down[i]]
        if nxt < DEPTH:
            nxt_w = list(moe_f32(nxt // 2))
            cast, cast_late = (cast + nxt_w[:2], nxt_w[2:]) if not moe else (cast + nxt_w, [])
        p["w_out"] = w_out[l].astype(BF16)

        def channel_mixer(attn, conv, four, xs, mod, late=()):
            if moe:
                x1, h, route, cnt, route_t = _mix_out(attn, conv, four, xs, mod, p, moe=True)
                mod_b = mod if mod.shape[0] > 1 else jnp.broadcast_to(mod, (B,) + mod.shape[1:])
                return _moe(h, x1, route, cnt, route_t, mod_b, p), []
            if not late:
                return _mix_out(attn, conv, four, xs, mod, p, moe=False), []
            out, *late_bf16 = _mix_out(attn, conv, four, xs, mod, p, moe=False, cast=late)
            return out, late_bf16

        if last:
            k_c, v_c = _front(ctx, mod_c, p["mix_norm_g"], p, None, full=False)
        else:
            q_c, k_c, v_c, y_c, zz_c = _front(ctx, mod_c, p["mix_norm_g"], p, None, full=True)
        q_x, k_x, v_x, y_x, zz_x = _front(x, mod_x, p["mix_norm_g"], p, rope_tabs, full=True)
        attn_x, *casted = _attend(q_x, [k_x, k_c], [v_x, v_c], cast)
        if moe:
            rw = jnp.pad(router_w[i], ((0, 0), (0, LANES - N_EXPERTS)))
            rw_hi = rw.astype(BF16)
            p["router_w"] = jnp.concatenate([rw_hi, (rw - rw_hi.astype(F32)).astype(BF16)], axis=1)
            wg, wu, wd = moe_bf16.pop(i) if i in moe_bf16 else [w.astype(BF16) for w in moe_f32(i)]
            p["moe_wg"] = wg.reshape(N_EXPERTS, D, D_FF)
            p["moe_wu"] = wu.reshape(N_EXPERTS, D, D_FF)
            p["moe_wd"] = wd.reshape(N_EXPERTS, D_FF, D)
        else:
            p["ffn_wg"], p["ffn_wu"], p["ffn_wd"] = casted[:3]
            casted = casted[3:]

        if not last:
            attn_c, = _attend(q_c, [k_c], [v_c])
            ctx_next, _ = channel_mixer(attn_c, *_conv_fourier(y_c, zz_c, cs_c, p), ctx, mod_c)

        x, late_bf16 = channel_mixer(attn_x, *_conv_fourier(y_x, zz_x, cs_x, p), x, mod_x, cast_late)
        if nxt < DEPTH:
            moe_bf16[nxt // 2] = list(casted) + late_bf16
        if not last:
            ctx = ctx_next
    return x
```
